```python
import jax, jax.numpy as jnp
from jax import lax
import numpy as np

D_MODEL = 2048
BATCH = 2
SEQ = 4096
DEPTH = 1

HEAD_DIM = 128
EPS = 1e-6
A_PATTERNS = ((128, 1), (512, 4), (2048, 16))
A_N_GROUPS = len(A_PATTERNS)
A_HEADS = 4
A_WIDTH = A_HEADS * HEAD_DIM
ROPE_THETA = 500000.0
ROPE_DIMS = HEAD_DIM // 4
B_Q_HEADS = 16
B_KV_HEADS = 4
B_GROUP = B_Q_HEADS // B_KV_HEADS
B_Q_WIDTH = B_Q_HEADS * HEAD_DIM
B_KV_WIDTH = B_KV_HEADS * HEAD_DIM
AXIAL_THETA = 10000.0
AXIAL_DIMS = HEAD_DIM // 2
GRID_W = 64
Q_BLOCK = 128
A_COLS = A_N_GROUPS * 3 * A_WIDTH
B_COLS = B_Q_WIDTH + 2 * B_KV_WIDTH
GATE_COLS = 2 * D_MODEL
IN_COLS = A_COLS + B_COLS + GATE_COLS
N_MOD = 6
N_EXPERTS = 64
N_EXPERT_GROUPS = 8
TOPK_GROUPS = 4
TOP_K = 8
D_EXPERT = D_MODEL // 4
D_SHARED = D_MODEL // 4
ROUTED_SCALE = 2.5
MOE_BLOCK = 128
NEG = -1e30

kernel_name = 'hybrid_dilated_gqa_moe_block'


def rmsnorm(x, g):
    xf = x.astype(jnp.float32)
    y = xf * lax.rsqrt(jnp.mean(xf * xf, axis=-1, keepdims=True) + EPS)
    return (y * g.astype(jnp.float32)).astype(x.dtype)


def rotate(x, cos, sin):
    half = x.shape[-1] // 2
    x1, x2 = x[..., :half], x[..., half:]
    return jnp.concatenate([x1 * cos - x2 * sin, x2 * cos + x1 * sin], axis=-1)


def rope_angles(pos, dims, theta):
    inv = jnp.power(jnp.float32(theta), -jnp.arange(0, dims, 2, dtype=jnp.float32) / dims)
    return pos.astype(jnp.float32)[:, None] * inv[None, :]


def partial_rope(x):
    s = x.shape[2]
    ang = rope_angles(jnp.arange(s), ROPE_DIMS, ROPE_THETA)
    cos, sin = jnp.cos(ang).astype(x.dtype), jnp.sin(ang).astype(x.dtype)
    return jnp.concatenate([rotate(x[..., :ROPE_DIMS], cos, sin), x[..., ROPE_DIMS:]], axis=-1)


def axial_rope(x):
    s = x.shape[1]
    rows = s // GRID_W
    row_id = jnp.broadcast_to(jnp.arange(rows)[:, None], (rows, GRID_W)).reshape(-1)
    col_id = jnp.broadcast_to(jnp.arange(GRID_W)[None, :], (rows, GRID_W)).reshape(-1)
    ang_r = rope_angles(row_id, AXIAL_DIMS, AXIAL_THETA)[:, None, :]
    ang_c = rope_angles(col_id, AXIAL_DIMS, AXIAL_THETA)[:, None, :]
    xr = rotate(x[..., :AXIAL_DIMS], jnp.cos(ang_r).astype(x.dtype), jnp.sin(ang_r).astype(x.dtype))
    xc = rotate(x[..., AXIAL_DIMS:], jnp.cos(ang_c).astype(x.dtype), jnp.sin(ang_c).astype(x.dtype))
    return jnp.concatenate([xr, xc], axis=-1)


def banded_attention(q, k, v, half_window):
    b, h, L, hd = q.shape
    P = half_window
    nb = -(-L // P)
    Lp = nb * P
    qb = jnp.pad(q, ((0, 0), (0, 0), (0, Lp - L), (0, 0))).reshape(b, h, nb, P, hd)
    kv_pad = ((0, 0), (0, 0), (P, Lp - L + P), (0, 0))
    kb = jnp.pad(k, kv_pad).reshape(b, h, nb + 2, P, hd)
    vb = jnp.pad(v, kv_pad).reshape(b, h, nb + 2, P, hd)
    kwin = jnp.concatenate([kb[:, :, :-2], kb[:, :, 1:-1], kb[:, :, 2:]], axis=3)
    vwin = jnp.concatenate([vb[:, :, :-2], vb[:, :, 1:-1], vb[:, :, 2:]], axis=3)
    q_pos = jnp.arange(Lp).reshape(nb, P)
    k_pos = jnp.arange(nb)[:, None] * P - P + jnp.arange(3 * P)[None, :]
    kp = k_pos[:, None, :]
    valid = (jnp.abs(q_pos[:, :, None] - kp) <= half_window) & (kp >= 0) & (kp < L)
    sc = jnp.einsum('bhnqd,bhnkd->bhnqk', qb, kwin).astype(jnp.float32) * (hd ** -0.5)
    sc = jnp.where(valid, sc, NEG)
    lse = jax.nn.logsumexp(sc, axis=-1)
    p = jnp.exp(sc - lse[..., None])
    o = jnp.einsum('bhnqk,bhnkd->bhnqd', p.astype(v.dtype), vwin)
    return o.reshape(b, h, Lp, hd)[:, :, :L], lse.reshape(b, h, Lp)[:, :, :L]


def dilated_attention(q, k, v, window, dilation):
    b, h, s, hd = q.shape
    L = s // dilation

    def split(z):
        return z.reshape(b, h, L, dilation, hd).transpose(0, 1, 3, 2, 4).reshape(b, h * dilation, L, hd)

    o, lse = banded_attention(split(q), split(k), split(v), (window // 2) // dilation)
    o = o.reshape(b, h, dilation, L, hd).transpose(0, 1, 3, 2, 4).reshape(b, h, s, hd)
    lse = lse.reshape(b, h, dilation, L).transpose(0, 1, 3, 2).reshape(b, h, s)
    return o, lse


def mixer_a(a_cols):
    b, s, _ = a_cols.shape
    a = a_cols.reshape(b, s, A_N_GROUPS, 3, A_HEADS, HEAD_DIM).transpose(2, 3, 0, 4, 1, 5)
    outs, lses = [], []
    for gi, (window, dilation) in enumerate(A_PATTERNS):
        q = partial_rope(a[gi, 0])
        k = partial_rope(a[gi, 1])
        o, lse = dilated_attention(q, k, a[gi, 2], window, dilation)
        outs.append(o)
        lses.append(lse)
    w = jax.nn.softmax(jnp.stack(lses, axis=0), axis=0)
    o = jnp.sum(w[..., None].astype(outs[0].dtype) * jnp.stack(outs, axis=0), axis=0)
    return o.transpose(0, 2, 1, 3).reshape(b, s, A_WIDTH)


def mixer_b(qc, kc, vc, q_norm_g, k_norm_g):
    b, s, _ = qc.shape
    q = axial_rope(rmsnorm(qc.reshape(b, s, B_Q_HEADS, HEAD_DIM), q_norm_g))
    k = axial_rope(rmsnorm(kc.reshape(b, s, B_KV_HEADS, HEAD_DIM), k_norm_g))
    v = vc.reshape(b, s, B_KV_HEADS, HEAD_DIM).transpose(0, 2, 1, 3)
    k = k.transpose(0, 2, 1, 3)
    q = q.reshape(b, s, B_KV_HEADS, B_GROUP, HEAD_DIM).transpose(0, 2, 3, 1, 4)
    nblk = s // Q_BLOCK
    qb = q.reshape(b, B_KV_HEADS, B_GROUP, nblk, Q_BLOCK, HEAD_DIM).transpose(3, 0, 1, 2, 4, 5)
    scale = HEAD_DIM ** -0.5

    def one_block(qblk):
        sc = jnp.einsum('bkgqd,bksd->bkgqs', qblk, k).astype(jnp.float32) * scale
        p = jax.nn.softmax(sc, axis=-1)
        return jnp.einsum('bkgqs,bksd->bkgqd', p.astype(v.dtype), v)

    o = lax.map(one_block, qb)
    return o.transpose(1, 0, 4, 2, 3, 5).reshape(b, s, B_Q_WIDTH)


def moe(h, w_router, e_bias, w1, w3, w2, ws1, ws3, ws2):
    b, s, d = h.shape
    t = h.reshape(b * s, d)
    n_tok = t.shape[0]
    scores = jax.nn.sigmoid((t @ w_router).astype(jnp.float32))
    sel = scores + e_bias.astype(jnp.float32)
    per_group = N_EXPERTS // N_EXPERT_GROUPS
    grp_score = jnp.sum(lax.top_k(sel.reshape(n_tok, N_EXPERT_GROUPS, per_group), 2)[0], axis=-1)
    _, top_g = lax.top_k(grp_score, TOPK_GROUPS)
    gmask = jnp.any(top_g[:, :, None] == jnp.arange(N_EXPERT_GROUPS)[None, None, :], axis=1)
    emask = jnp.repeat(gmask, per_group, axis=1)
    _, top_e = lax.top_k(jnp.where(emask, sel, -jnp.inf), TOP_K)
    chosen = jnp.take_along_axis(scores, top_e, axis=1)
    wts = chosen / jnp.sum(chosen, axis=-1, keepdims=True) * ROUTED_SCALE
    gate = jnp.sum(jax.nn.one_hot(top_e, N_EXPERTS, dtype=jnp.float32) * wts[..., None], axis=1)
    gate = gate.astype(t.dtype)
    nb = n_tok // MOE_BLOCK

    def expert_block(args):
        tb, gb = args
        a = jnp.einsum('td,edf->tef', tb, w1)
        u = jnp.einsum('td,edf->tef', tb, w3)
        hid = jax.nn.silu(a) * u * gb[..., None]
        return jnp.einsum('tef,efd->td', hid, w2)

    routed = lax.map(expert_block, (t.reshape(nb, MOE_BLOCK, d), gate.reshape(nb, MOE_BLOCK, N_EXPERTS)))
    shared = (jax.nn.silu(t @ ws1) * (t @ ws3)) @ ws2
    return (routed.reshape(n_tok, d) + shared).reshape(b, s, d)


def setup_inputs(seed: int = 0) -> dict:
    key = jax.random.key(seed)
    ks = jax.random.split(key, 24)
    f32 = jnp.float32

    def nrm(k, shape, fan_in, mult=1.0):
        return jax.random.normal(k, shape, f32) * (mult * fan_in ** -0.5)

    L = DEPTH
    return {
        'x': jax.random.normal(ks[0], (BATCH, SEQ, D_MODEL), f32),
        'c': jax.random.normal(ks[1], (BATCH, D_MODEL), f32),
        'w_ada': nrm(ks[2], (L, D_MODEL, N_MOD * D_MODEL), D_MODEL, 0.5),
        'b_ada': 0.02 * jax.random.normal(ks[3], (L, N_MOD * D_MODEL), f32),
        'g_attn': 1.0 + 0.05 * jax.random.normal(ks[4], (L, D_MODEL), f32),
        'w_in': nrm(ks[5], (L, D_MODEL, IN_COLS), D_MODEL),
        'b_gate': 0.02 * jax.random.normal(ks[6], (L, GATE_COLS), f32),
        'q_norm_g': 1.0 + 0.05 * jax.random.normal(ks[7], (L, HEAD_DIM), f32),
        'k_norm_g': 1.0 + 0.05 * jax.random.normal(ks[8], (L, HEAD_DIM), f32),
        'w_a_up': nrm(ks[9], (L, A_WIDTH, D_MODEL), A_WIDTH),
        'w_b_up': nrm(ks[10], (L, B_Q_WIDTH, D_MODEL), B_Q_WIDTH),
        'w_out': nrm(ks[11], (L, D_MODEL, D_MODEL), D_MODEL),
        'g_ffn': 1.0 + 0.05 * jax.random.normal(ks[12], (L, D_MODEL), f32),
        'w_router': nrm(ks[13], (L, D_MODEL, N_EXPERTS), D_MODEL),
        'e_bias': 0.01 * jax.random.normal(ks[14], (L, N_EXPERTS), f32),
        'w1': nrm(ks[15], (L, N_EXPERTS, D_MODEL, D_EXPERT), D_MODEL),
        'w3': nrm(ks[16], (L, N_EXPERTS, D_MODEL, D_EXPERT), D_MODEL),
        'w2': nrm(ks[17], (L, N_EXPERTS, D_EXPERT, D_MODEL), D_EXPERT),
        'ws1': nrm(ks[18], (L, D_MODEL, D_SHARED), D_MODEL),
        'ws3': nrm(ks[19], (L, D_MODEL, D_SHARED), D_MODEL),
        'ws2': nrm(ks[20], (L, D_SHARED, D_MODEL), D_SHARED),
        'g_final': 1.0 + 0.05 * jax.random.normal(ks[21], (D_MODEL,), f32),
    }


def reference(x, c, w_ada, b_ada, g_attn, w_in, b_gate, q_norm_g, k_norm_g, w_a_up, w_b_up, w_out,
              g_ffn, w_router, e_bias, w1, w3, w2, ws1, ws3, ws2, g_final):
    c_act = jax.nn.silu(c)
    for l in range(DEPTH):
        mod = c_act @ w_ada[l] + b_ada[l]
        sh1, sc1, gt1, sh2, sc2, gt2 = jnp.split(mod[:, None, :], N_MOD, axis=-1)
        h = rmsnorm(x, g_attn[l]) * (1.0 + sc1) + sh1
        proj = h @ w_in[l]
        o1 = A_COLS
        o2 = o1 + B_Q_WIDTH
        o3 = o2 + B_KV_WIDTH
        o4 = o3 + B_KV_WIDTH
        y_a = mixer_a(proj[..., :o1]) @ w_a_up[l]
        y_b = mixer_b(proj[..., o1:o2], proj[..., o2:o3], proj[..., o3:o4], q_norm_g[l], k_norm_g[l]) @ w_b_up[l]
        gates = jax.nn.sigmoid(proj[..., o4:] + b_gate[l])
        merged = gates[..., :D_MODEL] * y_a + gates[..., D_MODEL:] * y_b
        x = x + gt1 * (merged @ w_out[l])
        h2 = rmsnorm(x, g_ffn[l]) * (1.0 + sc2) + sh2
        x = x + gt2 * moe(h2, w_router[l], e_bias[l], w1[l], w3[l], w2[l], ws1[l], ws3[l], ws2[l])
    return rmsnorm(x, g_final)
```

```python
import functools

import jax
import jax.numpy as jnp
from jax import lax
from jax.experimental import pallas as pl
from jax.experimental.pallas import tpu as pltpu

F32 = jnp.float32
BF16 = jnp.bfloat16

D_MODEL = 2048
BATCH = 2
SEQ = 4096
N_TOK = BATCH * SEQ
HEAD_DIM = 128
EPS = 1e-6
A_PATTERNS = ((128, 1), (512, 4), (2048, 16))
A_N_GROUPS = len(A_PATTERNS)
A_HEADS = 4
A_WIDTH = A_HEADS * HEAD_DIM
ROPE_THETA = 500000.0
ROPE_DIMS = HEAD_DIM // 4
B_Q_HEADS = 16
B_KV_HEADS = 4
B_GROUP = B_Q_HEADS // B_KV_HEADS
B_Q_WIDTH = B_Q_HEADS * HEAD_DIM
B_KV_WIDTH = B_KV_HEADS * HEAD_DIM
AXIAL_THETA = 10000.0
AXIAL_DIMS = HEAD_DIM // 2
GRID_W = 64
A_COLS = A_N_GROUPS * 3 * A_WIDTH
B_COLS = B_Q_WIDTH + 2 * B_KV_WIDTH
GATE_COLS = 2 * D_MODEL
IN_COLS = A_COLS + B_COLS + GATE_COLS
N_MOD = 6
N_EXPERTS = 64
N_EXPERT_GROUPS = 8
PER_GROUP = N_EXPERTS // N_EXPERT_GROUPS
TOPK_GROUPS = 4
TOP_K = 8
D_EXPERT = D_MODEL // 4
D_SHARED = D_MODEL // 4
ROUTED_SCALE = 2.5
NEG = -1e30
ATTN_SCALE = HEAD_DIM ** -0.5

LANES = 128
SUBLANES = 8
VMEM_LIMIT = 56 * 1024 * 1024

COL_BLK = 512
N_COL_BLKS = IN_COLS // COL_BLK
A_BLKS = A_COLS // COL_BLK
BQ_BLK0 = A_BLKS
BK_BLK = BQ_BLK0 + B_Q_WIDTH // COL_BLK
BV_BLK = BK_BLK + 1
GATE_BLK0 = BV_BLK + 1


def _cparams(semantics):
    return pltpu.CompilerParams(dimension_semantics=semantics, vmem_limit_bytes=VMEM_LIMIT)


def _ada_kernel(c_ref, w_ref, b_ref, o_ref):
    c = c_ref[...]
    act = c * jax.nn.sigmoid(c)
    o_ref[...] = jnp.dot(act, w_ref[...], preferred_element_type=F32,
                         precision=lax.Precision.HIGHEST) + b_ref[...]


def _ada(c_pad, w_ada, b_ada):
    tn = 1024
    n_out = w_ada.shape[1]
    return pl.pallas_call(
        _ada_kernel,
        out_shape=jax.ShapeDtypeStruct((SUBLANES, n_out), F32),
        grid=(n_out // tn,),
        in_specs=[pl.BlockSpec((SUBLANES, D_MODEL), lambda n: (0, 0)),
                  pl.BlockSpec((D_MODEL, tn), lambda n: (0, n)),
                  pl.BlockSpec((1, tn), lambda n: (0, n))],
        out_specs=pl.BlockSpec((SUBLANES, tn), lambda n: (0, n)),
        compiler_params=_cparams(("parallel",)),
        name="ada_mod",
    )(c_pad, w_ada, b_ada)


def _normmod(x, g, sc, sh):
    y = x * lax.rsqrt(jnp.mean(x * x, axis=-1, keepdims=True) + EPS)
    return (y * g) * (1.0 + sc) + sh


def _normmod_kernel(x_ref, g_ref, sc_ref, sh_ref, o_ref):
    o_ref[...] = _normmod(x_ref[...], g_ref[...], sc_ref[...], sh_ref[...]).astype(o_ref.dtype)


def _norm_modulate(x2, g, mod4, sc_idx, sh_idx):
    tm = 512
    per_b = SEQ // tm
    return pl.pallas_call(
        _normmod_kernel,
        out_shape=jax.ShapeDtypeStruct((N_TOK, D_MODEL), BF16),
        grid=(N_TOK // tm,),
        in_specs=[pl.BlockSpec((tm, D_MODEL), lambda m: (m, 0)),
                  pl.BlockSpec((1, D_MODEL), lambda m: (0, 0)),
                  pl.BlockSpec((None, None, 1, D_MODEL), lambda m: (m // per_b, sc_idx, 0, 0)),
                  pl.BlockSpec((None, None, 1, D_MODEL), lambda m: (m // per_b, sh_idx, 0, 0))],
        out_specs=pl.BlockSpec((tm, D_MODEL), lambda m: (m, 0)),
        compiler_params=_cparams(("parallel",)),
        name="norm_modulate",
    )(x2, g, mod4, mod4)


def _tile4(t):
    return jnp.concatenate([t, t, t, t], axis=1)


def _rotary(y, cos_ref, sin_lo_ref, sin_hi_ref, shift):
    width = y.shape[1]
    return (y * _tile4(cos_ref[...])
            + pltpu.roll(y, width - shift, 1) * _tile4(sin_lo_ref[...])
            + pltpu.roll(y, shift, 1) * _tile4(sin_hi_ref[...]))


def _head_rmsnorm(y, g):
    outs = []
    for h in range(y.shape[1] // HEAD_DIM):
        yh = y[:, h * HEAD_DIM:(h + 1) * HEAD_DIM]
        outs.append(yh * lax.rsqrt(jnp.mean(yh * yh, axis=-1, keepdims=True) + EPS) * g)
    return jnp.concatenate(outs, axis=1)


def _proj_kernel(h_ref, w_ref, ca_ref, sal_ref, sah_ref, cb_ref, sbl_ref, sbh_ref,
                 qg_ref, kg_ref, bg_ref, o_ref):
    n = pl.program_id(1)
    acc = jnp.dot(h_ref[...], w_ref[...], preferred_element_type=F32)
    is_a = n < A_BLKS
    part = n % 3

    @pl.when(is_a & (part == 0))
    def _():
        o_ref[...] = (_rotary(acc, ca_ref, sal_ref, sah_ref, ROPE_DIMS // 2) * ATTN_SCALE).astype(o_ref.dtype)

    @pl.when(is_a & (part == 1))
    def _():
        o_ref[...] = _rotary(acc, ca_ref, sal_ref, sah_ref, ROPE_DIMS // 2).astype(o_ref.dtype)

    @pl.when((is_a & (part == 2)) | (n == BV_BLK))
    def _():
        o_ref[...] = acc.astype(o_ref.dtype)

    @pl.when((n >= BQ_BLK0) & (n < BK_BLK))
    def _():
        y = _head_rmsnorm(acc, qg_ref[...])
        o_ref[...] = (_rotary(y, cb_ref, sbl_ref, sbh_ref, AXIAL_DIMS // 2) * ATTN_SCALE).astype(o_ref.dtype)

    @pl.when(n == BK_BLK)
    def _():
        y = _head_rmsnorm(acc, kg_ref[...])
        o_ref[...] = _rotary(y, cb_ref, sbl_ref, sbh_ref, AXIAL_DIMS // 2).astype(o_ref.dtype)

    @pl.when(n >= GATE_BLK0)
    def _():
        o_ref[...] = jax.nn.sigmoid(acc + bg_ref[...]).astype(o_ref.dtype)


def _projection(h, w_in, tabs_a, tabs_b, qg, kg, b_gate):
    tm = 1024
    per_b = SEQ // tm
    tab_spec = pl.BlockSpec((tm, LANES), lambda m, n: (m % per_b, 0))
    vec_spec = pl.BlockSpec((1, HEAD_DIM), lambda m, n: (0, 0))
    n_gate_blks = GATE_COLS // COL_BLK
    return pl.pallas_call(
        _proj_kernel,
        out_shape=jax.ShapeDtypeStruct((N_TOK, IN_COLS), BF16),
        grid=(N_TOK // tm, N_COL_BLKS),
        in_specs=[pl.BlockSpec((tm, D_MODEL), lambda m, n: (m, 0)),
                  pl.BlockSpec((D_MODEL, COL_BLK), lambda m, n: (0, n)),
                  tab_spec, tab_spec, tab_spec, tab_spec, tab_spec, tab_spec,
                  vec_spec, vec_spec,
                  pl.BlockSpec((1, COL_BLK),
                               lambda m, n: (0, jnp.clip(n - GATE_BLK0, 0, n_gate_blks - 1)))],
        out_specs=pl.BlockSpec((tm, COL_BLK), lambda m, n: (m, n)),
        compiler_params=_cparams(("parallel", "arbitrary")),
        name="in_projection",
    )(h, w_in, *tabs_a, *tabs_b, qg, kg, b_gate)


def _attn_a_kernel(q_ref, kp_ref, kc_ref, kn_ref, vp_ref, vc_ref, vn_ref, o_ref, lse_ref, *, seq_len, half_w):
    i = pl.program_id(2)
    tq = q_ref.shape[0]
    tk = tq + 2 * half_w
    q = q_ref[...]
    k = jnp.concatenate([kp_ref[...], kc_ref[...], kn_ref[...]], axis=0)
    v = jnp.concatenate([vp_ref[...], vc_ref[...], vn_ref[...]], axis=0)
    q_pos = i * tq + lax.broadcasted_iota(jnp.int32, (tq, tk), 0)
    k_pos = i * tq - half_w + lax.broadcasted_iota(jnp.int32, (tq, tk), 1)
    valid = (jnp.abs(q_pos - k_pos) <= half_w) & (k_pos >= 0) & (k_pos < seq_len)
    lane = lax.broadcasted_iota(jnp.int32, (tq, LANES), 1)
    lse_tile = jnp.zeros((tq, LANES), F32)
    outs = []
    for h in range(A_HEADS):
        sl = slice(h * HEAD_DIM, (h + 1) * HEAD_DIM)
        s = lax.dot_general(q[:, sl], k[:, sl], (((1,), (1,)), ((), ())), preferred_element_type=F32)
        s = jnp.where(valid, s, NEG)
        m = jnp.max(s, axis=-1, keepdims=True)
        p = jnp.exp(s - m)
        l = jnp.sum(p, axis=-1, keepdims=True)
        o = jnp.dot(p.astype(BF16), v[:, sl], preferred_element_type=F32)
        outs.append(o / l)
        lse_tile = jnp.where(lane == h, m + jnp.log(l), lse_tile)
    o_ref[...] = jnp.concatenate(outs, axis=1).astype(o_ref.dtype)
    lse_ref[...] = lse_tile


def _attn_a(proj3, group, window, dilation):
    seq_len = SEQ // dilation
    half_w = (window // 2) // dilation
    tq = 2 * half_w
    nq = seq_len // tq
    n_half = seq_len // half_w
    view = proj3.reshape(BATCH, seq_len, dilation * IN_COLS)
    qb, kb, vb = 3 * group, 3 * group + 1, 3 * group + 2

    def cur(blk):
        return pl.BlockSpec((None, tq, COL_BLK), lambda b, r, i: (b, i, r * N_COL_BLKS + blk))

    def prev(blk):
        return pl.BlockSpec((None, half_w, COL_BLK),
                            lambda b, r, i: (b, jnp.maximum(2 * i - 1, 0), r * N_COL_BLKS + blk))

    def nxt(blk):
        return pl.BlockSpec((None, half_w, COL_BLK),
                            lambda b, r, i: (b, jnp.minimum(2 * i + 2, n_half - 1), r * N_COL_BLKS + blk))

    o, lse = pl.pallas_call(
        functools.partial(_attn_a_kernel, seq_len=seq_len, half_w=half_w),
        out_shape=(jax.ShapeDtypeStruct((BATCH, seq_len, dilation * A_WIDTH), BF16),
                   jax.ShapeDtypeStruct((BATCH, seq_len, dilation * LANES), F32)),
        grid=(BATCH, dilation, nq),
        in_specs=[cur(qb), prev(kb), cur(kb), nxt(kb), prev(vb), cur(vb), nxt(vb)],
        out_specs=(pl.BlockSpec((None, tq, A_WIDTH), lambda b, r, i: (b, i, r)),
                   pl.BlockSpec((None, tq, LANES), lambda b, r, i: (b, i, r))),
        compiler_params=_cparams(("parallel", "parallel", "parallel")),
        name=f"dilated_attention_g{group}",
    )(view, view, view, view, view, view, view)
    return o.reshape(N_TOK, A_WIDTH), lse.reshape(N_TOK, LANES)


def _attn_b_kernel(q_ref, k_ref, v_ref, o_ref, s_ref, *, tk):
    tq = q_ref.shape[0]
    rows = B_GROUP * tq
    n_chunks = SEQ // tk
    q = q_ref[...]
    qs = jnp.concatenate([q[:, g * HEAD_DIM:(g + 1) * HEAD_DIM] for g in range(B_GROUP)], axis=0)

    def scores(c, m):
        start = pl.multiple_of(c * tk, tk)
        s = lax.dot_general(qs, k_ref[pl.ds(start, tk), :], (((1,), (1,)), ((), ())),
                            preferred_element_type=F32)
        s_ref[c] = s
        return jnp.maximum(m, jnp.max(s, axis=-1, keepdims=True))

    m = lax.fori_loop(0, n_chunks, scores, jnp.full((rows, 1), -jnp.inf, F32))

    def accumulate(c, carry):
        l, acc = carry
        start = pl.multiple_of(c * tk, tk)
        p = jnp.exp(s_ref[c] - m)
        l = l + jnp.sum(p, axis=-1, keepdims=True)
        acc = acc + jnp.dot(p.astype(BF16), v_ref[pl.ds(start, tk), :], preferred_element_type=F32)
        return l, acc

    l, acc = lax.fori_loop(0, n_chunks, accumulate,
                           (jnp.zeros((rows, 1), F32), jnp.zeros((rows, HEAD_DIM), F32)))
    o = acc / l
    o_ref[...] = jnp.concatenate([o[g * tq:(g + 1) * tq] for g in range(B_GROUP)], axis=1).astype(o_ref.dtype)


def _attn_b(proj3):
    tq = 128
    tk = 512
    kcol0 = BK_BLK * COL_BLK // HEAD_DIM
    vcol0 = BV_BLK * COL_BLK // HEAD_DIM
    o = pl.pallas_call(
        functools.partial(_attn_b_kernel, tk=tk),
        out_shape=jax.ShapeDtypeStruct((BATCH, SEQ, B_Q_WIDTH), BF16),
        grid=(BATCH, B_KV_HEADS, SEQ // tq),
        in_specs=[pl.BlockSpec((None, tq, COL_BLK), lambda b, h, i: (b, i, BQ_BLK0 + h)),
                  pl.BlockSpec((None, SEQ, HEAD_DIM), lambda b, h, i: (b, 0, kcol0 + h)),
                  pl.BlockSpec((None, SEQ, HEAD_DIM), lambda b, h, i: (b, 0, vcol0 + h))],
        out_specs=pl.BlockSpec((None, tq, COL_BLK), lambda b, h, i: (b, i, h)),
        scratch_shapes=[pltpu.VMEM((SEQ // tk, B_GROUP * tq, tk), F32)],
        compiler_params=_cparams(("parallel", "parallel", "arbitrary")),
        name="gqa_attention",
    )(proj3, proj3, proj3)
    return o.reshape(N_TOK, B_Q_WIDTH)


def _merge_kernel(o0_ref, o1_ref, o2_ref, l0_ref, l1_ref, l2_ref, yb_ref, wa_ref, wb_ref, ga_ref, gb_ref,
                  out_ref, mix_ref):
    @pl.when(pl.program_id(1) == 0)
    def _():
        tm = o0_ref.shape[0]
        lses = (l0_ref[...], l1_ref[...], l2_ref[...])
        outs = (o0_ref[...], o1_ref[...], o2_ref[...])
        cols = []
        for h in range(A_HEADS):
            lh = [jnp.broadcast_to(l[:, h:h + 1], (tm, HEAD_DIM)) for l in lses]
            mx = jnp.maximum(jnp.maximum(lh[0], lh[1]), lh[2])
            e = [jnp.exp(v - mx) for v in lh]
            den = e[0] + e[1] + e[2]
            sl = slice(h * HEAD_DIM, (h + 1) * HEAD_DIM)
            cols.append(sum((e[g] / den) * outs[g][:, sl].astype(F32) for g in range(A_N_GROUPS)))
        mix_ref[...] = jnp.concatenate(cols, axis=1).astype(mix_ref.dtype)

    ya = jnp.dot(mix_ref[...], wa_ref[...], preferred_element_type=F32)
    yb = jnp.dot(yb_ref[...], wb_ref[...], preferred_element_type=F32)
    out_ref[...] = (ga_ref[...].astype(F32) * ya + gb_ref[...].astype(F32) * yb).astype(out_ref.dtype)


def _merge(o_groups, lse_groups, mix_b, proj, w_a_up, w_b_up):
    tm, tn = 512, 512
    n_blk = D_MODEL // tn
    o_spec = pl.BlockSpec((tm, A_WIDTH), lambda m, n: (m, 0))
    l_spec = pl.BlockSpec((tm, LANES), lambda m, n: (m, 0))
    return pl.pallas_call(
        _merge_kernel,
        out_shape=jax.ShapeDtypeStruct((N_TOK, D_MODEL), BF16),
        grid=(N_TOK // tm, n_blk),
        in_specs=[o_spec, o_spec, o_spec, l_spec, l_spec, l_spec,
                  pl.BlockSpec((tm, B_Q_WIDTH), lambda m, n: (m, 0)),
                  pl.BlockSpec((A_WIDTH, tn), lambda m, n: (0, n)),
                  pl.BlockSpec((B_Q_WIDTH, tn), lambda m, n: (0, n)),
                  pl.BlockSpec((tm, tn), lambda m, n: (m, GATE_BLK0 + n)),
                  pl.BlockSpec((tm, tn), lambda m, n: (m, GATE_BLK0 + n_blk + n))],
        out_specs=pl.BlockSpec((tm, tn), lambda m, n: (m, n)),
        scratch_shapes=[pltpu.VMEM((tm, A_WIDTH), BF16)],
        compiler_params=_cparams(("parallel", "arbitrary")),
        name="branch_merge",
    )(*o_groups, *lse_groups, mix_b, w_a_up, w_b_up, proj, proj)


def _outproj_kernel(mg_ref, w_ref, x_ref, gt_ref, g_ref, sc_ref, sh_ref, wr_ref, x1_ref, h2_ref, lg_ref):
    y = jnp.dot(mg_ref[...], w_ref[...], preferred_element_type=F32)
    x1 = x_ref[...] + gt_ref[...] * y
    x1_ref[...] = x1
    h2 = _normmod(x1, g_ref[...], sc_ref[...], sh_ref[...])
    h2_ref[...] = h2.astype(h2_ref.dtype)
    lg_ref[...] = lax.dot_general(wr_ref[...], h2, (((1,), (1,)), ((), ())),
                                  preferred_element_type=F32, precision=lax.Precision.HIGHEST)


def _out_projection(merged, w_out, x2, mod4, g_ffn, w_router_t):
    tm = 256
    per_b = SEQ // tm

    def mod_spec(j):
        return pl.BlockSpec((None, None, 1, D_MODEL), lambda m: (m // per_b, j, 0, 0))

    return pl.pallas_call(
        _outproj_kernel,
        out_shape=(jax.ShapeDtypeStruct((N_TOK, D_MODEL), F32),
                   jax.ShapeDtypeStruct((N_TOK, D_MODEL), BF16),
                   jax.ShapeDtypeStruct((N_EXPERTS, N_TOK), F32)),
        grid=(N_TOK // tm,),
        in_specs=[pl.BlockSpec((tm, D_MODEL), lambda m: (m, 0)),
                  pl.BlockSpec((D_MODEL, D_MODEL), lambda m: (0, 0)),
                  pl.BlockSpec((tm, D_MODEL), lambda m: (m, 0)),
                  mod_spec(2),
                  pl.BlockSpec((1, D_MODEL), lambda m: (0, 0)),
                  mod_spec(4), mod_spec(3),
                  pl.BlockSpec((N_EXPERTS, D_MODEL), lambda m: (0, 0))],
        out_specs=(pl.BlockSpec((tm, D_MODEL), lambda m: (m, 0)),
                   pl.BlockSpec((tm, D_MODEL), lambda m: (m, 0)),
                   pl.BlockSpec((N_EXPERTS, tm), lambda m: (0, m))),
        compiler_params=_cparams(("parallel",)),
        name="out_projection",
    )(merged, w_out, x2, mod4, g_ffn, mod4, mod4, w_router_t)


def _route_kernel(lg_ref, eb_ref, gate_ref):
    tt = lg_ref.shape[1]
    scores = jax.nn.sigmoid(lg_ref[...])
    sel = scores + eb_ref[...]
    neg_inf = -jnp.inf
    sel_g = [sel[PER_GROUP * g:PER_GROUP * (g + 1), :] for g in range(N_EXPERT_GROUPS)]
    grp = []
    for v in sel_g:
        m1 = jnp.max(v, axis=0, keepdims=True)
        is1 = v == m1
        n1 = jnp.sum(jnp.where(is1, 1.0, 0.0), axis=0, keepdims=True)
        rest = jnp.max(jnp.where(is1, neg_inf, v), axis=0, keepdims=True)
        grp.append(m1 + jnp.where(n1 >= 2.0, m1, rest))
    masked = []
    for g in range(N_EXPERT_GROUPS):
        rank = jnp.zeros((1, tt), F32)
        for g2 in range(N_EXPERT_GROUPS):
            if g2 != g:
                beats = (grp[g2] >= grp[g]) if g2 < g else (grp[g2] > grp[g])
                rank = rank + jnp.where(beats, 1.0, 0.0)
        keep = jnp.broadcast_to(rank < TOPK_GROUPS, (PER_GROUP, tt))
        masked.append(jnp.where(keep, sel_g[g], neg_inf))
    sub = lax.broadcasted_iota(jnp.int32, (PER_GROUP, tt), 0)
    ranks = [jnp.zeros((PER_GROUP, tt), F32) for _ in range(N_EXPERT_GROUPS)]
    for g2 in range(N_EXPERT_GROUPS):
        for m2 in range(PER_GROUP):
            vf = jnp.broadcast_to(masked[g2][m2:m2 + 1, :], (PER_GROUP, tt))
            for g in range(N_EXPERT_GROUPS):
                if g2 < g:
                    beats = vf >= masked[g]
                elif g2 > g:
                    beats = vf > masked[g]
                else:
                    beats = (vf > masked[g]) | ((vf == masked[g]) & (sub > m2))
                ranks[g] = ranks[g] + jnp.where(beats, 1.0, 0.0)
    chosen = [jnp.where(ranks[g] < TOP_K, scores[PER_GROUP * g:PER_GROUP * (g + 1), :], 0.0)
              for g in range(N_EXPERT_GROUPS)]
    total = chosen[0]
    for g in range(1, N_EXPERT_GROUPS):
        total = total + chosen[g]
    denom = jnp.sum(total, axis=0, keepdims=True)
    for g in range(N_EXPERT_GROUPS):
        gate_ref[PER_GROUP * g:PER_GROUP * (g + 1), :] = chosen[g] / denom * ROUTED_SCALE


def _route(logits_t, e_bias_col):
    tt = 1024
    return pl.pallas_call(
        _route_kernel,
        out_shape=jax.ShapeDtypeStruct((N_EXPERTS, N_TOK), F32),
        grid=(N_TOK // tt,),
        in_specs=[pl.BlockSpec((N_EXPERTS, tt), lambda t: (0, t)),
                  pl.BlockSpec((N_EXPERTS, 1), lambda t: (0, 0))],
        out_specs=pl.BlockSpec((N_EXPERTS, tt), lambda t: (0, t)),
        compiler_params=_cparams(("parallel",)),
        name="routing",
    )(logits_t, e_bias_col)


def _experts_kernel(x_ref, g_ref, w1_ref, w3_ref, w2_ref, o_ref, acc_ref, *, gated):
    e = pl.program_id(1)

    @pl.when(e == 0)
    def _():
        acc_ref[...] = jnp.zeros_like(acc_ref)

    x = x_ref[...]
    a = jnp.dot(x, w1_ref[...], preferred_element_type=F32)
    u = jnp.dot(x, w3_ref[...], preferred_element_type=F32)
    hid = (a * jax.nn.sigmoid(a)) * u
    if gated:
        lane = lax.broadcasted_iota(jnp.int32, g_ref.shape, 1)
        hid = hid * jnp.sum(jnp.where(lane == e, g_ref[...], 0.0), axis=1, keepdims=True)
    acc_ref[...] += jnp.dot(hid.astype(BF16), w2_ref[...], preferred_element_type=F32)

    @pl.when(e == pl.num_programs(1) - 1)
    def _():
        o_ref[...] = acc_ref[...].astype(o_ref.dtype)


def _experts(h2, gate, w1, w3, w2, gated):
    tm = 1024
    n_e = w1.shape[0]
    d_hid = w1.shape[2]
    return pl.pallas_call(
        functools.partial(_experts_kernel, gated=gated),
        out_shape=jax.ShapeDtypeStruct((N_TOK, D_MODEL), F32),
        grid=(N_TOK // tm, n_e),
        in_specs=[pl.BlockSpec((tm, D_MODEL), lambda m, e: (m, 0)),
                  pl.BlockSpec((tm, gate.shape[1]), lambda m, e: (m, 0)),
                  pl.BlockSpec((None, D_MODEL, d_hid), lambda m, e: (e, 0, 0)),
                  pl.BlockSpec((None, D_MODEL, d_hid), lambda m, e: (e, 0, 0)),
                  pl.BlockSpec((None, d_hid, D_MODEL), lambda m, e: (e, 0, 0))],
        out_specs=pl.BlockSpec((tm, D_MODEL), lambda m, e: (m, 0)),
        scratch_shapes=[pltpu.VMEM((tm, D_MODEL), F32)],
        compiler_params=_cparams(("parallel", "arbitrary")),
        name="experts_gated" if gated else "experts_shared",
    )(h2, gate, w1, w3, w2)


def _final_kernel(x1_ref, r_ref, s_ref, gt_ref, g_ref, o_ref):
    x = x1_ref[...] + gt_ref[...] * (r_ref[...] + s_ref[...])
    o_ref[...] = x * lax.rsqrt(jnp.mean(x * x, axis=-1, keepdims=True) + EPS) * g_ref[...]


def _final(x1, routed, shared, mod4, g_final):
    tm = 512
    per_b = SEQ // tm
    row = pl.BlockSpec((tm, D_MODEL), lambda m: (m, 0))
    return pl.pallas_call(
        _final_kernel,
        out_shape=jax.ShapeDtypeStruct((N_TOK, D_MODEL), F32),
        grid=(N_TOK // tm,),
        in_specs=[row, row, row,
                  pl.BlockSpec((None, None, 1, D_MODEL), lambda m: (m // per_b, 5, 0, 0)),
                  pl.BlockSpec((1, D_MODEL), lambda m: (0, 0))],
        out_specs=row,
        compiler_params=_cparams(("parallel",)),
        name="final_norm",
    )(x1, routed, shared, mod4, g_final)


def _rope_angles(pos, dims, theta):
    inv = jnp.power(jnp.float32(theta), -jnp.arange(0, dims, 2, dtype=jnp.float32) / dims)
    return pos.astype(jnp.float32)[:, None] * inv[None, :]


def _rotary_tables(angle_blocks):
    cos_parts, lo_parts, hi_parts = [], [], []
    used = 0
    for ang in angle_blocks:
        c, s = jnp.cos(ang), jnp.sin(ang)
        z = jnp.zeros_like(s)
        cos_parts += [c, c]
        lo_parts += [-s, z]
        hi_parts += [z, s]
        used += 2 * ang.shape[1]
    rest = HEAD_DIM - used
    if rest:
        cos_parts.append(jnp.ones((SEQ, rest), F32))
        lo_parts.append(jnp.zeros((SEQ, rest), F32))
        hi_parts.append(jnp.zeros((SEQ, rest), F32))
    return (jnp.concatenate(cos_parts, axis=1), jnp.concatenate(lo_parts, axis=1),
            jnp.concatenate(hi_parts, axis=1))


def kernel(x, c, w_ada, b_ada, g_attn, w_in, b_gate, q_norm_g, k_norm_g, w_a_up, w_b_up, w_out,
           g_ffn, w_router, e_bias, w1, w3, w2, ws1, ws3, ws2, g_final):
    l = 0
    x2 = x.reshape(N_TOK, D_MODEL)
    pos = jnp.arange(SEQ)
    tabs_a = _rotary_tables([_rope_angles(pos, ROPE_DIMS, ROPE_THETA)])
    tabs_b = _rotary_tables([_rope_angles(pos // GRID_W, AXIAL_DIMS, AXIAL_THETA),
                             _rope_angles(pos % GRID_W, AXIAL_DIMS, AXIAL_THETA)])

    c_pad = jnp.zeros((SUBLANES, D_MODEL), F32).at[:BATCH].set(c)
    mod = _ada(c_pad, w_ada[l], b_ada[l].reshape(1, -1))
    mod4 = mod[:BATCH].reshape(BATCH, N_MOD, 1, D_MODEL)

    h = _norm_modulate(x2, g_attn[l].reshape(1, -1), mod4, 1, 0)
    proj = _projection(h, w_in[l].astype(BF16), tabs_a, tabs_b,
                       q_norm_g[l].reshape(1, -1), k_norm_g[l].reshape(1, -1), b_gate[l].reshape(1, -1))
    proj3 = proj.reshape(BATCH, SEQ, IN_COLS)

    o_groups, lse_groups = [], []
    for gi, (window, dilation) in enumerate(A_PATTERNS):
        o, lse = _attn_a(proj3, gi, window, dilation)
        o_groups.append(o)
        lse_groups.append(lse)
    mix_b = _attn_b(proj3)

    merged = _merge(o_groups, lse_groups, mix_b, proj, w_a_up[l].astype(BF16), w_b_up[l].astype(BF16))
    x1, h2, logits_t = _out_projection(merged, w_out[l].astype(BF16), x2, mod4, g_ffn[l].reshape(1, -1),
                                       w_router[l].T)
    gate_t = _route(logits_t, e_bias[l].reshape(-1, 1))
    routed = _experts(h2, gate_t.T, w1[l].astype(BF16), w3[l].astype(BF16), w2[l].astype(BF16), True)
    shared = _experts(h2, jnp.ones((N_TOK, LANES), F32), ws1[l].astype(BF16)[None], ws3[l].astype(BF16)[None],
                      ws2[l].astype(BF16)[None], False)
    out = _final(x1, routed, shared, mod4, g_final.reshape(1, -1))
    return out.reshape(BATCH, SEQ, D_MODEL)
```

```python
import functools

import jax
import jax.numpy as jnp
from jax import lax
from jax.experimental import pallas as pl
from jax.experimental.pallas import tpu as pltpu

F32 = jnp.float32
BF16 = jnp.bfloat16

D_MODEL = 2048
BATCH = 2
SEQ = 4096
N_TOK = BATCH * SEQ
HEAD_DIM = 128
EPS = 1e-6
A_PATTERNS = ((128, 1), (512, 4), (2048, 16))
A_N_GROUPS = len(A_PATTERNS)
A_HEADS = 4
A_WIDTH = A_HEADS * HEAD_DIM
ROPE_THETA = 500000.0
ROPE_DIMS = HEAD_DIM // 4
B_Q_HEADS = 16
B_KV_HEADS = 4
B_GROUP = B_Q_HEADS // B_KV_HEADS
B_Q_WIDTH = B_Q_HEADS * HEAD_DIM
B_KV_WIDTH = B_KV_HEADS * HEAD_DIM
AXIAL_THETA = 10000.0
AXIAL_DIMS = HEAD_DIM // 2
GRID_W = 64
A_COLS = A_N_GROUPS * 3 * A_WIDTH
B_COLS = B_Q_WIDTH + 2 * B_KV_WIDTH
GATE_COLS = 2 * D_MODEL
IN_COLS = A_COLS + B_COLS + GATE_COLS
N_MOD = 6
N_EXPERTS = 64
N_EXPERT_GROUPS = 8
PER_GROUP = N_EXPERTS // N_EXPERT_GROUPS
TOPK_GROUPS = 4
TOP_K = 8
D_EXPERT = D_MODEL // 4
D_SHARED = D_MODEL // 4
ROUTED_SCALE = 2.5
NEG = -1e30
ATTN_SCALE = HEAD_DIM ** -0.5

LANES = 128
SUBLANES = 8
VMEM_LIMIT = 56 * 1024 * 1024

COL_BLK = 512
N_COL_BLKS = IN_COLS // COL_BLK
A_BLKS = A_COLS // COL_BLK
BQ_BLK0 = A_BLKS
BK_BLK = BQ_BLK0 + B_Q_WIDTH // COL_BLK
BV_BLK = BK_BLK + 1
GATE_BLK0 = BV_BLK + 1

CHUNK = 512
N_CHUNKS = N_TOK // CHUNK
GRANULE = 16
GRANULE_SHIFT = 4
TILE = 256
TILE_SHIFT = 8
GRAN_PER_TILE = TILE // GRANULE
GPT_SHIFT = TILE_SHIFT - GRANULE_SHIFT
N_SLOTS = 4
XS_WIDTH = D_MODEL + LANES
CHUNK_ROWS = -(-(CHUNK * TOP_K + N_EXPERTS * (GRANULE - 1)) // TILE) * TILE
MAX_ROWS = -(-(N_TOK * TOP_K + N_EXPERTS * N_CHUNKS * (GRANULE - 1) + N_EXPERTS * (TILE - 1)) // TILE) * TILE
MAX_TILES = MAX_ROWS // TILE


def _cparams(semantics):
    return pltpu.CompilerParams(dimension_semantics=semantics, vmem_limit_bytes=VMEM_LIMIT)


def _ada_kernel(c_ref, w_ref, b_ref, o_ref):
    c = c_ref[...]
    act = c * jax.nn.sigmoid(c)
    o_ref[...] = jnp.dot(act, w_ref[...], preferred_element_type=F32,
                         precision=lax.Precision.HIGHEST) + b_ref[...]


def _ada(c_pad, w_ada, b_ada):
    tn = 1024
    n_out = w_ada.shape[1]
    return pl.pallas_call(
        _ada_kernel,
        out_shape=jax.ShapeDtypeStruct((SUBLANES, n_out), F32),
        grid=(n_out // tn,),
        in_specs=[pl.BlockSpec((SUBLANES, D_MODEL), lambda n: (0, 0)),
                  pl.BlockSpec((D_MODEL, tn), lambda n: (0, n)),
                  pl.BlockSpec((1, tn), lambda n: (0, n))],
        out_specs=pl.BlockSpec((SUBLANES, tn), lambda n: (0, n)),
        compiler_params=_cparams(("parallel",)),
        name="ada_mod",
    )(c_pad, w_ada, b_ada)


def _normmod(x, g, sc, sh):
    y = x * lax.rsqrt(jnp.mean(x * x, axis=-1, keepdims=True) + EPS)
    return (y * g) * (1.0 + sc) + sh


def _normmod_kernel(x_ref, g_ref, sc_ref, sh_ref, o_ref):
    o_ref[...] = _normmod(x_ref[...], g_ref[...], sc_ref[...], sh_ref[...]).astype(o_ref.dtype)


def _norm_modulate(x2, g, mod4, sc_idx, sh_idx):
    tm = 512
    per_b = SEQ // tm
    return pl.pallas_call(
        _normmod_kernel,
        out_shape=jax.ShapeDtypeStruct((N_TOK, D_MODEL), BF16),
        grid=(N_TOK // tm,),
        in_specs=[pl.BlockSpec((tm, D_MODEL), lambda m: (m, 0)),
                  pl.BlockSpec((1, D_MODEL), lambda m: (0, 0)),
                  pl.BlockSpec((None, None, 1, D_MODEL), lambda m: (m // per_b, sc_idx, 0, 0)),
                  pl.BlockSpec((None, None, 1, D_MODEL), lambda m: (m // per_b, sh_idx, 0, 0))],
        out_specs=pl.BlockSpec((tm, D_MODEL), lambda m: (m, 0)),
        compiler_params=_cparams(("parallel",)),
        name="norm_modulate",
    )(x2, g, mod4, mod4)


def _tile4(t):
    return jnp.concatenate([t, t, t, t], axis=1)


def _rotary(y, cos_ref, sin_lo_ref, sin_hi_ref, shift):
    width = y.shape[1]
    return (y * _tile4(cos_ref[...])
            + pltpu.roll(y, width - shift, 1) * _tile4(sin_lo_ref[...])
            + pltpu.roll(y, shift, 1) * _tile4(sin_hi_ref[...]))


def _head_rmsnorm(y, g):
    outs = []
    for h in range(y.shape[1] // HEAD_DIM):
        yh = y[:, h * HEAD_DIM:(h + 1) * HEAD_DIM]
        outs.append(yh * lax.rsqrt(jnp.mean(yh * yh, axis=-1, keepdims=True) + EPS) * g)
    return jnp.concatenate(outs, axis=1)


def _proj_kernel(h_ref, w_ref, ca_ref, sal_ref, sah_ref, cb_ref, sbl_ref, sbh_ref,
                 qg_ref, kg_ref, bg_ref, o_ref):
    n = pl.program_id(1)
    acc = jnp.dot(h_ref[...], w_ref[...], preferred_element_type=F32)
    is_a = n < A_BLKS
    part = n % 3

    @pl.when(is_a & (part == 0))
    def _():
        o_ref[...] = (_rotary(acc, ca_ref, sal_ref, sah_ref, ROPE_DIMS // 2) * ATTN_SCALE).astype(o_ref.dtype)

    @pl.when(is_a & (part == 1))
    def _():
        o_ref[...] = _rotary(acc, ca_ref, sal_ref, sah_ref, ROPE_DIMS // 2).astype(o_ref.dtype)

    @pl.when((is_a & (part == 2)) | (n == BV_BLK))
    def _():
        o_ref[...] = acc.astype(o_ref.dtype)

    @pl.when((n >= BQ_BLK0) & (n < BK_BLK))
    def _():
        y = _head_rmsnorm(acc, qg_ref[...])
        o_ref[...] = (_rotary(y, cb_ref, sbl_ref, sbh_ref, AXIAL_DIMS // 2) * ATTN_SCALE).astype(o_ref.dtype)

    @pl.when(n == BK_BLK)
    def _():
        y = _head_rmsnorm(acc, kg_ref[...])
        o_ref[...] = _rotary(y, cb_ref, sbl_ref, sbh_ref, AXIAL_DIMS // 2).astype(o_ref.dtype)

    @pl.when(n >= GATE_BLK0)
    def _():
        o_ref[...] = jax.nn.sigmoid(acc + bg_ref[...]).astype(o_ref.dtype)


def _projection(h, w_in, tabs_a, tabs_b, qg, kg, b_gate):
    tm = 1024
    per_b = SEQ // tm
    tab_spec = pl.BlockSpec((tm, LANES), lambda m, n: (m % per_b, 0))
    vec_spec = pl.BlockSpec((1, HEAD_DIM), lambda m, n: (0, 0))
    n_gate_blks = GATE_COLS // COL_BLK
    return pl.pallas_call(
        _proj_kernel,
        out_shape=jax.ShapeDtypeStruct((N_TOK, IN_COLS), BF16),
        grid=(N_TOK // tm, N_COL_BLKS),
        in_specs=[pl.BlockSpec((tm, D_MODEL), lambda m, n: (m, 0)),
                  pl.BlockSpec((D_MODEL, COL_BLK), lambda m, n: (0, n)),
                  tab_spec, tab_spec, tab_spec, tab_spec, tab_spec, tab_spec,
                  vec_spec, vec_spec,
                  pl.BlockSpec((1, COL_BLK),
                               lambda m, n: (0, jnp.clip(n - GATE_BLK0, 0, n_gate_blks - 1)))],
        out_specs=pl.BlockSpec((tm, COL_BLK), lambda m, n: (m, n)),
        compiler_params=_cparams(("parallel", "arbitrary")),
        name="in_projection",
    )(h, w_in, *tabs_a, *tabs_b, qg, kg, b_gate)


def _attn_a_kernel(q_ref, kp_ref, kc_ref, kn_ref, vp_ref, vc_ref, vn_ref, o_ref, lse_ref, *, seq_len, half_w):
    i = pl.program_id(2)
    tq = q_ref.shape[0]
    tk = tq + 2 * half_w
    q = q_ref[...]
    k = jnp.concatenate([kp_ref[...], kc_ref[...], kn_ref[...]], axis=0)
    v = jnp.concatenate([vp_ref[...], vc_ref[...], vn_ref[...]], axis=0)
    q_pos = i * tq + lax.broadcasted_iota(jnp.int32, (tq, tk), 0)
    k_pos = i * tq - half_w + lax.broadcasted_iota(jnp.int32, (tq, tk), 1)
    valid = (jnp.abs(q_pos - k_pos) <= half_w) & (k_pos >= 0) & (k_pos < seq_len)
    lane = lax.broadcasted_iota(jnp.int32, (tq, LANES), 1)
    lse_tile = jnp.zeros((tq, LANES), F32)
    outs = []
    for h in range(A_HEADS):
        sl = slice(h * HEAD_DIM, (h + 1) * HEAD_DIM)
        s = lax.dot_general(q[:, sl], k[:, sl], (((1,), (1,)), ((), ())), preferred_element_type=F32)
        s = jnp.where(valid, s, NEG)
        m = jnp.max(s, axis=-1, keepdims=True)
        p = jnp.exp(s - m)
        l = jnp.sum(p, axis=-1, keepdims=True)
        o = jnp.dot(p.astype(BF16), v[:, sl], preferred_element_type=F32)
        outs.append(o / l)
        lse_tile = jnp.where(lane == h, m + jnp.log(l), lse_tile)
    o_ref[...] = jnp.concatenate(outs, axis=1).astype(o_ref.dtype)
    lse_ref[...] = lse_tile


def _attn_a(proj3, group, window, dilation):
    seq_len = SEQ // dilation
    half_w = (window // 2) // dilation
    tq = 2 * half_w
    nq = seq_len // tq
    n_half = seq_len // half_w
    view = proj3.reshape(BATCH, seq_len, dilation * IN_COLS)
    qb, kb, vb = 3 * group, 3 * group + 1, 3 * group + 2

    def cur(blk):
        return pl.BlockSpec((None, tq, COL_BLK), lambda b, r, i: (b, i, r * N_COL_BLKS + blk))

    def prev(blk):
        return pl.BlockSpec((None, half_w, COL_BLK),
                            lambda b, r, i: (b, jnp.maximum(2 * i - 1, 0), r * N_COL_BLKS + blk))

    def nxt(blk):
        return pl.BlockSpec((None, half_w, COL_BLK),
                            lambda b, r, i: (b, jnp.minimum(2 * i + 2, n_half - 1), r * N_COL_BLKS + blk))

    o, lse = pl.pallas_call(
        functools.partial(_attn_a_kernel, seq_len=seq_len, half_w=half_w),
        out_shape=(jax.ShapeDtypeStruct((BATCH, seq_len, dilation * A_WIDTH), BF16),
                   jax.ShapeDtypeStruct((BATCH, seq_len, dilation * LANES), F32)),
        grid=(BATCH, dilation, nq),
        in_specs=[cur(qb), prev(kb), cur(kb), nxt(kb), prev(vb), cur(vb), nxt(vb)],
        out_specs=(pl.BlockSpec((None, tq, A_WIDTH), lambda b, r, i: (b, i, r)),
                   pl.BlockSpec((None, tq, LANES), lambda b, r, i: (b, i, r))),
        compiler_params=_cparams(("parallel", "parallel", "parallel")),
        name=f"dilated_attention_g{group}",
    )(view, view, view, view, view, view, view)
    return o.reshape(N_TOK, A_WIDTH), lse.reshape(N_TOK, LANES)


def _attn_b_kernel(q_ref, k_ref, v_ref, o_ref, s_ref, *, tk):
    tq = q_ref.shape[0]
    rows = B_GROUP * tq
    n_chunks = SEQ // tk
    q = q_ref[...]
    qs = jnp.concatenate([q[:, g * HEAD_DIM:(g + 1) * HEAD_DIM] for g in range(B_GROUP)], axis=0)

    def scores(c, m):
        start = pl.multiple_of(c * tk, tk)
        s = lax.dot_general(qs, k_ref[pl.ds(start, tk), :], (((1,), (1,)), ((), ())),
                            preferred_element_type=F32)
        s_ref[c] = s
        return jnp.maximum(m, jnp.max(s, axis=-1, keepdims=True))

    m = lax.fori_loop(0, n_chunks, scores, jnp.full((rows, 1), -jnp.inf, F32))

    def accumulate(c, carry):
        l, acc = carry
        start = pl.multiple_of(c * tk, tk)
        p = jnp.exp(s_ref[c] - m)
        l = l + jnp.sum(p, axis=-1, keepdims=True)
        acc = acc + jnp.dot(p.astype(BF16), v_ref[pl.ds(start, tk), :], preferred_element_type=F32)
        return l, acc

    l, acc = lax.fori_loop(0, n_chunks, accumulate,
                           (jnp.zeros((rows, 1), F32), jnp.zeros((rows, HEAD_DIM), F32)))
    o = acc / l
    o_ref[...] = jnp.concatenate([o[g * tq:(g + 1) * tq] for g in range(B_GROUP)], axis=1).astype(o_ref.dtype)


def _attn_b(proj3):
    tq = 128
    tk = 512
    kcol0 = BK_BLK * COL_BLK // HEAD_DIM
    vcol0 = BV_BLK * COL_BLK // HEAD_DIM
    o = pl.pallas_call(
        functools.partial(_attn_b_kernel, tk=tk),
        out_shape=jax.ShapeDtypeStruct((BATCH, SEQ, B_Q_WIDTH), BF16),
        grid=(BATCH, B_KV_HEADS, SEQ // tq),
        in_specs=[pl.BlockSpec((None, tq, COL_BLK), lambda b, h, i: (b, i, BQ_BLK0 + h)),
                  pl.BlockSpec((None, SEQ, HEAD_DIM), lambda b, h, i: (b, 0, kcol0 + h)),
                  pl.BlockSpec((None, SEQ, HEAD_DIM), lambda b, h, i: (b, 0, vcol0 + h))],
        out_specs=pl.BlockSpec((None, tq, COL_BLK), lambda b, h, i: (b, i, h)),
        scratch_shapes=[pltpu.VMEM((SEQ // tk, B_GROUP * tq, tk), F32)],
        compiler_params=_cparams(("parallel", "parallel", "arbitrary")),
        name="gqa_attention",
    )(proj3, proj3, proj3)
    return o.reshape(N_TOK, B_Q_WIDTH)


def _merge_kernel(o0_ref, o1_ref, o2_ref, l0_ref, l1_ref, l2_ref, yb_ref, wa_ref, wb_ref, ga_ref, gb_ref,
                  out_ref, mix_ref):
    @pl.when(pl.program_id(1) == 0)
    def _():
        tm = o0_ref.shape[0]
        lses = (l0_ref[...], l1_ref[...], l2_ref[...])
        outs = (o0_ref[...], o1_ref[...], o2_ref[...])
        cols = []
        for h in range(A_HEADS):
            lh = [jnp.broadcast_to(l[:, h:h + 1], (tm, HEAD_DIM)) for l in lses]
            mx = jnp.maximum(jnp.maximum(lh[0], lh[1]), lh[2])
            e = [jnp.exp(v - mx) for v in lh]
            den = e[0] + e[1] + e[2]
            sl = slice(h * HEAD_DIM, (h + 1) * HEAD_DIM)
            cols.append(sum((e[g] / den) * outs[g][:, sl].astype(F32) for g in range(A_N_GROUPS)))
        mix_ref[...] = jnp.concatenate(cols, axis=1).astype(mix_ref.dtype)

    ya = jnp.dot(mix_ref[...], wa_ref[...], preferred_element_type=F32)
    yb = jnp.dot(yb_ref[...], wb_ref[...], preferred_element_type=F32)
    out_ref[...] = (ga_ref[...].astype(F32) * ya + gb_ref[...].astype(F32) * yb).astype(out_ref.dtype)


def _merge(o_groups, lse_groups, mix_b, proj, w_a_up, w_b_up):
    tm, tn = 512, 512
    n_blk = D_MODEL // tn
    o_spec = pl.BlockSpec((tm, A_WIDTH), lambda m, n: (m, 0))
    l_spec = pl.BlockSpec((tm, LANES), lambda m, n: (m, 0))
    return pl.pallas_call(
        _merge_kernel,
        out_shape=jax.ShapeDtypeStruct((N_TOK, D_MODEL), BF16),
        grid=(N_TOK // tm, n_blk),
        in_specs=[o_spec, o_spec, o_spec, l_spec, l_spec, l_spec,
                  pl.BlockSpec((tm, B_Q_WIDTH), lambda m, n: (m, 0)),
                  pl.BlockSpec((A_WIDTH, tn), lambda m, n: (0, n)),
                  pl.BlockSpec((B_Q_WIDTH, tn), lambda m, n: (0, n)),
                  pl.BlockSpec((tm, tn), lambda m, n: (m, GATE_BLK0 + n)),
                  pl.BlockSpec((tm, tn), lambda m, n: (m, GATE_BLK0 + n_blk + n))],
        out_specs=pl.BlockSpec((tm, tn), lambda m, n: (m, n)),
        scratch_shapes=[pltpu.VMEM((tm, A_WIDTH), BF16)],
        compiler_params=_cparams(("parallel", "arbitrary")),
        name="branch_merge",
    )(*o_groups, *lse_groups, mix_b, w_a_up, w_b_up, proj, proj)


def _outproj_kernel(mg_ref, w_ref, x_ref, gt_ref, g_ref, sc_ref, sh_ref, wr_ref, x1_ref, h2_ref, lg_ref):
    y = jnp.dot(mg_ref[...], w_ref[...], preferred_element_type=F32)
    x1 = x_ref[...] + gt_ref[...] * y
    x1_ref[...] = x1
    h2 = _normmod(x1, g_ref[...], sc_ref[...], sh_ref[...])
    h2_ref[...] = h2.astype(h2_ref.dtype)
    lg_ref[...] = lax.dot_general(wr_ref[...], h2, (((1,), (1,)), ((), ())),
                                  preferred_element_type=F32, precision=lax.Precision.HIGHEST)


def _out_projection(merged, w_out, x2, mod4, g_ffn, w_router_t):
    tm = 256
    per_b = SEQ // tm

    def mod_spec(j):
        return pl.BlockSpec((None, None, 1, D_MODEL), lambda m: (m // per_b, j, 0, 0))

    return pl.pallas_call(
        _outproj_kernel,
        out_shape=(jax.ShapeDtypeStruct((N_TOK, D_MODEL), F32),
                   jax.ShapeDtypeStruct((N_TOK, D_MODEL), BF16),
                   jax.ShapeDtypeStruct((N_EXPERTS, N_TOK), F32)),
        grid=(N_TOK // tm,),
        in_specs=[pl.BlockSpec((tm, D_MODEL), lambda m: (m, 0)),
                  pl.BlockSpec((D_MODEL, D_MODEL), lambda m: (0, 0)),
                  pl.BlockSpec((tm, D_MODEL), lambda m: (m, 0)),
                  mod_spec(2),
                  pl.BlockSpec((1, D_MODEL), lambda m: (0, 0)),
                  mod_spec(4), mod_spec(3),
                  pl.BlockSpec((N_EXPERTS, D_MODEL), lambda m: (0, 0))],
        out_specs=(pl.BlockSpec((tm, D_MODEL), lambda m: (m, 0)),
                   pl.BlockSpec((tm, D_MODEL), lambda m: (m, 0)),
                   pl.BlockSpec((N_EXPERTS, tm), lambda m: (0, m))),
        compiler_params=_cparams(("parallel",)),
        name="out_projection",
    )(merged, w_out, x2, mod4, g_ffn, mod4, mod4, w_router_t)


def _route_kernel(lg_ref, eb_ref, csm_ref, cnt_ref, lp_ref, gx_ref):
    tt = lg_ref.shape[1]
    scores = jax.nn.sigmoid(lg_ref[...])
    sel = scores + eb_ref[...]
    neg_inf = -jnp.inf
    sel_g = [sel[PER_GROUP * g:PER_GROUP * (g + 1), :] for g in range(N_EXPERT_GROUPS)]
    grp = []
    for v in sel_g:
        m1 = jnp.max(v, axis=0, keepdims=True)
        is1 = v == m1
        n1 = jnp.sum(jnp.where(is1, 1.0, 0.0), axis=0, keepdims=True)
        rest = jnp.max(jnp.where(is1, neg_inf, v), axis=0, keepdims=True)
        grp.append(m1 + jnp.where(n1 >= 2.0, m1, rest))
    masked = []
    for g in range(N_EXPERT_GROUPS):
        rank = jnp.zeros((1, tt), F32)
        for g2 in range(N_EXPERT_GROUPS):
            if g2 != g:
                beats = (grp[g2] >= grp[g]) if g2 < g else (grp[g2] > grp[g])
                rank = rank + jnp.where(beats, 1.0, 0.0)
        keep = jnp.broadcast_to(rank < TOPK_GROUPS, (PER_GROUP, tt))
        masked.append(jnp.where(keep, sel_g[g], neg_inf))
    sub = lax.broadcasted_iota(jnp.int32, (PER_GROUP, tt), 0)
    ranks = [jnp.zeros((PER_GROUP, tt), F32) for _ in range(N_EXPERT_GROUPS)]
    for g2 in range(N_EXPERT_GROUPS):
        for m2 in range(PER_GROUP):
            vf = jnp.broadcast_to(masked[g2][m2:m2 + 1, :], (PER_GROUP, tt))
            for g in range(N_EXPERT_GROUPS):
                if g2 < g:
                    beats = vf >= masked[g]
                elif g2 > g:
                    beats = vf > masked[g]
                else:
                    beats = (vf > masked[g]) | ((vf == masked[g]) & (sub > m2))
                ranks[g] = ranks[g] + jnp.where(beats, 1.0, 0.0)
    picked = [ranks[g] < TOP_K for g in range(N_EXPERT_GROUPS)]
    chosen = [jnp.where(picked[g], scores[PER_GROUP * g:PER_GROUP * (g + 1), :], 0.0)
              for g in range(N_EXPERT_GROUPS)]
    total = chosen[0]
    for g in range(1, N_EXPERT_GROUPS):
        total = total + chosen[g]
    denom = jnp.sum(total, axis=0, keepdims=True)
    gate = jnp.concatenate([chosen[g] / denom * ROUTED_SCALE for g in range(N_EXPERT_GROUPS)], axis=0)
    mask = jnp.concatenate([jnp.where(picked[g], 1.0, 0.0) for g in range(N_EXPERT_GROUPS)], axis=0)
    mask_bf = mask.astype(BF16)

    upper = jnp.where(lax.broadcasted_iota(jnp.int32, (tt, tt), 0) <= lax.broadcasted_iota(jnp.int32, (tt, tt), 1),
                      1.0, 0.0).astype(BF16)
    cs = jnp.dot(mask_bf, upper, preferred_element_type=F32)
    csm_ref[...] = jnp.where(mask > 0.0, cs, 0.0)
    cnt_ref[...] = lax.dot_general(jnp.ones((SUBLANES, tt), BF16), mask_bf, (((1,), (1,)), ((), ())),
                                   preferred_element_type=F32)

    padded = jnp.ceil(cs[:, tt - 1:tt] / GRANULE) * GRANULE
    lower = jnp.where(lax.broadcasted_iota(jnp.int32, (N_EXPERTS, N_EXPERTS), 1)
                      < lax.broadcasted_iota(jnp.int32, (N_EXPERTS, N_EXPERTS), 0), 1.0, 0.0)
    first_row = jnp.dot(lower, jnp.broadcast_to(padded, (N_EXPERTS, LANES)), preferred_element_type=F32,
                        precision=lax.Precision.HIGHEST)[:, 0:1]
    row_of = first_row + cs - 1.0

    ordinal = jnp.dot(lower.astype(BF16), mask_bf, preferred_element_type=F32)
    sub = lax.broadcasted_iota(jnp.int32, (SUBLANES, tt), 0)
    rows8 = jnp.zeros((SUBLANES, tt), F32)
    for k in range(TOP_K):
        hit = (ordinal == k) & (mask > 0.0)
        row_k = jnp.sum(jnp.where(hit, row_of, 0.0), axis=0, keepdims=True)
        rows8 = jnp.where(sub == k, jnp.broadcast_to(row_k, (SUBLANES, tt)), rows8)
    lp_ref[...] = jnp.concatenate([rows8, jnp.zeros((LANES - SUBLANES, tt), F32)], axis=0).T

    hi = gate.astype(BF16).astype(F32)
    gx_ref[...] = jnp.concatenate([hi, gate - hi], axis=0).T.astype(BF16)


def _route(logits_t, e_bias_col):
    return pl.pallas_call(
        _route_kernel,
        out_shape=(jax.ShapeDtypeStruct((N_EXPERTS, N_TOK), F32),
                   jax.ShapeDtypeStruct((N_CHUNKS, SUBLANES, N_EXPERTS), F32),
                   jax.ShapeDtypeStruct((N_TOK, LANES), F32),
                   jax.ShapeDtypeStruct((N_TOK, LANES), BF16)),
        grid=(N_CHUNKS,),
        in_specs=[pl.BlockSpec((N_EXPERTS, CHUNK), lambda c: (0, c)),
                  pl.BlockSpec((N_EXPERTS, 1), lambda c: (0, 0))],
        out_specs=(pl.BlockSpec((N_EXPERTS, CHUNK), lambda c: (0, c)),
                   pl.BlockSpec((None, SUBLANES, N_EXPERTS), lambda c: (c, 0, 0)),
                   pl.BlockSpec((CHUNK, LANES), lambda c: (c, 0)),
                   pl.BlockSpec((CHUNK, LANES), lambda c: (c, 0))),
        compiler_params=_cparams(("parallel",)),
        name="routing",
    )(logits_t, e_bias_col)


def _plan_kernel(cnt_ref, goff_ref, tail_ref, tile_e_ref, ntile_ref):
    def per_expert(e, carry):
        base, tbase = carry

        def per_chunk(c, off):
            goff_ref[c, e] = base + off
            return off + (((cnt_ref[c, e] + (GRANULE - 1)) >> GRANULE_SHIFT) << GRANULE_SHIFT)

        tot = lax.fori_loop(0, N_CHUNKS, per_chunk, jnp.int32(0))
        region = ((tot + (TILE - 1)) >> TILE_SHIFT) << TILE_SHIFT
        tail_ref[0, e] = base + tot
        tail_ref[1, e] = (region - tot) >> GRANULE_SHIFT
        n_t = region >> TILE_SHIFT

        def per_tile(j, _):
            tile_e_ref[tbase + j] = e
            return 0

        lax.fori_loop(0, n_t, per_tile, 0)
        return base + region, tbase + n_t

    _, n_tiles = lax.fori_loop(0, N_EXPERTS, per_expert, (jnp.int32(0), jnp.int32(0)))
    ntile_ref[0] = n_tiles
    last_e = tile_e_ref[jnp.maximum(n_tiles - 1, 0)]

    def fill(j, _):
        tile_e_ref[j] = last_e
        return 0

    lax.fori_loop(n_tiles, MAX_TILES, fill, 0)


def _plan(cnt):
    smem = pl.BlockSpec(memory_space=pltpu.SMEM)
    return pl.pallas_call(
        _plan_kernel,
        out_shape=(jax.ShapeDtypeStruct((N_CHUNKS, N_EXPERTS), jnp.int32),
                   jax.ShapeDtypeStruct((2, N_EXPERTS), jnp.int32),
                   jax.ShapeDtypeStruct((MAX_TILES,), jnp.int32),
                   jax.ShapeDtypeStruct((1,), jnp.int32)),
        in_specs=[smem],
        out_specs=(smem, smem, smem, smem),
        name="row_plan",
    )(cnt)


def _chunk_granules(cnt_ref, goff_ref, c, table_ref, per_granule=None):
    def per_expert(e, n_before):
        n_gran = (cnt_ref[c, e] + (GRANULE - 1)) >> GRANULE_SHIFT
        base = goff_ref[c, e]

        def per_j(j, _):
            table_ref[n_before + j] = base + j * GRANULE
            if per_granule is not None:
                per_granule(e, j, n_before + j)
            return 0

        lax.fori_loop(0, n_gran, per_j, 0)
        return n_before + n_gran

    return lax.fori_loop(0, N_EXPERTS, per_expert, jnp.int32(0))


def _dispatch_kernel(cnt_ref, goff_ref, tail_ref, x_ref, gx_ref, csm_ref, xs_hbm,
                     onehot_ref, table_ref, buf_ref, zero_ref, sems, zsem):
    c = pl.program_id(0)
    sub = lax.broadcasted_iota(jnp.int32, (GRANULE, CHUNK), 0)

    def build_granule(e, j, g):
        want = (sub + (j * GRANULE + 1)).astype(F32)
        hit = csm_ref[pl.ds(e, 1), :] == want
        onehot_ref[pl.ds(pl.multiple_of(g * GRANULE, GRANULE), GRANULE), :] = jnp.where(hit, 1.0, 0.0).astype(BF16)

    n_gran = _chunk_granules(cnt_ref, goff_ref, c, table_ref, build_granule)
    n_tiles = (n_gran + (GRAN_PER_TILE - 1)) >> GPT_SHIFT

    def clear(g, _):
        onehot_ref[pl.ds(pl.multiple_of(g * GRANULE, GRANULE), GRANULE), :] = jnp.zeros((GRANULE, CHUNK), BF16)
        return 0

    lax.fori_loop(n_gran, n_tiles * GRAN_PER_TILE, clear, 0)

    def granule_copy(slot, i, row):
        return pltpu.make_async_copy(buf_ref.at[slot, pl.ds(pl.multiple_of(i * GRANULE, GRANULE), GRANULE), :],
                                     xs_hbm.at[pl.ds(pl.multiple_of(row, GRANULE), GRANULE), :],
                                     sems.at[slot])

    def tile_granules(k):
        return jnp.minimum(GRAN_PER_TILE, n_gran - k * GRAN_PER_TILE)

    def wait_tile(k):
        slot = k % N_SLOTS

        def w(i, _):
            granule_copy(slot, i, 0).wait()
            return 0

        lax.fori_loop(0, tile_granules(k), w, 0)

    def per_tile(k, _):
        slot = k % N_SLOTS

        @pl.when(k >= N_SLOTS)
        def _():
            wait_tile(k - N_SLOTS)

        oh = onehot_ref[pl.ds(pl.multiple_of(k * TILE, TILE), TILE), :]
        buf_ref[slot, :, 0:D_MODEL] = jnp.dot(oh, x_ref[...], preferred_element_type=F32).astype(BF16)
        buf_ref[slot, :, D_MODEL:XS_WIDTH] = jnp.dot(oh, gx_ref[...], preferred_element_type=F32).astype(BF16)

        def s(i, _):
            granule_copy(slot, i, table_ref[k * GRAN_PER_TILE + i]).start()
            return 0

        lax.fori_loop(0, tile_granules(k), s, 0)
        return 0

    lax.fori_loop(0, n_tiles, per_tile, 0)

    def drain(k, _):
        wait_tile(k)
        return 0

    lax.fori_loop(jnp.maximum(n_tiles - N_SLOTS, 0), n_tiles, drain, 0)

    @pl.when(c == N_CHUNKS - 1)
    def _():
        zero_ref[...] = jnp.zeros_like(zero_ref)

        def tail_copy(row):
            return pltpu.make_async_copy(zero_ref, xs_hbm.at[pl.ds(pl.multiple_of(row, GRANULE), GRANULE), :], zsem)

        def start_e(e, _):
            def st(j, _):
                tail_copy(tail_ref[0, e] + j * GRANULE).start()
                return 0
            lax.fori_loop(0, tail_ref[1, e], st, 0)
            return 0

        def wait_e(e, _):
            def wt(j, _):
                tail_copy(0).wait()
                return 0
            lax.fori_loop(0, tail_ref[1, e], wt, 0)
            return 0

        lax.fori_loop(0, N_EXPERTS, start_e, 0)
        lax.fori_loop(0, N_EXPERTS, wait_e, 0)


def _dispatch(cnt, goff, tail, h2, gx, csm):
    grid_spec = pltpu.PrefetchScalarGridSpec(
        num_scalar_prefetch=3,
        grid=(N_CHUNKS,),
        in_specs=[pl.BlockSpec((CHUNK, D_MODEL), lambda c, *_: (c, 0)),
                  pl.BlockSpec((CHUNK, LANES), lambda c, *_: (c, 0)),
                  pl.BlockSpec((N_EXPERTS, CHUNK), lambda c, *_: (0, c))],
        out_specs=pl.BlockSpec(memory_space=pl.ANY),
        scratch_shapes=[pltpu.VMEM((CHUNK_ROWS, CHUNK), BF16),
                        pltpu.SMEM((CHUNK_ROWS // GRANULE,), jnp.int32),
                        pltpu.VMEM((N_SLOTS, TILE, XS_WIDTH), BF16),
                        pltpu.VMEM((GRANULE, XS_WIDTH), BF16),
                        pltpu.SemaphoreType.DMA((N_SLOTS,)),
                        pltpu.SemaphoreType.DMA(())])
    return pl.pallas_call(
        _dispatch_kernel,
        out_shape=jax.ShapeDtypeStruct((MAX_ROWS, XS_WIDTH), BF16),
        grid_spec=grid_spec,
        compiler_params=_cparams(("arbitrary",)),
        name="moe_dispatch",
    )(cnt, goff, tail, h2, gx, csm)


def _routed_kernel(tile_e_ref, ntile_ref, xs_ref, w1_ref, w3_ref, w2_ref, ys_ref, w1s_ref, w3s_ref, w2s_ref):
    i = pl.program_id(0)
    e = tile_e_ref[i]

    @pl.when(i < ntile_ref[0])
    def _():
        @pl.when((i == 0) | (e != tile_e_ref[jnp.maximum(i - 1, 0)]))
        def _():
            w1s_ref[...] = w1_ref[...].astype(BF16)
            w3s_ref[...] = w3_ref[...].astype(BF16)
            w2s_ref[...] = w2_ref[...].astype(BF16)

        x = xs_ref[:, 0:D_MODEL]
        gx = xs_ref[:, D_MODEL:XS_WIDTH].astype(F32)
        lane = lax.broadcasted_iota(jnp.int32, gx.shape, 1)
        gate = jnp.sum(jnp.where((lane == e) | (lane == e + N_EXPERTS), gx, 0.0), axis=1, keepdims=True)
        a = jnp.dot(x, w1s_ref[...], preferred_element_type=F32)
        u = jnp.dot(x, w3s_ref[...], preferred_element_type=F32)
        hid = (a * jax.nn.sigmoid(a)) * u * gate
        ys_ref[...] = jnp.dot(hid.astype(BF16), w2s_ref[...], preferred_element_type=F32).astype(ys_ref.dtype)


def _routed_experts(tile_e, ntile, xs, w1, w3, w2):
    def row_map(i, tile_e_ref, ntile_ref):
        return (jnp.minimum(i, jnp.maximum(ntile_ref[0] - 1, 0)), 0)

    def w_map(i, tile_e_ref, ntile_ref):
        return (tile_e_ref[i], 0, 0)

    grid_spec = pltpu.PrefetchScalarGridSpec(
        num_scalar_prefetch=2,
        grid=(MAX_TILES,),
        in_specs=[pl.BlockSpec((TILE, XS_WIDTH), row_map),
                  pl.BlockSpec((None, D_MODEL, D_EXPERT), w_map),
                  pl.BlockSpec((None, D_MODEL, D_EXPERT), w_map),
                  pl.BlockSpec((None, D_EXPERT, D_MODEL), w_map)],
        out_specs=pl.BlockSpec((TILE, D_MODEL), row_map),
        scratch_shapes=[pltpu.VMEM((D_MODEL, D_EXPERT), BF16),
                        pltpu.VMEM((D_MODEL, D_EXPERT), BF16),
                        pltpu.VMEM((D_EXPERT, D_MODEL), BF16)])
    return pl.pallas_call(
        _routed_kernel,
        out_shape=jax.ShapeDtypeStruct((MAX_ROWS, D_MODEL), BF16),
        grid_spec=grid_spec,
        compiler_params=_cparams(("arbitrary",)),
        name="routed_experts",
    )(tile_e, ntile, xs, w1, w3, w2)


def _combine_kernel(cnt_ref, goff_ref, lp_ref, x1_ref, sh_ref, gt_ref, gf_ref, ys_hbm, o_ref,
                    table_ref, buf_ref, acc_ref, sems):
    c = pl.program_id(0)
    n_gran = _chunk_granules(cnt_ref, goff_ref, c, table_ref)
    n_tiles = (n_gran + (GRAN_PER_TILE - 1)) >> GPT_SHIFT

    def granule_copy(slot, i, row):
        return pltpu.make_async_copy(ys_hbm.at[pl.ds(pl.multiple_of(row, GRANULE), GRANULE), :],
                                     buf_ref.at[slot, pl.ds(pl.multiple_of(i * GRANULE, GRANULE), GRANULE), :],
                                     sems.at[slot])

    def tile_granules(k):
        return jnp.minimum(GRAN_PER_TILE, n_gran - k * GRAN_PER_TILE)

    def start_tile(k):
        slot = k % N_SLOTS

        def s(i, _):
            granule_copy(slot, i, table_ref[k * GRAN_PER_TILE + i]).start()
            return 0

        lax.fori_loop(0, tile_granules(k), s, 0)

    def wait_tile(k):
        slot = k % N_SLOTS

        def w(i, _):
            granule_copy(slot, i, 0).wait()
            return 0

        lax.fori_loop(0, tile_granules(k), w, 0)

    for k0 in range(N_SLOTS - 1):
        @pl.when(k0 < n_tiles)
        def _():
            start_tile(k0)

    acc_ref[...] = jnp.zeros_like(acc_ref)
    lane = lax.broadcasted_iota(jnp.int32, (CHUNK, TILE), 1)

    def per_tile(k, _):
        @pl.when(k + (N_SLOTS - 1) < n_tiles)
        def _():
            start_tile(k + (N_SLOTS - 1))

        wait_tile(k)
        slot = k % N_SLOTS

        def clear(i, _):
            buf_ref[slot, pl.ds(pl.multiple_of(i * GRANULE, GRANULE), GRANULE), :] = jnp.zeros((GRANULE, D_MODEL), BF16)
            return 0

        lax.fori_loop(tile_granules(k), GRAN_PER_TILE, clear, 0)

        want = (lane + k * TILE).astype(F32)
        onehot = jnp.zeros((CHUNK, TILE), F32)
        for j in range(TOP_K):
            onehot = jnp.where(lp_ref[:, j:j + 1] == want, 1.0, onehot)
        acc_ref[...] += jnp.dot(onehot.astype(BF16), buf_ref[slot], preferred_element_type=F32)
        return 0

    lax.fori_loop(0, n_tiles, per_tile, 0)

    x = x1_ref[...] + gt_ref[...] * (acc_ref[...] + sh_ref[...])
    o_ref[...] = x * lax.rsqrt(jnp.mean(x * x, axis=-1, keepdims=True) + EPS) * gf_ref[...]


def _combine(cnt, goff, lp, x1, shared, mod4, g_final, ys):
    per_b = SEQ // CHUNK
    row = pl.BlockSpec((CHUNK, D_MODEL), lambda c, *_: (c, 0))
    grid_spec = pltpu.PrefetchScalarGridSpec(
        num_scalar_prefetch=2,
        grid=(N_CHUNKS,),
        in_specs=[pl.BlockSpec((CHUNK, LANES), lambda c, *_: (c, 0)),
                  row, row,
                  pl.BlockSpec((None, None, 1, D_MODEL), lambda c, *_: (c // per_b, 5, 0, 0)),
                  pl.BlockSpec((1, D_MODEL), lambda c, *_: (0, 0)),
                  pl.BlockSpec(memory_space=pl.ANY)],
        out_specs=row,
        scratch_shapes=[pltpu.SMEM((CHUNK_ROWS // GRANULE,), jnp.int32),
                        pltpu.VMEM((N_SLOTS, TILE, D_MODEL), BF16),
                        pltpu.VMEM((CHUNK, D_MODEL), F32),
                        pltpu.SemaphoreType.DMA((N_SLOTS,))])
    return pl.pallas_call(
        _combine_kernel,
        out_shape=jax.ShapeDtypeStruct((N_TOK, D_MODEL), F32),
        grid_spec=grid_spec,
        compiler_params=_cparams(("arbitrary",)),
        name="moe_combine",
    )(cnt, goff, lp, x1, shared, mod4, g_final, ys)


def _shared_kernel(x_ref, w1_ref, w3_ref, w2_ref, o_ref):
    x = x_ref[...]
    a = jnp.dot(x, w1_ref[...], preferred_element_type=F32)
    u = jnp.dot(x, w3_ref[...], preferred_element_type=F32)
    hid = (a * jax.nn.sigmoid(a)) * u
    o_ref[...] = jnp.dot(hid.astype(BF16), w2_ref[...], preferred_element_type=F32).astype(o_ref.dtype)


def _shared_expert(h2, w1, w3, w2):
    tm = 1024
    return pl.pallas_call(
        _shared_kernel,
        out_shape=jax.ShapeDtypeStruct((N_TOK, D_MODEL), F32),
        grid=(N_TOK // tm,),
        in_specs=[pl.BlockSpec((tm, D_MODEL), lambda m: (m, 0)),
                  pl.BlockSpec((D_MODEL, D_SHARED), lambda m: (0, 0)),
                  pl.BlockSpec((D_MODEL, D_SHARED), lambda m: (0, 0)),
                  pl.BlockSpec((D_SHARED, D_MODEL), lambda m: (0, 0))],
        out_specs=pl.BlockSpec((tm, D_MODEL), lambda m: (m, 0)),
        compiler_params=_cparams(("parallel",)),
        name="shared_expert",
    )(h2, w1, w3, w2)


def _rope_angles(pos, dims, theta):
    inv = jnp.power(jnp.float32(theta), -jnp.arange(0, dims, 2, dtype=jnp.float32) / dims)
    return pos.astype(jnp.float32)[:, None] * inv[None, :]


def _rotary_tables(angle_blocks):
    cos_parts, lo_parts, hi_parts = [], [], []
    used = 0
    for ang in angle_blocks:
        c, s = jnp.cos(ang), jnp.sin(ang)
        z = jnp.zeros_like(s)
        cos_parts += [c, c]
        lo_parts += [-s, z]
        hi_parts += [z, s]
        used += 2 * ang.shape[1]
    rest = HEAD_DIM - used
    if rest:
        cos_parts.append(jnp.ones((SEQ, rest), F32))
        lo_parts.append(jnp.zeros((SEQ, rest), F32))
        hi_parts.append(jnp.zeros((SEQ, rest), F32))
    return (jnp.concatenate(cos_parts, axis=1), jnp.concatenate(lo_parts, axis=1),
            jnp.concatenate(hi_parts, axis=1))


def kernel(x, c, w_ada, b_ada, g_attn, w_in, b_gate, q_norm_g, k_norm_g, w_a_up, w_b_up, w_out,
           g_ffn, w_router, e_bias, w1, w3, w2, ws1, ws3, ws2, g_final):
    l = 0
    x2 = x.reshape(N_TOK, D_MODEL)
    pos = jnp.arange(SEQ)
    tabs_a = _rotary_tables([_rope_angles(pos, ROPE_DIMS, ROPE_THETA)])
    tabs_b = _rotary_tables([_rope_angles(pos // GRID_W, AXIAL_DIMS, AXIAL_THETA),
                             _rope_angles(pos % GRID_W, AXIAL_DIMS, AXIAL_THETA)])

    c_pad = jnp.zeros((SUBLANES, D_MODEL), F32).at[:BATCH].set(c)
    mod = _ada(c_pad, w_ada[l], b_ada[l].reshape(1, -1))
    mod4 = mod[:BATCH].reshape(BATCH, N_MOD, 1, D_MODEL)

    h = _norm_modulate(x2, g_attn[l].reshape(1, -1), mod4, 1, 0)
    proj = _projection(h, w_in[l].astype(BF16), tabs_a, tabs_b,
                       q_norm_g[l].reshape(1, -1), k_norm_g[l].reshape(1, -1), b_gate[l].reshape(1, -1))
    proj3 = proj.reshape(BATCH, SEQ, IN_COLS)

    o_groups, lse_groups = [], []
    for gi, (window, dilation) in enumerate(A_PATTERNS):
        o, lse = _attn_a(proj3, gi, window, dilation)
        o_groups.append(o)
        lse_groups.append(lse)
    mix_b = _attn_b(proj3)

    merged = _merge(o_groups, lse_groups, mix_b, proj, w_a_up[l].astype(BF16), w_b_up[l].astype(BF16))
    x1, h2, logits_t = _out_projection(merged, w_out[l].astype(BF16), x2, mod4, g_ffn[l].reshape(1, -1),
                                       w_router[l].T)
    csm, cnt_f, lp, gx = _route(logits_t, e_bias[l].reshape(-1, 1))
    cnt = cnt_f[:, 0, :].astype(jnp.int32)
    goff, tail, tile_e, ntile = _plan(cnt)
    xs = _dispatch(cnt, goff, tail, h2, gx, csm)
    ys = _routed_experts(tile_e, ntile, xs, w1[l], w3[l], w2[l])
    shared = _shared_expert(h2, ws1[l].astype(BF16), ws3[l].astype(BF16), ws2[l].astype(BF16))
    out = _combine(cnt, goff, lp, x1, shared, mod4, g_final.reshape(1, -1), ys)
    return out.reshape(BATCH, SEQ, D_MODEL)
```

```python
import functools

import jax
import jax.numpy as jnp
from jax import lax
from jax.experimental import pallas as pl
from jax.experimental.pallas import tpu as pltpu

F32 = jnp.float32
BF16 = jnp.bfloat16

D_MODEL = 2048
BATCH = 2
SEQ = 4096
N_TOK = BATCH * SEQ
HEAD_DIM = 128
EPS = 1e-6
A_PATTERNS = ((128, 1), (512, 4), (2048, 16))
A_N_GROUPS = len(A_PATTERNS)
A_HEADS = 4
A_WIDTH = A_HEADS * HEAD_DIM
ROPE_THETA = 500000.0
ROPE_DIMS = HEAD_DIM // 4
B_Q_HEADS = 16
B_KV_HEADS = 4
B_GROUP = B_Q_HEADS // B_KV_HEADS
B_Q_WIDTH = B_Q_HEADS * HEAD_DIM
B_KV_WIDTH = B_KV_HEADS * HEAD_DIM
AXIAL_THETA = 10000.0
AXIAL_DIMS = HEAD_DIM // 2
GRID_W = 64
A_COLS = A_N_GROUPS * 3 * A_WIDTH
B_COLS = B_Q_WIDTH + 2 * B_KV_WIDTH
GATE_COLS = 2 * D_MODEL
IN_COLS = A_COLS + B_COLS + GATE_COLS
N_MOD = 6
N_EXPERTS = 64
N_EXPERT_GROUPS = 8
PER_GROUP = N_EXPERTS // N_EXPERT_GROUPS
TOPK_GROUPS = 4
TOP_K = 8
D_EXPERT = D_MODEL // 4
D_SHARED = D_MODEL // 4
ROUTED_SCALE = 2.5
NEG = -1e30
ATTN_SCALE = HEAD_DIM ** -0.5
LOG2E = 1.4426950408889634

LANES = 128
SUBLANES = 8
VMEM_LIMIT = 56 * 1024 * 1024

COL_BLK = 512
N_COL_BLKS = IN_COLS // COL_BLK
A_BLKS = A_COLS // COL_BLK
BQ_BLK0 = A_BLKS
BK_BLK = BQ_BLK0 + B_Q_WIDTH // COL_BLK
BV_BLK = BK_BLK + 1
GATE_BLK0 = BV_BLK + 1

CHUNK = 512
N_CHUNKS = N_TOK // CHUNK
GRANULE = 16
GRANULE_SHIFT = 4
TILE = 256
TILE_SHIFT = 8
GRAN_PER_TILE = TILE // GRANULE
GPT_SHIFT = TILE_SHIFT - GRANULE_SHIFT
N_SLOTS = 4
COMBINE_TILE = 1024
COMBINE_GPT = COMBINE_TILE // GRANULE
COMBINE_GPT_SHIFT = 6
COMBINE_SLOTS = 3
XS_WIDTH = D_MODEL + LANES
CHUNK_ROWS = -(-(CHUNK * TOP_K + N_EXPERTS * (GRANULE - 1)) // TILE) * TILE
MAX_ROWS = -(-(N_TOK * TOP_K + N_EXPERTS * N_CHUNKS * (GRANULE - 1) + N_EXPERTS * (TILE - 1)) // TILE) * TILE
MAX_TILES = MAX_ROWS // TILE


def _cparams(semantics):
    return pltpu.CompilerParams(dimension_semantics=semantics, vmem_limit_bytes=VMEM_LIMIT)


def _ada_kernel(c_ref, w_ref, b_ref, o_ref):
    c = c_ref[...]
    act = c * jax.nn.sigmoid(c)
    o_ref[...] = jnp.dot(act, w_ref[...], preferred_element_type=F32,
                         precision=lax.Precision.HIGHEST) + b_ref[...]


def _ada(c_pad, w_ada, b_ada):
    tn = 1024
    n_out = w_ada.shape[1]
    return pl.pallas_call(
        _ada_kernel,
        out_shape=jax.ShapeDtypeStruct((SUBLANES, n_out), F32),
        grid=(n_out // tn,),
        in_specs=[pl.BlockSpec((SUBLANES, D_MODEL), lambda n: (0, 0)),
                  pl.BlockSpec((D_MODEL, tn), lambda n: (0, n)),
                  pl.BlockSpec((1, tn), lambda n: (0, n))],
        out_specs=pl.BlockSpec((SUBLANES, tn), lambda n: (0, n)),
        compiler_params=_cparams(("parallel",)),
        name="ada_mod",
    )(c_pad, w_ada, b_ada)


def _normmod(x, g, sc, sh):
    y = x * lax.rsqrt(jnp.mean(x * x, axis=-1, keepdims=True) + EPS)
    return (y * g) * (1.0 + sc) + sh


def _normmod_kernel(x_ref, g_ref, sc_ref, sh_ref, o_ref):
    o_ref[...] = _normmod(x_ref[...], g_ref[...], sc_ref[...], sh_ref[...]).astype(o_ref.dtype)


def _norm_modulate(x2, g, mod4, sc_idx, sh_idx):
    tm = 512
    per_b = SEQ // tm
    return pl.pallas_call(
        _normmod_kernel,
        out_shape=jax.ShapeDtypeStruct((N_TOK, D_MODEL), BF16),
        grid=(N_TOK // tm,),
        in_specs=[pl.BlockSpec((tm, D_MODEL), lambda m: (m, 0)),
                  pl.BlockSpec((1, D_MODEL), lambda m: (0, 0)),
                  pl.BlockSpec((None, None, 1, D_MODEL), lambda m: (m // per_b, sc_idx, 0, 0)),
                  pl.BlockSpec((None, None, 1, D_MODEL), lambda m: (m // per_b, sh_idx, 0, 0))],
        out_specs=pl.BlockSpec((tm, D_MODEL), lambda m: (m, 0)),
        compiler_params=_cparams(("parallel",)),
        name="norm_modulate",
    )(x2, g, mod4, mod4)


def _tile4(t):
    return jnp.concatenate([t, t, t, t], axis=1)


def _rotary(y, tabs, rows, shift):
    cos_ref, sin_lo_ref, sin_hi_ref = tabs
    width = y.shape[1]
    return (y * _tile4(cos_ref[rows, :])
            + pltpu.roll(y, width - shift, 1) * _tile4(sin_lo_ref[rows, :])
            + pltpu.roll(y, shift, 1) * _tile4(sin_hi_ref[rows, :]))


def _head_rmsnorm(y, g):
    outs = []
    for h in range(y.shape[1] // HEAD_DIM):
        yh = y[:, h * HEAD_DIM:(h + 1) * HEAD_DIM]
        outs.append(yh * lax.rsqrt(jnp.mean(yh * yh, axis=-1, keepdims=True) + EPS) * g)
    return jnp.concatenate(outs, axis=1)


def _proj_kernel(h_ref, w_ref, ca_ref, sal_ref, sah_ref, cb_ref, sbl_ref, sbh_ref,
                 qg_ref, kg_ref, bg_ref, o_ref):
    n = pl.program_id(1)
    is_a = n < A_BLKS
    part = n % 3
    tabs_a = (ca_ref, sal_ref, sah_ref)
    tabs_b = (cb_ref, sbl_ref, sbh_ref)
    sub_rows = 256

    def run(epilogue):
        for r in range(h_ref.shape[0] // sub_rows):
            rows = pl.ds(r * sub_rows, sub_rows)
            acc = jnp.dot(h_ref[rows, :], w_ref[...], preferred_element_type=F32)
            o_ref[rows, :] = epilogue(acc, rows).astype(o_ref.dtype)

    @pl.when(is_a & (part == 0))
    def _():
        run(lambda acc, rows: _rotary(acc, tabs_a, rows, ROPE_DIMS // 2) * ATTN_SCALE)

    @pl.when(is_a & (part == 1))
    def _():
        run(lambda acc, rows: _rotary(acc, tabs_a, rows, ROPE_DIMS // 2))

    @pl.when((is_a & (part == 2)) | (n == BV_BLK))
    def _():
        run(lambda acc, rows: acc)

    @pl.when((n >= BQ_BLK0) & (n < BK_BLK))
    def _():
        run(lambda acc, rows: _rotary(_head_rmsnorm(acc, qg_ref[...]), tabs_b, rows, AXIAL_DIMS // 2)
            * (ATTN_SCALE * LOG2E))

    @pl.when(n == BK_BLK)
    def _():
        run(lambda acc, rows: _rotary(_head_rmsnorm(acc, kg_ref[...]), tabs_b, rows, AXIAL_DIMS // 2))

    @pl.when(n >= GATE_BLK0)
    def _():
        run(lambda acc, rows: jax.nn.sigmoid(acc + bg_ref[...]))


def _projection(h, w_in, tabs_a, tabs_b, qg, kg, b_gate):
    tm = 1024
    per_b = SEQ // tm
    tab_spec = pl.BlockSpec((tm, LANES), lambda m, n: (m % per_b, 0))
    vec_spec = pl.BlockSpec((1, HEAD_DIM), lambda m, n: (0, 0))
    n_gate_blks = GATE_COLS // COL_BLK
    return pl.pallas_call(
        _proj_kernel,
        out_shape=jax.ShapeDtypeStruct((N_TOK, IN_COLS), BF16),
        grid=(N_TOK // tm, N_COL_BLKS),
        in_specs=[pl.BlockSpec((tm, D_MODEL), lambda m, n: (m, 0)),
                  pl.BlockSpec((D_MODEL, COL_BLK), lambda m, n: (0, n)),
                  tab_spec, tab_spec, tab_spec, tab_spec, tab_spec, tab_spec,
                  vec_spec, vec_spec,
                  pl.BlockSpec((1, COL_BLK),
                               lambda m, n: (0, jnp.clip(n - GATE_BLK0, 0, n_gate_blks - 1)))],
        out_specs=pl.BlockSpec((tm, COL_BLK), lambda m, n: (m, n)),
        compiler_params=_cparams(("parallel", "arbitrary")),
        name="in_projection",
    )(h, w_in, *tabs_a, *tabs_b, qg, kg, b_gate)


def _attn_a_kernel(q_ref, k_ref, v_ref, o_ref, lse_ref, *, dilation, reach):
    i = pl.program_id(1)
    tq = q_ref.shape[0]
    win = tq + 2 * reach
    start = pl.multiple_of(jnp.clip(i * tq - reach, 0, SEQ - win), GRANULE)
    q = q_ref[...]
    k = k_ref[pl.ds(start, win), :]
    v = v_ref[pl.ds(start, win), :]
    diff = (i * tq - start) + lax.broadcasted_iota(jnp.int32, (tq, win), 0) \
        - lax.broadcasted_iota(jnp.int32, (tq, win), 1)
    valid = (jnp.abs(diff) <= reach) & ((diff & (dilation - 1)) == 0)
    lane = lax.broadcasted_iota(jnp.int32, (tq, LANES), 1)
    lse_tile = jnp.zeros((tq, LANES), F32)
    outs = []
    for h in range(A_HEADS):
        sl = slice(h * HEAD_DIM, (h + 1) * HEAD_DIM)
        s = lax.dot_general(q[:, sl], k[:, sl], (((1,), (1,)), ((), ())), preferred_element_type=F32)
        s = jnp.where(valid, s, NEG)
        m = jnp.max(s, axis=-1, keepdims=True)
        p = jnp.exp(s - m)
        l = jnp.sum(p, axis=-1, keepdims=True)
        o = jnp.dot(p.astype(BF16), v[:, sl], preferred_element_type=F32)
        outs.append(o / l)
        lse_tile = jnp.where(lane == h, m + jnp.log(l), lse_tile)
    o_ref[...] = jnp.concatenate(outs, axis=1).astype(o_ref.dtype)
    lse_ref[...] = lse_tile


def _attn_a(proj3, group, window, dilation):
    assert dilation & (dilation - 1) == 0
    reach = window // 2
    tq = 256
    qb, kb, vb = 3 * group, 3 * group + 1, 3 * group + 2
    o, lse = pl.pallas_call(
        functools.partial(_attn_a_kernel, dilation=dilation, reach=reach),
        out_shape=(jax.ShapeDtypeStruct((BATCH, SEQ, A_WIDTH), BF16),
                   jax.ShapeDtypeStruct((BATCH, SEQ, LANES), F32)),
        grid=(BATCH, SEQ // tq),
        in_specs=[pl.BlockSpec((None, tq, COL_BLK), lambda b, i: (b, i, qb)),
                  pl.BlockSpec((None, SEQ, COL_BLK), lambda b, i: (b, 0, kb)),
                  pl.BlockSpec((None, SEQ, COL_BLK), lambda b, i: (b, 0, vb))],
        out_specs=(pl.BlockSpec((None, tq, A_WIDTH), lambda b, i: (b, i, 0)),
                   pl.BlockSpec((None, tq, LANES), lambda b, i: (b, i, 0))),
        compiler_params=_cparams(("parallel", "arbitrary")),
        name=f"dilated_attention_g{group}",
    )(proj3, proj3, proj3)
    return o.reshape(N_TOK, A_WIDTH), lse.reshape(N_TOK, LANES)


def _attn_b_kernel(q_ref, k_ref, v_ref, o_ref, vx_ref, acc_ref, *, tk):
    tq = q_ref.shape[0]
    n_chunks = SEQ // tk

    @pl.when(pl.program_id(2) == 0)
    def _():
        vx_ref[:, 0:HEAD_DIM] = v_ref[...]
        vx_ref[:, HEAD_DIM:2 * HEAD_DIM] = jnp.ones((SEQ, HEAD_DIM), BF16)

    q = q_ref[...]
    qs = jnp.concatenate([q[:, g * HEAD_DIM:(g + 1) * HEAD_DIM] for g in range(B_GROUP)], axis=0)
    acc_ref[...] = jnp.zeros_like(acc_ref)
    m = jnp.full((B_GROUP * tq, 1), -jnp.inf, F32)
    for c in range(n_chunks):
        keys = slice(c * tk, (c + 1) * tk)
        s = lax.dot_general(qs, k_ref[keys, :], (((1,), (1,)), ((), ())), preferred_element_type=F32)
        m_new = jnp.maximum(m, jnp.max(s, axis=-1, keepdims=True))
        p = jnp.exp2(s - m_new).astype(BF16)
        acc_ref[...] = jnp.exp2(m - m_new) * acc_ref[...] + jnp.dot(p, vx_ref[keys, :],
                                                                    preferred_element_type=F32)
        m = m_new
    o = acc_ref[:, 0:HEAD_DIM] / acc_ref[:, HEAD_DIM:2 * HEAD_DIM]
    o_ref[...] = jnp.concatenate([o[g * tq:(g + 1) * tq] for g in range(B_GROUP)], axis=1).astype(o_ref.dtype)


def _attn_b(proj3):
    tq = 128
    tk = 512
    kcol0 = BK_BLK * COL_BLK // HEAD_DIM
    vcol0 = BV_BLK * COL_BLK // HEAD_DIM
    o = pl.pallas_call(
        functools.partial(_attn_b_kernel, tk=tk),
        out_shape=jax.ShapeDtypeStruct((BATCH, SEQ, B_Q_WIDTH), BF16),
        grid=(BATCH, B_KV_HEADS, SEQ // tq),
        in_specs=[pl.BlockSpec((None, tq, COL_BLK), lambda b, h, i: (b, i, BQ_BLK0 + h)),
                  pl.BlockSpec((None, SEQ, HEAD_DIM), lambda b, h, i: (b, 0, kcol0 + h)),
                  pl.BlockSpec((None, SEQ, HEAD_DIM), lambda b, h, i: (b, 0, vcol0 + h))],
        out_specs=pl.BlockSpec((None, tq, COL_BLK), lambda b, h, i: (b, i, h)),
        scratch_shapes=[pltpu.VMEM((SEQ, 2 * HEAD_DIM), BF16),
                        pltpu.VMEM((B_GROUP * tq, 2 * HEAD_DIM), F32)],
        compiler_params=_cparams(("parallel", "parallel", "arbitrary")),
        name="gqa_attention",
    )(proj3, proj3, proj3)
    return o.reshape(N_TOK, B_Q_WIDTH)


def _merge_kernel(o0_ref, o1_ref, o2_ref, l0_ref, l1_ref, l2_ref, yb_ref, wa_ref, wb_ref, ga_ref, gb_ref,
                  out_ref, mix_ref):
    @pl.when(pl.program_id(1) == 0)
    def _():
        tm = o0_ref.shape[0]
        lses = (l0_ref[...], l1_ref[...], l2_ref[...])
        outs = (o0_ref[...], o1_ref[...], o2_ref[...])
        cols = []
        for h in range(A_HEADS):
            lh = [jnp.broadcast_to(l[:, h:h + 1], (tm, HEAD_DIM)) for l in lses]
            mx = jnp.maximum(jnp.maximum(lh[0], lh[1]), lh[2])
            e = [jnp.exp(v - mx) for v in lh]
            den = e[0] + e[1] + e[2]
            sl = slice(h * HEAD_DIM, (h + 1) * HEAD_DIM)
            cols.append(sum((e[g] / den) * outs[g][:, sl].astype(F32) for g in range(A_N_GROUPS)))
        mix_ref[...] = jnp.concatenate(cols, axis=1).astype(mix_ref.dtype)

    ya = jnp.dot(mix_ref[...], wa_ref[...], preferred_element_type=F32)
    yb = jnp.dot(yb_ref[...], wb_ref[...], preferred_element_type=F32)
    out_ref[...] = (ga_ref[...].astype(F32) * ya + gb_ref[...].astype(F32) * yb).astype(out_ref.dtype)


def _merge(o_groups, lse_groups, mix_b, proj, w_a_up, w_b_up):
    tm, tn = 512, 512
    n_blk = D_MODEL // tn
    o_spec = pl.BlockSpec((tm, A_WIDTH), lambda m, n: (m, 0))
    l_spec = pl.BlockSpec((tm, LANES), lambda m, n: (m, 0))
    return pl.pallas_call(
        _merge_kernel,
        out_shape=jax.ShapeDtypeStruct((N_TOK, D_MODEL), BF16),
        grid=(N_TOK // tm, n_blk),
        in_specs=[o_spec, o_spec, o_spec, l_spec, l_spec, l_spec,
                  pl.BlockSpec((tm, B_Q_WIDTH), lambda m, n: (m, 0)),
                  pl.BlockSpec((A_WIDTH, tn), lambda m, n: (0, n)),
                  pl.BlockSpec((B_Q_WIDTH, tn), lambda m, n: (0, n)),
                  pl.BlockSpec((tm, tn), lambda m, n: (m, GATE_BLK0 + n)),
                  pl.BlockSpec((tm, tn), lambda m, n: (m, GATE_BLK0 + n_blk + n))],
        out_specs=pl.BlockSpec((tm, tn), lambda m, n: (m, n)),
        scratch_shapes=[pltpu.VMEM((tm, A_WIDTH), BF16)],
        compiler_params=_cparams(("parallel", "arbitrary")),
        name="branch_merge",
    )(*o_groups, *lse_groups, mix_b, w_a_up, w_b_up, proj, proj)


def _outproj_kernel(mg_ref, w_ref, x_ref, gt_ref, g_ref, sc_ref, sh_ref, wr_ref, x1_ref, h2_ref, lg_ref):
    y = jnp.dot(mg_ref[...], w_ref[...], preferred_element_type=F32)
    x1 = x_ref[...] + gt_ref[...] * y
    x1_ref[...] = x1
    h2 = _normmod(x1, g_ref[...], sc_ref[...], sh_ref[...])
    h2_ref[...] = h2.astype(h2_ref.dtype)
    lg_ref[...] = lax.dot_general(wr_ref[...], h2, (((1,), (1,)), ((), ())),
                                  preferred_element_type=F32, precision=lax.Precision.HIGHEST)


def _out_projection(merged, w_out, x2, mod4, g_ffn, w_router_t):
    tm = 256
    per_b = SEQ // tm

    def mod_spec(j):
        return pl.BlockSpec((None, None, 1, D_MODEL), lambda m: (m // per_b, j, 0, 0))

    return pl.pallas_call(
        _outproj_kernel,
        out_shape=(jax.ShapeDtypeStruct((N_TOK, D_MODEL), F32),
                   jax.ShapeDtypeStruct((N_TOK, D_MODEL), BF16),
                   jax.ShapeDtypeStruct((N_EXPERTS, N_TOK), F32)),
        grid=(N_TOK // tm,),
        in_specs=[pl.BlockSpec((tm, D_MODEL), lambda m: (m, 0)),
                  pl.BlockSpec((D_MODEL, D_MODEL), lambda m: (0, 0)),
                  pl.BlockSpec((tm, D_MODEL), lambda m: (m, 0)),
                  mod_spec(2),
                  pl.BlockSpec((1, D_MODEL), lambda m: (0, 0)),
                  mod_spec(4), mod_spec(3),
                  pl.BlockSpec((N_EXPERTS, D_MODEL), lambda m: (0, 0))],
        out_specs=(pl.BlockSpec((tm, D_MODEL), lambda m: (m, 0)),
                   pl.BlockSpec((tm, D_MODEL), lambda m: (m, 0)),
                   pl.BlockSpec((N_EXPERTS, tm), lambda m: (0, m))),
        compiler_params=_cparams(("parallel",)),
        name="out_projection",
    )(merged, w_out, x2, mod4, g_ffn, mod4, mod4, w_router_t)


def _route_kernel(lg_ref, eb_ref, csm_ref, cnt_ref, lp_ref, gx_ref):
    tt = lg_ref.shape[1]
    scores = jax.nn.sigmoid(lg_ref[...])
    sel = scores + eb_ref[...]
    neg_inf = -jnp.inf
    sel_g = [sel[PER_GROUP * g:PER_GROUP * (g + 1), :] for g in range(N_EXPERT_GROUPS)]
    grp = []
    for v in sel_g:
        m1 = jnp.max(v, axis=0, keepdims=True)
        is1 = v == m1
        n1 = jnp.sum(jnp.where(is1, 1.0, 0.0), axis=0, keepdims=True)
        rest = jnp.max(jnp.where(is1, neg_inf, v), axis=0, keepdims=True)
        grp.append(m1 + jnp.where(n1 >= 2.0, m1, rest))
    masked = []
    for g in range(N_EXPERT_GROUPS):
        rank = jnp.zeros((1, tt), F32)
        for g2 in range(N_EXPERT_GROUPS):
            if g2 != g:
                beats = (grp[g2] >= grp[g]) if g2 < g else (grp[g2] > grp[g])
                rank = rank + jnp.where(beats, 1.0, 0.0)
        keep = jnp.broadcast_to(rank < TOPK_GROUPS, (PER_GROUP, tt))
        masked.append(jnp.where(keep, sel_g[g], neg_inf))
    sub = lax.broadcasted_iota(jnp.int32, (PER_GROUP, tt), 0)
    ranks = [jnp.zeros((PER_GROUP, tt), F32) for _ in range(N_EXPERT_GROUPS)]
    for g2 in range(N_EXPERT_GROUPS):
        for m2 in range(PER_GROUP):
            vf = jnp.broadcast_to(masked[g2][m2:m2 + 1, :], (PER_GROUP, tt))
            for g in range(N_EXPERT_GROUPS):
                if g2 < g:
                    beats = vf >= masked[g]
                elif g2 > g:
                    beats = vf > masked[g]
                else:
                    beats = (vf > masked[g]) | ((vf == masked[g]) & (sub > m2))
                ranks[g] = ranks[g] + jnp.where(beats, 1.0, 0.0)
    picked = [ranks[g] < TOP_K for g in range(N_EXPERT_GROUPS)]
    chosen = [jnp.where(picked[g], scores[PER_GROUP * g:PER_GROUP * (g + 1), :], 0.0)
              for g in range(N_EXPERT_GROUPS)]
    total = chosen[0]
    for g in range(1, N_EXPERT_GROUPS):
        total = total + chosen[g]
    denom = jnp.sum(total, axis=0, keepdims=True)
    gate = jnp.concatenate([chosen[g] / denom * ROUTED_SCALE for g in range(N_EXPERT_GROUPS)], axis=0)
    mask = jnp.concatenate([jnp.where(picked[g], 1.0, 0.0) for g in range(N_EXPERT_GROUPS)], axis=0)
    mask_bf = mask.astype(BF16)

    upper = jnp.where(lax.broadcasted_iota(jnp.int32, (tt, tt), 0) <= lax.broadcasted_iota(jnp.int32, (tt, tt), 1),
                      1.0, 0.0).astype(BF16)
    cs = jnp.dot(mask_bf, upper, preferred_element_type=F32)
    csm_ref[...] = jnp.where(mask > 0.0, cs, 0.0)
    cnt_ref[...] = lax.dot_general(jnp.ones((SUBLANES, tt), BF16), mask_bf, (((1,), (1,)), ((), ())),
                                   preferred_element_type=F32)

    padded = jnp.ceil(cs[:, tt - 1:tt] / GRANULE) * GRANULE
    lower = jnp.where(lax.broadcasted_iota(jnp.int32, (N_EXPERTS, N_EXPERTS), 1)
                      < lax.broadcasted_iota(jnp.int32, (N_EXPERTS, N_EXPERTS), 0), 1.0, 0.0)
    first_row = jnp.dot(lower, jnp.broadcast_to(padded, (N_EXPERTS, LANES)), preferred_element_type=F32,
                        precision=lax.Precision.HIGHEST)[:, 0:1]
    row_of = first_row + cs - 1.0

    ordinal = jnp.dot(lower.astype(BF16), mask_bf, preferred_element_type=F32)
    sub = lax.broadcasted_iota(jnp.int32, (SUBLANES, tt), 0)
    rows8 = jnp.zeros((SUBLANES, tt), F32)
    for k in range(TOP_K):
        hit = (ordinal == k) & (mask > 0.0)
        row_k = jnp.sum(jnp.where(hit, row_of, 0.0), axis=0, keepdims=True)
        rows8 = jnp.where(sub == k, jnp.broadcast_to(row_k, (SUBLANES, tt)), rows8)
    lp_ref[...] = jnp.concatenate([rows8, jnp.zeros((LANES - SUBLANES, tt), F32)], axis=0).T

    hi = gate.astype(BF16).astype(F32)
    gx_ref[...] = jnp.concatenate([hi, gate - hi], axis=0).T.astype(BF16)


def _route(logits_t, e_bias_col):
    return pl.pallas_call(
        _route_kernel,
        out_shape=(jax.ShapeDtypeStruct((N_EXPERTS, N_TOK), F32),
                   jax.ShapeDtypeStruct((N_CHUNKS, SUBLANES, N_EXPERTS), F32),
                   jax.ShapeDtypeStruct((N_TOK, LANES), F32),
                   jax.ShapeDtypeStruct((N_TOK, LANES), BF16)),
        grid=(N_CHUNKS,),
        in_specs=[pl.BlockSpec((N_EXPERTS, CHUNK), lambda c: (0, c)),
                  pl.BlockSpec((N_EXPERTS, 1), lambda c: (0, 0))],
        out_specs=(pl.BlockSpec((N_EXPERTS, CHUNK), lambda c: (0, c)),
                   pl.BlockSpec((None, SUBLANES, N_EXPERTS), lambda c: (c, 0, 0)),
                   pl.BlockSpec((CHUNK, LANES), lambda c: (c, 0)),
                   pl.BlockSpec((CHUNK, LANES), lambda c: (c, 0))),
        compiler_params=_cparams(("parallel",)),
        name="routing",
    )(logits_t, e_bias_col)


def _plan_kernel(cnt_ref, goff_ref, tail_ref, tile_e_ref, ntile_ref):
    def per_expert(e, carry):
        base, tbase = carry

        def per_chunk(c, off):
            goff_ref[c, e] = base + off
            return off + (((cnt_ref[c, e] + (GRANULE - 1)) >> GRANULE_SHIFT) << GRANULE_SHIFT)

        tot = lax.fori_loop(0, N_CHUNKS, per_chunk, jnp.int32(0))
        region = ((tot + (TILE - 1)) >> TILE_SHIFT) << TILE_SHIFT
        tail_ref[0, e] = base + tot
        tail_ref[1, e] = (region - tot) >> GRANULE_SHIFT
        n_t = region >> TILE_SHIFT

        def per_tile(j, _):
            tile_e_ref[tbase + j] = e
            return 0

        lax.fori_loop(0, n_t, per_tile, 0)
        return base + region, tbase + n_t

    _, n_tiles = lax.fori_loop(0, N_EXPERTS, per_expert, (jnp.int32(0), jnp.int32(0)))
    ntile_ref[0] = n_tiles
    last_e = tile_e_ref[jnp.maximum(n_tiles - 1, 0)]

    def fill(j, _):
        tile_e_ref[j] = last_e
        return 0

    lax.fori_loop(n_tiles, MAX_TILES, fill, 0)


def _plan(cnt):
    smem = pl.BlockSpec(memory_space=pltpu.SMEM)
    return pl.pallas_call(
        _plan_kernel,
        out_shape=(jax.ShapeDtypeStruct((N_CHUNKS, N_EXPERTS), jnp.int32),
                   jax.ShapeDtypeStruct((2, N_EXPERTS), jnp.int32),
                   jax.ShapeDtypeStruct((MAX_TILES,), jnp.int32),
                   jax.ShapeDtypeStruct((1,), jnp.int32)),
        in_specs=[smem],
        out_specs=(smem, smem, smem, smem),
        name="row_plan",
    )(cnt)


def _chunk_granules(cnt_ref, goff_ref, c, table_ref, per_granule=None):
    def per_expert(e, n_before):
        n_gran = (cnt_ref[c, e] + (GRANULE - 1)) >> GRANULE_SHIFT
        base = goff_ref[c, e]

        def per_j(j, _):
            table_ref[n_before + j] = base + j * GRANULE
            if per_granule is not None:
                per_granule(e, j, n_before + j)
            return 0

        lax.fori_loop(0, n_gran, per_j, 0)
        return n_before + n_gran

    return lax.fori_loop(0, N_EXPERTS, per_expert, jnp.int32(0))


def _dispatch_kernel(cnt_ref, goff_ref, tail_ref, x_ref, gx_ref, csm_ref, xs_hbm,
                     onehot_ref, table_ref, buf_ref, zero_ref, sems, zsem):
    c = pl.program_id(0)
    sub = lax.broadcasted_iota(jnp.int32, (GRANULE, CHUNK), 0)

    def build_granule(e, j, g):
        want = (sub + (j * GRANULE + 1)).astype(F32)
        hit = csm_ref[pl.ds(e, 1), :] == want
        onehot_ref[pl.ds(pl.multiple_of(g * GRANULE, GRANULE), GRANULE), :] = jnp.where(hit, 1.0, 0.0).astype(BF16)

    n_gran = _chunk_granules(cnt_ref, goff_ref, c, table_ref, build_granule)
    n_tiles = (n_gran + (GRAN_PER_TILE - 1)) >> GPT_SHIFT

    def clear(g, _):
        onehot_ref[pl.ds(pl.multiple_of(g * GRANULE, GRANULE), GRANULE), :] = jnp.zeros((GRANULE, CHUNK), BF16)
        return 0

    lax.fori_loop(n_gran, n_tiles * GRAN_PER_TILE, clear, 0)

    def granule_copy(slot, i, row):
        return pltpu.make_async_copy(buf_ref.at[slot, pl.ds(pl.multiple_of(i * GRANULE, GRANULE), GRANULE), :],
                                     xs_hbm.at[pl.ds(pl.multiple_of(row, GRANULE), GRANULE), :],
                                     sems.at[slot])

    def tile_granules(k):
        return jnp.minimum(GRAN_PER_TILE, n_gran - k * GRAN_PER_TILE)

    def wait_tile(k):
        slot = k % N_SLOTS

        def w(i, _):
            granule_copy(slot, i, 0).wait()
            return 0

        lax.fori_loop(0, tile_granules(k), w, 0)

    def per_tile(k, _):
        slot = k % N_SLOTS

        @pl.when(k >= N_SLOTS)
        def _():
            wait_tile(k - N_SLOTS)

        oh = onehot_ref[pl.ds(pl.multiple_of(k * TILE, TILE), TILE), :]
        buf_ref[slot, :, 0:D_MODEL] = jnp.dot(oh, x_ref[...], preferred_element_type=F32).astype(BF16)
        buf_ref[slot, :, D_MODEL:XS_WIDTH] = jnp.dot(oh, gx_ref[...], preferred_element_type=F32).astype(BF16)

        def s(i, _):
            granule_copy(slot, i, table_ref[k * GRAN_PER_TILE + i]).start()
            return 0

        lax.fori_loop(0, tile_granules(k), s, 0)
        return 0

    lax.fori_loop(0, n_tiles, per_tile, 0)

    def drain(k, _):
        wait_tile(k)
        return 0

    lax.fori_loop(jnp.maximum(n_tiles - N_SLOTS, 0), n_tiles, drain, 0)

    @pl.when(c == N_CHUNKS - 1)
    def _():
        zero_ref[...] = jnp.zeros_like(zero_ref)

        def tail_copy(row):
            return pltpu.make_async_copy(zero_ref, xs_hbm.at[pl.ds(pl.multiple_of(row, GRANULE), GRANULE), :], zsem)

        def start_e(e, _):
            def st(j, _):
                tail_copy(tail_ref[0, e] + j * GRANULE).start()
                return 0
            lax.fori_loop(0, tail_ref[1, e], st, 0)
            return 0

        def wait_e(e, _):
            def wt(j, _):
                tail_copy(0).wait()
                return 0
            lax.fori_loop(0, tail_ref[1, e], wt, 0)
            return 0

        lax.fori_loop(0, N_EXPERTS, start_e, 0)
        lax.fori_loop(0, N_EXPERTS, wait_e, 0)


def _dispatch(cnt, goff, tail, h2, gx, csm):
    grid_spec = pltpu.PrefetchScalarGridSpec(
        num_scalar_prefetch=3,
        grid=(N_CHUNKS,),
        in_specs=[pl.BlockSpec((CHUNK, D_MODEL), lambda c, *_: (c, 0)),
                  pl.BlockSpec((CHUNK, LANES), lambda c, *_: (c, 0)),
                  pl.BlockSpec((N_EXPERTS, CHUNK), lambda c, *_: (0, c))],
        out_specs=pl.BlockSpec(memory_space=pl.ANY),
        scratch_shapes=[pltpu.VMEM((CHUNK_ROWS, CHUNK), BF16),
                        pltpu.SMEM((CHUNK_ROWS // GRANULE,), jnp.int32),
                        pltpu.VMEM((N_SLOTS, TILE, XS_WIDTH), BF16),
                        pltpu.VMEM((GRANULE, XS_WIDTH), BF16),
                        pltpu.SemaphoreType.DMA((N_SLOTS,)),
                        pltpu.SemaphoreType.DMA(())])
    return pl.pallas_call(
        _dispatch_kernel,
        out_shape=jax.ShapeDtypeStruct((MAX_ROWS, XS_WIDTH), BF16),
        grid_spec=grid_spec,
        compiler_params=_cparams(("arbitrary",)),
        name="moe_dispatch",
    )(cnt, goff, tail, h2, gx, csm)


def _routed_kernel(tile_e_ref, ntile_ref, xs_ref, w1_ref, w3_ref, w2_ref, ys_ref, w1s_ref, w3s_ref, w2s_ref):
    i = pl.program_id(0)
    e = tile_e_ref[i]

    @pl.when(i < ntile_ref[0])
    def _():
        @pl.when((i == 0) | (e != tile_e_ref[jnp.maximum(i - 1, 0)]))
        def _():
            w1s_ref[...] = w1_ref[...].astype(BF16)
            w3s_ref[...] = w3_ref[...].astype(BF16)
            w2s_ref[...] = w2_ref[...].astype(BF16)

        x = xs_ref[:, 0:D_MODEL]
        gx = xs_ref[:, D_MODEL:XS_WIDTH].astype(F32)
        lane = lax.broadcasted_iota(jnp.int32, gx.shape, 1)
        gate = jnp.sum(jnp.where((lane == e) | (lane == e + N_EXPERTS), gx, 0.0), axis=1, keepdims=True)
        a = jnp.dot(x, w1s_ref[...], preferred_element_type=F32)
        u = jnp.dot(x, w3s_ref[...], preferred_element_type=F32)
        hid = (a * jax.nn.sigmoid(a)) * u * gate
        ys_ref[...] = jnp.dot(hid.astype(BF16), w2s_ref[...], preferred_element_type=F32).astype(ys_ref.dtype)


def _routed_experts(tile_e, ntile, xs, w1, w3, w2):
    def row_map(i, tile_e_ref, ntile_ref):
        return (jnp.minimum(i, jnp.maximum(ntile_ref[0] - 1, 0)), 0)

    def w_map(i, tile_e_ref, ntile_ref):
        return (tile_e_ref[i], 0, 0)

    grid_spec = pltpu.PrefetchScalarGridSpec(
        num_scalar_prefetch=2,
        grid=(MAX_TILES,),
        in_specs=[pl.BlockSpec((TILE, XS_WIDTH), row_map),
                  pl.BlockSpec((None, D_MODEL, D_EXPERT), w_map),
                  pl.BlockSpec((None, D_MODEL, D_EXPERT), w_map),
                  pl.BlockSpec((None, D_EXPERT, D_MODEL), w_map)],
        out_specs=pl.BlockSpec((TILE, D_MODEL), row_map),
        scratch_shapes=[pltpu.VMEM((D_MODEL, D_EXPERT), BF16),
                        pltpu.VMEM((D_MODEL, D_EXPERT), BF16),
                        pltpu.VMEM((D_EXPERT, D_MODEL), BF16)])
    return pl.pallas_call(
        _routed_kernel,
        out_shape=jax.ShapeDtypeStruct((MAX_ROWS, D_MODEL), BF16),
        grid_spec=grid_spec,
        compiler_params=_cparams(("arbitrary",)),
        name="routed_experts",
    )(tile_e, ntile, xs, w1, w3, w2)


def _combine_kernel(cnt_ref, goff_ref, lp_ref, x1_ref, sh_ref, gt_ref, gf_ref, ys_hbm, o_ref,
                    table_ref, buf_ref, acc_ref, sems):
    c = pl.program_id(0)
    n_gran = _chunk_granules(cnt_ref, goff_ref, c, table_ref)
    n_tiles = (n_gran + (COMBINE_GPT - 1)) >> COMBINE_GPT_SHIFT

    def granule_copy(slot, i, row):
        return pltpu.make_async_copy(ys_hbm.at[pl.ds(pl.multiple_of(row, GRANULE), GRANULE), :],
                                     buf_ref.at[slot, pl.ds(pl.multiple_of(i * GRANULE, GRANULE), GRANULE), :],
                                     sems.at[slot])

    def tile_granules(k):
        return jnp.minimum(COMBINE_GPT, n_gran - k * COMBINE_GPT)

    def start_tile(k):
        slot = k % COMBINE_SLOTS

        def s(i, _):
            granule_copy(slot, i, table_ref[k * COMBINE_GPT + i]).start()
            return 0

        lax.fori_loop(0, tile_granules(k), s, 0)

    def wait_tile(k):
        slot = k % COMBINE_SLOTS

        def w(i, _):
            granule_copy(slot, i, 0).wait()
            return 0

        lax.fori_loop(0, tile_granules(k), w, 0)

    for k0 in range(COMBINE_SLOTS - 1):
        @pl.when(k0 < n_tiles)
        def _():
            start_tile(k0)

    acc_ref[...] = jnp.zeros_like(acc_ref)
    lane = lax.broadcasted_iota(jnp.int32, (CHUNK, COMBINE_TILE), 1)

    def per_tile(k, _):
        @pl.when(k + (COMBINE_SLOTS - 1) < n_tiles)
        def _():
            start_tile(k + (COMBINE_SLOTS - 1))

        wait_tile(k)
        slot = k % COMBINE_SLOTS

        def clear(i, _):
            buf_ref[slot, pl.ds(pl.multiple_of(i * GRANULE, GRANULE), GRANULE), :] = jnp.zeros((GRANULE, D_MODEL), BF16)
            return 0

        lax.fori_loop(tile_granules(k), COMBINE_GPT, clear, 0)

        want = (lane + k * COMBINE_TILE).astype(F32)
        onehot = jnp.zeros((CHUNK, COMBINE_TILE), F32)
        for j in range(TOP_K):
            onehot = jnp.where(lp_ref[:, j:j + 1] == want, 1.0, onehot)
        acc_ref[...] += jnp.dot(onehot.astype(BF16), buf_ref[slot], preferred_element_type=F32)
        return 0

    lax.fori_loop(0, n_tiles, per_tile, 0)

    x = x1_ref[...] + gt_ref[...] * (acc_ref[...] + sh_ref[...])
    o_ref[...] = x * lax.rsqrt(jnp.mean(x * x, axis=-1, keepdims=True) + EPS) * gf_ref[...]


def _combine(cnt, goff, lp, x1, shared, mod4, g_final, ys):
    per_b = SEQ // CHUNK
    row = pl.BlockSpec((CHUNK, D_MODEL), lambda c, *_: (c, 0))
    grid_spec = pltpu.PrefetchScalarGridSpec(
        num_scalar_prefetch=2,
        grid=(N_CHUNKS,),
        in_specs=[pl.BlockSpec((CHUNK, LANES), lambda c, *_: (c, 0)),
                  row, row,
                  pl.BlockSpec((None, None, 1, D_MODEL), lambda c, *_: (c // per_b, 5, 0, 0)),
                  pl.BlockSpec((1, D_MODEL), lambda c, *_: (0, 0)),
                  pl.BlockSpec(memory_space=pl.ANY)],
        out_specs=row,
        scratch_shapes=[pltpu.SMEM((CHUNK_ROWS // GRANULE,), jnp.int32),
                        pltpu.VMEM((COMBINE_SLOTS, COMBINE_TILE, D_MODEL), BF16),
                        pltpu.VMEM((CHUNK, D_MODEL), F32),
                        pltpu.SemaphoreType.DMA((COMBINE_SLOTS,))])
    return pl.pallas_call(
        _combine_kernel,
        out_shape=jax.ShapeDtypeStruct((N_TOK, D_MODEL), F32),
        grid_spec=grid_spec,
        compiler_params=_cparams(("arbitrary",)),
        name="moe_combine",
    )(cnt, goff, lp, x1, shared, mod4, g_final, ys)


def _shared_kernel(x_ref, w1_ref, w3_ref, w2_ref, o_ref):
    x = x_ref[...]
    a = jnp.dot(x, w1_ref[...], preferred_element_type=F32)
    u = jnp.dot(x, w3_ref[...], preferred_element_type=F32)
    hid = (a * jax.nn.sigmoid(a)) * u
    o_ref[...] = jnp.dot(hid.astype(BF16), w2_ref[...], preferred_element_type=F32).astype(o_ref.dtype)


def _shared_expert(h2, w1, w3, w2):
    tm = 1024
    return pl.pallas_call(
        _shared_kernel,
        out_shape=jax.ShapeDtypeStruct((N_TOK, D_MODEL), F32),
        grid=(N_TOK // tm,),
        in_specs=[pl.BlockSpec((tm, D_MODEL), lambda m: (m, 0)),
                  pl.BlockSpec((D_MODEL, D_SHARED), lambda m: (0, 0)),
                  pl.BlockSpec((D_MODEL, D_SHARED), lambda m: (0, 0)),
                  pl.BlockSpec((D_SHARED, D_MODEL), lambda m: (0, 0))],
        out_specs=pl.BlockSpec((tm, D_MODEL), lambda m: (m, 0)),
        compiler_params=_cparams(("parallel",)),
        name="shared_expert",
    )(h2, w1, w3, w2)


def _rope_angles(pos, dims, theta):
    inv = jnp.power(jnp.float32(theta), -jnp.arange(0, dims, 2, dtype=jnp.float32) / dims)
    return pos.astype(jnp.float32)[:, None] * inv[None, :]


def _rotary_tables(angle_blocks):
    cos_parts, lo_parts, hi_parts = [], [], []
    used = 0
    for ang in angle_blocks:
        c, s = jnp.cos(ang), jnp.sin(ang)
        z = jnp.zeros_like(s)
        cos_parts += [c, c]
        lo_parts += [-s, z]
        hi_parts += [z, s]
        used += 2 * ang.shape[1]
    rest = HEAD_DIM - used
    if rest:
        cos_parts.append(jnp.ones((SEQ, rest), F32))
        lo_parts.append(jnp.zeros((SEQ, rest), F32))
        hi_parts.append(jnp.zeros((SEQ, rest), F32))
    return (jnp.concatenate(cos_parts, axis=1), jnp.concatenate(lo_parts, axis=1),
            jnp.concatenate(hi_parts, axis=1))


def kernel(x, c, w_ada, b_ada, g_attn, w_in, b_gate, q_norm_g, k_norm_g, w_a_up, w_b_up, w_out,
           g_ffn, w_router, e_bias, w1, w3, w2, ws1, ws3, ws2, g_final):
    l = 0
    x2 = x.reshape(N_TOK, D_MODEL)
    pos = jnp.arange(SEQ)
    tabs_a = _rotary_tables([_rope_angles(pos, ROPE_DIMS, ROPE_THETA)])
    tabs_b = _rotary_tables([_rope_angles(pos // GRID_W, AXIAL_DIMS, AXIAL_THETA),
                             _rope_angles(pos % GRID_W, AXIAL_DIMS, AXIAL_THETA)])

    c_pad = jnp.zeros((SUBLANES, D_MODEL), F32).at[:BATCH].set(c)
    mod = _ada(c_pad, w_ada[l], b_ada[l].reshape(1, -1))
    mod4 = mod[:BATCH].reshape(BATCH, N_MOD, 1, D_MODEL)

    h = _norm_modulate(x2, g_attn[l].reshape(1, -1), mod4, 1, 0)
    proj = _projection(h, w_in[l].astype(BF16), tabs_a, tabs_b,
                       q_norm_g[l].reshape(1, -1), k_norm_g[l].reshape(1, -1), b_gate[l].reshape(1, -1))
    proj3 = proj.reshape(BATCH, SEQ, IN_COLS)

    o_groups, lse_groups = [], []
    for gi, (window, dilation) in enumerate(A_PATTERNS):
        o, lse = _attn_a(proj3, gi, window, dilation)
        o_groups.append(o)
        lse_groups.append(lse)
    mix_b = _attn_b(proj3)

    merged = _merge(o_groups, lse_groups, mix_b, proj, w_a_up[l].astype(BF16), w_b_up[l].astype(BF16))
    x1, h2, logits_t = _out_projection(merged, w_out[l].astype(BF16), x2, mod4, g_ffn[l].reshape(1, -1),
                                       w_router[l].T)
    csm, cnt_f, lp, gx = _route(logits_t, e_bias[l].reshape(-1, 1))
    cnt = cnt_f[:, 0, :].astype(jnp.int32)
    goff, tail, tile_e, ntile = _plan(cnt)
    xs = _dispatch(cnt, goff, tail, h2, gx, csm)
    ys = _routed_experts(tile_e, ntile, xs, w1[l], w3[l], w2[l])
    shared = _shared_expert(h2, ws1[l].astype(BF16), ws3[l].astype(BF16), ws2[l].astype(BF16))
    out = _combine(cnt, goff, lp, x1, shared, mod4, g_final.reshape(1, -1), ys)
    return out.reshape(BATCH, SEQ, D_MODEL)
```

```python
import functools

import jax
import jax.numpy as jnp
from jax import lax
from jax.experimental import pallas as pl
from jax.experimental.pallas import tpu as pltpu

F32 = jnp.float32
BF16 = jnp.bfloat16

D_MODEL = 2048
BATCH = 2
SEQ = 4096
N_TOK = BATCH * SEQ
HEAD_DIM = 128
EPS = 1e-6
A_PATTERNS = ((128, 1), (512, 4), (2048, 16))
A_N_GROUPS = len(A_PATTERNS)
A_HEADS = 4
A_WIDTH = A_HEADS * HEAD_DIM
ROPE_THETA = 500000.0
ROPE_DIMS = HEAD_DIM // 4
B_Q_HEADS = 16
B_KV_HEADS = 4
B_GROUP = B_Q_HEADS // B_KV_HEADS
B_Q_WIDTH = B_Q_HEADS * HEAD_DIM
B_KV_WIDTH = B_KV_HEADS * HEAD_DIM
AXIAL_THETA = 10000.0
AXIAL_DIMS = HEAD_DIM // 2
GRID_W = 64
A_COLS = A_N_GROUPS * 3 * A_WIDTH
B_COLS = B_Q_WIDTH + 2 * B_KV_WIDTH
GATE_COLS = 2 * D_MODEL
IN_COLS = A_COLS + B_COLS + GATE_COLS
N_MOD = 6
N_EXPERTS = 64
N_EXPERT_GROUPS = 8
PER_GROUP = N_EXPERTS // N_EXPERT_GROUPS
TOPK_GROUPS = 4
TOP_K = 8
D_EXPERT = D_MODEL // 4
D_SHARED = D_MODEL // 4
ROUTED_SCALE = 2.5
NEG = -1e30
ATTN_SCALE = HEAD_DIM ** -0.5
LOG2E = 1.4426950408889634

LANES = 128
SUBLANES = 8
VMEM_LIMIT = 56 * 1024 * 1024

COL_BLK = 512
N_COL_BLKS = IN_COLS // COL_BLK
A_BLKS = A_COLS // COL_BLK
BQ_BLK0 = A_BLKS
BK_BLK = BQ_BLK0 + B_Q_WIDTH // COL_BLK
BV_BLK = BK_BLK + 1
GATE_BLK0 = BV_BLK + 1

CHUNK = 512
N_CHUNKS = N_TOK // CHUNK
GRANULE = 16
GRANULE_SHIFT = GRANULE.bit_length() - 1
TILE = 256
TILE_SHIFT = TILE.bit_length() - 1
GRAN_PER_TILE = TILE // GRANULE
GPT_SHIFT = TILE_SHIFT - GRANULE_SHIFT
N_SLOTS = 4
COMBINE_TILE = 1024
COMBINE_GPT = COMBINE_TILE // GRANULE
COMBINE_GPT_SHIFT = COMBINE_GPT.bit_length() - 1
COMBINE_SLOTS = 3
XS_WIDTH = D_MODEL + LANES
CHUNK_ROWS = -(-(CHUNK * TOP_K + N_EXPERTS * (GRANULE - 1)) // TILE) * TILE
MAX_ROWS = -(-(N_TOK * TOP_K + N_EXPERTS * N_CHUNKS * (GRANULE - 1) + N_EXPERTS * (TILE - 1)) // TILE) * TILE
MAX_TILES = MAX_ROWS // TILE


def _cparams(semantics):
    return pltpu.CompilerParams(dimension_semantics=semantics, vmem_limit_bytes=VMEM_LIMIT)


def _ada_kernel(c_ref, w_ref, b_ref, o_ref):
    c = c_ref[...]
    act = c * jax.nn.sigmoid(c)
    o_ref[...] = jnp.dot(act, w_ref[...], preferred_element_type=F32,
                         precision=lax.Precision.HIGHEST) + b_ref[...]


def _ada(c_pad, w_ada, b_ada):
    tn = 1024
    n_out = w_ada.shape[1]
    return pl.pallas_call(
        _ada_kernel,
        out_shape=jax.ShapeDtypeStruct((SUBLANES, n_out), F32),
        grid=(n_out // tn,),
        in_specs=[pl.BlockSpec((SUBLANES, D_MODEL), lambda n: (0, 0)),
                  pl.BlockSpec((D_MODEL, tn), lambda n: (0, n)),
                  pl.BlockSpec((1, tn), lambda n: (0, n))],
        out_specs=pl.BlockSpec((SUBLANES, tn), lambda n: (0, n)),
        compiler_params=_cparams(("parallel",)),
        name="ada_mod",
    )(c_pad, w_ada, b_ada)


def _normmod(x, g, sc, sh):
    y = x * lax.rsqrt(jnp.mean(x * x, axis=-1, keepdims=True) + EPS)
    return (y * g) * (1.0 + sc) + sh


def _normmod_kernel(x_ref, g_ref, sc_ref, sh_ref, o_ref):
    o_ref[...] = _normmod(x_ref[...], g_ref[...], sc_ref[...], sh_ref[...]).astype(o_ref.dtype)


def _norm_modulate(x2, g, mod4, sc_idx, sh_idx):
    tm = 512
    per_b = SEQ // tm
    return pl.pallas_call(
        _normmod_kernel,
        out_shape=jax.ShapeDtypeStruct((N_TOK, D_MODEL), BF16),
        grid=(N_TOK // tm,),
        in_specs=[pl.BlockSpec((tm, D_MODEL), lambda m: (m, 0)),
                  pl.BlockSpec((1, D_MODEL), lambda m: (0, 0)),
                  pl.BlockSpec((None, None, 1, D_MODEL), lambda m: (m // per_b, sc_idx, 0, 0)),
                  pl.BlockSpec((None, None, 1, D_MODEL), lambda m: (m // per_b, sh_idx, 0, 0))],
        out_specs=pl.BlockSpec((tm, D_MODEL), lambda m: (m, 0)),
        compiler_params=_cparams(("parallel",)),
        name="norm_modulate",
    )(x2, g, mod4, mod4)


def _tile4(t):
    return jnp.concatenate([t, t, t, t], axis=1)


def _rotary(y, tabs, rows, shift):
    cos_ref, sin_lo_ref, sin_hi_ref = tabs
    width = y.shape[1]
    return (y * _tile4(cos_ref[rows, :])
            + pltpu.roll(y, width - shift, 1) * _tile4(sin_lo_ref[rows, :])
            + pltpu.roll(y, shift, 1) * _tile4(sin_hi_ref[rows, :]))


def _head_rmsnorm(y, g):
    outs = []
    for h in range(y.shape[1] // HEAD_DIM):
        yh = y[:, h * HEAD_DIM:(h + 1) * HEAD_DIM]
        outs.append(yh * lax.rsqrt(jnp.mean(yh * yh, axis=-1, keepdims=True) + EPS) * g)
    return jnp.concatenate(outs, axis=1)


def _proj_kernel(h_ref, w_ref, ca_ref, sal_ref, sah_ref, cb_ref, sbl_ref, sbh_ref,
                 qg_ref, kg_ref, bg_ref, o_ref):
    n = pl.program_id(1)
    is_a = n < A_BLKS
    part = n % 3
    tabs_a = (ca_ref, sal_ref, sah_ref)
    tabs_b = (cb_ref, sbl_ref, sbh_ref)
    sub_rows = 256

    def run(epilogue):
        w = w_ref[...].astype(BF16)
        for r in range(h_ref.shape[0] // sub_rows):
            rows = pl.ds(r * sub_rows, sub_rows)
            acc = jnp.dot(h_ref[rows, :], w, preferred_element_type=F32)
            o_ref[rows, :] = epilogue(acc, rows).astype(o_ref.dtype)

    @pl.when(is_a & (part == 0))
    def _():
        run(lambda acc, rows: _rotary(acc, tabs_a, rows, ROPE_DIMS // 2) * ATTN_SCALE)

    @pl.when(is_a & (part == 1))
    def _():
        run(lambda acc, rows: _rotary(acc, tabs_a, rows, ROPE_DIMS // 2))

    @pl.when((is_a & (part == 2)) | (n == BV_BLK))
    def _():
        run(lambda acc, rows: acc)

    @pl.when((n >= BQ_BLK0) & (n < BK_BLK))
    def _():
        run(lambda acc, rows: _rotary(_head_rmsnorm(acc, qg_ref[...]), tabs_b, rows, AXIAL_DIMS // 2)
            * (ATTN_SCALE * LOG2E))

    @pl.when(n == BK_BLK)
    def _():
        run(lambda acc, rows: _rotary(_head_rmsnorm(acc, kg_ref[...]), tabs_b, rows, AXIAL_DIMS // 2))

    @pl.when(n >= GATE_BLK0)
    def _():
        run(lambda acc, rows: jax.nn.sigmoid(acc + bg_ref[...]))


def _projection(h, w_in, tabs_a, tabs_b, qg, kg, b_gate):
    tm = 1024
    per_b = SEQ // tm
    tab_spec = pl.BlockSpec((tm, LANES), lambda m, n: (m % per_b, 0))
    vec_spec = pl.BlockSpec((1, HEAD_DIM), lambda m, n: (0, 0))
    n_gate_blks = GATE_COLS // COL_BLK
    return pl.pallas_call(
        _proj_kernel,
        out_shape=jax.ShapeDtypeStruct((N_TOK, IN_COLS), BF16),
        grid=(N_TOK // tm, N_COL_BLKS),
        in_specs=[pl.BlockSpec((tm, D_MODEL), lambda m, n: (m, 0)),
                  pl.BlockSpec((D_MODEL, COL_BLK), lambda m, n: (0, n)),
                  tab_spec, tab_spec, tab_spec, tab_spec, tab_spec, tab_spec,
                  vec_spec, vec_spec,
                  pl.BlockSpec((1, COL_BLK),
                               lambda m, n: (0, jnp.clip(n - GATE_BLK0, 0, n_gate_blks - 1)))],
        out_specs=pl.BlockSpec((tm, COL_BLK), lambda m, n: (m, n)),
        compiler_params=_cparams(("parallel", "arbitrary")),
        name="in_projection",
    )(h, w_in, *tabs_a, *tabs_b, qg, kg, b_gate)


def _attn_a_kernel(q_ref, k_ref, v_ref, o_ref, lse_ref, *, dilation, reach):
    i = pl.program_id(1)
    tq = q_ref.shape[0]
    win = tq + 2 * reach
    start = pl.multiple_of(jnp.clip(i * tq - reach, 0, SEQ - win), GRANULE)
    q = q_ref[...]
    k = k_ref[pl.ds(start, win), :]
    v = v_ref[pl.ds(start, win), :]
    diff = (i * tq - start) + lax.broadcasted_iota(jnp.int32, (tq, win), 0) \
        - lax.broadcasted_iota(jnp.int32, (tq, win), 1)
    valid = (jnp.abs(diff) <= reach) & ((diff & (dilation - 1)) == 0)
    lane = lax.broadcasted_iota(jnp.int32, (tq, LANES), 1)
    lse_tile = jnp.zeros((tq, LANES), F32)
    outs = []
    for h in range(A_HEADS):
        sl = slice(h * HEAD_DIM, (h + 1) * HEAD_DIM)
        s = lax.dot_general(q[:, sl], k[:, sl], (((1,), (1,)), ((), ())), preferred_element_type=F32)
        s = jnp.where(valid, s, NEG)
        m = jnp.max(s, axis=-1, keepdims=True)
        p = jnp.exp(s - m)
        l = jnp.sum(p, axis=-1, keepdims=True)
        o = jnp.dot(p.astype(BF16), v[:, sl], preferred_element_type=F32)
        outs.append(o / l)
        lse_tile = jnp.where(lane == h, m + jnp.log(l), lse_tile)
    o_ref[...] = jnp.concatenate(outs, axis=1).astype(o_ref.dtype)
    lse_ref[...] = lse_tile


def _attn_a(proj3, group, window, dilation):
    assert dilation & (dilation - 1) == 0
    reach = window // 2
    tq = 256
    qb, kb, vb = 3 * group, 3 * group + 1, 3 * group + 2
    o, lse = pl.pallas_call(
        functools.partial(_attn_a_kernel, dilation=dilation, reach=reach),
        out_shape=(jax.ShapeDtypeStruct((BATCH, SEQ, A_WIDTH), BF16),
                   jax.ShapeDtypeStruct((BATCH, SEQ, LANES), F32)),
        grid=(BATCH, SEQ // tq),
        in_specs=[pl.BlockSpec((None, tq, COL_BLK), lambda b, i: (b, i, qb)),
                  pl.BlockSpec((None, SEQ, COL_BLK), lambda b, i: (b, 0, kb)),
                  pl.BlockSpec((None, SEQ, COL_BLK), lambda b, i: (b, 0, vb))],
        out_specs=(pl.BlockSpec((None, tq, A_WIDTH), lambda b, i: (b, i, 0)),
                   pl.BlockSpec((None, tq, LANES), lambda b, i: (b, i, 0))),
        compiler_params=_cparams(("parallel", "arbitrary")),
        name=f"dilated_attention_g{group}",
    )(proj3, proj3, proj3)
    return o.reshape(N_TOK, A_WIDTH), lse.reshape(N_TOK, LANES)


def _attn_b_kernel(q_ref, k_ref, v_ref, o_ref, vx_ref, acc_ref, *, tk):
    tq = q_ref.shape[0]
    n_chunks = SEQ // tk

    @pl.when(pl.program_id(2) == 0)
    def _():
        vx_ref[:, 0:HEAD_DIM] = v_ref[...]
        vx_ref[:, HEAD_DIM:2 * HEAD_DIM] = jnp.ones((SEQ, HEAD_DIM), BF16)

    q = q_ref[...]
    qs = jnp.concatenate([q[:, g * HEAD_DIM:(g + 1) * HEAD_DIM] for g in range(B_GROUP)], axis=0)
    acc_ref[...] = jnp.zeros_like(acc_ref)
    m = jnp.full((B_GROUP * tq, 1), -jnp.inf, F32)
    for c in range(n_chunks):
        keys = slice(c * tk, (c + 1) * tk)
        s = lax.dot_general(qs, k_ref[keys, :], (((1,), (1,)), ((), ())), preferred_element_type=F32)
        m_new = jnp.maximum(m, jnp.max(s, axis=-1, keepdims=True))
        p = jnp.exp2(s - m_new).astype(BF16)
        acc_ref[...] = jnp.exp2(m - m_new) * acc_ref[...] + jnp.dot(p, vx_ref[keys, :],
                                                                    preferred_element_type=F32)
        m = m_new
    o = acc_ref[:, 0:HEAD_DIM] / acc_ref[:, HEAD_DIM:2 * HEAD_DIM]
    o_ref[...] = jnp.concatenate([o[g * tq:(g + 1) * tq] for g in range(B_GROUP)], axis=1).astype(o_ref.dtype)


def _attn_b(proj3):
    tq = 128
    tk = 512
    kcol0 = BK_BLK * COL_BLK // HEAD_DIM
    vcol0 = BV_BLK * COL_BLK // HEAD_DIM
    o = pl.pallas_call(
        functools.partial(_attn_b_kernel, tk=tk),
        out_shape=jax.ShapeDtypeStruct((BATCH, SEQ, B_Q_WIDTH), BF16),
        grid=(BATCH, B_KV_HEADS, SEQ // tq),
        in_specs=[pl.BlockSpec((None, tq, COL_BLK), lambda b, h, i: (b, i, BQ_BLK0 + h)),
                  pl.BlockSpec((None, SEQ, HEAD_DIM), lambda b, h, i: (b, 0, kcol0 + h)),
                  pl.BlockSpec((None, SEQ, HEAD_DIM), lambda b, h, i: (b, 0, vcol0 + h))],
        out_specs=pl.BlockSpec((None, tq, COL_BLK), lambda b, h, i: (b, i, h)),
        scratch_shapes=[pltpu.VMEM((SEQ, 2 * HEAD_DIM), BF16),
                        pltpu.VMEM((B_GROUP * tq, 2 * HEAD_DIM), F32)],
        compiler_params=_cparams(("parallel", "parallel", "arbitrary")),
        name="gqa_attention",
    )(proj3, proj3, proj3)
    return o.reshape(N_TOK, B_Q_WIDTH)


def _merge_kernel(o0_ref, o1_ref, o2_ref, l0_ref, l1_ref, l2_ref, yb_ref, wa_ref, wb_ref, ga_ref, gb_ref,
                  out_ref, mix_ref):
    @pl.when(pl.program_id(1) == 0)
    def _():
        tm = o0_ref.shape[0]
        lses = (l0_ref[...], l1_ref[...], l2_ref[...])
        outs = (o0_ref[...], o1_ref[...], o2_ref[...])
        cols = []
        for h in range(A_HEADS):
            lh = [jnp.broadcast_to(l[:, h:h + 1], (tm, HEAD_DIM)) for l in lses]
            mx = jnp.maximum(jnp.maximum(lh[0], lh[1]), lh[2])
            e = [jnp.exp(v - mx) for v in lh]
            den = e[0] + e[1] + e[2]
            sl = slice(h * HEAD_DIM, (h + 1) * HEAD_DIM)
            cols.append(sum((e[g] / den) * outs[g][:, sl].astype(F32) for g in range(A_N_GROUPS)))
        mix_ref[...] = jnp.concatenate(cols, axis=1).astype(mix_ref.dtype)

    ya = jnp.dot(mix_ref[...], wa_ref[...], preferred_element_type=F32)
    yb = jnp.dot(yb_ref[...], wb_ref[...], preferred_element_type=F32)
    out_ref[...] = (ga_ref[...].astype(F32) * ya + gb_ref[...].astype(F32) * yb).astype(out_ref.dtype)


def _merge(o_groups, lse_groups, mix_b, proj, w_a_up, w_b_up):
    tm, tn = 512, 512
    n_blk = D_MODEL // tn
    o_spec = pl.BlockSpec((tm, A_WIDTH), lambda m, n: (m, 0))
    l_spec = pl.BlockSpec((tm, LANES), lambda m, n: (m, 0))
    return pl.pallas_call(
        _merge_kernel,
        out_shape=jax.ShapeDtypeStruct((N_TOK, D_MODEL), BF16),
        grid=(N_TOK // tm, n_blk),
        in_specs=[o_spec, o_spec, o_spec, l_spec, l_spec, l_spec,
                  pl.BlockSpec((tm, B_Q_WIDTH), lambda m, n: (m, 0)),
                  pl.BlockSpec((A_WIDTH, tn), lambda m, n: (0, n)),
                  pl.BlockSpec((B_Q_WIDTH, tn), lambda m, n: (0, n)),
                  pl.BlockSpec((tm, tn), lambda m, n: (m, GATE_BLK0 + n)),
                  pl.BlockSpec((tm, tn), lambda m, n: (m, GATE_BLK0 + n_blk + n))],
        out_specs=pl.BlockSpec((tm, tn), lambda m, n: (m, n)),
        scratch_shapes=[pltpu.VMEM((tm, A_WIDTH), BF16)],
        compiler_params=_cparams(("parallel", "arbitrary")),
        name="branch_merge",
    )(*o_groups, *lse_groups, mix_b, w_a_up, w_b_up, proj, proj)


def _outproj_kernel(mg_ref, w_ref, x_ref, gt_ref, g_ref, sc_ref, sh_ref, wr_ref, x1_ref, h2_ref, lg_ref):
    sub_rows = 256
    for r in range(mg_ref.shape[0] // sub_rows):
        rows = pl.ds(r * sub_rows, sub_rows)
        y = jnp.dot(mg_ref[rows, :], w_ref[...], preferred_element_type=F32)
        x1 = x_ref[rows, :] + gt_ref[...] * y
        x1_ref[rows, :] = x1
        h2 = _normmod(x1, g_ref[...], sc_ref[...], sh_ref[...])
        h2_ref[rows, :] = h2.astype(h2_ref.dtype)
        lg_ref[:, rows] = lax.dot_general(wr_ref[...], h2, (((1,), (1,)), ((), ())),
                                          preferred_element_type=F32, precision=lax.Precision.HIGHEST)


def _out_projection(merged, w_out, x2, mod4, g_ffn, w_router_t):
    tm = 512
    per_b = SEQ // tm

    def mod_spec(j):
        return pl.BlockSpec((None, None, 1, D_MODEL), lambda m: (m // per_b, j, 0, 0))

    return pl.pallas_call(
        _outproj_kernel,
        out_shape=(jax.ShapeDtypeStruct((N_TOK, D_MODEL), F32),
                   jax.ShapeDtypeStruct((N_TOK, D_MODEL), BF16),
                   jax.ShapeDtypeStruct((N_EXPERTS, N_TOK), F32)),
        grid=(N_TOK // tm,),
        in_specs=[pl.BlockSpec((tm, D_MODEL), lambda m: (m, 0)),
                  pl.BlockSpec((D_MODEL, D_MODEL), lambda m: (0, 0)),
                  pl.BlockSpec((tm, D_MODEL), lambda m: (m, 0)),
                  mod_spec(2),
                  pl.BlockSpec((1, D_MODEL), lambda m: (0, 0)),
                  mod_spec(4), mod_spec(3),
                  pl.BlockSpec((N_EXPERTS, D_MODEL), lambda m: (0, 0))],
        out_specs=(pl.BlockSpec((tm, D_MODEL), lambda m: (m, 0)),
                   pl.BlockSpec((tm, D_MODEL), lambda m: (m, 0)),
                   pl.BlockSpec((N_EXPERTS, tm), lambda m: (0, m))),
        compiler_params=_cparams(("parallel",)),
        name="out_projection",
    )(merged, w_out, x2, mod4, g_ffn, mod4, mod4, w_router_t)


def _route_kernel(lg_ref, eb_ref, csm_ref, cnt_ref, lp_ref, gx_ref):
    tt = lg_ref.shape[1]
    scores = jax.nn.sigmoid(lg_ref[...])
    sel = scores + eb_ref[...]
    neg_inf = -jnp.inf
    sel_g = [sel[PER_GROUP * g:PER_GROUP * (g + 1), :] for g in range(N_EXPERT_GROUPS)]
    grp = []
    for v in sel_g:
        m1 = jnp.max(v, axis=0, keepdims=True)
        is1 = v == m1
        n1 = jnp.sum(jnp.where(is1, 1.0, 0.0), axis=0, keepdims=True)
        rest = jnp.max(jnp.where(is1, neg_inf, v), axis=0, keepdims=True)
        grp.append(m1 + jnp.where(n1 >= 2.0, m1, rest))
    masked = []
    for g in range(N_EXPERT_GROUPS):
        rank = jnp.zeros((1, tt), F32)
        for g2 in range(N_EXPERT_GROUPS):
            if g2 != g:
                beats = (grp[g2] >= grp[g]) if g2 < g else (grp[g2] > grp[g])
                rank = rank + jnp.where(beats, 1.0, 0.0)
        keep = jnp.broadcast_to(rank < TOPK_GROUPS, (PER_GROUP, tt))
        masked.append(jnp.where(keep, sel_g[g], neg_inf))
    sub = lax.broadcasted_iota(jnp.int32, (PER_GROUP, tt), 0)
    ranks = [jnp.zeros((PER_GROUP, tt), F32) for _ in range(N_EXPERT_GROUPS)]
    for g2 in range(N_EXPERT_GROUPS):
        for m2 in range(PER_GROUP):
            vf = jnp.broadcast_to(masked[g2][m2:m2 + 1, :], (PER_GROUP, tt))
            for g in range(N_EXPERT_GROUPS):
                if g2 < g:
                    beats = vf >= masked[g]
                elif g2 > g:
                    beats = vf > masked[g]
                else:
                    beats = (vf > masked[g]) | ((vf == masked[g]) & (sub > m2))
                ranks[g] = ranks[g] + jnp.where(beats, 1.0, 0.0)
    picked = [ranks[g] < TOP_K for g in range(N_EXPERT_GROUPS)]
    chosen = [jnp.where(picked[g], scores[PER_GROUP * g:PER_GROUP * (g + 1), :], 0.0)
              for g in range(N_EXPERT_GROUPS)]
    total = chosen[0]
    for g in range(1, N_EXPERT_GROUPS):
        total = total + chosen[g]
    denom = jnp.sum(total, axis=0, keepdims=True)
    gate = jnp.concatenate([chosen[g] / denom * ROUTED_SCALE for g in range(N_EXPERT_GROUPS)], axis=0)
    mask = jnp.concatenate([jnp.where(picked[g], 1.0, 0.0) for g in range(N_EXPERT_GROUPS)], axis=0)
    mask_bf = mask.astype(BF16)

    upper = jnp.where(lax.broadcasted_iota(jnp.int32, (tt, tt), 0) <= lax.broadcasted_iota(jnp.int32, (tt, tt), 1),
                      1.0, 0.0).astype(BF16)
    cs = jnp.dot(mask_bf, upper, preferred_element_type=F32)
    csm_ref[...] = jnp.where(mask > 0.0, cs, 0.0)
    cnt_ref[...] = lax.dot_general(jnp.ones((SUBLANES, tt), BF16), mask_bf, (((1,), (1,)), ((), ())),
                                   preferred_element_type=F32)

    padded = jnp.ceil(cs[:, tt - 1:tt] / GRANULE) * GRANULE
    lower = jnp.where(lax.broadcasted_iota(jnp.int32, (N_EXPERTS, N_EXPERTS), 1)
                      < lax.broadcasted_iota(jnp.int32, (N_EXPERTS, N_EXPERTS), 0), 1.0, 0.0)
    first_row = jnp.dot(lower, jnp.broadcast_to(padded, (N_EXPERTS, LANES)), preferred_element_type=F32,
                        precision=lax.Precision.HIGHEST)[:, 0:1]
    row_of = first_row + cs - 1.0

    ordinal = jnp.dot(lower.astype(BF16), mask_bf, preferred_element_type=F32)
    sub = lax.broadcasted_iota(jnp.int32, (SUBLANES, tt), 0)
    rows8 = jnp.zeros((SUBLANES, tt), F32)
    for k in range(TOP_K):
        hit = (ordinal == k) & (mask > 0.0)
        row_k = jnp.sum(jnp.where(hit, row_of, 0.0), axis=0, keepdims=True)
        rows8 = jnp.where(sub == k, jnp.broadcast_to(row_k, (SUBLANES, tt)), rows8)
    lp_ref[...] = jnp.concatenate([rows8, jnp.zeros((LANES - SUBLANES, tt), F32)], axis=0).T

    hi = gate.astype(BF16).astype(F32)
    gx_ref[...] = jnp.concatenate([hi, gate - hi], axis=0).T.astype(BF16)


def _route(logits_t, e_bias_col):
    return pl.pallas_call(
        _route_kernel,
        out_shape=(jax.ShapeDtypeStruct((N_EXPERTS, N_TOK), F32),
                   jax.ShapeDtypeStruct((N_CHUNKS, SUBLANES, N_EXPERTS), F32),
                   jax.ShapeDtypeStruct((N_TOK, LANES), F32),
                   jax.ShapeDtypeStruct((N_TOK, LANES), BF16)),
        grid=(N_CHUNKS,),
        in_specs=[pl.BlockSpec((N_EXPERTS, CHUNK), lambda c: (0, c)),
                  pl.BlockSpec((N_EXPERTS, 1), lambda c: (0, 0))],
        out_specs=(pl.BlockSpec((N_EXPERTS, CHUNK), lambda c: (0, c)),
                   pl.BlockSpec((None, SUBLANES, N_EXPERTS), lambda c: (c, 0, 0)),
                   pl.BlockSpec((CHUNK, LANES), lambda c: (c, 0)),
                   pl.BlockSpec((CHUNK, LANES), lambda c: (c, 0))),
        compiler_params=_cparams(("parallel",)),
        name="routing",
    )(logits_t, e_bias_col)


def _plan_kernel(cnt_ref, goff_ref, tail_ref, tile_e_ref, ntile_ref, next_ref):
    def per_expert(e, carry):
        base, tbase, prev = carry

        def per_chunk(c, off):
            goff_ref[c, e] = base + off
            return off + (((cnt_ref[c, e] + (GRANULE - 1)) >> GRANULE_SHIFT) << GRANULE_SHIFT)

        tot = lax.fori_loop(0, N_CHUNKS, per_chunk, jnp.int32(0))
        region = ((tot + (TILE - 1)) >> TILE_SHIFT) << TILE_SHIFT
        tail_ref[0, e] = base + tot
        tail_ref[1, e] = (region - tot) >> GRANULE_SHIFT
        n_t = region >> TILE_SHIFT

        def per_tile(j, _):
            tile_e_ref[tbase + j] = e
            return 0

        lax.fori_loop(0, n_t, per_tile, 0)
        next_ref[e] = jnp.int32(-1)

        @pl.when((n_t > 0) & (prev >= 0))
        def _():
            next_ref[jnp.maximum(prev, 0)] = e

        return base + region, tbase + n_t, jnp.where(n_t > 0, e, prev)

    _, n_tiles, _ = lax.fori_loop(0, N_EXPERTS, per_expert, (jnp.int32(0), jnp.int32(0), jnp.int32(-1)))
    ntile_ref[0] = n_tiles
    last_e = tile_e_ref[jnp.maximum(n_tiles - 1, 0)]

    def fill(j, _):
        tile_e_ref[j] = last_e
        return 0

    lax.fori_loop(n_tiles, MAX_TILES, fill, 0)


def _plan(cnt):
    smem = pl.BlockSpec(memory_space=pltpu.SMEM)
    return pl.pallas_call(
        _plan_kernel,
        out_shape=(jax.ShapeDtypeStruct((N_CHUNKS, N_EXPERTS), jnp.int32),
                   jax.ShapeDtypeStruct((2, N_EXPERTS), jnp.int32),
                   jax.ShapeDtypeStruct((MAX_TILES,), jnp.int32),
                   jax.ShapeDtypeStruct((1,), jnp.int32),
                   jax.ShapeDtypeStruct((N_EXPERTS,), jnp.int32)),
        in_specs=[smem],
        out_specs=(smem, smem, smem, smem, smem),
        name="row_plan",
    )(cnt)


def _chunk_granules(cnt_ref, goff_ref, c, table_ref, per_granule=None):
    def per_expert(e, n_before):
        n_gran = (cnt_ref[c, e] + (GRANULE - 1)) >> GRANULE_SHIFT
        base = goff_ref[c, e]

        def per_j(j, _):
            table_ref[n_before + j] = base + j * GRANULE
            if per_granule is not None:
                per_granule(e, j, n_before + j)
            return 0

        lax.fori_loop(0, n_gran, per_j, 0)
        return n_before + n_gran

    return lax.fori_loop(0, N_EXPERTS, per_expert, jnp.int32(0))


def _dispatch_kernel(cnt_ref, goff_ref, tail_ref, x_ref, gx_ref, csm_ref, xs_hbm,
                     onehot_ref, table_ref, buf_ref, zero_ref, sems, zsem):
    c = pl.program_id(0)
    sub = lax.broadcasted_iota(jnp.int32, (GRANULE, CHUNK), 0)

    def build_granule(e, j, g):
        want = (sub + (j * GRANULE + 1)).astype(F32)
        hit = csm_ref[pl.ds(e, 1), :] == want
        onehot_ref[pl.ds(pl.multiple_of(g * GRANULE, GRANULE), GRANULE), :] = jnp.where(hit, 1.0, 0.0).astype(BF16)

    n_gran = _chunk_granules(cnt_ref, goff_ref, c, table_ref, build_granule)
    n_tiles = (n_gran + (GRAN_PER_TILE - 1)) >> GPT_SHIFT

    def clear(g, _):
        onehot_ref[pl.ds(pl.multiple_of(g * GRANULE, GRANULE), GRANULE), :] = jnp.zeros((GRANULE, CHUNK), BF16)
        return 0

    lax.fori_loop(n_gran, n_tiles * GRAN_PER_TILE, clear, 0)

    def granule_copy(slot, i, row):
        return pltpu.make_async_copy(buf_ref.at[slot, pl.ds(pl.multiple_of(i * GRANULE, GRANULE), GRANULE), :],
                                     xs_hbm.at[pl.ds(pl.multiple_of(row, GRANULE), GRANULE), :],
                                     sems.at[slot])

    def tile_granules(k):
        return jnp.minimum(GRAN_PER_TILE, n_gran - k * GRAN_PER_TILE)

    def wait_tile(k):
        slot = k % N_SLOTS

        def w(i, _):
            granule_copy(slot, i, 0).wait()
            return 0

        lax.fori_loop(0, tile_granules(k), w, 0)

    def per_tile(k, _):
        slot = k % N_SLOTS

        @pl.when(k >= N_SLOTS)
        def _():
            wait_tile(k - N_SLOTS)

        oh = onehot_ref[pl.ds(pl.multiple_of(k * TILE, TILE), TILE), :]
        buf_ref[slot, :, 0:D_MODEL] = jnp.dot(oh, x_ref[...], preferred_element_type=F32).astype(BF16)
        buf_ref[slot, :, D_MODEL:XS_WIDTH] = jnp.dot(oh, gx_ref[...], preferred_element_type=F32).astype(BF16)

        def s(i, _):
            granule_copy(slot, i, table_ref[k * GRAN_PER_TILE + i]).start()
            return 0

        lax.fori_loop(0, tile_granules(k), s, 0)
        return 0

    lax.fori_loop(0, n_tiles, per_tile, 0)

    def drain(k, _):
        wait_tile(k)
        return 0

    lax.fori_loop(jnp.maximum(n_tiles - N_SLOTS, 0), n_tiles, drain, 0)

    @pl.when(c == N_CHUNKS - 1)
    def _():
        zero_ref[...] = jnp.zeros_like(zero_ref)

        def tail_copy(row):
            return pltpu.make_async_copy(zero_ref, xs_hbm.at[pl.ds(pl.multiple_of(row, GRANULE), GRANULE), :], zsem)

        def start_e(e, _):
            def st(j, _):
                tail_copy(tail_ref[0, e] + j * GRANULE).start()
                return 0
            lax.fori_loop(0, tail_ref[1, e], st, 0)
            return 0

        def wait_e(e, _):
            def wt(j, _):
                tail_copy(0).wait()
                return 0
            lax.fori_loop(0, tail_ref[1, e], wt, 0)
            return 0

        lax.fori_loop(0, N_EXPERTS, start_e, 0)
        lax.fori_loop(0, N_EXPERTS, wait_e, 0)


def _dispatch(cnt, goff, tail, h2, gx, csm):
    grid_spec = pltpu.PrefetchScalarGridSpec(
        num_scalar_prefetch=3,
        grid=(N_CHUNKS,),
        in_specs=[pl.BlockSpec((CHUNK, D_MODEL), lambda c, *_: (c, 0)),
                  pl.BlockSpec((CHUNK, LANES), lambda c, *_: (c, 0)),
                  pl.BlockSpec((N_EXPERTS, CHUNK), lambda c, *_: (0, c))],
        out_specs=pl.BlockSpec(memory_space=pl.ANY),
        scratch_shapes=[pltpu.VMEM((CHUNK_ROWS, CHUNK), BF16),
                        pltpu.SMEM((CHUNK_ROWS // GRANULE,), jnp.int32),
                        pltpu.VMEM((N_SLOTS, TILE, XS_WIDTH), BF16),
                        pltpu.VMEM((GRANULE, XS_WIDTH), BF16),
                        pltpu.SemaphoreType.DMA((N_SLOTS,)),
                        pltpu.SemaphoreType.DMA(())])
    return pl.pallas_call(
        _dispatch_kernel,
        out_shape=jax.ShapeDtypeStruct((MAX_ROWS, XS_WIDTH), BF16),
        grid_spec=grid_spec,
        compiler_params=_cparams(("arbitrary",)),
        name="moe_dispatch",
    )(cnt, goff, tail, h2, gx, csm)


def _routed_kernel(tile_e_ref, ntile_ref, next_ref, xs_ref, w1_hbm, w3_hbm, w2_hbm, ys_ref,
                   w1f_ref, w3f_ref, w2f_ref, w1s_ref, w3s_ref, w2s_ref, ord_ref, sems):
    i = pl.program_id(0)
    e = tile_e_ref[i]

    def fetch(expert, slot):
        return (pltpu.make_async_copy(w1_hbm.at[expert], w1f_ref.at[slot], sems.at[slot]),
                pltpu.make_async_copy(w3_hbm.at[expert], w3f_ref.at[slot], sems.at[slot]),
                pltpu.make_async_copy(w2_hbm.at[expert], w2f_ref.at[slot], sems.at[slot]))

    @pl.when(i < ntile_ref[0])
    def _():
        @pl.when((i == 0) | (e != tile_e_ref[jnp.maximum(i - 1, 0)]))
        def _():
            @pl.when(i == 0)
            def _():
                ord_ref[0] = 0
                for cp in fetch(e, 0):
                    cp.start()

            slot = ord_ref[0] & 1
            nxt = next_ref[e]

            @pl.when(nxt >= 0)
            def _():
                for cp in fetch(jnp.maximum(nxt, 0), 1 - slot):
                    cp.start()

            for cp in fetch(e, slot):
                cp.wait()
            w1s_ref[...] = w1f_ref[slot].astype(BF16)
            w3s_ref[...] = w3f_ref[slot].astype(BF16)
            w2s_ref[...] = w2f_ref[slot].astype(BF16)
            ord_ref[0] = ord_ref[0] + 1

        x = xs_ref[:, 0:D_MODEL]
        gx = xs_ref[:, D_MODEL:XS_WIDTH].astype(F32)
        lane = lax.broadcasted_iota(jnp.int32, gx.shape, 1)
        gate = jnp.sum(jnp.where((lane == e) | (lane == e + N_EXPERTS), gx, 0.0), axis=1, keepdims=True)
        a = jnp.dot(x, w1s_ref[...], preferred_element_type=F32)
        u = jnp.dot(x, w3s_ref[...], preferred_element_type=F32)
        hid = (a * jax.nn.sigmoid(a)) * u * gate
        ys_ref[...] = jnp.dot(hid.astype(BF16), w2s_ref[...], preferred_element_type=F32).astype(ys_ref.dtype)


def _routed_experts(tile_e, ntile, next_e, xs, w1, w3, w2):
    def row_map(i, tile_e_ref, ntile_ref, next_ref):
        return (jnp.minimum(i, jnp.maximum(ntile_ref[0] - 1, 0)), 0)

    hbm = pl.BlockSpec(memory_space=pl.ANY)
    grid_spec = pltpu.PrefetchScalarGridSpec(
        num_scalar_prefetch=3,
        grid=(MAX_TILES,),
        in_specs=[pl.BlockSpec((TILE, XS_WIDTH), row_map), hbm, hbm, hbm],
        out_specs=pl.BlockSpec((TILE, D_MODEL), row_map),
        scratch_shapes=[pltpu.VMEM((2, D_MODEL, D_EXPERT), F32),
                        pltpu.VMEM((2, D_MODEL, D_EXPERT), F32),
                        pltpu.VMEM((2, D_EXPERT, D_MODEL), F32),
                        pltpu.VMEM((D_MODEL, D_EXPERT), BF16),
                        pltpu.VMEM((D_MODEL, D_EXPERT), BF16),
                        pltpu.VMEM((D_EXPERT, D_MODEL), BF16),
                        pltpu.SMEM((1,), jnp.int32),
                        pltpu.SemaphoreType.DMA((2,))])
    return pl.pallas_call(
        _routed_kernel,
        out_shape=jax.ShapeDtypeStruct((MAX_ROWS, D_MODEL), BF16),
        grid_spec=grid_spec,
        compiler_params=_cparams(("arbitrary",)),
        name="routed_experts",
    )(tile_e, ntile, next_e, xs, w1, w3, w2)


def _combine_kernel(cnt_ref, goff_ref, lp_ref, x1_ref, sh_ref, gt_ref, gf_ref, ys_hbm, o_ref,
                    table_ref, buf_ref, acc_ref, sems):
    c = pl.program_id(0)
    n_gran = _chunk_granules(cnt_ref, goff_ref, c, table_ref)
    n_tiles = (n_gran + (COMBINE_GPT - 1)) >> COMBINE_GPT_SHIFT

    def granule_copy(slot, i, row):
        return pltpu.make_async_copy(ys_hbm.at[pl.ds(pl.multiple_of(row, GRANULE), GRANULE), :],
                                     buf_ref.at[slot, pl.ds(pl.multiple_of(i * GRANULE, GRANULE), GRANULE), :],
                                     sems.at[slot])

    def tile_granules(k):
        return jnp.minimum(COMBINE_GPT, n_gran - k * COMBINE_GPT)

    def start_tile(k):
        slot = k % COMBINE_SLOTS

        def s(i, _):
            granule_copy(slot, i, table_ref[k * COMBINE_GPT + i]).start()
            return 0

        lax.fori_loop(0, tile_granules(k), s, 0)

    def wait_tile(k):
        slot = k % COMBINE_SLOTS

        def w(i, _):
            granule_copy(slot, i, 0).wait()
            return 0

        lax.fori_loop(0, tile_granules(k), w, 0)

    for k0 in range(COMBINE_SLOTS - 1):
        @pl.when(k0 < n_tiles)
        def _():
            start_tile(k0)

    acc_ref[...] = jnp.zeros_like(acc_ref)
    lane = lax.broadcasted_iota(jnp.int32, (CHUNK, COMBINE_TILE), 1)

    def per_tile(k, _):
        @pl.when(k + (COMBINE_SLOTS - 1) < n_tiles)
        def _():
            start_tile(k + (COMBINE_SLOTS - 1))

        wait_tile(k)
        slot = k % COMBINE_SLOTS

        def clear(i, _):
            buf_ref[slot, pl.ds(pl.multiple_of(i * GRANULE, GRANULE), GRANULE), :] = jnp.zeros((GRANULE, D_MODEL), BF16)
            return 0

        lax.fori_loop(tile_granules(k), COMBINE_GPT, clear, 0)

        want = (lane + k * COMBINE_TILE).astype(F32)
        onehot = jnp.zeros((CHUNK, COMBINE_TILE), F32)
        for j in range(TOP_K):
            onehot = jnp.where(lp_ref[:, j:j + 1] == want, 1.0, onehot)
        acc_ref[...] += jnp.dot(onehot.astype(BF16), buf_ref[slot], preferred_element_type=F32)
        return 0

    lax.fori_loop(0, n_tiles, per_tile, 0)

    x = x1_ref[...] + gt_ref[...] * (acc_ref[...] + sh_ref[...])
    o_ref[...] = x * lax.rsqrt(jnp.mean(x * x, axis=-1, keepdims=True) + EPS) * gf_ref[...]


def _combine(cnt, goff, lp, x1, shared, mod4, g_final, ys):
    per_b = SEQ // CHUNK
    row = pl.BlockSpec((CHUNK, D_MODEL), lambda c, *_: (c, 0))
    grid_spec = pltpu.PrefetchScalarGridSpec(
        num_scalar_prefetch=2,
        grid=(N_CHUNKS,),
        in_specs=[pl.BlockSpec((CHUNK, LANES), lambda c, *_: (c, 0)),
                  row, row,
                  pl.BlockSpec((None, None, 1, D_MODEL), lambda c, *_: (c // per_b, 5, 0, 0)),
                  pl.BlockSpec((1, D_MODEL), lambda c, *_: (0, 0)),
                  pl.BlockSpec(memory_space=pl.ANY)],
        out_specs=row,
        scratch_shapes=[pltpu.SMEM((CHUNK_ROWS // GRANULE,), jnp.int32),
                        pltpu.VMEM((COMBINE_SLOTS, COMBINE_TILE, D_MODEL), BF16),
                        pltpu.VMEM((CHUNK, D_MODEL), F32),
                        pltpu.SemaphoreType.DMA((COMBINE_SLOTS,))])
    return pl.pallas_call(
        _combine_kernel,
        out_shape=jax.ShapeDtypeStruct((N_TOK, D_MODEL), F32),
        grid_spec=grid_spec,
        compiler_params=_cparams(("arbitrary",)),
        name="moe_combine",
    )(cnt, goff, lp, x1, shared, mod4, g_final, ys)


def _shared_kernel(x_ref, w1_ref, w3_ref, w2_ref, o_ref):
    x = x_ref[...]
    a = jnp.dot(x, w1_ref[...], preferred_element_type=F32)
    u = jnp.dot(x, w3_ref[...], preferred_element_type=F32)
    hid = (a * jax.nn.sigmoid(a)) * u
    o_ref[...] = jnp.dot(hid.astype(BF16), w2_ref[...], preferred_element_type=F32).astype(o_ref.dtype)


def _shared_expert(h2, w1, w3, w2):
    tm = 1024
    return pl.pallas_call(
        _shared_kernel,
        out_shape=jax.ShapeDtypeStruct((N_TOK, D_MODEL), F32),
        grid=(N_TOK // tm,),
        in_specs=[pl.BlockSpec((tm, D_MODEL), lambda m: (m, 0)),
                  pl.BlockSpec((D_MODEL, D_SHARED), lambda m: (0, 0)),
                  pl.BlockSpec((D_MODEL, D_SHARED), lambda m: (0, 0)),
                  pl.BlockSpec((D_SHARED, D_MODEL), lambda m: (0, 0))],
        out_specs=pl.BlockSpec((tm, D_MODEL), lambda m: (m, 0)),
        compiler_params=_cparams(("parallel",)),
        name="shared_expert",
    )(h2, w1, w3, w2)


def _rope_angles(pos, dims, theta):
    inv = jnp.power(jnp.float32(theta), -jnp.arange(0, dims, 2, dtype=jnp.float32) / dims)
    return pos.astype(jnp.float32)[:, None] * inv[None, :]


def _rotary_tables(angle_blocks):
    cos_parts, lo_parts, hi_parts = [], [], []
    used = 0
    for ang in angle_blocks:
        c, s = jnp.cos(ang), jnp.sin(ang)
        z = jnp.zeros_like(s)
        cos_parts += [c, c]
        lo_parts += [-s, z]
        hi_parts += [z, s]
        used += 2 * ang.shape[1]
    rest = HEAD_DIM - used
    if rest:
        cos_parts.append(jnp.ones((SEQ, rest), F32))
        lo_parts.append(jnp.zeros((SEQ, rest), F32))
        hi_parts.append(jnp.zeros((SEQ, rest), F32))
    return (jnp.concatenate(cos_parts, axis=1), jnp.concatenate(lo_parts, axis=1),
            jnp.concatenate(hi_parts, axis=1))


def kernel(x, c, w_ada, b_ada, g_attn, w_in, b_gate, q_norm_g, k_norm_g, w_a_up, w_b_up, w_out,
           g_ffn, w_router, e_bias, w1, w3, w2, ws1, ws3, ws2, g_final):
    l = 0
    x2 = x.reshape(N_TOK, D_MODEL)
    pos = jnp.arange(SEQ)
    tabs_a = _rotary_tables([_rope_angles(pos, ROPE_DIMS, ROPE_THETA)])
    tabs_b = _rotary_tables([_rope_angles(pos // GRID_W, AXIAL_DIMS, AXIAL_THETA),
                             _rope_angles(pos % GRID_W, AXIAL_DIMS, AXIAL_THETA)])

    c_pad = jnp.zeros((SUBLANES, D_MODEL), F32).at[:BATCH].set(c)
    mod = _ada(c_pad, w_ada[l], b_ada[l].reshape(1, -1))
    mod4 = mod[:BATCH].reshape(BATCH, N_MOD, 1, D_MODEL)

    h = _norm_modulate(x2, g_attn[l].reshape(1, -1), mod4, 1, 0)
    proj = _projection(h, w_in[l], tabs_a, tabs_b,
                       q_norm_g[l].reshape(1, -1), k_norm_g[l].reshape(1, -1), b_gate[l].reshape(1, -1))
    proj3 = proj.reshape(BATCH, SEQ, IN_COLS)

    o_groups, lse_groups = [], []
    for gi, (window, dilation) in enumerate(A_PATTERNS):
        o, lse = _attn_a(proj3, gi, window, dilation)
        o_groups.append(o)
        lse_groups.append(lse)
    mix_b = _attn_b(proj3)

    merged = _merge(o_groups, lse_groups, mix_b, proj, w_a_up[l].astype(BF16), w_b_up[l].astype(BF16))
    x1, h2, logits_t = _out_projection(merged, w_out[l].astype(BF16), x2, mod4, g_ffn[l].reshape(1, -1),
                                       w_router[l].T)
    csm, cnt_f, lp, gx = _route(logits_t, e_bias[l].reshape(-1, 1))
    cnt = cnt_f[:, 0, :].astype(jnp.int32)
    goff, tail, tile_e, ntile, next_e = _plan(cnt)
    xs = _dispatch(cnt, goff, tail, h2, gx, csm)
    ys = _routed_experts(tile_e, ntile, next_e, xs, w1[l], w3[l], w2[l])
    shared = _shared_expert(h2, ws1[l].astype(BF16), ws3[l].astype(BF16), ws2[l].astype(BF16))
    out = _combine(cnt, goff, lp, x1, shared, mod4, g_final.reshape(1, -1), ys)
    return out.reshape(BATCH, SEQ, D_MODEL)
```

```python
import functools

import jax
import jax.numpy as jnp
from jax import lax
from jax.experimental import pallas as pl
from jax.experimental.pallas import tpu as pltpu

F32 = jnp.float32
BF16 = jnp.bfloat16

D_MODEL = 2048
BATCH = 2
SEQ = 4096
N_TOK = BATCH * SEQ
HEAD_DIM = 128
EPS = 1e-6
A_PATTERNS = ((128, 1), (512, 4), (2048, 16))
A_N_GROUPS = len(A_PATTERNS)
A_HEADS = 4
A_WIDTH = A_HEADS * HEAD_DIM
ROPE_THETA = 500000.0
ROPE_DIMS = HEAD_DIM // 4
B_Q_HEADS = 16
B_KV_HEADS = 4
B_GROUP = B_Q_HEADS // B_KV_HEADS
B_Q_WIDTH = B_Q_HEADS * HEAD_DIM
B_KV_WIDTH = B_KV_HEADS * HEAD_DIM
AXIAL_THETA = 10000.0
AXIAL_DIMS = HEAD_DIM // 2
GRID_W = 64
A_COLS = A_N_GROUPS * 3 * A_WIDTH
B_COLS = B_Q_WIDTH + 2 * B_KV_WIDTH
GATE_COLS = 2 * D_MODEL
QKV_COLS = A_COLS + B_COLS
IN_COLS = QKV_COLS + GATE_COLS
N_MOD = 6
N_EXPERTS = 64
N_EXPERT_GROUPS = 8
PER_GROUP = N_EXPERTS // N_EXPERT_GROUPS
TOPK_GROUPS = 4
TOP_K = 8
D_EXPERT = D_MODEL // 4
D_SHARED = D_MODEL // 4
ROUTED_SCALE = 2.5
NEG = -1e30
ATTN_SCALE = HEAD_DIM ** -0.5
LOG2E = 1.4426950408889634
LN2 = 0.6931471805599453

LANES = 128
SUBLANES = 8
VMEM_LIMIT = 56 * 1024 * 1024

COL_BLK = 512
N_COL_BLKS = IN_COLS // COL_BLK
A_BLKS = A_COLS // COL_BLK
BQ_BLK0 = A_BLKS
BK_BLK = BQ_BLK0 + B_Q_WIDTH // COL_BLK
BV_BLK = BK_BLK + 1
GATE_BLK0 = BV_BLK + 1

CHUNK = 512
N_CHUNKS = N_TOK // CHUNK
GRANULE = 16
GRANULE_SHIFT = GRANULE.bit_length() - 1
TILE = 256
TILE_SHIFT = TILE.bit_length() - 1
GRAN_PER_TILE = TILE // GRANULE
GPT_SHIFT = TILE_SHIFT - GRANULE_SHIFT
N_SLOTS = 4
COMBINE_TILE = 1024
COMBINE_GPT = COMBINE_TILE // GRANULE
COMBINE_GPT_SHIFT = COMBINE_GPT.bit_length() - 1
COMBINE_SLOTS = 3
XS_WIDTH = D_MODEL + LANES
CHUNK_ROWS = -(-(CHUNK * TOP_K + N_EXPERTS * (GRANULE - 1)) // TILE) * TILE
MAX_ROWS = -(-(N_TOK * TOP_K + N_EXPERTS * N_CHUNKS * (GRANULE - 1) + N_EXPERTS * (TILE - 1)) // TILE) * TILE
MAX_TILES = MAX_ROWS // TILE


def _cparams(semantics):
    return pltpu.CompilerParams(dimension_semantics=semantics, vmem_limit_bytes=VMEM_LIMIT)


def _ada_kernel(c_ref, w_ref, b_ref, o_ref):
    c = c_ref[...]
    act = c * jax.nn.sigmoid(c)
    o_ref[...] = jnp.dot(act, w_ref[...], preferred_element_type=F32,
                         precision=lax.Precision.HIGHEST) + b_ref[...]


def _ada(c_pad, w_ada, b_ada):
    tn = 1024
    n_out = w_ada.shape[1]
    return pl.pallas_call(
        _ada_kernel,
        out_shape=jax.ShapeDtypeStruct((SUBLANES, n_out), F32),
        grid=(n_out // tn,),
        in_specs=[pl.BlockSpec((SUBLANES, D_MODEL), lambda n: (0, 0)),
                  pl.BlockSpec((D_MODEL, tn), lambda n: (0, n)),
                  pl.BlockSpec((1, tn), lambda n: (0, n))],
        out_specs=pl.BlockSpec((SUBLANES, tn), lambda n: (0, n)),
        compiler_params=_cparams(("parallel",)),
        name="ada_mod",
    )(c_pad, w_ada, b_ada)


def _normmod(x, g, sc, sh):
    y = x * lax.rsqrt(jnp.mean(x * x, axis=-1, keepdims=True) + EPS)
    return (y * g) * (1.0 + sc) + sh


def _normmod_kernel(x_ref, g_ref, sc_ref, sh_ref, o_ref):
    o_ref[...] = _normmod(x_ref[...], g_ref[...], sc_ref[...], sh_ref[...]).astype(o_ref.dtype)


def _norm_modulate(x2, g, mod4, sc_idx, sh_idx):
    tm = 512
    per_b = SEQ // tm
    return pl.pallas_call(
        _normmod_kernel,
        out_shape=jax.ShapeDtypeStruct((N_TOK, D_MODEL), BF16),
        grid=(N_TOK // tm,),
        in_specs=[pl.BlockSpec((tm, D_MODEL), lambda m: (m, 0)),
                  pl.BlockSpec((1, D_MODEL), lambda m: (0, 0)),
                  pl.BlockSpec((None, None, 1, D_MODEL), lambda m: (m // per_b, sc_idx, 0, 0)),
                  pl.BlockSpec((None, None, 1, D_MODEL), lambda m: (m // per_b, sh_idx, 0, 0))],
        out_specs=pl.BlockSpec((tm, D_MODEL), lambda m: (m, 0)),
        compiler_params=_cparams(("parallel",)),
        name="norm_modulate",
    )(x2, g, mod4, mod4)


def _tile4(t):
    return jnp.concatenate([t, t, t, t], axis=1)


def _rotary(y, tabs, rows, shift):
    cos_ref, sin_lo_ref, sin_hi_ref = tabs
    width = y.shape[1]
    return (y * _tile4(cos_ref[rows, :])
            + pltpu.roll(y, width - shift, 1) * _tile4(sin_lo_ref[rows, :])
            + pltpu.roll(y, shift, 1) * _tile4(sin_hi_ref[rows, :]))


def _head_rmsnorm(y, g):
    outs = []
    for h in range(y.shape[1] // HEAD_DIM):
        yh = y[:, h * HEAD_DIM:(h + 1) * HEAD_DIM]
        outs.append(yh * lax.rsqrt(jnp.mean(yh * yh, axis=-1, keepdims=True) + EPS) * g)
    return jnp.concatenate(outs, axis=1)


def _proj_kernel(h_ref, w_ref, ca_ref, sal_ref, sah_ref, cb_ref, sbl_ref, sbh_ref,
                 qg_ref, kg_ref, bg_ref, o_ref, gate_ref):
    n = pl.program_id(1)
    is_a = n < A_BLKS
    part = n % 3
    tabs_a = (ca_ref, sal_ref, sah_ref)
    tabs_b = (cb_ref, sbl_ref, sbh_ref)
    sub_rows = 256

    def run(epilogue, dst_ref=o_ref):
        w = w_ref[...].astype(BF16)
        for r in range(h_ref.shape[0] // sub_rows):
            rows = pl.ds(r * sub_rows, sub_rows)
            acc = jnp.dot(h_ref[rows, :], w, preferred_element_type=F32)
            dst_ref[rows, :] = epilogue(acc, rows).astype(dst_ref.dtype)

    @pl.when(is_a & (part == 0))
    def _():
        run(lambda acc, rows: _rotary(acc, tabs_a, rows, ROPE_DIMS // 2) * (ATTN_SCALE * LOG2E))

    @pl.when(is_a & (part == 1))
    def _():
        run(lambda acc, rows: _rotary(acc, tabs_a, rows, ROPE_DIMS // 2))

    @pl.when((is_a & (part == 2)) | (n == BV_BLK))
    def _():
        run(lambda acc, rows: acc)

    @pl.when((n >= BQ_BLK0) & (n < BK_BLK))
    def _():
        run(lambda acc, rows: _rotary(_head_rmsnorm(acc, qg_ref[...]), tabs_b, rows, AXIAL_DIMS // 2)
            * (ATTN_SCALE * LOG2E))

    @pl.when(n == BK_BLK)
    def _():
        run(lambda acc, rows: _rotary(_head_rmsnorm(acc, kg_ref[...]), tabs_b, rows, AXIAL_DIMS // 2))

    @pl.when(n >= GATE_BLK0)
    def _():
        run(lambda acc, rows: jax.nn.sigmoid(acc + bg_ref[...]), gate_ref)


def _projection(h, w_in, tabs_a, tabs_b, qg, kg, b_gate):
    tm = 1024
    per_b = SEQ // tm
    tab_spec = pl.BlockSpec((tm, LANES), lambda m, n: (m % per_b, 0))
    vec_spec = pl.BlockSpec((1, HEAD_DIM), lambda m, n: (0, 0))
    n_gate_blks = GATE_COLS // COL_BLK

    def gate_blk(n):
        return jnp.clip(n - GATE_BLK0, 0, n_gate_blks - 1)

    return pl.pallas_call(
        _proj_kernel,
        out_shape=(jax.ShapeDtypeStruct((N_TOK, QKV_COLS), BF16),
                   jax.ShapeDtypeStruct((N_TOK, GATE_COLS), BF16)),
        grid=(N_TOK // tm, N_COL_BLKS),
        in_specs=[pl.BlockSpec((tm, D_MODEL), lambda m, n: (m, 0)),
                  pl.BlockSpec((D_MODEL, COL_BLK), lambda m, n: (0, n)),
                  tab_spec, tab_spec, tab_spec, tab_spec, tab_spec, tab_spec,
                  vec_spec, vec_spec,
                  pl.BlockSpec((1, COL_BLK), lambda m, n: (0, gate_blk(n)))],
        out_specs=(pl.BlockSpec((tm, COL_BLK), lambda m, n: (m, jnp.minimum(n, GATE_BLK0 - 1))),
                   pl.BlockSpec((tm, COL_BLK), lambda m, n: (m, gate_blk(n)))),
        compiler_params=_cparams(("parallel", "arbitrary")),
        name="in_projection",
    )(h, w_in, *tabs_a, *tabs_b, qg, kg, b_gate)


def _attn_a_kernel(q_ref, k_ref, v_ref, o_ref, lse_ref, *, dilation, reach):
    i = pl.program_id(1)
    tq = q_ref.shape[0]
    win = tq + 2 * reach
    start = pl.multiple_of(jnp.clip(i * tq - reach, 0, SEQ - win), GRANULE)
    q = q_ref[...]
    k = k_ref[pl.ds(start, win), :]
    v = v_ref[pl.ds(start, win), :]
    diff = (i * tq - start) + lax.broadcasted_iota(jnp.int32, (tq, win), 0) \
        - lax.broadcasted_iota(jnp.int32, (tq, win), 1)
    valid = (jnp.abs(diff) <= reach) & ((diff & (dilation - 1)) == 0)
    lane = lax.broadcasted_iota(jnp.int32, (tq, LANES), 1)
    lse_tile = jnp.zeros((tq, LANES), F32)
    outs = []
    for h in range(A_HEADS):
        sl = slice(h * HEAD_DIM, (h + 1) * HEAD_DIM)
        s = lax.dot_general(q[:, sl], k[:, sl], (((1,), (1,)), ((), ())), preferred_element_type=F32)
        s = jnp.where(valid, s, NEG)
        m = jnp.max(s, axis=-1, keepdims=True)
        p = jnp.exp2(s - m)
        l = jnp.sum(p, axis=-1, keepdims=True)
        o = jnp.dot(p.astype(BF16), v[:, sl], preferred_element_type=F32)
        outs.append(o / l)
        lse_tile = jnp.where(lane == h, m * LN2 + jnp.log(l), lse_tile)
    o_ref[...] = jnp.concatenate(outs, axis=1).astype(o_ref.dtype)
    lse_ref[...] = lse_tile


def _attn_a(proj3, group, window, dilation):
    assert dilation & (dilation - 1) == 0
    reach = window // 2
    tq = 256
    qb, kb, vb = 3 * group, 3 * group + 1, 3 * group + 2
    o, lse = pl.pallas_call(
        functools.partial(_attn_a_kernel, dilation=dilation, reach=reach),
        out_shape=(jax.ShapeDtypeStruct((BATCH, SEQ, A_WIDTH), BF16),
                   jax.ShapeDtypeStruct((BATCH, SEQ, LANES), F32)),
        grid=(BATCH, SEQ // tq),
        in_specs=[pl.BlockSpec((None, tq, COL_BLK), lambda b, i: (b, i, qb)),
                  pl.BlockSpec((None, SEQ, COL_BLK), lambda b, i: (b, 0, kb)),
                  pl.BlockSpec((None, SEQ, COL_BLK), lambda b, i: (b, 0, vb))],
        out_specs=(pl.BlockSpec((None, tq, A_WIDTH), lambda b, i: (b, i, 0)),
                   pl.BlockSpec((None, tq, LANES), lambda b, i: (b, i, 0))),
        compiler_params=_cparams(("parallel", "arbitrary")),
        name=f"dilated_attention_g{group}",
    )(proj3, proj3, proj3)
    return o.reshape(N_TOK, A_WIDTH), lse.reshape(N_TOK, LANES)


def _attn_b_kernel(q_ref, k_ref, v_ref, o_ref, vx_ref, acc_ref, *, tk):
    tq = q_ref.shape[0]
    n_chunks = SEQ // tk

    @pl.when(pl.program_id(2) == 0)
    def _():
        vx_ref[:, 0:HEAD_DIM] = v_ref[...]
        vx_ref[:, HEAD_DIM:2 * HEAD_DIM] = jnp.ones((SEQ, HEAD_DIM), BF16)

    q = q_ref[...]
    qs = jnp.concatenate([q[:, g * HEAD_DIM:(g + 1) * HEAD_DIM] for g in range(B_GROUP)], axis=0)
    acc_ref[...] = jnp.zeros_like(acc_ref)
    m = jnp.full((B_GROUP * tq, 1), -jnp.inf, F32)
    for c in range(n_chunks):
        keys = slice(c * tk, (c + 1) * tk)
        s = lax.dot_general(qs, k_ref[keys, :], (((1,), (1,)), ((), ())), preferred_element_type=F32)
        m_new = jnp.maximum(m, jnp.max(s, axis=-1, keepdims=True))
        p = jnp.exp2(s - m_new).astype(BF16)
        acc_ref[...] = jnp.exp2(m - m_new) * acc_ref[...] + jnp.dot(p, vx_ref[keys, :],
                                                                    preferred_element_type=F32)
        m = m_new
    o = acc_ref[:, 0:HEAD_DIM] / acc_ref[:, HEAD_DIM:2 * HEAD_DIM]
    o_ref[...] = jnp.concatenate([o[g * tq:(g + 1) * tq] for g in range(B_GROUP)], axis=1).astype(o_ref.dtype)


def _attn_b(proj3):
    tq = 128
    tk = 256
    kcol0 = BK_BLK * COL_BLK // HEAD_DIM
    vcol0 = BV_BLK * COL_BLK // HEAD_DIM
    o = pl.pallas_call(
        functools.partial(_attn_b_kernel, tk=tk),
        out_shape=jax.ShapeDtypeStruct((BATCH, SEQ, B_Q_WIDTH), BF16),
        grid=(BATCH, B_KV_HEADS, SEQ // tq),
        in_specs=[pl.BlockSpec((None, tq, COL_BLK), lambda b, h, i: (b, i, BQ_BLK0 + h)),
                  pl.BlockSpec((None, SEQ, HEAD_DIM), lambda b, h, i: (b, 0, kcol0 + h)),
                  pl.BlockSpec((None, SEQ, HEAD_DIM), lambda b, h, i: (b, 0, vcol0 + h))],
        out_specs=pl.BlockSpec((None, tq, COL_BLK), lambda b, h, i: (b, i, h)),
        scratch_shapes=[pltpu.VMEM((SEQ, 2 * HEAD_DIM), BF16),
                        pltpu.VMEM((B_GROUP * tq, 2 * HEAD_DIM), F32)],
        compiler_params=_cparams(("parallel", "parallel", "arbitrary")),
        name="gqa_attention",
    )(proj3, proj3, proj3)
    return o.reshape(N_TOK, B_Q_WIDTH)


def _merge_kernel(o0_ref, o1_ref, o2_ref, l0_ref, l1_ref, l2_ref, yb_ref, wa_ref, wb_ref, ga_ref, gb_ref,
                  out_ref):
    sub_rows = 256
    for r in range(out_ref.shape[0] // sub_rows):
        rows = pl.ds(r * sub_rows, sub_rows)
        lses = (l0_ref[rows, :], l1_ref[rows, :], l2_ref[rows, :])
        outs = (o0_ref[rows, :], o1_ref[rows, :], o2_ref[rows, :])
        cols = []
        for h in range(A_HEADS):
            lh = [jnp.broadcast_to(l[:, h:h + 1], (sub_rows, HEAD_DIM)) for l in lses]
            mx = jnp.maximum(jnp.maximum(lh[0], lh[1]), lh[2])
            e = [jnp.exp(v - mx) for v in lh]
            den = e[0] + e[1] + e[2]
            sl = slice(h * HEAD_DIM, (h + 1) * HEAD_DIM)
            cols.append(sum((e[g] / den) * outs[g][:, sl].astype(F32) for g in range(A_N_GROUPS)))
        mix_a = jnp.concatenate(cols, axis=1).astype(BF16)
        mix_b = yb_ref[rows, :]
        for cb in range(D_MODEL // COL_BLK):
            cs = pl.ds(cb * COL_BLK, COL_BLK)
            ya = jnp.dot(mix_a, wa_ref[:, cs], preferred_element_type=F32)
            yb = jnp.dot(mix_b, wb_ref[:, cs], preferred_element_type=F32)
            out_ref[rows, cs] = (ga_ref[rows, cs].astype(F32) * ya
                                 + gb_ref[rows, cs].astype(F32) * yb).astype(out_ref.dtype)


def _merge(o_groups, lse_groups, mix_b, gates, w_a_up, w_b_up):
    tm = 512
    o_spec = pl.BlockSpec((tm, A_WIDTH), lambda m: (m, 0))
    l_spec = pl.BlockSpec((tm, LANES), lambda m: (m, 0))
    return pl.pallas_call(
        _merge_kernel,
        out_shape=jax.ShapeDtypeStruct((N_TOK, D_MODEL), BF16),
        grid=(N_TOK // tm,),
        in_specs=[o_spec, o_spec, o_spec, l_spec, l_spec, l_spec,
                  pl.BlockSpec((tm, B_Q_WIDTH), lambda m: (m, 0)),
                  pl.BlockSpec((A_WIDTH, D_MODEL), lambda m: (0, 0)),
                  pl.BlockSpec((B_Q_WIDTH, D_MODEL), lambda m: (0, 0)),
                  pl.BlockSpec((tm, D_MODEL), lambda m: (m, 0)),
                  pl.BlockSpec((tm, D_MODEL), lambda m: (m, 1))],
        out_specs=pl.BlockSpec((tm, D_MODEL), lambda m: (m, 0)),
        compiler_params=_cparams(("parallel",)),
        name="branch_merge",
    )(*o_groups, *lse_groups, mix_b, w_a_up, w_b_up, gates, gates)


def _outproj_kernel(mg_ref, w_ref, x_ref, gt_ref, g_ref, sc_ref, sh_ref, wr_ref, x1_ref, h2_ref, lg_ref):
    sub_rows = 256
    for r in range(mg_ref.shape[0] // sub_rows):
        rows = pl.ds(r * sub_rows, sub_rows)
        y = jnp.dot(mg_ref[rows, :], w_ref[...], preferred_element_type=F32)
        x1 = x_ref[rows, :] + gt_ref[...] * y
        x1_ref[rows, :] = x1
        h2 = _normmod(x1, g_ref[...], sc_ref[...], sh_ref[...])
        h2_ref[rows, :] = h2.astype(h2_ref.dtype)
        lg_ref[:, rows] = lax.dot_general(wr_ref[...], h2, (((1,), (1,)), ((), ())),
                                          preferred_element_type=F32, precision=lax.Precision.HIGHEST)


def _out_projection(merged, w_out, x2, mod4, g_ffn, w_router_t):
    tm = 512
    per_b = SEQ // tm

    def mod_spec(j):
        return pl.BlockSpec((None, None, 1, D_MODEL), lambda m: (m // per_b, j, 0, 0))

    return pl.pallas_call(
        _outproj_kernel,
        out_shape=(jax.ShapeDtypeStruct((N_TOK, D_MODEL), F32),
                   jax.ShapeDtypeStruct((N_TOK, D_MODEL), BF16),
                   jax.ShapeDtypeStruct((N_EXPERTS, N_TOK), F32)),
        grid=(N_TOK // tm,),
        in_specs=[pl.BlockSpec((tm, D_MODEL), lambda m: (m, 0)),
                  pl.BlockSpec((D_MODEL, D_MODEL), lambda m: (0, 0)),
                  pl.BlockSpec((tm, D_MODEL), lambda m: (m, 0)),
                  mod_spec(2),
                  pl.BlockSpec((1, D_MODEL), lambda m: (0, 0)),
                  mod_spec(4), mod_spec(3),
                  pl.BlockSpec((N_EXPERTS, D_MODEL), lambda m: (0, 0))],
        out_specs=(pl.BlockSpec((tm, D_MODEL), lambda m: (m, 0)),
                   pl.BlockSpec((tm, D_MODEL), lambda m: (m, 0)),
                   pl.BlockSpec((N_EXPERTS, tm), lambda m: (0, m))),
        compiler_params=_cparams(("parallel",)),
        name="out_projection",
    )(merged, w_out, x2, mod4, g_ffn, mod4, mod4, w_router_t)


def _route_kernel(lg_ref, eb_ref, csm_ref, cnt_ref, gx_ref):
    tt = lg_ref.shape[1]
    scores = jax.nn.sigmoid(lg_ref[...])
    sel = scores + eb_ref[...]
    neg_inf = -jnp.inf
    sel_g = [sel[PER_GROUP * g:PER_GROUP * (g + 1), :] for g in range(N_EXPERT_GROUPS)]
    grp = []
    for v in sel_g:
        m1 = jnp.max(v, axis=0, keepdims=True)
        is1 = v == m1
        n1 = jnp.sum(jnp.where(is1, 1.0, 0.0), axis=0, keepdims=True)
        rest = jnp.max(jnp.where(is1, neg_inf, v), axis=0, keepdims=True)
        grp.append(m1 + jnp.where(n1 >= 2.0, m1, rest))
    masked = []
    for g in range(N_EXPERT_GROUPS):
        rank = jnp.zeros((1, tt), F32)
        for g2 in range(N_EXPERT_GROUPS):
            if g2 != g:
                beats = (grp[g2] >= grp[g]) if g2 < g else (grp[g2] > grp[g])
                rank = rank + jnp.where(beats, 1.0, 0.0)
        keep = jnp.broadcast_to(rank < TOPK_GROUPS, (PER_GROUP, tt))
        masked.append(jnp.where(keep, sel_g[g], neg_inf))
    sub = lax.broadcasted_iota(jnp.int32, (PER_GROUP, tt), 0)
    ranks = [jnp.zeros((PER_GROUP, tt), F32) for _ in range(N_EXPERT_GROUPS)]
    for g2 in range(N_EXPERT_GROUPS):
        for m2 in range(PER_GROUP):
            vf = jnp.broadcast_to(masked[g2][m2:m2 + 1, :], (PER_GROUP, tt))
            for g in range(N_EXPERT_GROUPS):
                if g2 < g:
                    beats = vf >= masked[g]
                elif g2 > g:
                    beats = vf > masked[g]
                else:
                    beats = (vf > masked[g]) | ((vf == masked[g]) & (sub > m2))
                ranks[g] = ranks[g] + jnp.where(beats, 1.0, 0.0)
    picked = [ranks[g] < TOP_K for g in range(N_EXPERT_GROUPS)]
    chosen = [jnp.where(picked[g], scores[PER_GROUP * g:PER_GROUP * (g + 1), :], 0.0)
              for g in range(N_EXPERT_GROUPS)]
    total = chosen[0]
    for g in range(1, N_EXPERT_GROUPS):
        total = total + chosen[g]
    denom = jnp.sum(total, axis=0, keepdims=True)
    gate = jnp.concatenate([chosen[g] / denom * ROUTED_SCALE for g in range(N_EXPERT_GROUPS)], axis=0)
    mask = jnp.concatenate([jnp.where(picked[g], 1.0, 0.0) for g in range(N_EXPERT_GROUPS)], axis=0)
    mask_bf = mask.astype(BF16)

    upper = jnp.where(lax.broadcasted_iota(jnp.int32, (tt, tt), 0) <= lax.broadcasted_iota(jnp.int32, (tt, tt), 1),
                      1.0, 0.0).astype(BF16)
    cs = jnp.dot(mask_bf, upper, preferred_element_type=F32)
    csm_ref[...] = jnp.where(mask > 0.0, cs, 0.0)
    cnt_ref[...] = lax.dot_general(jnp.ones((SUBLANES, tt), BF16), mask_bf, (((1,), (1,)), ((), ())),
                                   preferred_element_type=F32)

    hi = gate.astype(BF16).astype(F32)
    gx_ref[...] = jnp.concatenate([hi, gate - hi], axis=0).T.astype(BF16)


def _route(logits_t, e_bias_col):
    return pl.pallas_call(
        _route_kernel,
        out_shape=(jax.ShapeDtypeStruct((N_EXPERTS, N_TOK), F32),
                   jax.ShapeDtypeStruct((N_CHUNKS, SUBLANES, N_EXPERTS), F32),
                   jax.ShapeDtypeStruct((N_TOK, LANES), BF16)),
        grid=(N_CHUNKS,),
        in_specs=[pl.BlockSpec((N_EXPERTS, CHUNK), lambda c: (0, c)),
                  pl.BlockSpec((N_EXPERTS, 1), lambda c: (0, 0))],
        out_specs=(pl.BlockSpec((N_EXPERTS, CHUNK), lambda c: (0, c)),
                   pl.BlockSpec((None, SUBLANES, N_EXPERTS), lambda c: (c, 0, 0)),
                   pl.BlockSpec((CHUNK, LANES), lambda c: (c, 0))),
        compiler_params=_cparams(("parallel",)),
        name="routing",
    )(logits_t, e_bias_col)


def _plan_kernel(cnt_ref, goff_ref, tail_ref, tile_e_ref, ntile_ref, next_ref):
    def per_expert(e, carry):
        base, tbase, prev = carry

        def per_chunk(c, off):
            goff_ref[c, e] = base + off
            return off + (((cnt_ref[c, e] + (GRANULE - 1)) >> GRANULE_SHIFT) << GRANULE_SHIFT)

        tot = lax.fori_loop(0, N_CHUNKS, per_chunk, jnp.int32(0))
        region = ((tot + (TILE - 1)) >> TILE_SHIFT) << TILE_SHIFT
        tail_ref[0, e] = base + tot
        tail_ref[1, e] = (region - tot) >> GRANULE_SHIFT
        n_t = region >> TILE_SHIFT

        def per_tile(j, _):
            tile_e_ref[tbase + j] = e
            return 0

        lax.fori_loop(0, n_t, per_tile, 0)
        next_ref[e] = jnp.int32(-1)

        @pl.when((n_t > 0) & (prev >= 0))
        def _():
            next_ref[jnp.maximum(prev, 0)] = e

        return base + region, tbase + n_t, jnp.where(n_t > 0, e, prev)

    _, n_tiles, _ = lax.fori_loop(0, N_EXPERTS, per_expert, (jnp.int32(0), jnp.int32(0), jnp.int32(-1)))
    ntile_ref[0] = n_tiles
    last_e = tile_e_ref[jnp.maximum(n_tiles - 1, 0)]

    def fill(j, _):
        tile_e_ref[j] = last_e
        return 0

    lax.fori_loop(n_tiles, MAX_TILES, fill, 0)


def _plan(cnt):
    smem = pl.BlockSpec(memory_space=pltpu.SMEM)
    return pl.pallas_call(
        _plan_kernel,
        out_shape=(jax.ShapeDtypeStruct((N_CHUNKS, N_EXPERTS), jnp.int32),
                   jax.ShapeDtypeStruct((2, N_EXPERTS), jnp.int32),
                   jax.ShapeDtypeStruct((MAX_TILES,), jnp.int32),
                   jax.ShapeDtypeStruct((1,), jnp.int32),
                   jax.ShapeDtypeStruct((N_EXPERTS,), jnp.int32)),
        in_specs=[smem],
        out_specs=(smem, smem, smem, smem, smem),
        name="row_plan",
    )(cnt)


def _chunk_granules(cnt_ref, goff_ref, c, table_ref, per_granule=None):
    def per_expert(e, n_before):
        n_gran = (cnt_ref[c, e] + (GRANULE - 1)) >> GRANULE_SHIFT
        base = goff_ref[c, e]

        def per_j(j, _):
            table_ref[n_before + j] = base + j * GRANULE
            if per_granule is not None:
                per_granule(e, j, n_before + j)
            return 0

        lax.fori_loop(0, n_gran, per_j, 0)
        return n_before + n_gran

    return lax.fori_loop(0, N_EXPERTS, per_expert, jnp.int32(0))


def _dispatch_kernel(cnt_ref, goff_ref, tail_ref, x_ref, gx_ref, csm_ref, xs_hbm,
                     onehot_ref, table_ref, buf_ref, zero_ref, sems, zsem):
    c = pl.program_id(0)
    sub = lax.broadcasted_iota(jnp.int32, (GRANULE, CHUNK), 0)

    def build_granule(e, j, g):
        want = (sub + (j * GRANULE + 1)).astype(F32)
        hit = csm_ref[pl.ds(e, 1), :] == want
        onehot_ref[pl.ds(pl.multiple_of(g * GRANULE, GRANULE), GRANULE), :] = jnp.where(hit, 1.0, 0.0).astype(BF16)

    n_gran = _chunk_granules(cnt_ref, goff_ref, c, table_ref, build_granule)
    n_tiles = (n_gran + (GRAN_PER_TILE - 1)) >> GPT_SHIFT

    def clear(g, _):
        onehot_ref[pl.ds(pl.multiple_of(g * GRANULE, GRANULE), GRANULE), :] = jnp.zeros((GRANULE, CHUNK), BF16)
        return 0

    lax.fori_loop(n_gran, n_tiles * GRAN_PER_TILE, clear, 0)

    def granule_copy(slot, i, row):
        return pltpu.make_async_copy(buf_ref.at[slot, pl.ds(pl.multiple_of(i * GRANULE, GRANULE), GRANULE), :],
                                     xs_hbm.at[pl.ds(pl.multiple_of(row, GRANULE), GRANULE), :],
                                     sems.at[slot])

    def tile_granules(k):
        return jnp.minimum(GRAN_PER_TILE, n_gran - k * GRAN_PER_TILE)

    def wait_tile(k):
        slot = k % N_SLOTS

        def w(i, _):
            granule_copy(slot, i, 0).wait()
            return 0

        lax.fori_loop(0, tile_granules(k), w, 0)

    def per_tile(k, _):
        slot = k % N_SLOTS

        @pl.when(k >= N_SLOTS)
        def _():
            wait_tile(k - N_SLOTS)

        oh = onehot_ref[pl.ds(pl.multiple_of(k * TILE, TILE), TILE), :]
        buf_ref[slot, :, 0:D_MODEL] = jnp.dot(oh, x_ref[...], preferred_element_type=F32).astype(BF16)
        buf_ref[slot, :, D_MODEL:XS_WIDTH] = jnp.dot(oh, gx_ref[...], preferred_element_type=F32).astype(BF16)

        def s(i, _):
            granule_copy(slot, i, table_ref[k * GRAN_PER_TILE + i]).start()
            return 0

        lax.fori_loop(0, tile_granules(k), s, 0)
        return 0

    lax.fori_loop(0, n_tiles, per_tile, 0)

    def drain(k, _):
        wait_tile(k)
        return 0

    lax.fori_loop(jnp.maximum(n_tiles - N_SLOTS, 0), n_tiles, drain, 0)

    @pl.when(c == N_CHUNKS - 1)
    def _():
        zero_ref[...] = jnp.zeros_like(zero_ref)

        def tail_copy(row):
            return pltpu.make_async_copy(zero_ref, xs_hbm.at[pl.ds(pl.multiple_of(row, GRANULE), GRANULE), :], zsem)

        def start_e(e, _):
            def st(j, _):
                tail_copy(tail_ref[0, e] + j * GRANULE).start()
                return 0
            lax.fori_loop(0, tail_ref[1, e], st, 0)
            return 0

        def wait_e(e, _):
            def wt(j, _):
                tail_copy(0).wait()
                return 0
            lax.fori_loop(0, tail_ref[1, e], wt, 0)
            return 0

        lax.fori_loop(0, N_EXPERTS, start_e, 0)
        lax.fori_loop(0, N_EXPERTS, wait_e, 0)


def _dispatch(cnt, goff, tail, h2, gx, csm):
    grid_spec = pltpu.PrefetchScalarGridSpec(
        num_scalar_prefetch=3,
        grid=(N_CHUNKS,),
        in_specs=[pl.BlockSpec((CHUNK, D_MODEL), lambda c, *_: (c, 0)),
                  pl.BlockSpec((CHUNK, LANES), lambda c, *_: (c, 0)),
                  pl.BlockSpec((N_EXPERTS, CHUNK), lambda c, *_: (0, c))],
        out_specs=pl.BlockSpec(memory_space=pl.ANY),
        scratch_shapes=[pltpu.VMEM((CHUNK_ROWS, CHUNK), BF16),
                        pltpu.SMEM((CHUNK_ROWS // GRANULE,), jnp.int32),
                        pltpu.VMEM((N_SLOTS, TILE, XS_WIDTH), BF16),
                        pltpu.VMEM((GRANULE, XS_WIDTH), BF16),
                        pltpu.SemaphoreType.DMA((N_SLOTS,)),
                        pltpu.SemaphoreType.DMA(())])
    return pl.pallas_call(
        _dispatch_kernel,
        out_shape=jax.ShapeDtypeStruct((MAX_ROWS, XS_WIDTH), BF16),
        grid_spec=grid_spec,
        compiler_params=_cparams(("arbitrary",)),
        name="moe_dispatch",
    )(cnt, goff, tail, h2, gx, csm)


def _routed_kernel(tile_e_ref, ntile_ref, next_ref, xs_ref, w1_hbm, w3_hbm, w2_hbm, ys_ref,
                   w1f_ref, w3f_ref, w2f_ref, w1s_ref, w3s_ref, w2s_ref, ord_ref, sems):
    i = pl.program_id(0)
    e = tile_e_ref[i]

    def fetch(expert, slot):
        return (pltpu.make_async_copy(w1_hbm.at[expert], w1f_ref.at[slot], sems.at[slot]),
                pltpu.make_async_copy(w3_hbm.at[expert], w3f_ref.at[slot], sems.at[slot]),
                pltpu.make_async_copy(w2_hbm.at[expert], w2f_ref.at[slot], sems.at[slot]))

    @pl.when(i < ntile_ref[0])
    def _():
        @pl.when((i == 0) | (e != tile_e_ref[jnp.maximum(i - 1, 0)]))
        def _():
            @pl.when(i == 0)
            def _():
                ord_ref[0] = 0
                for cp in fetch(e, 0):
                    cp.start()

            slot = ord_ref[0] & 1
            nxt = next_ref[e]

            @pl.when(nxt >= 0)
            def _():
                for cp in fetch(jnp.maximum(nxt, 0), 1 - slot):
                    cp.start()

            for cp in fetch(e, slot):
                cp.wait()
            w1s_ref[...] = w1f_ref[slot].astype(BF16)
            w3s_ref[...] = w3f_ref[slot].astype(BF16)
            w2s_ref[...] = w2f_ref[slot].astype(BF16)
            ord_ref[0] = ord_ref[0] + 1

        x = xs_ref[:, 0:D_MODEL]
        gx = xs_ref[:, D_MODEL:XS_WIDTH].astype(F32)
        lane = lax.broadcasted_iota(jnp.int32, gx.shape, 1)
        gate = jnp.sum(jnp.where((lane == e) | (lane == e + N_EXPERTS), gx, 0.0), axis=1, keepdims=True)
        a = jnp.dot(x, w1s_ref[...], preferred_element_type=F32)
        u = jnp.dot(x, w3s_ref[...], preferred_element_type=F32)
        hid = (a * jax.nn.sigmoid(a)) * u * gate
        ys_ref[...] = jnp.dot(hid.astype(BF16), w2s_ref[...], preferred_element_type=F32).astype(ys_ref.dtype)


def _routed_experts(tile_e, ntile, next_e, xs, w1, w3, w2):
    def row_map(i, tile_e_ref, ntile_ref, next_ref):
        return (jnp.minimum(i, jnp.maximum(ntile_ref[0] - 1, 0)), 0)

    hbm = pl.BlockSpec(memory_space=pl.ANY)
    grid_spec = pltpu.PrefetchScalarGridSpec(
        num_scalar_prefetch=3,
        grid=(MAX_TILES,),
        in_specs=[pl.BlockSpec((TILE, XS_WIDTH), row_map), hbm, hbm, hbm],
        out_specs=pl.BlockSpec((TILE, D_MODEL), row_map),
        scratch_shapes=[pltpu.VMEM((2, D_MODEL, D_EXPERT), F32),
                        pltpu.VMEM((2, D_MODEL, D_EXPERT), F32),
                        pltpu.VMEM((2, D_EXPERT, D_MODEL), F32),
                        pltpu.VMEM((D_MODEL, D_EXPERT), BF16),
                        pltpu.VMEM((D_MODEL, D_EXPERT), BF16),
                        pltpu.VMEM((D_EXPERT, D_MODEL), BF16),
                        pltpu.SMEM((1,), jnp.int32),
                        pltpu.SemaphoreType.DMA((2,))])
    return pl.pallas_call(
        _routed_kernel,
        out_shape=jax.ShapeDtypeStruct((MAX_ROWS, D_MODEL), BF16),
        grid_spec=grid_spec,
        compiler_params=_cparams(("arbitrary",)),
        name="routed_experts",
    )(tile_e, ntile, next_e, xs, w1, w3, w2)


def _combine_kernel(cnt_ref, goff_ref, csm_ref, x1_ref, sh_ref, gt_ref, gf_ref, ys_hbm, o_ref,
                    table_ref, buf_ref, acc_ref, pick_ref, sems):
    c = pl.program_id(0)
    sub = lax.broadcasted_iota(jnp.int32, (GRANULE, CHUNK), 0)

    def build_granule(e, j, g):
        want = (sub + (j * GRANULE + 1)).astype(F32)
        hit = csm_ref[pl.ds(e, 1), :] == want
        pick_ref[pl.ds(pl.multiple_of(g * GRANULE, GRANULE), GRANULE), :] = jnp.where(hit, 1.0, 0.0).astype(BF16)

    n_gran = _chunk_granules(cnt_ref, goff_ref, c, table_ref, build_granule)
    n_tiles = (n_gran + (COMBINE_GPT - 1)) >> COMBINE_GPT_SHIFT

    def clear_pick(g, _):
        pick_ref[pl.ds(pl.multiple_of(g * GRANULE, GRANULE), GRANULE), :] = jnp.zeros((GRANULE, CHUNK), BF16)
        return 0

    lax.fori_loop(n_gran, n_tiles * COMBINE_GPT, clear_pick, 0)

    def granule_copy(slot, i, row):
        return pltpu.make_async_copy(ys_hbm.at[pl.ds(pl.multiple_of(row, GRANULE), GRANULE), :],
                                     buf_ref.at[slot, pl.ds(pl.multiple_of(i * GRANULE, GRANULE), GRANULE), :],
                                     sems.at[slot])

    def tile_granules(k):
        return jnp.minimum(COMBINE_GPT, n_gran - k * COMBINE_GPT)

    def start_tile(k):
        slot = k % COMBINE_SLOTS

        def s(i, _):
            granule_copy(slot, i, table_ref[k * COMBINE_GPT + i]).start()
            return 0

        lax.fori_loop(0, tile_granules(k), s, 0)

    def wait_tile(k):
        slot = k % COMBINE_SLOTS

        def w(i, _):
            granule_copy(slot, i, 0).wait()
            return 0

        lax.fori_loop(0, tile_granules(k), w, 0)

    for k0 in range(COMBINE_SLOTS - 1):
        @pl.when(k0 < n_tiles)
        def _():
            start_tile(k0)

    acc_ref[...] = jnp.zeros_like(acc_ref)

    def per_tile(k, _):
        @pl.when(k + (COMBINE_SLOTS - 1) < n_tiles)
        def _():
            start_tile(k + (COMBINE_SLOTS - 1))

        wait_tile(k)
        slot = k % COMBINE_SLOTS

        def clear(i, _):
            buf_ref[slot, pl.ds(pl.multiple_of(i * GRANULE, GRANULE), GRANULE), :] = jnp.zeros((GRANULE, D_MODEL), BF16)
            return 0

        lax.fori_loop(tile_granules(k), COMBINE_GPT, clear, 0)

        pick = pick_ref[pl.ds(pl.multiple_of(k * COMBINE_TILE, COMBINE_TILE), COMBINE_TILE), :]
        acc_ref[...] += lax.dot_general(pick, buf_ref[slot], (((0,), (0,)), ((), ())),
                                        preferred_element_type=F32)
        return 0

    lax.fori_loop(0, n_tiles, per_tile, 0)

    x = x1_ref[...] + gt_ref[...] * (acc_ref[...] + sh_ref[...])
    o_ref[...] = x * lax.rsqrt(jnp.mean(x * x, axis=-1, keepdims=True) + EPS) * gf_ref[...]


def _combine(cnt, goff, csm, x1, shared, mod4, g_final, ys):
    per_b = SEQ // CHUNK
    row = pl.BlockSpec((CHUNK, D_MODEL), lambda c, *_: (c, 0))
    grid_spec = pltpu.PrefetchScalarGridSpec(
        num_scalar_prefetch=2,
        grid=(N_CHUNKS,),
        in_specs=[pl.BlockSpec((N_EXPERTS, CHUNK), lambda c, *_: (0, c)),
                  row, row,
                  pl.BlockSpec((None, None, 1, D_MODEL), lambda c, *_: (c // per_b, 5, 0, 0)),
                  pl.BlockSpec((1, D_MODEL), lambda c, *_: (0, 0)),
                  pl.BlockSpec(memory_space=pl.ANY)],
        out_specs=row,
        scratch_shapes=[pltpu.SMEM((CHUNK_ROWS // GRANULE,), jnp.int32),
                        pltpu.VMEM((COMBINE_SLOTS, COMBINE_TILE, D_MODEL), BF16),
                        pltpu.VMEM((CHUNK, D_MODEL), F32),
                        pltpu.VMEM((CHUNK_ROWS, CHUNK), BF16),
                        pltpu.SemaphoreType.DMA((COMBINE_SLOTS,))])
    return pl.pallas_call(
        _combine_kernel,
        out_shape=jax.ShapeDtypeStruct((N_TOK, D_MODEL), F32),
        grid_spec=grid_spec,
        compiler_params=_cparams(("arbitrary",)),
        name="moe_combine",
    )(cnt, goff, csm, x1, shared, mod4, g_final, ys)


def _shared_kernel(x_ref, w1_ref, w3_ref, w2_ref, o_ref):
    x = x_ref[...]
    a = jnp.dot(x, w1_ref[...], preferred_element_type=F32)
    u = jnp.dot(x, w3_ref[...], preferred_element_type=F32)
    hid = (a * jax.nn.sigmoid(a)) * u
    o_ref[...] = jnp.dot(hid.astype(BF16), w2_ref[...], preferred_element_type=F32).astype(o_ref.dtype)


def _shared_expert(h2, w1, w3, w2):
    tm = 1024
    return pl.pallas_call(
        _shared_kernel,
        out_shape=jax.ShapeDtypeStruct((N_TOK, D_MODEL), F32),
        grid=(N_TOK // tm,),
        in_specs=[pl.BlockSpec((tm, D_MODEL), lambda m: (m, 0)),
                  pl.BlockSpec((D_MODEL, D_SHARED), lambda m: (0, 0)),
                  pl.BlockSpec((D_MODEL, D_SHARED), lambda m: (0, 0)),
                  pl.BlockSpec((D_SHARED, D_MODEL), lambda m: (0, 0))],
        out_specs=pl.BlockSpec((tm, D_MODEL), lambda m: (m, 0)),
        compiler_params=_cparams(("parallel",)),
        name="shared_expert",
    )(h2, w1, w3, w2)


def _rope_angles(pos, dims, theta):
    inv = jnp.power(jnp.float32(theta), -jnp.arange(0, dims, 2, dtype=jnp.float32) / dims)
    return pos.astype(jnp.float32)[:, None] * inv[None, :]


def _rotary_tables(angle_blocks):
    cos_parts, lo_parts, hi_parts = [], [], []
    used = 0
    for ang in angle_blocks:
        c, s = jnp.cos(ang), jnp.sin(ang)
        z = jnp.zeros_like(s)
        cos_parts += [c, c]
        lo_parts += [-s, z]
        hi_parts += [z, s]
        used += 2 * ang.shape[1]
    rest = HEAD_DIM - used
    if rest:
        cos_parts.append(jnp.ones((SEQ, rest), F32))
        lo_parts.append(jnp.zeros((SEQ, rest), F32))
        hi_parts.append(jnp.zeros((SEQ, rest), F32))
    return (jnp.concatenate(cos_parts, axis=1), jnp.concatenate(lo_parts, axis=1),
            jnp.concatenate(hi_parts, axis=1))


def kernel(x, c, w_ada, b_ada, g_attn, w_in, b_gate, q_norm_g, k_norm_g, w_a_up, w_b_up, w_out,
           g_ffn, w_router, e_bias, w1, w3, w2, ws1, ws3, ws2, g_final):
    l = 0
    x2 = x.reshape(N_TOK, D_MODEL)
    pos = jnp.arange(SEQ)
    tabs_a = _rotary_tables([_rope_angles(pos, ROPE_DIMS, ROPE_THETA)])
    tabs_b = _rotary_tables([_rope_angles(pos // GRID_W, AXIAL_DIMS, AXIAL_THETA),
                             _rope_angles(pos % GRID_W, AXIAL_DIMS, AXIAL_THETA)])

    c_pad = jnp.zeros((SUBLANES, D_MODEL), F32).at[:BATCH].set(c)
    mod = _ada(c_pad, w_ada[l], b_ada[l].reshape(1, -1))
    mod4 = mod[:BATCH].reshape(BATCH, N_MOD, 1, D_MODEL)

    h = _norm_modulate(x2, g_attn[l].reshape(1, -1), mod4, 1, 0)
    proj, gates = _projection(h, w_in[l], tabs_a, tabs_b,
                       q_norm_g[l].reshape(1, -1), k_norm_g[l].reshape(1, -1), b_gate[l].reshape(1, -1))
    proj3 = proj.reshape(BATCH, SEQ, QKV_COLS)

    o_groups, lse_groups = [], []
    for gi, (window, dilation) in enumerate(A_PATTERNS):
        o, lse = _attn_a(proj3, gi, window, dilation)
        o_groups.append(o)
        lse_groups.append(lse)
    mix_b = _attn_b(proj3)

    merged = _merge(o_groups, lse_groups, mix_b, gates, w_a_up[l].astype(BF16), w_b_up[l].astype(BF16))
    x1, h2, logits_t = _out_projection(merged, w_out[l].astype(BF16), x2, mod4, g_ffn[l].reshape(1, -1),
                                       w_router[l].T)
    csm, cnt_f, gx = _route(logits_t, e_bias[l].reshape(-1, 1))
    cnt = cnt_f[:, 0, :].astype(jnp.int32)
    goff, tail, tile_e, ntile, next_e = _plan(cnt)
    xs = _dispatch(cnt, goff, tail, h2, gx, csm)
    ys = _routed_experts(tile_e, ntile, next_e, xs, w1[l], w3[l], w2[l])
    shared = _shared_expert(h2, ws1[l].astype(BF16), ws3[l].astype(BF16), ws2[l].astype(BF16))
    out = _combine(cnt, goff, csm, x1, shared, mod4, g_final.reshape(1, -1), ys)
    return out.reshape(BATCH, SEQ, D_MODEL)
```

```python
import functools

import jax
import jax.numpy as jnp
from jax import lax
from jax.experimental import pallas as pl
from jax.experimental.pallas import tpu as pltpu

F32 = jnp.float32
BF16 = jnp.bfloat16

D_MODEL = 2048
BATCH = 2
SEQ = 4096
N_TOK = BATCH * SEQ
HEAD_DIM = 128
EPS = 1e-6
A_PATTERNS = ((128, 1), (512, 4), (2048, 16))
A_N_GROUPS = len(A_PATTERNS)
A_HEADS = 4
A_WIDTH = A_HEADS * HEAD_DIM
ROPE_THETA = 500000.0
ROPE_DIMS = HEAD_DIM // 4
B_Q_HEADS = 16
B_KV_HEADS = 4
B_GROUP = B_Q_HEADS // B_KV_HEADS
B_Q_WIDTH = B_Q_HEADS * HEAD_DIM
B_KV_WIDTH = B_KV_HEADS * HEAD_DIM
AXIAL_THETA = 10000.0
AXIAL_DIMS = HEAD_DIM // 2
GRID_W = 64
A_COLS = A_N_GROUPS * 3 * A_WIDTH
B_COLS = B_Q_WIDTH + 2 * B_KV_WIDTH
GATE_COLS = 2 * D_MODEL
QKV_COLS = A_COLS + B_COLS
IN_COLS = QKV_COLS + GATE_COLS
N_MOD = 6
N_EXPERTS = 64
N_EXPERT_GROUPS = 8
PER_GROUP = N_EXPERTS // N_EXPERT_GROUPS
TOPK_GROUPS = 4
TOP_K = 8
D_EXPERT = D_MODEL // 4
D_SHARED = D_MODEL // 4
ROUTED_SCALE = 2.5
NEG = -1e30
ATTN_SCALE = HEAD_DIM ** -0.5
LOG2E = 1.4426950408889634
LN2 = 0.6931471805599453

LANES = 128
SUBLANES = 8
VMEM_LIMIT = 56 * 1024 * 1024

COL_BLK = 512
N_COL_BLKS = IN_COLS // COL_BLK
A_BLKS = A_COLS // COL_BLK
BQ_BLK0 = A_BLKS
BK_BLK = BQ_BLK0 + B_Q_WIDTH // COL_BLK
BV_BLK = BK_BLK + 1
GATE_BLK0 = BV_BLK + 1

CHUNK = 512
N_CHUNKS = N_TOK // CHUNK
GRANULE = 16
GRANULE_SHIFT = GRANULE.bit_length() - 1
TILE = 256
TILE_SHIFT = TILE.bit_length() - 1
GRAN_PER_TILE = TILE // GRANULE
GPT_SHIFT = TILE_SHIFT - GRANULE_SHIFT
N_SLOTS = 4
COMBINE_TILE = 1024
COMBINE_GPT = COMBINE_TILE // GRANULE
COMBINE_GPT_SHIFT = COMBINE_GPT.bit_length() - 1
COMBINE_SLOTS = 3
XS_WIDTH = D_MODEL + LANES
CHUNK_ROWS = -(-(CHUNK * TOP_K + N_EXPERTS * (GRANULE - 1)) // TILE) * TILE
MAX_ROWS = -(-(N_TOK * TOP_K + N_EXPERTS * N_CHUNKS * (GRANULE - 1) + N_EXPERTS * (TILE - 1)) // TILE) * TILE
MAX_TILES = MAX_ROWS // TILE


def _cparams(semantics):
    return pltpu.CompilerParams(dimension_semantics=semantics, vmem_limit_bytes=VMEM_LIMIT)


def _ada_kernel(c_ref, w_ref, b_ref, o_ref):
    c = c_ref[...]
    act = c * jax.nn.sigmoid(c)
    o_ref[...] = jnp.dot(act, w_ref[...], preferred_element_type=F32,
                         precision=lax.Precision.HIGHEST) + b_ref[...]


def _ada(c_pad, w_ada, b_ada):
    tn = 1024
    n_out = w_ada.shape[1]
    return pl.pallas_call(
        _ada_kernel,
        out_shape=jax.ShapeDtypeStruct((SUBLANES, n_out), F32),
        grid=(n_out // tn,),
        in_specs=[pl.BlockSpec((SUBLANES, D_MODEL), lambda n: (0, 0)),
                  pl.BlockSpec((D_MODEL, tn), lambda n: (0, n)),
                  pl.BlockSpec((1, tn), lambda n: (0, n))],
        out_specs=pl.BlockSpec((SUBLANES, tn), lambda n: (0, n)),
        compiler_params=_cparams(("parallel",)),
        name="ada_mod",
    )(c_pad, w_ada, b_ada)


def _normmod(x, g, sc, sh):
    y = x * lax.rsqrt(jnp.mean(x * x, axis=-1, keepdims=True) + EPS)
    return (y * g) * (1.0 + sc) + sh


def _normmod_kernel(x_ref, g_ref, sc_ref, sh_ref, o_ref):
    o_ref[...] = _normmod(x_ref[...], g_ref[...], sc_ref[...], sh_ref[...]).astype(o_ref.dtype)


def _norm_modulate(x2, g, mod4, sc_idx, sh_idx):
    tm = 512
    per_b = SEQ // tm
    return pl.pallas_call(
        _normmod_kernel,
        out_shape=jax.ShapeDtypeStruct((N_TOK, D_MODEL), BF16),
        grid=(N_TOK // tm,),
        in_specs=[pl.BlockSpec((tm, D_MODEL), lambda m: (m, 0)),
                  pl.BlockSpec((1, D_MODEL), lambda m: (0, 0)),
                  pl.BlockSpec((None, None, 1, D_MODEL), lambda m: (m // per_b, sc_idx, 0, 0)),
                  pl.BlockSpec((None, None, 1, D_MODEL), lambda m: (m // per_b, sh_idx, 0, 0))],
        out_specs=pl.BlockSpec((tm, D_MODEL), lambda m: (m, 0)),
        compiler_params=_cparams(("parallel",)),
        name="norm_modulate",
    )(x2, g, mod4, mod4)


def _tile4(t):
    return jnp.concatenate([t, t, t, t], axis=1)


def _rotary(y, tabs, rows, shift):
    cos_ref, sin_lo_ref, sin_hi_ref = tabs
    width = y.shape[1]
    return (y * _tile4(cos_ref[rows, :])
            + pltpu.roll(y, width - shift, 1) * _tile4(sin_lo_ref[rows, :])
            + pltpu.roll(y, shift, 1) * _tile4(sin_hi_ref[rows, :]))


def _head_rmsnorm(y, g):
    outs = []
    for h in range(y.shape[1] // HEAD_DIM):
        yh = y[:, h * HEAD_DIM:(h + 1) * HEAD_DIM]
        outs.append(yh * lax.rsqrt(jnp.mean(yh * yh, axis=-1, keepdims=True) + EPS) * g)
    return jnp.concatenate(outs, axis=1)


def _proj_kernel(h_ref, w_ref, ca_ref, sal_ref, sah_ref, cb_ref, sbl_ref, sbh_ref,
                 qg_ref, kg_ref, bg_ref, o_ref, gate_ref):
    n = pl.program_id(1)
    is_a = n < A_BLKS
    part = n % 3
    tabs_a = (ca_ref, sal_ref, sah_ref)
    tabs_b = (cb_ref, sbl_ref, sbh_ref)
    sub_rows = 256

    def run(epilogue, dst_ref=o_ref):
        w = w_ref[...].astype(BF16)
        for r in range(h_ref.shape[0] // sub_rows):
            rows = pl.ds(r * sub_rows, sub_rows)
            acc = jnp.dot(h_ref[rows, :], w, preferred_element_type=F32)
            dst_ref[rows, :] = epilogue(acc, rows).astype(dst_ref.dtype)

    @pl.when(is_a & (part == 0))
    def _():
        run(lambda acc, rows: _rotary(acc, tabs_a, rows, ROPE_DIMS // 2) * (ATTN_SCALE * LOG2E))

    @pl.when(is_a & (part == 1))
    def _():
        run(lambda acc, rows: _rotary(acc, tabs_a, rows, ROPE_DIMS // 2))

    @pl.when((is_a & (part == 2)) | (n == BV_BLK))
    def _():
        run(lambda acc, rows: acc)

    @pl.when((n >= BQ_BLK0) & (n < BK_BLK))
    def _():
        run(lambda acc, rows: _rotary(_head_rmsnorm(acc, qg_ref[...]), tabs_b, rows, AXIAL_DIMS // 2)
            * (ATTN_SCALE * LOG2E))

    @pl.when(n == BK_BLK)
    def _():
        run(lambda acc, rows: _rotary(_head_rmsnorm(acc, kg_ref[...]), tabs_b, rows, AXIAL_DIMS // 2))

    @pl.when(n >= GATE_BLK0)
    def _():
        run(lambda acc, rows: jax.nn.sigmoid(acc + bg_ref[...]), gate_ref)


def _projection(h, w_in, tabs_a, tabs_b, qg, kg, b_gate):
    tm = 1024
    per_b = SEQ // tm
    tab_spec = pl.BlockSpec((tm, LANES), lambda m, n: (m % per_b, 0))
    vec_spec = pl.BlockSpec((1, HEAD_DIM), lambda m, n: (0, 0))
    n_gate_blks = GATE_COLS // COL_BLK

    def gate_blk(n):
        return jnp.clip(n - GATE_BLK0, 0, n_gate_blks - 1)

    return pl.pallas_call(
        _proj_kernel,
        out_shape=(jax.ShapeDtypeStruct((N_TOK, QKV_COLS), BF16),
                   jax.ShapeDtypeStruct((N_TOK, GATE_COLS), BF16)),
        grid=(N_TOK // tm, N_COL_BLKS),
        in_specs=[pl.BlockSpec((tm, D_MODEL), lambda m, n: (m, 0)),
                  pl.BlockSpec((D_MODEL, COL_BLK), lambda m, n: (0, n)),
                  tab_spec, tab_spec, tab_spec, tab_spec, tab_spec, tab_spec,
                  vec_spec, vec_spec,
                  pl.BlockSpec((1, COL_BLK), lambda m, n: (0, gate_blk(n)))],
        out_specs=(pl.BlockSpec((tm, COL_BLK), lambda m, n: (m, jnp.minimum(n, GATE_BLK0 - 1))),
                   pl.BlockSpec((tm, COL_BLK), lambda m, n: (m, gate_blk(n)))),
        compiler_params=_cparams(("parallel", "arbitrary")),
        name="in_projection",
    )(h, w_in, *tabs_a, *tabs_b, qg, kg, b_gate)


def _attn_a_kernel(q_ref, k_ref, v_ref, o_ref, lse_ref, *, dilation, reach):
    i = pl.program_id(1)
    tq = q_ref.shape[0]
    win = tq + 2 * reach
    start = pl.multiple_of(jnp.clip(i * tq - reach, 0, SEQ - win), GRANULE)
    q = q_ref[...]
    k = k_ref[pl.ds(start, win), :]
    v = v_ref[pl.ds(start, win), :]
    diff = (i * tq - start) + lax.broadcasted_iota(jnp.int32, (tq, win), 0) \
        - lax.broadcasted_iota(jnp.int32, (tq, win), 1)
    valid = (jnp.abs(diff) <= reach) & ((diff & (dilation - 1)) == 0)
    lane = lax.broadcasted_iota(jnp.int32, (tq, LANES), 1)
    lse_tile = jnp.zeros((tq, LANES), F32)
    outs = []
    for h in range(A_HEADS):
        sl = slice(h * HEAD_DIM, (h + 1) * HEAD_DIM)
        s = lax.dot_general(q[:, sl], k[:, sl], (((1,), (1,)), ((), ())), preferred_element_type=F32)
        s = jnp.where(valid, s, NEG)
        m = jnp.max(s, axis=-1, keepdims=True)
        p = jnp.exp2(s - m)
        l = jnp.sum(p, axis=-1, keepdims=True)
        o = jnp.dot(p.astype(BF16), v[:, sl], preferred_element_type=F32)
        outs.append(o / l)
        lse_tile = jnp.where(lane == h, m * LN2 + jnp.log(l), lse_tile)
    o_ref[...] = jnp.concatenate(outs, axis=1).astype(o_ref.dtype)
    lse_ref[...] = lse_tile


def _attn_a(proj3, group, window, dilation):
    assert dilation & (dilation - 1) == 0
    reach = window // 2
    tq = 256
    qb, kb, vb = 3 * group, 3 * group + 1, 3 * group + 2
    o, lse = pl.pallas_call(
        functools.partial(_attn_a_kernel, dilation=dilation, reach=reach),
        out_shape=(jax.ShapeDtypeStruct((BATCH, SEQ, A_WIDTH), BF16),
                   jax.ShapeDtypeStruct((BATCH, SEQ, LANES), F32)),
        grid=(BATCH, SEQ // tq),
        in_specs=[pl.BlockSpec((None, tq, COL_BLK), lambda b, i: (b, i, qb)),
                  pl.BlockSpec((None, SEQ, COL_BLK), lambda b, i: (b, 0, kb)),
                  pl.BlockSpec((None, SEQ, COL_BLK), lambda b, i: (b, 0, vb))],
        out_specs=(pl.BlockSpec((None, tq, A_WIDTH), lambda b, i: (b, i, 0)),
                   pl.BlockSpec((None, tq, LANES), lambda b, i: (b, i, 0))),
        compiler_params=_cparams(("parallel", "arbitrary")),
        name=f"dilated_attention_g{group}",
    )(proj3, proj3, proj3)
    return o.reshape(N_TOK, A_WIDTH), lse.reshape(N_TOK, LANES)


def _attn_b_kernel(q_ref, k_ref, v_ref, o_ref, vx_ref, acc_ref, *, tk):
    tq = q_ref.shape[0]
    n_chunks = SEQ // tk

    @pl.when(pl.program_id(2) == 0)
    def _():
        vx_ref[:, 0:HEAD_DIM] = v_ref[...]
        vx_ref[:, HEAD_DIM:2 * HEAD_DIM] = jnp.ones((SEQ, HEAD_DIM), BF16)

    q = q_ref[...]
    qs = jnp.concatenate([q[:, g * HEAD_DIM:(g + 1) * HEAD_DIM] for g in range(B_GROUP)], axis=0)
    acc_ref[...] = jnp.zeros_like(acc_ref)
    m = jnp.full((B_GROUP * tq, 1), -jnp.inf, F32)
    for c in range(n_chunks):
        keys = slice(c * tk, (c + 1) * tk)
        s = lax.dot_general(qs, k_ref[keys, :], (((1,), (1,)), ((), ())), preferred_element_type=F32)
        m_new = jnp.maximum(m, jnp.max(s, axis=-1, keepdims=True))
        p = jnp.exp2(s - m_new).astype(BF16)
        acc_ref[...] = jnp.exp2(m - m_new) * acc_ref[...] + jnp.dot(p, vx_ref[keys, :],
                                                                    preferred_element_type=F32)
        m = m_new
    o = acc_ref[:, 0:HEAD_DIM] / acc_ref[:, HEAD_DIM:2 * HEAD_DIM]
    o_ref[...] = jnp.concatenate([o[g * tq:(g + 1) * tq] for g in range(B_GROUP)], axis=1).astype(o_ref.dtype)


def _attn_b(proj3):
    tq = 128
    tk = 256
    kcol0 = BK_BLK * COL_BLK // HEAD_DIM
    vcol0 = BV_BLK * COL_BLK // HEAD_DIM
    o = pl.pallas_call(
        functools.partial(_attn_b_kernel, tk=tk),
        out_shape=jax.ShapeDtypeStruct((BATCH, SEQ, B_Q_WIDTH), BF16),
        grid=(BATCH, B_KV_HEADS, SEQ // tq),
        in_specs=[pl.BlockSpec((None, tq, COL_BLK), lambda b, h, i: (b, i, BQ_BLK0 + h)),
                  pl.BlockSpec((None, SEQ, HEAD_DIM), lambda b, h, i: (b, 0, kcol0 + h)),
                  pl.BlockSpec((None, SEQ, HEAD_DIM), lambda b, h, i: (b, 0, vcol0 + h))],
        out_specs=pl.BlockSpec((None, tq, COL_BLK), lambda b, h, i: (b, i, h)),
        scratch_shapes=[pltpu.VMEM((SEQ, 2 * HEAD_DIM), BF16),
                        pltpu.VMEM((B_GROUP * tq, 2 * HEAD_DIM), F32)],
        compiler_params=_cparams(("parallel", "parallel", "arbitrary")),
        name="gqa_attention",
    )(proj3, proj3, proj3)
    return o.reshape(N_TOK, B_Q_WIDTH)


def _merge_kernel(o0_ref, o1_ref, o2_ref, l0_ref, l1_ref, l2_ref, yb_ref, wa_ref, wb_ref, ga_ref, gb_ref,
                  out_ref):
    sub_rows = 256
    for r in range(out_ref.shape[0] // sub_rows):
        rows = pl.ds(r * sub_rows, sub_rows)
        lses = (l0_ref[rows, :], l1_ref[rows, :], l2_ref[rows, :])
        outs = (o0_ref[rows, :], o1_ref[rows, :], o2_ref[rows, :])
        cols = []
        for h in range(A_HEADS):
            lh = [jnp.broadcast_to(l[:, h:h + 1], (sub_rows, HEAD_DIM)) for l in lses]
            mx = jnp.maximum(jnp.maximum(lh[0], lh[1]), lh[2])
            e = [jnp.exp(v - mx) for v in lh]
            den = e[0] + e[1] + e[2]
            sl = slice(h * HEAD_DIM, (h + 1) * HEAD_DIM)
            cols.append(sum((e[g] / den) * outs[g][:, sl].astype(F32) for g in range(A_N_GROUPS)))
        mix_a = jnp.concatenate(cols, axis=1).astype(BF16)
        mix_b = yb_ref[rows, :]
        for cb in range(D_MODEL // COL_BLK):
            cs = pl.ds(cb * COL_BLK, COL_BLK)
            ya = jnp.dot(mix_a, wa_ref[:, cs], preferred_element_type=F32)
            yb = jnp.dot(mix_b, wb_ref[:, cs], preferred_element_type=F32)
            out_ref[rows, cs] = (ga_ref[rows, cs].astype(F32) * ya
                                 + gb_ref[rows, cs].astype(F32) * yb).astype(out_ref.dtype)


def _merge(o_groups, lse_groups, mix_b, gates, w_a_up, w_b_up):
    tm = 512
    o_spec = pl.BlockSpec((tm, A_WIDTH), lambda m: (m, 0))
    l_spec = pl.BlockSpec((tm, LANES), lambda m: (m, 0))
    return pl.pallas_call(
        _merge_kernel,
        out_shape=jax.ShapeDtypeStruct((N_TOK, D_MODEL), BF16),
        grid=(N_TOK // tm,),
        in_specs=[o_spec, o_spec, o_spec, l_spec, l_spec, l_spec,
                  pl.BlockSpec((tm, B_Q_WIDTH), lambda m: (m, 0)),
                  pl.BlockSpec((A_WIDTH, D_MODEL), lambda m: (0, 0)),
                  pl.BlockSpec((B_Q_WIDTH, D_MODEL), lambda m: (0, 0)),
                  pl.BlockSpec((tm, D_MODEL), lambda m: (m, 0)),
                  pl.BlockSpec((tm, D_MODEL), lambda m: (m, 1))],
        out_specs=pl.BlockSpec((tm, D_MODEL), lambda m: (m, 0)),
        compiler_params=_cparams(("parallel",)),
        name="branch_merge",
    )(*o_groups, *lse_groups, mix_b, w_a_up, w_b_up, gates, gates)


def _outproj_kernel(mg_ref, w_ref, x_ref, gt_ref, g_ref, sc_ref, sh_ref, wr_ref, x1_ref, h2_ref, lg_ref):
    sub_rows = 256
    wr = wr_ref[...]
    wr_hi = wr.astype(BF16)
    wr_lo = (wr - wr_hi.astype(F32)).astype(BF16)
    for r in range(mg_ref.shape[0] // sub_rows):
        rows = pl.ds(r * sub_rows, sub_rows)
        y = jnp.dot(mg_ref[rows, :], w_ref[...], preferred_element_type=F32)
        x1 = x_ref[rows, :] + gt_ref[...] * y
        x1_ref[rows, :] = x1
        h2 = _normmod(x1, g_ref[...], sc_ref[...], sh_ref[...])
        h2_hi = h2.astype(BF16)
        h2_ref[rows, :] = h2_hi
        h2_lo = (h2 - h2_hi.astype(F32)).astype(BF16)
        nt = (((1,), (1,)), ((), ()))
        lg_ref[:, rows] = (lax.dot_general(wr_hi, h2_hi, nt, preferred_element_type=F32)
                           + lax.dot_general(wr_hi, h2_lo, nt, preferred_element_type=F32)
                           + lax.dot_general(wr_lo, h2_hi, nt, preferred_element_type=F32))


def _out_projection(merged, w_out, x2, mod4, g_ffn, w_router_t):
    tm = 512
    per_b = SEQ // tm

    def mod_spec(j):
        return pl.BlockSpec((None, None, 1, D_MODEL), lambda m: (m // per_b, j, 0, 0))

    return pl.pallas_call(
        _outproj_kernel,
        out_shape=(jax.ShapeDtypeStruct((N_TOK, D_MODEL), F32),
                   jax.ShapeDtypeStruct((N_TOK, D_MODEL), BF16),
                   jax.ShapeDtypeStruct((N_EXPERTS, N_TOK), F32)),
        grid=(N_TOK // tm,),
        in_specs=[pl.BlockSpec((tm, D_MODEL), lambda m: (m, 0)),
                  pl.BlockSpec((D_MODEL, D_MODEL), lambda m: (0, 0)),
                  pl.BlockSpec((tm, D_MODEL), lambda m: (m, 0)),
                  mod_spec(2),
                  pl.BlockSpec((1, D_MODEL), lambda m: (0, 0)),
                  mod_spec(4), mod_spec(3),
                  pl.BlockSpec((N_EXPERTS, D_MODEL), lambda m: (0, 0))],
        out_specs=(pl.BlockSpec((tm, D_MODEL), lambda m: (m, 0)),
                   pl.BlockSpec((tm, D_MODEL), lambda m: (m, 0)),
                   pl.BlockSpec((N_EXPERTS, tm), lambda m: (0, m))),
        compiler_params=_cparams(("parallel",)),
        name="out_projection",
    )(merged, w_out, x2, mod4, g_ffn, mod4, mod4, w_router_t)


def _route_kernel(lg_ref, eb_ref, csm_ref, cnt_ref, gx_ref):
    tt = lg_ref.shape[1]
    scores = jax.nn.sigmoid(lg_ref[...])
    sel = scores + eb_ref[...]
    neg_inf = -jnp.inf
    sel_g = [sel[PER_GROUP * g:PER_GROUP * (g + 1), :] for g in range(N_EXPERT_GROUPS)]
    grp = []
    for v in sel_g:
        m1 = jnp.max(v, axis=0, keepdims=True)
        is1 = v == m1
        n1 = jnp.sum(jnp.where(is1, 1.0, 0.0), axis=0, keepdims=True)
        rest = jnp.max(jnp.where(is1, neg_inf, v), axis=0, keepdims=True)
        grp.append(m1 + jnp.where(n1 >= 2.0, m1, rest))
    masked = []
    for g in range(N_EXPERT_GROUPS):
        rank = jnp.zeros((1, tt), F32)
        for g2 in range(N_EXPERT_GROUPS):
            if g2 != g:
                beats = (grp[g2] >= grp[g]) if g2 < g else (grp[g2] > grp[g])
                rank = rank + jnp.where(beats, 1.0, 0.0)
        keep = jnp.broadcast_to(rank < TOPK_GROUPS, (PER_GROUP, tt))
        masked.append(jnp.where(keep, sel_g[g], neg_inf))
    sub = lax.broadcasted_iota(jnp.int32, (PER_GROUP, tt), 0)
    ranks = [jnp.zeros((PER_GROUP, tt), F32) for _ in range(N_EXPERT_GROUPS)]
    for g2 in range(N_EXPERT_GROUPS):
        for m2 in range(PER_GROUP):
            vf = jnp.broadcast_to(masked[g2][m2:m2 + 1, :], (PER_GROUP, tt))
            for g in range(N_EXPERT_GROUPS):
                if g2 < g:
                    beats = vf >= masked[g]
                elif g2 > g:
                    beats = vf > masked[g]
                else:
                    beats = (vf > masked[g]) | ((vf == masked[g]) & (sub > m2))
                ranks[g] = ranks[g] + jnp.where(beats, 1.0, 0.0)
    picked = [ranks[g] < TOP_K for g in range(N_EXPERT_GROUPS)]
    chosen = [jnp.where(picked[g], scores[PER_GROUP * g:PER_GROUP * (g + 1), :], 0.0)
              for g in range(N_EXPERT_GROUPS)]
    total = chosen[0]
    for g in range(1, N_EXPERT_GROUPS):
        total = total + chosen[g]
    denom = jnp.sum(total, axis=0, keepdims=True)
    gate = jnp.concatenate([chosen[g] / denom * ROUTED_SCALE for g in range(N_EXPERT_GROUPS)], axis=0)
    mask = jnp.concatenate([jnp.where(picked[g], 1.0, 0.0) for g in range(N_EXPERT_GROUPS)], axis=0)
    mask_bf = mask.astype(BF16)

    upper = jnp.where(lax.broadcasted_iota(jnp.int32, (tt, tt), 0) <= lax.broadcasted_iota(jnp.int32, (tt, tt), 1),
                      1.0, 0.0).astype(BF16)
    cs = jnp.dot(mask_bf, upper, preferred_element_type=F32)
    csm_ref[...] = jnp.where(mask > 0.0, cs, 0.0)
    cnt_ref[...] = lax.dot_general(jnp.ones((SUBLANES, tt), BF16), mask_bf, (((1,), (1,)), ((), ())),
                                   preferred_element_type=F32)

    hi = gate.astype(BF16).astype(F32)
    gx_ref[...] = jnp.concatenate([hi, gate - hi], axis=0).T.astype(BF16)


def _route(logits_t, e_bias_col):
    return pl.pallas_call(
        _route_kernel,
        out_shape=(jax.ShapeDtypeStruct((N_EXPERTS, N_TOK), F32),
                   jax.ShapeDtypeStruct((N_CHUNKS, SUBLANES, N_EXPERTS), F32),
                   jax.ShapeDtypeStruct((N_TOK, LANES), BF16)),
        grid=(N_CHUNKS,),
        in_specs=[pl.BlockSpec((N_EXPERTS, CHUNK), lambda c: (0, c)),
                  pl.BlockSpec((N_EXPERTS, 1), lambda c: (0, 0))],
        out_specs=(pl.BlockSpec((N_EXPERTS, CHUNK), lambda c: (0, c)),
                   pl.BlockSpec((None, SUBLANES, N_EXPERTS), lambda c: (c, 0, 0)),
                   pl.BlockSpec((CHUNK, LANES), lambda c: (c, 0))),
        compiler_params=_cparams(("parallel",)),
        name="routing",
    )(logits_t, e_bias_col)


def _plan_kernel(cnt_ref, goff_ref, tail_ref, tile_e_ref, ntile_ref, next_ref):
    def per_expert(e, carry):
        base, tbase, prev = carry

        def per_chunk(c, off):
            goff_ref[c, e] = base + off
            return off + (((cnt_ref[c, e] + (GRANULE - 1)) >> GRANULE_SHIFT) << GRANULE_SHIFT)

        tot = lax.fori_loop(0, N_CHUNKS, per_chunk, jnp.int32(0))
        region = ((tot + (TILE - 1)) >> TILE_SHIFT) << TILE_SHIFT
        tail_ref[0, e] = base + tot
        tail_ref[1, e] = (region - tot) >> GRANULE_SHIFT
        n_t = region >> TILE_SHIFT

        def per_tile(j, _):
            tile_e_ref[tbase + j] = e
            return 0

        lax.fori_loop(0, n_t, per_tile, 0)
        next_ref[e] = jnp.int32(-1)

        @pl.when((n_t > 0) & (prev >= 0))
        def _():
            next_ref[jnp.maximum(prev, 0)] = e

        return base + region, tbase + n_t, jnp.where(n_t > 0, e, prev)

    _, n_tiles, _ = lax.fori_loop(0, N_EXPERTS, per_expert, (jnp.int32(0), jnp.int32(0), jnp.int32(-1)))
    ntile_ref[0] = n_tiles
    last_e = tile_e_ref[jnp.maximum(n_tiles - 1, 0)]

    def fill(j, _):
        tile_e_ref[j] = last_e
        return 0

    lax.fori_loop(n_tiles, MAX_TILES, fill, 0)


def _plan(cnt):
    smem = pl.BlockSpec(memory_space=pltpu.SMEM)
    return pl.pallas_call(
        _plan_kernel,
        out_shape=(jax.ShapeDtypeStruct((N_CHUNKS, N_EXPERTS), jnp.int32),
                   jax.ShapeDtypeStruct((2, N_EXPERTS), jnp.int32),
                   jax.ShapeDtypeStruct((MAX_TILES,), jnp.int32),
                   jax.ShapeDtypeStruct((1,), jnp.int32),
                   jax.ShapeDtypeStruct((N_EXPERTS,), jnp.int32)),
        in_specs=[smem],
        out_specs=(smem, smem, smem, smem, smem),
        name="row_plan",
    )(cnt)


def _chunk_granules(cnt_ref, goff_ref, c, table_ref, per_granule=None):
    def per_expert(e, n_before):
        n_gran = (cnt_ref[c, e] + (GRANULE - 1)) >> GRANULE_SHIFT
        base = goff_ref[c, e]

        def per_j(j, _):
            table_ref[n_before + j] = base + j * GRANULE
            if per_granule is not None:
                per_granule(e, j, n_before + j)
            return 0

        lax.fori_loop(0, n_gran, per_j, 0)
        return n_before + n_gran

    return lax.fori_loop(0, N_EXPERTS, per_expert, jnp.int32(0))


def _dispatch_kernel(cnt_ref, goff_ref, tail_ref, x_ref, gx_ref, csm_ref, xs_hbm,
                     onehot_ref, table_ref, buf_ref, zero_ref, sems, zsem):
    c = pl.program_id(0)
    sub = lax.broadcasted_iota(jnp.int32, (GRANULE, CHUNK), 0)

    def build_granule(e, j, g):
        want = (sub + (j * GRANULE + 1)).astype(F32)
        hit = csm_ref[pl.ds(e, 1), :] == want
        onehot_ref[pl.ds(pl.multiple_of(g * GRANULE, GRANULE), GRANULE), :] = jnp.where(hit, 1.0, 0.0).astype(BF16)

    n_gran = _chunk_granules(cnt_ref, goff_ref, c, table_ref, build_granule)
    n_tiles = (n_gran + (GRAN_PER_TILE - 1)) >> GPT_SHIFT

    def clear(g, _):
        onehot_ref[pl.ds(pl.multiple_of(g * GRANULE, GRANULE), GRANULE), :] = jnp.zeros((GRANULE, CHUNK), BF16)
        return 0

    lax.fori_loop(n_gran, n_tiles * GRAN_PER_TILE, clear, 0)

    def granule_copy(slot, i, row):
        return pltpu.make_async_copy(buf_ref.at[slot, pl.ds(pl.multiple_of(i * GRANULE, GRANULE), GRANULE), :],
                                     xs_hbm.at[pl.ds(pl.multiple_of(row, GRANULE), GRANULE), :],
                                     sems.at[slot])

    def tile_granules(k):
        return jnp.minimum(GRAN_PER_TILE, n_gran - k * GRAN_PER_TILE)

    def wait_tile(k):
        slot = k % N_SLOTS

        def w(i, _):
            granule_copy(slot, i, 0).wait()
            return 0

        lax.fori_loop(0, tile_granules(k), w, 0)

    def per_tile(k, _):
        slot = k % N_SLOTS

        @pl.when(k >= N_SLOTS)
        def _():
            wait_tile(k - N_SLOTS)

        oh = onehot_ref[pl.ds(pl.multiple_of(k * TILE, TILE), TILE), :]
        buf_ref[slot, :, 0:D_MODEL] = jnp.dot(oh, x_ref[...], preferred_element_type=F32).astype(BF16)
        buf_ref[slot, :, D_MODEL:XS_WIDTH] = jnp.dot(oh, gx_ref[...], preferred_element_type=F32).astype(BF16)

        def s(i, _):
            granule_copy(slot, i, table_ref[k * GRAN_PER_TILE + i]).start()
            return 0

        lax.fori_loop(0, tile_granules(k), s, 0)
        return 0

    lax.fori_loop(0, n_tiles, per_tile, 0)

    def drain(k, _):
        wait_tile(k)
        return 0

    lax.fori_loop(jnp.maximum(n_tiles - N_SLOTS, 0), n_tiles, drain, 0)

    @pl.when(c == N_CHUNKS - 1)
    def _():
        zero_ref[...] = jnp.zeros_like(zero_ref)

        def tail_copy(row):
            return pltpu.make_async_copy(zero_ref, xs_hbm.at[pl.ds(pl.multiple_of(row, GRANULE), GRANULE), :], zsem)

        def start_e(e, _):
            def st(j, _):
                tail_copy(tail_ref[0, e] + j * GRANULE).start()
                return 0
            lax.fori_loop(0, tail_ref[1, e], st, 0)
            return 0

        def wait_e(e, _):
            def wt(j, _):
                tail_copy(0).wait()
                return 0
            lax.fori_loop(0, tail_ref[1, e], wt, 0)
            return 0

        lax.fori_loop(0, N_EXPERTS, start_e, 0)
        lax.fori_loop(0, N_EXPERTS, wait_e, 0)


def _dispatch(cnt, goff, tail, h2, gx, csm):
    grid_spec = pltpu.PrefetchScalarGridSpec(
        num_scalar_prefetch=3,
        grid=(N_CHUNKS,),
        in_specs=[pl.BlockSpec((CHUNK, D_MODEL), lambda c, *_: (c, 0)),
                  pl.BlockSpec((CHUNK, LANES), lambda c, *_: (c, 0)),
                  pl.BlockSpec((N_EXPERTS, CHUNK), lambda c, *_: (0, c))],
        out_specs=pl.BlockSpec(memory_space=pl.ANY),
        scratch_shapes=[pltpu.VMEM((CHUNK_ROWS, CHUNK), BF16),
                        pltpu.SMEM((CHUNK_ROWS // GRANULE,), jnp.int32),
                        pltpu.VMEM((N_SLOTS, TILE, XS_WIDTH), BF16),
                        pltpu.VMEM((GRANULE, XS_WIDTH), BF16),
                        pltpu.SemaphoreType.DMA((N_SLOTS,)),
                        pltpu.SemaphoreType.DMA(())])
    return pl.pallas_call(
        _dispatch_kernel,
        out_shape=jax.ShapeDtypeStruct((MAX_ROWS, XS_WIDTH), BF16),
        grid_spec=grid_spec,
        compiler_params=_cparams(("arbitrary",)),
        name="moe_dispatch",
    )(cnt, goff, tail, h2, gx, csm)


def _routed_kernel(tile_e_ref, ntile_ref, next_ref, xs_hbm, w1_hbm, w3_hbm, w2_hbm, ys_hbm,
                   xbuf_ref, ybuf_ref, w1f_ref, w3f_ref, w2f_ref, w1s_ref, w3s_ref, w2s_ref, xsem, ysem, wsem):
    n_tiles = ntile_ref[0]

    def x_copy(t, slot):
        return pltpu.make_async_copy(xs_hbm.at[pl.ds(pl.multiple_of(t * TILE, TILE), TILE), :],
                                     xbuf_ref.at[slot], xsem.at[slot])

    def y_copy(t, slot):
        return pltpu.make_async_copy(ybuf_ref.at[slot],
                                     ys_hbm.at[pl.ds(pl.multiple_of(t * TILE, TILE), TILE), :], ysem.at[slot])

    def fetch(expert, slot):
        return (pltpu.make_async_copy(w1_hbm.at[expert], w1f_ref.at[slot], wsem.at[slot]),
                pltpu.make_async_copy(w3_hbm.at[expert], w3f_ref.at[slot], wsem.at[slot]),
                pltpu.make_async_copy(w2_hbm.at[expert], w2f_ref.at[slot], wsem.at[slot]))

    @pl.when(n_tiles > 0)
    def _():
        x_copy(0, 0).start()
        for cp in fetch(tile_e_ref[0], 0):
            cp.start()

    def per_tile(t, n_started):
        slot = t & 1
        e = tile_e_ref[t]
        x_copy(t, slot).wait()

        @pl.when(t + 1 < n_tiles)
        def _():
            x_copy(t + 1, 1 - slot).start()

        first = (t == 0) | (e != tile_e_ref[jnp.maximum(t - 1, 0)])

        @pl.when(first)
        def _():
            wslot = n_started & 1
            nxt = next_ref[e]

            @pl.when(nxt >= 0)
            def _():
                for cp in fetch(jnp.maximum(nxt, 0), 1 - wslot):
                    cp.start()

            for cp in fetch(e, wslot):
                cp.wait()
            w1s_ref[...] = w1f_ref[wslot].astype(BF16)
            w3s_ref[...] = w3f_ref[wslot].astype(BF16)
            w2s_ref[...] = w2f_ref[wslot].astype(BF16)

        @pl.when(t >= 2)
        def _():
            y_copy(t - 2, slot).wait()

        x = xbuf_ref[slot, :, 0:D_MODEL]
        gx = xbuf_ref[slot, :, D_MODEL:XS_WIDTH].astype(F32)
        lane = lax.broadcasted_iota(jnp.int32, gx.shape, 1)
        gate = jnp.sum(jnp.where((lane == e) | (lane == e + N_EXPERTS), gx, 0.0), axis=1, keepdims=True)
        a = jnp.dot(x, w1s_ref[...], preferred_element_type=F32)
        u = jnp.dot(x, w3s_ref[...], preferred_element_type=F32)
        hid = (a * jax.nn.sigmoid(a)) * u * gate
        ybuf_ref[slot] = jnp.dot(hid.astype(BF16), w2s_ref[...], preferred_element_type=F32).astype(ybuf_ref.dtype)
        y_copy(t, slot).start()
        return n_started + jnp.where(first, 1, 0)

    lax.fori_loop(0, n_tiles, per_tile, jnp.int32(0))

    @pl.when(n_tiles >= 2)
    def _():
        y_copy(n_tiles - 2, n_tiles & 1).wait()

    @pl.when(n_tiles >= 1)
    def _():
        y_copy(n_tiles - 1, (n_tiles - 1) & 1).wait()


def _routed_experts(tile_e, ntile, next_e, xs, w1, w3, w2):
    hbm = pl.BlockSpec(memory_space=pl.ANY)
    grid_spec = pltpu.PrefetchScalarGridSpec(
        num_scalar_prefetch=3,
        grid=(1,),
        in_specs=[hbm, hbm, hbm, hbm],
        out_specs=hbm,
        scratch_shapes=[pltpu.VMEM((2, TILE, XS_WIDTH), BF16),
                        pltpu.VMEM((2, TILE, D_MODEL), BF16),
                        pltpu.VMEM((2, D_MODEL, D_EXPERT), F32),
                        pltpu.VMEM((2, D_MODEL, D_EXPERT), F32),
                        pltpu.VMEM((2, D_EXPERT, D_MODEL), F32),
                        pltpu.VMEM((D_MODEL, D_EXPERT), BF16),
                        pltpu.VMEM((D_MODEL, D_EXPERT), BF16),
                        pltpu.VMEM((D_EXPERT, D_MODEL), BF16),
                        pltpu.SemaphoreType.DMA((2,)),
                        pltpu.SemaphoreType.DMA((2,)),
                        pltpu.SemaphoreType.DMA((2,))])
    return pl.pallas_call(
        _routed_kernel,
        out_shape=jax.ShapeDtypeStruct((MAX_ROWS, D_MODEL), BF16),
        grid_spec=grid_spec,
        compiler_params=_cparams(("arbitrary",)),
        name="routed_experts",
    )(tile_e, ntile, next_e, xs, w1, w3, w2)


def _combine_kernel(cnt_ref, goff_ref, csm_ref, x1_ref, sh_ref, gt_ref, gf_ref, ys_hbm, o_ref,
                    table_ref, buf_ref, acc_ref, pick_ref, sems):
    c = pl.program_id(0)
    sub = lax.broadcasted_iota(jnp.int32, (GRANULE, CHUNK), 0)

    def build_granule(e, j, g):
        want = (sub + (j * GRANULE + 1)).astype(F32)
        hit = csm_ref[pl.ds(e, 1), :] == want
        pick_ref[pl.ds(pl.multiple_of(g * GRANULE, GRANULE), GRANULE), :] = jnp.where(hit, 1.0, 0.0).astype(BF16)

    n_gran = _chunk_granules(cnt_ref, goff_ref, c, table_ref, build_granule)
    n_tiles = (n_gran + (COMBINE_GPT - 1)) >> COMBINE_GPT_SHIFT

    def clear_pick(g, _):
        pick_ref[pl.ds(pl.multiple_of(g * GRANULE, GRANULE), GRANULE), :] = jnp.zeros((GRANULE, CHUNK), BF16)
        return 0

    lax.fori_loop(n_gran, n_tiles * COMBINE_GPT, clear_pick, 0)

    def granule_copy(slot, i, row):
        return pltpu.make_async_copy(ys_hbm.at[pl.ds(pl.multiple_of(row, GRANULE), GRANULE), :],
                                     buf_ref.at[slot, pl.ds(pl.multiple_of(i * GRANULE, GRANULE), GRANULE), :],
                                     sems.at[slot])

    def tile_granules(k):
        return jnp.minimum(COMBINE_GPT, n_gran - k * COMBINE_GPT)

    def start_tile(k):
        slot = k % COMBINE_SLOTS

        def s(i, _):
            granule_copy(slot, i, table_ref[k * COMBINE_GPT + i]).start()
            return 0

        lax.fori_loop(0, tile_granules(k), s, 0)

    def wait_tile(k):
        slot = k % COMBINE_SLOTS

        def w(i, _):
            granule_copy(slot, i, 0).wait()
            return 0

        lax.fori_loop(0, tile_granules(k), w, 0)

    for k0 in range(COMBINE_SLOTS - 1):
        @pl.when(k0 < n_tiles)
        def _():
            start_tile(k0)

    acc_ref[...] = jnp.zeros_like(acc_ref)

    def per_tile(k, _):
        @pl.when(k + (COMBINE_SLOTS - 1) < n_tiles)
        def _():
            start_tile(k + (COMBINE_SLOTS - 1))

        wait_tile(k)
        slot = k % COMBINE_SLOTS

        def clear(i, _):
            buf_ref[slot, pl.ds(pl.multiple_of(i * GRANULE, GRANULE), GRANULE), :] = jnp.zeros((GRANULE, D_MODEL), BF16)
            return 0

        lax.fori_loop(tile_granules(k), COMBINE_GPT, clear, 0)

        pick = pick_ref[pl.ds(pl.multiple_of(k * COMBINE_TILE, COMBINE_TILE), COMBINE_TILE), :]
        acc_ref[...] += lax.dot_general(pick, buf_ref[slot], (((0,), (0,)), ((), ())),
                                        preferred_element_type=F32)
        return 0

    lax.fori_loop(0, n_tiles, per_tile, 0)

    x = x1_ref[...] + gt_ref[...] * (acc_ref[...] + sh_ref[...])
    o_ref[...] = x * lax.rsqrt(jnp.mean(x * x, axis=-1, keepdims=True) + EPS) * gf_ref[...]


def _combine(cnt, goff, csm, x1, shared, mod4, g_final, ys):
    per_b = SEQ // CHUNK
    row = pl.BlockSpec((CHUNK, D_MODEL), lambda c, *_: (c, 0))
    grid_spec = pltpu.PrefetchScalarGridSpec(
        num_scalar_prefetch=2,
        grid=(N_CHUNKS,),
        in_specs=[pl.BlockSpec((N_EXPERTS, CHUNK), lambda c, *_: (0, c)),
                  row, row,
                  pl.BlockSpec((None, None, 1, D_MODEL), lambda c, *_: (c // per_b, 5, 0, 0)),
                  pl.BlockSpec((1, D_MODEL), lambda c, *_: (0, 0)),
                  pl.BlockSpec(memory_space=pl.ANY)],
        out_specs=row,
        scratch_shapes=[pltpu.SMEM((CHUNK_ROWS // GRANULE,), jnp.int32),
                        pltpu.VMEM((COMBINE_SLOTS, COMBINE_TILE, D_MODEL), BF16),
                        pltpu.VMEM((CHUNK, D_MODEL), F32),
                        pltpu.VMEM((CHUNK_ROWS, CHUNK), BF16),
                        pltpu.SemaphoreType.DMA((COMBINE_SLOTS,))])
    return pl.pallas_call(
        _combine_kernel,
        out_shape=jax.ShapeDtypeStruct((N_TOK, D_MODEL), F32),
        grid_spec=grid_spec,
        compiler_params=_cparams(("arbitrary",)),
        name="moe_combine",
    )(cnt, goff, csm, x1, shared, mod4, g_final, ys)


def _shared_kernel(x_ref, w1_ref, w3_ref, w2_ref, o_ref):
    x = x_ref[...]
    a = jnp.dot(x, w1_ref[...], preferred_element_type=F32)
    u = jnp.dot(x, w3_ref[...], preferred_element_type=F32)
    hid = (a * jax.nn.sigmoid(a)) * u
    o_ref[...] = jnp.dot(hid.astype(BF16), w2_ref[...], preferred_element_type=F32).astype(o_ref.dtype)


def _shared_expert(h2, w1, w3, w2):
    tm = 1024
    return pl.pallas_call(
        _shared_kernel,
        out_shape=jax.ShapeDtypeStruct((N_TOK, D_MODEL), F32),
        grid=(N_TOK // tm,),
        in_specs=[pl.BlockSpec((tm, D_MODEL), lambda m: (m, 0)),
                  pl.BlockSpec((D_MODEL, D_SHARED), lambda m: (0, 0)),
                  pl.BlockSpec((D_MODEL, D_SHARED), lambda m: (0, 0)),
                  pl.BlockSpec((D_SHARED, D_MODEL), lambda m: (0, 0))],
        out_specs=pl.BlockSpec((tm, D_MODEL), lambda m: (m, 0)),
        compiler_params=_cparams(("parallel",)),
        name="shared_expert",
    )(h2, w1, w3, w2)


def _rope_angles(pos, dims, theta):
    inv = jnp.power(jnp.float32(theta), -jnp.arange(0, dims, 2, dtype=jnp.float32) / dims)
    return pos.astype(jnp.float32)[:, None] * inv[None, :]


def _rotary_tables(angle_blocks):
    cos_parts, lo_parts, hi_parts = [], [], []
    used = 0
    for ang in angle_blocks:
        c, s = jnp.cos(ang), jnp.sin(ang)
        z = jnp.zeros_like(s)
        cos_parts += [c, c]
        lo_parts += [-s, z]
        hi_parts += [z, s]
        used += 2 * ang.shape[1]
    rest = HEAD_DIM - used
    if rest:
        cos_parts.append(jnp.ones((SEQ, rest), F32))
        lo_parts.append(jnp.zeros((SEQ, rest), F32))
        hi_parts.append(jnp.zeros((SEQ, rest), F32))
    return (jnp.concatenate(cos_parts, axis=1), jnp.concatenate(lo_parts, axis=1),
            jnp.concatenate(hi_parts, axis=1))


def kernel(x, c, w_ada, b_ada, g_attn, w_in, b_gate, q_norm_g, k_norm_g, w_a_up, w_b_up, w_out,
           g_ffn, w_router, e_bias, w1, w3, w2, ws1, ws3, ws2, g_final):
    l = 0
    x2 = x.reshape(N_TOK, D_MODEL)
    pos = jnp.arange(SEQ)
    tabs_a = _rotary_tables([_rope_angles(pos, ROPE_DIMS, ROPE_THETA)])
    tabs_b = _rotary_tables([_rope_angles(pos // GRID_W, AXIAL_DIMS, AXIAL_THETA),
                             _rope_angles(pos % GRID_W, AXIAL_DIMS, AXIAL_THETA)])

    c_pad = jnp.zeros((SUBLANES, D_MODEL), F32).at[:BATCH].set(c)
    mod = _ada(c_pad, w_ada[l], b_ada[l].reshape(1, -1))
    mod4 = mod[:BATCH].reshape(BATCH, N_MOD, 1, D_MODEL)

    h = _norm_modulate(x2, g_attn[l].reshape(1, -1), mod4, 1, 0)
    proj, gates = _projection(h, w_in[l], tabs_a, tabs_b,
                       q_norm_g[l].reshape(1, -1), k_norm_g[l].reshape(1, -1), b_gate[l].reshape(1, -1))
    proj3 = proj.reshape(BATCH, SEQ, QKV_COLS)

    o_groups, lse_groups = [], []
    for gi, (window, dilation) in enumerate(A_PATTERNS):
        o, lse = _attn_a(proj3, gi, window, dilation)
        o_groups.append(o)
        lse_groups.append(lse)
    mix_b = _attn_b(proj3)

    merged = _merge(o_groups, lse_groups, mix_b, gates, w_a_up[l].astype(BF16), w_b_up[l].astype(BF16))
    x1, h2, logits_t = _out_projection(merged, w_out[l].astype(BF16), x2, mod4, g_ffn[l].reshape(1, -1),
                                       w_router[l].T)
    csm, cnt_f, gx = _route(logits_t, e_bias[l].reshape(-1, 1))
    cnt = cnt_f[:, 0, :].astype(jnp.int32)
    goff, tail, tile_e, ntile, next_e = _plan(cnt)
    xs = _dispatch(cnt, goff, tail, h2, gx, csm)
    ys = _routed_experts(tile_e, ntile, next_e, xs, w1[l], w3[l], w2[l])
    shared = _shared_expert(h2, ws1[l].astype(BF16), ws3[l].astype(BF16), ws2[l].astype(BF16))
    out = _combine(cnt, goff, csm, x1, shared, mod4, g_final.reshape(1, -1), ys)
    return out.reshape(BATCH, SEQ, D_MODEL)
```

```python
import functools

import jax
import jax.numpy as jnp
from jax import lax
from jax.experimental import pallas as pl
from jax.experimental.pallas import tpu as pltpu

F32 = jnp.float32
BF16 = jnp.bfloat16

D_MODEL = 2048
BATCH = 2
SEQ = 4096
N_TOK = BATCH * SEQ
HEAD_DIM = 128
EPS = 1e-6
A_PATTERNS = ((128, 1), (512, 4), (2048, 16))
A_N_GROUPS = len(A_PATTERNS)
A_HEADS = 4
A_WIDTH = A_HEADS * HEAD_DIM
ROPE_THETA = 500000.0
ROPE_DIMS = HEAD_DIM // 4
B_Q_HEADS = 16
B_KV_HEADS = 4
B_GROUP = B_Q_HEADS // B_KV_HEADS
B_Q_WIDTH = B_Q_HEADS * HEAD_DIM
B_KV_WIDTH = B_KV_HEADS * HEAD_DIM
AXIAL_THETA = 10000.0
AXIAL_DIMS = HEAD_DIM // 2
GRID_W = 64
A_COLS = A_N_GROUPS * 3 * A_WIDTH
B_COLS = B_Q_WIDTH + 2 * B_KV_WIDTH
GATE_COLS = 2 * D_MODEL
QKV_COLS = A_COLS + B_COLS
IN_COLS = QKV_COLS + GATE_COLS
N_MOD = 6
N_EXPERTS = 64
N_EXPERT_GROUPS = 8
PER_GROUP = N_EXPERTS // N_EXPERT_GROUPS
TOPK_GROUPS = 4
TOP_K = 8
D_EXPERT = D_MODEL // 4
D_SHARED = D_MODEL // 4
ROUTED_SCALE = 2.5
NEG = -1e30
ATTN_SCALE = HEAD_DIM ** -0.5
LOG2E = 1.4426950408889634
LN2 = 0.6931471805599453

LANES = 128
SUBLANES = 8
VMEM_LIMIT = 56 * 1024 * 1024

COL_BLK = 512
PERM_BLOCK = 256
N_COL_BLKS = IN_COLS // COL_BLK
A_BLKS = A_COLS // COL_BLK
BQ_BLK0 = A_BLKS
BK_BLK = BQ_BLK0 + B_Q_WIDTH // COL_BLK
BV_BLK = BK_BLK + 1
GATE_BLK0 = BV_BLK + 1

CHUNK = 512
N_CHUNKS = N_TOK // CHUNK
GRANULE = 16
GRANULE_SHIFT = GRANULE.bit_length() - 1
TILE = 256
TILE_SHIFT = TILE.bit_length() - 1
GRAN_PER_TILE = TILE // GRANULE
GPT_SHIFT = TILE_SHIFT - GRANULE_SHIFT
N_SLOTS = 4
COMBINE_TILE = 1024
COMBINE_GPT = COMBINE_TILE // GRANULE
COMBINE_GPT_SHIFT = COMBINE_GPT.bit_length() - 1
COMBINE_SLOTS = 3
WEIGHT_DMA_PRIORITY = 1
XS_WIDTH = D_MODEL + LANES
CHUNK_ROWS = -(-(CHUNK * TOP_K + N_EXPERTS * (GRANULE - 1)) // TILE) * TILE
MAX_ROWS = -(-(N_TOK * TOP_K + N_EXPERTS * N_CHUNKS * (GRANULE - 1) + N_EXPERTS * (TILE - 1)) // TILE) * TILE
MAX_TILES = MAX_ROWS // TILE


def _cparams(semantics):
    return pltpu.CompilerParams(dimension_semantics=semantics, vmem_limit_bytes=VMEM_LIMIT)


def _ada_kernel(c_ref, w_ref, b_ref, o_ref):
    c = c_ref[...]
    act = c * jax.nn.sigmoid(c)
    o_ref[...] = jnp.dot(act, w_ref[...], preferred_element_type=F32,
                         precision=lax.Precision.HIGHEST) + b_ref[...]


def _ada(c_pad, w_ada, b_ada):
    tn = 1024
    n_out = w_ada.shape[1]
    return pl.pallas_call(
        _ada_kernel,
        out_shape=jax.ShapeDtypeStruct((SUBLANES, n_out), F32),
        grid=(n_out // tn,),
        in_specs=[pl.BlockSpec((SUBLANES, D_MODEL), lambda n: (0, 0)),
                  pl.BlockSpec((D_MODEL, tn), lambda n: (0, n)),
                  pl.BlockSpec((1, tn), lambda n: (0, n))],
        out_specs=pl.BlockSpec((SUBLANES, tn), lambda n: (0, n)),
        compiler_params=_cparams(("parallel",)),
        name="ada_mod",
    )(c_pad, w_ada, b_ada)


def _normmod(x, g, sc, sh):
    y = x * lax.rsqrt(jnp.mean(x * x, axis=-1, keepdims=True) + EPS)
    return (y * g) * (1.0 + sc) + sh


def _normmod_kernel(x_ref, g_ref, sc_ref, sh_ref, o_ref):
    o_ref[...] = _normmod(x_ref[...], g_ref[...], sc_ref[...], sh_ref[...]).astype(o_ref.dtype)


def _norm_modulate(x2, g, mod4, sc_idx, sh_idx):
    tm = 512
    per_b = SEQ // tm
    return pl.pallas_call(
        _normmod_kernel,
        out_shape=jax.ShapeDtypeStruct((N_TOK, D_MODEL), BF16),
        grid=(N_TOK // tm,),
        in_specs=[pl.BlockSpec((tm, D_MODEL), lambda m: (m, 0)),
                  pl.BlockSpec((1, D_MODEL), lambda m: (0, 0)),
                  pl.BlockSpec((None, None, 1, D_MODEL), lambda m: (m // per_b, sc_idx, 0, 0)),
                  pl.BlockSpec((None, None, 1, D_MODEL), lambda m: (m // per_b, sh_idx, 0, 0))],
        out_specs=pl.BlockSpec((tm, D_MODEL), lambda m: (m, 0)),
        compiler_params=_cparams(("parallel",)),
        name="norm_modulate",
    )(x2, g, mod4, mod4)


def _tile4(t):
    return jnp.concatenate([t, t, t, t], axis=1)


def _rotary(y, tabs, rows, shift):
    cos_ref, sin_lo_ref, sin_hi_ref = tabs
    width = y.shape[1]
    return (y * _tile4(cos_ref[rows, :])
            + pltpu.roll(y, width - shift, 1) * _tile4(sin_lo_ref[rows, :])
            + pltpu.roll(y, shift, 1) * _tile4(sin_hi_ref[rows, :]))


def _head_rmsnorm(y, g):
    outs = []
    for h in range(y.shape[1] // HEAD_DIM):
        yh = y[:, h * HEAD_DIM:(h + 1) * HEAD_DIM]
        outs.append(yh * lax.rsqrt(jnp.mean(yh * yh, axis=-1, keepdims=True) + EPS) * g)
    return jnp.concatenate(outs, axis=1)


def _proj_kernel(h_ref, w_ref, ca_ref, sal_ref, sah_ref, cb_ref, sbl_ref, sbh_ref,
                 qg_ref, kg_ref, bg_ref, perm_ref, o_ref, gate_ref):
    n = pl.program_id(1)
    is_a = n < A_BLKS
    part = n % 3
    dilated = n >= 3
    tabs_a = (ca_ref, sal_ref, sah_ref)
    tabs_b = (cb_ref, sbl_ref, sbh_ref)
    sub_rows = PERM_BLOCK

    def run(epilogue, dst_ref=o_ref, permute=False):
        w = w_ref[...].astype(BF16)
        for r in range(h_ref.shape[0] // sub_rows):
            rows = pl.ds(r * sub_rows, sub_rows)
            acc = jnp.dot(h_ref[rows, :], w, preferred_element_type=F32)
            val = epilogue(acc, rows).astype(dst_ref.dtype)
            if permute:
                val = jnp.dot(perm_ref[...], val, preferred_element_type=F32).astype(dst_ref.dtype)
            dst_ref[rows, :] = val

    def rope_q(acc, rows):
        return _rotary(acc, tabs_a, rows, ROPE_DIMS // 2) * (ATTN_SCALE * LOG2E)

    def rope_k(acc, rows):
        return _rotary(acc, tabs_a, rows, ROPE_DIMS // 2)

    def plain(acc, rows):
        return acc

    @pl.when(is_a & (part == 0) & jnp.logical_not(dilated))
    def _():
        run(rope_q)

    @pl.when(is_a & (part == 0) & dilated)
    def _():
        run(rope_q, permute=True)

    @pl.when(is_a & (part == 1) & jnp.logical_not(dilated))
    def _():
        run(rope_k)

    @pl.when(is_a & (part == 1) & dilated)
    def _():
        run(rope_k, permute=True)

    @pl.when((is_a & (part == 2) & jnp.logical_not(dilated)) | (n == BV_BLK))
    def _():
        run(plain)

    @pl.when(is_a & (part == 2) & dilated)
    def _():
        run(plain, permute=True)

    @pl.when((n >= BQ_BLK0) & (n < BK_BLK))
    def _():
        run(lambda acc, rows: _rotary(_head_rmsnorm(acc, qg_ref[...]), tabs_b, rows, AXIAL_DIMS // 2)
            * (ATTN_SCALE * LOG2E))

    @pl.when(n == BK_BLK)
    def _():
        run(lambda acc, rows: _rotary(_head_rmsnorm(acc, kg_ref[...]), tabs_b, rows, AXIAL_DIMS // 2))

    @pl.when(n >= GATE_BLK0)
    def _():
        run(lambda acc, rows: jax.nn.sigmoid(acc + bg_ref[...]), gate_ref)


def _residue_perm(dilation):
    per_res = PERM_BLOCK // dilation
    row = jnp.arange(PERM_BLOCK)
    src = (row % per_res) * dilation + row // per_res
    return (src[:, None] == jnp.arange(PERM_BLOCK)[None, :]).astype(BF16)


def _projection(h, w_in, tabs_a, tabs_b, qg, kg, b_gate, perms):
    tm = 2048
    per_b = SEQ // tm
    tab_spec = pl.BlockSpec((tm, LANES), lambda m, n: (m % per_b, 0))
    vec_spec = pl.BlockSpec((1, HEAD_DIM), lambda m, n: (0, 0))
    n_gate_blks = GATE_COLS // COL_BLK

    def gate_blk(n):
        return jnp.clip(n - GATE_BLK0, 0, n_gate_blks - 1)

    return pl.pallas_call(
        _proj_kernel,
        out_shape=(jax.ShapeDtypeStruct((N_TOK, QKV_COLS), BF16),
                   jax.ShapeDtypeStruct((N_TOK, GATE_COLS), BF16)),
        grid=(N_TOK // tm, N_COL_BLKS),
        in_specs=[pl.BlockSpec((tm, D_MODEL), lambda m, n: (m, 0)),
                  pl.BlockSpec((D_MODEL, COL_BLK), lambda m, n: (0, n)),
                  tab_spec, tab_spec, tab_spec, tab_spec, tab_spec, tab_spec,
                  vec_spec, vec_spec,
                  pl.BlockSpec((1, COL_BLK), lambda m, n: (0, gate_blk(n))),
                  pl.BlockSpec((None, PERM_BLOCK, PERM_BLOCK),
                               lambda m, n: (jnp.clip(n // 3, 0, A_N_GROUPS - 1), 0, 0))],
        out_specs=(pl.BlockSpec((tm, COL_BLK), lambda m, n: (m, jnp.minimum(n, GATE_BLK0 - 1))),
                   pl.BlockSpec((tm, COL_BLK), lambda m, n: (m, gate_blk(n)))),
        compiler_params=_cparams(("parallel", "arbitrary")),
        name="in_projection",
    )(h, w_in, *tabs_a, *tabs_b, qg, kg, b_gate, perms)


def _attn_a_kernel(q_ref, k_ref, v_ref, o_ref, lse_ref, *, dilation, reach):
    i = pl.program_id(1)
    tq = q_ref.shape[0]
    win = tq + 2 * reach
    start = pl.multiple_of(jnp.clip(i * tq - reach, 0, SEQ - win), GRANULE)
    q = q_ref[...]
    k = k_ref[pl.ds(start, win), :]
    v = v_ref[pl.ds(start, win), :]
    diff = (i * tq - start) + lax.broadcasted_iota(jnp.int32, (tq, win), 0) \
        - lax.broadcasted_iota(jnp.int32, (tq, win), 1)
    valid = (jnp.abs(diff) <= reach) & ((diff & (dilation - 1)) == 0)
    lane = lax.broadcasted_iota(jnp.int32, (tq, LANES), 1)
    lse_tile = jnp.zeros((tq, LANES), F32)
    outs = []
    for h in range(A_HEADS):
        sl = slice(h * HEAD_DIM, (h + 1) * HEAD_DIM)
        s = lax.dot_general(q[:, sl], k[:, sl], (((1,), (1,)), ((), ())), preferred_element_type=F32)
        s = jnp.where(valid, s, NEG)
        m = jnp.max(s, axis=-1, keepdims=True)
        p = jnp.exp2(s - m)
        l = jnp.sum(p, axis=-1, keepdims=True)
        o = jnp.dot(p.astype(BF16), v[:, sl], preferred_element_type=F32)
        outs.append(o / l)
        lse_tile = jnp.where(lane == h, m * LN2 + jnp.log(l), lse_tile)
    o_ref[...] = jnp.concatenate(outs, axis=1).astype(o_ref.dtype)
    lse_ref[...] = lse_tile


def _attn_a(proj3, group, window, dilation):
    assert dilation & (dilation - 1) == 0
    reach = window // 2
    tq = 256
    qb, kb, vb = 3 * group, 3 * group + 1, 3 * group + 2
    o, lse = pl.pallas_call(
        functools.partial(_attn_a_kernel, dilation=dilation, reach=reach),
        out_shape=(jax.ShapeDtypeStruct((BATCH, SEQ, A_WIDTH), BF16),
                   jax.ShapeDtypeStruct((BATCH, SEQ, LANES), F32)),
        grid=(BATCH, SEQ // tq),
        in_specs=[pl.BlockSpec((None, tq, COL_BLK), lambda b, i: (b, i, qb)),
                  pl.BlockSpec((None, SEQ, COL_BLK), lambda b, i: (b, 0, kb)),
                  pl.BlockSpec((None, SEQ, COL_BLK), lambda b, i: (b, 0, vb))],
        out_specs=(pl.BlockSpec((None, tq, A_WIDTH), lambda b, i: (b, i, 0)),
                   pl.BlockSpec((None, tq, LANES), lambda b, i: (b, i, 0))),
        compiler_params=_cparams(("parallel", "arbitrary")),
        name=f"dilated_attention_g{group}",
    )(proj3, proj3, proj3)
    return o.reshape(N_TOK, A_WIDTH), lse.reshape(N_TOK, LANES)


def _attn_res_kernel(q_ref, k_ref, v_ref, o_ref, lse_ref, *, half_w):
    n_blk, per_res, _ = q_ref.shape
    length = n_blk * per_res
    q = q_ref[...].reshape(length, A_WIDTH)
    k = k_ref[...].reshape(length, A_WIDTH)
    v = v_ref[...].reshape(length, A_WIDTH)
    tq = 2 * half_w
    win = 4 * half_w
    lane = lax.broadcasted_iota(jnp.int32, (tq, LANES), 1)
    row_minus_col = lax.broadcasted_iota(jnp.int32, (tq, win), 0) - lax.broadcasted_iota(jnp.int32, (tq, win), 1)
    o_blocks, lse_blocks = [], []
    for qb in range(length // tq):
        start = min(max(qb * tq - half_w, 0), length - win)
        valid = jnp.abs(row_minus_col + (qb * tq - start)) <= half_w
        lse_tile = jnp.zeros((tq, LANES), F32)
        outs = []
        for h in range(A_HEADS):
            sl = slice(h * HEAD_DIM, (h + 1) * HEAD_DIM)
            s = lax.dot_general(q[qb * tq:(qb + 1) * tq, sl], k[start:start + win, sl],
                                (((1,), (1,)), ((), ())), preferred_element_type=F32)
            s = jnp.where(valid, s, NEG)
            m = jnp.max(s, axis=-1, keepdims=True)
            p = jnp.exp2(s - m)
            l = jnp.sum(p, axis=-1, keepdims=True)
            o = jnp.dot(p.astype(BF16), v[start:start + win, sl], preferred_element_type=F32)
            outs.append(o / l)
            lse_tile = jnp.where(lane == h, m * LN2 + jnp.log(l), lse_tile)
        o_blocks.append(jnp.concatenate(outs, axis=1).astype(o_ref.dtype))
        lse_blocks.append(lse_tile)
    o_ref[...] = jnp.concatenate(o_blocks, axis=0).reshape(n_blk, per_res, A_WIDTH)
    lse_ref[...] = jnp.concatenate(lse_blocks, axis=0).reshape(n_blk, per_res, LANES)


def _attn_a_dilated(proj3, group, window, dilation):
    per_res = PERM_BLOCK // dilation
    n_blk = SEQ // PERM_BLOCK
    half_w = (window // 2) // dilation
    view = proj3.reshape(BATCH, n_blk, dilation, per_res, QKV_COLS)

    def spec(width, blk):
        return pl.BlockSpec((None, n_blk, None, per_res, width), lambda b, r: (b, 0, r, 0, blk))

    o, lse = pl.pallas_call(
        functools.partial(_attn_res_kernel, half_w=half_w),
        out_shape=(jax.ShapeDtypeStruct((BATCH, n_blk, dilation, per_res, A_WIDTH), BF16),
                   jax.ShapeDtypeStruct((BATCH, n_blk, dilation, per_res, LANES), F32)),
        grid=(BATCH, dilation),
        in_specs=[spec(COL_BLK, 3 * group), spec(COL_BLK, 3 * group + 1), spec(COL_BLK, 3 * group + 2)],
        out_specs=(spec(A_WIDTH, 0), spec(LANES, 0)),
        compiler_params=_cparams(("parallel", "parallel")),
        name=f"dilated_attention_g{group}",
    )(view, view, view)
    return o.reshape(N_TOK, A_WIDTH), lse.reshape(N_TOK, LANES)


def _attn_b_kernel(q_ref, k_ref, v_ref, o_ref, vx_ref, acc_ref, *, tk):
    tq = q_ref.shape[0]
    n_chunks = SEQ // tk

    @pl.when(pl.program_id(2) == 0)
    def _():
        vx_ref[:, 0:HEAD_DIM] = v_ref[...]
        vx_ref[:, HEAD_DIM:2 * HEAD_DIM] = jnp.ones((SEQ, HEAD_DIM), BF16)

    q = q_ref[...]
    qs = jnp.concatenate([q[:, g * HEAD_DIM:(g + 1) * HEAD_DIM] for g in range(B_GROUP)], axis=0)
    acc_ref[...] = jnp.zeros_like(acc_ref)
    m = jnp.full((B_GROUP * tq, 1), -jnp.inf, F32)
    for c in range(n_chunks):
        keys = slice(c * tk, (c + 1) * tk)
        s = lax.dot_general(qs, k_ref[keys, :], (((1,), (1,)), ((), ())), preferred_element_type=F32)
        m_new = jnp.maximum(m, jnp.max(s, axis=-1, keepdims=True))
        p = jnp.exp2(s - m_new).astype(BF16)
        acc_ref[...] = jnp.exp2(m - m_new) * acc_ref[...] + jnp.dot(p, vx_ref[keys, :],
                                                                    preferred_element_type=F32)
        m = m_new
    o = acc_ref[:, 0:HEAD_DIM] / acc_ref[:, HEAD_DIM:2 * HEAD_DIM]
    o_ref[...] = jnp.concatenate([o[g * tq:(g + 1) * tq] for g in range(B_GROUP)], axis=1).astype(o_ref.dtype)


def _attn_b(proj3):
    tq = 128
    tk = 256
    kcol0 = BK_BLK * COL_BLK // HEAD_DIM
    vcol0 = BV_BLK * COL_BLK // HEAD_DIM
    o = pl.pallas_call(
        functools.partial(_attn_b_kernel, tk=tk),
        out_shape=jax.ShapeDtypeStruct((BATCH, SEQ, B_Q_WIDTH), BF16),
        grid=(BATCH, B_KV_HEADS, SEQ // tq),
        in_specs=[pl.BlockSpec((None, tq, COL_BLK), lambda b, h, i: (b, i, BQ_BLK0 + h)),
                  pl.BlockSpec((None, SEQ, HEAD_DIM), lambda b, h, i: (b, 0, kcol0 + h)),
                  pl.BlockSpec((None, SEQ, HEAD_DIM), lambda b, h, i: (b, 0, vcol0 + h))],
        out_specs=pl.BlockSpec((None, tq, COL_BLK), lambda b, h, i: (b, i, h)),
        scratch_shapes=[pltpu.VMEM((SEQ, 2 * HEAD_DIM), BF16),
                        pltpu.VMEM((B_GROUP * tq, 2 * HEAD_DIM), F32)],
        compiler_params=_cparams(("parallel", "parallel", "arbitrary")),
        name="gqa_attention",
    )(proj3, proj3, proj3)
    return o.reshape(N_TOK, B_Q_WIDTH)


def _merge_kernel(o0_ref, o1_ref, o2_ref, l0_ref, l1_ref, l2_ref, yb_ref, wa_ref, wb_ref, ga_ref, gb_ref,
                  unperm_ref, out_ref):
    sub_rows = PERM_BLOCK
    for r in range(out_ref.shape[0] // sub_rows):
        rows = pl.ds(r * sub_rows, sub_rows)
        outs = [o0_ref[rows, :].astype(F32)]
        lses = [l0_ref[rows, :]]
        for g, (o_ref, l_ref) in enumerate(((o1_ref, l1_ref), (o2_ref, l2_ref))):
            outs.append(jnp.dot(unperm_ref[g], o_ref[rows, :], preferred_element_type=F32))
            lses.append(jnp.dot(unperm_ref[g].astype(F32), l_ref[rows, :], preferred_element_type=F32,
                                precision=lax.Precision.HIGHEST))
        cols = []
        for h in range(A_HEADS):
            lh = [jnp.broadcast_to(l[:, h:h + 1], (sub_rows, HEAD_DIM)) for l in lses]
            mx = jnp.maximum(jnp.maximum(lh[0], lh[1]), lh[2])
            e = [jnp.exp(v - mx) for v in lh]
            den = e[0] + e[1] + e[2]
            sl = slice(h * HEAD_DIM, (h + 1) * HEAD_DIM)
            cols.append(sum((e[g] / den) * outs[g][:, sl] for g in range(A_N_GROUPS)))
        mix_a = jnp.concatenate(cols, axis=1).astype(BF16)
        mix_b = yb_ref[rows, :]
        for cb in range(D_MODEL // COL_BLK):
            cs = pl.ds(cb * COL_BLK, COL_BLK)
            ya = jnp.dot(mix_a, wa_ref[:, cs], preferred_element_type=F32)
            yb = jnp.dot(mix_b, wb_ref[:, cs], preferred_element_type=F32)
            out_ref[rows, cs] = (ga_ref[rows, cs].astype(F32) * ya
                                 + gb_ref[rows, cs].astype(F32) * yb).astype(out_ref.dtype)


def _merge(o_groups, lse_groups, mix_b, gates, w_a_up, w_b_up, unperms):
    tm = 512
    o_spec = pl.BlockSpec((tm, A_WIDTH), lambda m: (m, 0))
    l_spec = pl.BlockSpec((tm, LANES), lambda m: (m, 0))
    return pl.pallas_call(
        _merge_kernel,
        out_shape=jax.ShapeDtypeStruct((N_TOK, D_MODEL), BF16),
        grid=(N_TOK // tm,),
        in_specs=[o_spec, o_spec, o_spec, l_spec, l_spec, l_spec,
                  pl.BlockSpec((tm, B_Q_WIDTH), lambda m: (m, 0)),
                  pl.BlockSpec((A_WIDTH, D_MODEL), lambda m: (0, 0)),
                  pl.BlockSpec((B_Q_WIDTH, D_MODEL), lambda m: (0, 0)),
                  pl.BlockSpec((tm, D_MODEL), lambda m: (m, 0)),
                  pl.BlockSpec((tm, D_MODEL), lambda m: (m, 1)),
                  pl.BlockSpec((A_N_GROUPS - 1, PERM_BLOCK, PERM_BLOCK), lambda m: (0, 0, 0))],
        out_specs=pl.BlockSpec((tm, D_MODEL), lambda m: (m, 0)),
        compiler_params=_cparams(("parallel",)),
        name="branch_merge",
    )(*o_groups, *lse_groups, mix_b, w_a_up, w_b_up, gates, gates, unperms)


def _outproj_kernel(mg_ref, w_ref, x_ref, gt_ref, g_ref, sc_ref, sh_ref, wr_ref, x1_ref, h2_ref, lg_ref):
    sub_rows = 256
    wr = wr_ref[...]
    wr_hi = wr.astype(BF16)
    wr_lo = (wr - wr_hi.astype(F32)).astype(BF16)
    for r in range(mg_ref.shape[0] // sub_rows):
        rows = pl.ds(r * sub_rows, sub_rows)
        y = jnp.dot(mg_ref[rows, :], w_ref[...], preferred_element_type=F32)
        x1 = x_ref[rows, :] + gt_ref[...] * y
        x1_ref[rows, :] = x1
        h2 = _normmod(x1, g_ref[...], sc_ref[...], sh_ref[...])
        h2_hi = h2.astype(BF16)
        h2_ref[rows, :] = h2_hi
        h2_lo = (h2 - h2_hi.astype(F32)).astype(BF16)
        nt = (((1,), (1,)), ((), ()))
        lg_ref[:, rows] = (lax.dot_general(wr_hi, h2_hi, nt, preferred_element_type=F32)
                           + lax.dot_general(wr_hi, h2_lo, nt, preferred_element_type=F32)
                           + lax.dot_general(wr_lo, h2_hi, nt, preferred_element_type=F32))


def _out_projection(merged, w_out, x2, mod4, g_ffn, w_router_t):
    tm = 512
    per_b = SEQ // tm

    def mod_spec(j):
        return pl.BlockSpec((None, None, 1, D_MODEL), lambda m: (m // per_b, j, 0, 0))

    return pl.pallas_call(
        _outproj_kernel,
        out_shape=(jax.ShapeDtypeStruct((N_TOK, D_MODEL), F32),
                   jax.ShapeDtypeStruct((N_TOK, D_MODEL), BF16),
                   jax.ShapeDtypeStruct((N_EXPERTS, N_TOK), F32)),
        grid=(N_TOK // tm,),
        in_specs=[pl.BlockSpec((tm, D_MODEL), lambda m: (m, 0)),
                  pl.BlockSpec((D_MODEL, D_MODEL), lambda m: (0, 0)),
                  pl.BlockSpec((tm, D_MODEL), lambda m: (m, 0)),
                  mod_spec(2),
                  pl.BlockSpec((1, D_MODEL), lambda m: (0, 0)),
                  mod_spec(4), mod_spec(3),
                  pl.BlockSpec((N_EXPERTS, D_MODEL), lambda m: (0, 0))],
        out_specs=(pl.BlockSpec((tm, D_MODEL), lambda m: (m, 0)),
                   pl.BlockSpec((tm, D_MODEL), lambda m: (m, 0)),
                   pl.BlockSpec((N_EXPERTS, tm), lambda m: (0, m))),
        compiler_params=_cparams(("parallel",)),
        name="out_projection",
    )(merged, w_out, x2, mod4, g_ffn, mod4, mod4, w_router_t)


def _route_kernel(lg_ref, eb_ref, csm_ref, cnt_ref, gx_ref):
    tt = lg_ref.shape[1]
    scores = jax.nn.sigmoid(lg_ref[...])
    sel = scores + eb_ref[...]
    neg_inf = -jnp.inf
    sel_g = [sel[PER_GROUP * g:PER_GROUP * (g + 1), :] for g in range(N_EXPERT_GROUPS)]
    grp = []
    for v in sel_g:
        m1 = jnp.max(v, axis=0, keepdims=True)
        is1 = v == m1
        n1 = jnp.sum(jnp.where(is1, 1.0, 0.0), axis=0, keepdims=True)
        rest = jnp.max(jnp.where(is1, neg_inf, v), axis=0, keepdims=True)
        grp.append(m1 + jnp.where(n1 >= 2.0, m1, rest))
    masked = []
    for g in range(N_EXPERT_GROUPS):
        rank = jnp.zeros((1, tt), F32)
        for g2 in range(N_EXPERT_GROUPS):
            if g2 != g:
                beats = (grp[g2] >= grp[g]) if g2 < g else (grp[g2] > grp[g])
                rank = rank + jnp.where(beats, 1.0, 0.0)
        keep = jnp.broadcast_to(rank < TOPK_GROUPS, (PER_GROUP, tt))
        masked.append(jnp.where(keep, sel_g[g], neg_inf))
    sub = lax.broadcasted_iota(jnp.int32, (PER_GROUP, tt), 0)
    ranks = [jnp.zeros((PER_GROUP, tt), F32) for _ in range(N_EXPERT_GROUPS)]
    for g2 in range(N_EXPERT_GROUPS):
        for m2 in range(PER_GROUP):
            vf = jnp.broadcast_to(masked[g2][m2:m2 + 1, :], (PER_GROUP, tt))
            for g in range(N_EXPERT_GROUPS):
                if g2 < g:
                    beats = vf >= masked[g]
                elif g2 > g:
                    beats = vf > masked[g]
                else:
                    beats = (vf > masked[g]) | ((vf == masked[g]) & (sub > m2))
                ranks[g] = ranks[g] + jnp.where(beats, 1.0, 0.0)
    picked = [ranks[g] < TOP_K for g in range(N_EXPERT_GROUPS)]
    chosen = [jnp.where(picked[g], scores[PER_GROUP * g:PER_GROUP * (g + 1), :], 0.0)
              for g in range(N_EXPERT_GROUPS)]
    total = chosen[0]
    for g in range(1, N_EXPERT_GROUPS):
        total = total + chosen[g]
    denom = jnp.sum(total, axis=0, keepdims=True)
    gate = jnp.concatenate([chosen[g] / denom * ROUTED_SCALE for g in range(N_EXPERT_GROUPS)], axis=0)
    mask = jnp.concatenate([jnp.where(picked[g], 1.0, 0.0) for g in range(N_EXPERT_GROUPS)], axis=0)
    mask_bf = mask.astype(BF16)

    upper = jnp.where(lax.broadcasted_iota(jnp.int32, (tt, tt), 0) <= lax.broadcasted_iota(jnp.int32, (tt, tt), 1),
                      1.0, 0.0).astype(BF16)
    cs = jnp.dot(mask_bf, upper, preferred_element_type=F32)
    csm_ref[...] = jnp.where(mask > 0.0, cs, 0.0)
    cnt_ref[...] = lax.dot_general(jnp.ones((SUBLANES, tt), BF16), mask_bf, (((1,), (1,)), ((), ())),
                                   preferred_element_type=F32)

    hi = gate.astype(BF16).astype(F32)
    gx_ref[...] = jnp.concatenate([hi, gate - hi], axis=0).T.astype(BF16)


def _route(logits_t, e_bias_col):
    return pl.pallas_call(
        _route_kernel,
        out_shape=(jax.ShapeDtypeStruct((N_EXPERTS, N_TOK), F32),
                   jax.ShapeDtypeStruct((N_CHUNKS, SUBLANES, N_EXPERTS), F32),
                   jax.ShapeDtypeStruct((N_TOK, LANES), BF16)),
        grid=(N_CHUNKS,),
        in_specs=[pl.BlockSpec((N_EXPERTS, CHUNK), lambda c: (0, c)),
                  pl.BlockSpec((N_EXPERTS, 1), lambda c: (0, 0))],
        out_specs=(pl.BlockSpec((N_EXPERTS, CHUNK), lambda c: (0, c)),
                   pl.BlockSpec((None, SUBLANES, N_EXPERTS), lambda c: (c, 0, 0)),
                   pl.BlockSpec((CHUNK, LANES), lambda c: (c, 0))),
        compiler_params=_cparams(("parallel",)),
        name="routing",
    )(logits_t, e_bias_col)


def _plan_kernel(cnt_ref, goff_ref, tail_ref, tile_e_ref, ntile_ref, next_ref):
    def per_expert(e, carry):
        base, tbase, prev = carry

        def per_chunk(c, off):
            goff_ref[c, e] = base + off
            return off + (((cnt_ref[c, e] + (GRANULE - 1)) >> GRANULE_SHIFT) << GRANULE_SHIFT)

        tot = lax.fori_loop(0, N_CHUNKS, per_chunk, jnp.int32(0))
        region = ((tot + (TILE - 1)) >> TILE_SHIFT) << TILE_SHIFT
        tail_ref[0, e] = base + tot
        tail_ref[1, e] = (region - tot) >> GRANULE_SHIFT
        n_t = region >> TILE_SHIFT

        def per_tile(j, _):
            tile_e_ref[tbase + j] = e
            return 0

        lax.fori_loop(0, n_t, per_tile, 0)
        next_ref[e] = jnp.int32(-1)

        @pl.when((n_t > 0) & (prev >= 0))
        def _():
            next_ref[jnp.maximum(prev, 0)] = e

        return base + region, tbase + n_t, jnp.where(n_t > 0, e, prev)

    _, n_tiles, _ = lax.fori_loop(0, N_EXPERTS, per_expert, (jnp.int32(0), jnp.int32(0), jnp.int32(-1)))
    ntile_ref[0] = n_tiles
    last_e = tile_e_ref[jnp.maximum(n_tiles - 1, 0)]

    def fill(j, _):
        tile_e_ref[j] = last_e
        return 0

    lax.fori_loop(n_tiles, MAX_TILES, fill, 0)


def _plan(cnt):
    smem = pl.BlockSpec(memory_space=pltpu.SMEM)
    return pl.pallas_call(
        _plan_kernel,
        out_shape=(jax.ShapeDtypeStruct((N_CHUNKS, N_EXPERTS), jnp.int32),
                   jax.ShapeDtypeStruct((2, N_EXPERTS), jnp.int32),
                   jax.ShapeDtypeStruct((MAX_TILES,), jnp.int32),
                   jax.ShapeDtypeStruct((1,), jnp.int32),
                   jax.ShapeDtypeStruct((N_EXPERTS,), jnp.int32)),
        in_specs=[smem],
        out_specs=(smem, smem, smem, smem, smem),
        name="row_plan",
    )(cnt)


def _chunk_granules(cnt_ref, goff_ref, c, table_ref, per_granule=None):
    def per_expert(e, n_before):
        n_gran = (cnt_ref[c, e] + (GRANULE - 1)) >> GRANULE_SHIFT
        base = goff_ref[c, e]

        def per_j(j, _):
            table_ref[n_before + j] = base + j * GRANULE
            if per_granule is not None:
                per_granule(e, j, n_before + j)
            return 0

        lax.fori_loop(0, n_gran, per_j, 0)
        return n_before + n_gran

    return lax.fori_loop(0, N_EXPERTS, per_expert, jnp.int32(0))


def _dispatch_kernel(cnt_ref, goff_ref, tail_ref, x_ref, gx_ref, csm_ref, xs_hbm,
                     onehot_ref, table_ref, buf_ref, zero_ref, xcat_ref, sems, zsem):
    c = pl.program_id(0)
    sub = lax.broadcasted_iota(jnp.int32, (GRANULE, CHUNK), 0)
    xcat_ref[:, 0:D_MODEL] = x_ref[...]
    xcat_ref[:, D_MODEL:XS_WIDTH] = gx_ref[...]

    def build_granule(e, j, g):
        want = (sub + (j * GRANULE + 1)).astype(F32)
        hit = csm_ref[pl.ds(e, 1), :] == want
        onehot_ref[pl.ds(pl.multiple_of(g * GRANULE, GRANULE), GRANULE), :] = jnp.where(hit, 1.0, 0.0).astype(BF16)

    n_gran = _chunk_granules(cnt_ref, goff_ref, c, table_ref, build_granule)
    n_tiles = (n_gran + (GRAN_PER_TILE - 1)) >> GPT_SHIFT

    def clear(g, _):
        onehot_ref[pl.ds(pl.multiple_of(g * GRANULE, GRANULE), GRANULE), :] = jnp.zeros((GRANULE, CHUNK), BF16)
        return 0

    lax.fori_loop(n_gran, n_tiles * GRAN_PER_TILE, clear, 0)

    def granule_copy(slot, i, row):
        return pltpu.make_async_copy(buf_ref.at[slot, pl.ds(pl.multiple_of(i * GRANULE, GRANULE), GRANULE), :],
                                     xs_hbm.at[pl.ds(pl.multiple_of(row, GRANULE), GRANULE), :],
                                     sems.at[slot])

    def tile_granules(k):
        return jnp.minimum(GRAN_PER_TILE, n_gran - k * GRAN_PER_TILE)

    def wait_tile(k):
        slot = k % N_SLOTS

        def w(i, _):
            granule_copy(slot, i, 0).wait()
            return 0

        lax.fori_loop(0, tile_granules(k), w, 0)

    def per_tile(k, _):
        slot = k % N_SLOTS

        @pl.when(k >= N_SLOTS)
        def _():
            wait_tile(k - N_SLOTS)

        oh = onehot_ref[pl.ds(pl.multiple_of(k * TILE, TILE), TILE), :]
        buf_ref[slot] = jnp.dot(oh, xcat_ref[...], preferred_element_type=F32).astype(BF16)

        def s(i, _):
            granule_copy(slot, i, table_ref[k * GRAN_PER_TILE + i]).start()
            return 0

        lax.fori_loop(0, tile_granules(k), s, 0)
        return 0

    lax.fori_loop(0, n_tiles, per_tile, 0)

    def drain(k, _):
        wait_tile(k)
        return 0

    lax.fori_loop(jnp.maximum(n_tiles - N_SLOTS, 0), n_tiles, drain, 0)

    @pl.when(c == N_CHUNKS - 1)
    def _():
        zero_ref[...] = jnp.zeros_like(zero_ref)

        def tail_copy(row):
            return pltpu.make_async_copy(zero_ref, xs_hbm.at[pl.ds(pl.multiple_of(row, GRANULE), GRANULE), :], zsem)

        def start_e(e, _):
            def st(j, _):
                tail_copy(tail_ref[0, e] + j * GRANULE).start()
                return 0
            lax.fori_loop(0, tail_ref[1, e], st, 0)
            return 0

        def wait_e(e, _):
            def wt(j, _):
                tail_copy(0).wait()
                return 0
            lax.fori_loop(0, tail_ref[1, e], wt, 0)
            return 0

        lax.fori_loop(0, N_EXPERTS, start_e, 0)
        lax.fori_loop(0, N_EXPERTS, wait_e, 0)


def _dispatch(cnt, goff, tail, h2, gx, csm):
    grid_spec = pltpu.PrefetchScalarGridSpec(
        num_scalar_prefetch=3,
        grid=(N_CHUNKS,),
        in_specs=[pl.BlockSpec((CHUNK, D_MODEL), lambda c, *_: (c, 0)),
                  pl.BlockSpec((CHUNK, LANES), lambda c, *_: (c, 0)),
                  pl.BlockSpec((N_EXPERTS, CHUNK), lambda c, *_: (0, c))],
        out_specs=pl.BlockSpec(memory_space=pl.ANY),
        scratch_shapes=[pltpu.VMEM((CHUNK_ROWS, CHUNK), BF16),
                        pltpu.SMEM((CHUNK_ROWS // GRANULE,), jnp.int32),
                        pltpu.VMEM((N_SLOTS, TILE, XS_WIDTH), BF16),
                        pltpu.VMEM((GRANULE, XS_WIDTH), BF16),
                        pltpu.VMEM((CHUNK, XS_WIDTH), BF16),
                        pltpu.SemaphoreType.DMA((N_SLOTS,)),
                        pltpu.SemaphoreType.DMA(())])
    return pl.pallas_call(
        _dispatch_kernel,
        out_shape=jax.ShapeDtypeStruct((MAX_ROWS, XS_WIDTH), BF16),
        grid_spec=grid_spec,
        compiler_params=_cparams(("arbitrary",)),
        name="moe_dispatch",
    )(cnt, goff, tail, h2, gx, csm)


def _routed_kernel(tile_e_ref, ntile_ref, next_ref, xs_hbm, w1_hbm, w3_hbm, w2_hbm, ys_hbm,
                   xbuf_ref, ybuf_ref, w1f_ref, w3f_ref, w2f_ref, w1s_ref, w3s_ref, w2s_ref, xsem, ysem, wsem):
    n_tiles = ntile_ref[0]

    def x_copy(t, slot):
        return pltpu.make_async_copy(xs_hbm.at[pl.ds(pl.multiple_of(t * TILE, TILE), TILE), :],
                                     xbuf_ref.at[slot], xsem.at[slot])

    def y_copy(t, slot):
        return pltpu.make_async_copy(ybuf_ref.at[slot],
                                     ys_hbm.at[pl.ds(pl.multiple_of(t * TILE, TILE), TILE), :], ysem.at[slot])

    def fetch(expert, slot):
        return (pltpu.make_async_copy(w1_hbm.at[expert], w1f_ref.at[slot], wsem.at[slot]),
                pltpu.make_async_copy(w3_hbm.at[expert], w3f_ref.at[slot], wsem.at[slot]),
                pltpu.make_async_copy(w2_hbm.at[expert], w2f_ref.at[slot], wsem.at[slot]))

    @pl.when(n_tiles > 0)
    def _():
        x_copy(0, 0).start()
        for cp in fetch(tile_e_ref[0], 0):
            cp.start()

    def per_tile(t, n_started):
        slot = t & 1
        e = tile_e_ref[t]
        x_copy(t, slot).wait()

        @pl.when(t + 1 < n_tiles)
        def _():
            x_copy(t + 1, 1 - slot).start()

        first = (t == 0) | (e != tile_e_ref[jnp.maximum(t - 1, 0)])

        @pl.when(first)
        def _():
            wslot = n_started & 1
            nxt = next_ref[e]

            @pl.when(nxt >= 0)
            def _():
                for cp in fetch(jnp.maximum(nxt, 0), 1 - wslot):
                    cp.start(priority=WEIGHT_DMA_PRIORITY)

            for cp in fetch(e, wslot):
                cp.wait()
            w1s_ref[...] = w1f_ref[wslot].astype(BF16)
            w3s_ref[...] = w3f_ref[wslot].astype(BF16)
            w2s_ref[...] = w2f_ref[wslot].astype(BF16)

        @pl.when(t >= 2)
        def _():
            y_copy(t - 2, slot).wait()

        x = xbuf_ref[slot, :, 0:D_MODEL]
        gx = xbuf_ref[slot, :, D_MODEL:XS_WIDTH].astype(F32)
        lane = lax.broadcasted_iota(jnp.int32, gx.shape, 1)
        gate = jnp.sum(jnp.where((lane == e) | (lane == e + N_EXPERTS), gx, 0.0), axis=1, keepdims=True)
        a = jnp.dot(x, w1s_ref[...], preferred_element_type=F32)
        u = jnp.dot(x, w3s_ref[...], preferred_element_type=F32)
        hid = (a * jax.nn.sigmoid(a)) * u * gate
        ybuf_ref[slot] = jnp.dot(hid.astype(BF16), w2s_ref[...], preferred_element_type=F32).astype(ybuf_ref.dtype)
        y_copy(t, slot).start()
        return n_started + jnp.where(first, 1, 0)

    lax.fori_loop(0, n_tiles, per_tile, jnp.int32(0))

    @pl.when(n_tiles >= 2)
    def _():
        y_copy(n_tiles - 2, n_tiles & 1).wait()

    @pl.when(n_tiles >= 1)
    def _():
        y_copy(n_tiles - 1, (n_tiles - 1) & 1).wait()


def _routed_experts(tile_e, ntile, next_e, xs, w1, w3, w2):
    hbm = pl.BlockSpec(memory_space=pl.ANY)
    grid_spec = pltpu.PrefetchScalarGridSpec(
        num_scalar_prefetch=3,
        grid=(1,),
        in_specs=[hbm, hbm, hbm, hbm],
        out_specs=hbm,
        scratch_shapes=[pltpu.VMEM((2, TILE, XS_WIDTH), BF16),
                        pltpu.VMEM((2, TILE, D_MODEL), BF16),
                        pltpu.VMEM((2, D_MODEL, D_EXPERT), F32),
                        pltpu.VMEM((2, D_MODEL, D_EXPERT), F32),
                        pltpu.VMEM((2, D_EXPERT, D_MODEL), F32),
                        pltpu.VMEM((D_MODEL, D_EXPERT), BF16),
                        pltpu.VMEM((D_MODEL, D_EXPERT), BF16),
                        pltpu.VMEM((D_EXPERT, D_MODEL), BF16),
                        pltpu.SemaphoreType.DMA((2,)),
                        pltpu.SemaphoreType.DMA((2,)),
                        pltpu.SemaphoreType.DMA((2,))])
    return pl.pallas_call(
        _routed_kernel,
        out_shape=jax.ShapeDtypeStruct((MAX_ROWS, D_MODEL), BF16),
        grid_spec=grid_spec,
        compiler_params=_cparams(("arbitrary",)),
        name="routed_experts",
    )(tile_e, ntile, next_e, xs, w1, w3, w2)


def _combine_kernel(cnt_ref, goff_ref, csm_ref, x1_ref, sh_ref, gt_ref, gf_ref, ys_hbm, o_ref,
                    table_ref, buf_ref, acc_ref, pick_ref, sems):
    c = pl.program_id(0)
    sub = lax.broadcasted_iota(jnp.int32, (GRANULE, CHUNK), 0)

    def build_granule(e, j, g):
        want = (sub + (j * GRANULE + 1)).astype(F32)
        hit = csm_ref[pl.ds(e, 1), :] == want
        pick_ref[pl.ds(pl.multiple_of(g * GRANULE, GRANULE), GRANULE), :] = jnp.where(hit, 1.0, 0.0).astype(BF16)

    n_gran = _chunk_granules(cnt_ref, goff_ref, c, table_ref, build_granule)
    n_tiles = (n_gran + (COMBINE_GPT - 1)) >> COMBINE_GPT_SHIFT

    def clear_pick(g, _):
        pick_ref[pl.ds(pl.multiple_of(g * GRANULE, GRANULE), GRANULE), :] = jnp.zeros((GRANULE, CHUNK), BF16)
        return 0

    lax.fori_loop(n_gran, n_tiles * COMBINE_GPT, clear_pick, 0)

    def granule_copy(slot, i, row):
        return pltpu.make_async_copy(ys_hbm.at[pl.ds(pl.multiple_of(row, GRANULE), GRANULE), :],
                                     buf_ref.at[slot, pl.ds(pl.multiple_of(i * GRANULE, GRANULE), GRANULE), :],
                                     sems.at[slot])

    def tile_granules(k):
        return jnp.minimum(COMBINE_GPT, n_gran - k * COMBINE_GPT)

    def start_tile(k):
        slot = k % COMBINE_SLOTS

        def s(i, _):
            granule_copy(slot, i, table_ref[k * COMBINE_GPT + i]).start()
            return 0

        lax.fori_loop(0, tile_granules(k), s, 0)

    def wait_tile(k):
        slot = k % COMBINE_SLOTS

        def w(i, _):
            granule_copy(slot, i, 0).wait()
            return 0

        lax.fori_loop(0, tile_granules(k), w, 0)

    for k0 in range(COMBINE_SLOTS - 1):
        @pl.when(k0 < n_tiles)
        def _():
            start_tile(k0)

    acc_ref[...] = jnp.zeros_like(acc_ref)

    def per_tile(k, _):
        @pl.when(k + (COMBINE_SLOTS - 1) < n_tiles)
        def _():
            start_tile(k + (COMBINE_SLOTS - 1))

        wait_tile(k)
        slot = k % COMBINE_SLOTS

        def clear(i, _):
            buf_ref[slot, pl.ds(pl.multiple_of(i * GRANULE, GRANULE), GRANULE), :] = jnp.zeros((GRANULE, D_MODEL), BF16)
            return 0

        lax.fori_loop(tile_granules(k), COMBINE_GPT, clear, 0)

        pick = pick_ref[pl.ds(pl.multiple_of(k * COMBINE_TILE, COMBINE_TILE), COMBINE_TILE), :]
        acc_ref[...] += lax.dot_general(pick, buf_ref[slot], (((0,), (0,)), ((), ())),
                                        preferred_element_type=F32)
        return 0

    lax.fori_loop(0, n_tiles, per_tile, 0)

    x = x1_ref[...] + gt_ref[...] * (acc_ref[...] + sh_ref[...])
    o_ref[...] = x * lax.rsqrt(jnp.mean(x * x, axis=-1, keepdims=True) + EPS) * gf_ref[...]


def _combine(cnt, goff, csm, x1, shared, mod4, g_final, ys):
    per_b = SEQ // CHUNK
    row = pl.BlockSpec((CHUNK, D_MODEL), lambda c, *_: (c, 0))
    grid_spec = pltpu.PrefetchScalarGridSpec(
        num_scalar_prefetch=2,
        grid=(N_CHUNKS,),
        in_specs=[pl.BlockSpec((N_EXPERTS, CHUNK), lambda c, *_: (0, c)),
                  row, row,
                  pl.BlockSpec((None, None, 1, D_MODEL), lambda c, *_: (c // per_b, 5, 0, 0)),
                  pl.BlockSpec((1, D_MODEL), lambda c, *_: (0, 0)),
                  pl.BlockSpec(memory_space=pl.ANY)],
        out_specs=row,
        scratch_shapes=[pltpu.SMEM((CHUNK_ROWS // GRANULE,), jnp.int32),
                        pltpu.VMEM((COMBINE_SLOTS, COMBINE_TILE, D_MODEL), BF16),
                        pltpu.VMEM((CHUNK, D_MODEL), F32),
                        pltpu.VMEM((CHUNK_ROWS, CHUNK), BF16),
                        pltpu.SemaphoreType.DMA((COMBINE_SLOTS,))])
    return pl.pallas_call(
        _combine_kernel,
        out_shape=jax.ShapeDtypeStruct((N_TOK, D_MODEL), F32),
        grid_spec=grid_spec,
        compiler_params=_cparams(("arbitrary",)),
        name="moe_combine",
    )(cnt, goff, csm, x1, shared, mod4, g_final, ys)


def _shared_kernel(x_ref, w1_ref, w3_ref, w2_ref, o_ref):
    x = x_ref[...]
    a = jnp.dot(x, w1_ref[...], preferred_element_type=F32)
    u = jnp.dot(x, w3_ref[...], preferred_element_type=F32)
    hid = (a * jax.nn.sigmoid(a)) * u
    o_ref[...] = jnp.dot(hid.astype(BF16), w2_ref[...], preferred_element_type=F32).astype(o_ref.dtype)


def _shared_expert(h2, w1, w3, w2):
    tm = 1024
    return pl.pallas_call(
        _shared_kernel,
        out_shape=jax.ShapeDtypeStruct((N_TOK, D_MODEL), F32),
        grid=(N_TOK // tm,),
        in_specs=[pl.BlockSpec((tm, D_MODEL), lambda m: (m, 0)),
                  pl.BlockSpec((D_MODEL, D_SHARED), lambda m: (0, 0)),
                  pl.BlockSpec((D_MODEL, D_SHARED), lambda m: (0, 0)),
                  pl.BlockSpec((D_SHARED, D_MODEL), lambda m: (0, 0))],
        out_specs=pl.BlockSpec((tm, D_MODEL), lambda m: (m, 0)),
        compiler_params=_cparams(("parallel",)),
        name="shared_expert",
    )(h2, w1, w3, w2)


def _rope_angles(pos, dims, theta):
    inv = jnp.power(jnp.float32(theta), -jnp.arange(0, dims, 2, dtype=jnp.float32) / dims)
    return pos.astype(jnp.float32)[:, None] * inv[None, :]


def _rotary_tables(angle_blocks):
    cos_parts, lo_parts, hi_parts = [], [], []
    used = 0
    for ang in angle_blocks:
        c, s = jnp.cos(ang), jnp.sin(ang)
        z = jnp.zeros_like(s)
        cos_parts += [c, c]
        lo_parts += [-s, z]
        hi_parts += [z, s]
        used += 2 * ang.shape[1]
    rest = HEAD_DIM - used
    if rest:
        cos_parts.append(jnp.ones((SEQ, rest), F32))
        lo_parts.append(jnp.zeros((SEQ, rest), F32))
        hi_parts.append(jnp.zeros((SEQ, rest), F32))
    return (jnp.concatenate(cos_parts, axis=1), jnp.concatenate(lo_parts, axis=1),
            jnp.concatenate(hi_parts, axis=1))


def kernel(x, c, w_ada, b_ada, g_attn, w_in, b_gate, q_norm_g, k_norm_g, w_a_up, w_b_up, w_out,
           g_ffn, w_router, e_bias, w1, w3, w2, ws1, ws3, ws2, g_final):
    l = 0
    x2 = x.reshape(N_TOK, D_MODEL)
    pos = jnp.arange(SEQ)
    tabs_a = _rotary_tables([_rope_angles(pos, ROPE_DIMS, ROPE_THETA)])
    tabs_b = _rotary_tables([_rope_angles(pos // GRID_W, AXIAL_DIMS, AXIAL_THETA),
                             _rope_angles(pos % GRID_W, AXIAL_DIMS, AXIAL_THETA)])

    c_pad = jnp.zeros((SUBLANES, D_MODEL), F32).at[:BATCH].set(c)
    mod = _ada(c_pad, w_ada[l], b_ada[l].reshape(1, -1))
    mod4 = mod[:BATCH].reshape(BATCH, N_MOD, 1, D_MODEL)

    h = _norm_modulate(x2, g_attn[l].reshape(1, -1), mod4, 1, 0)
    perms = jnp.stack([_residue_perm(dilation) for _, dilation in A_PATTERNS])
    proj, gates = _projection(h, w_in[l], tabs_a, tabs_b, q_norm_g[l].reshape(1, -1),
                              k_norm_g[l].reshape(1, -1), b_gate[l].reshape(1, -1), perms)
    proj3 = proj.reshape(BATCH, SEQ, QKV_COLS)

    o_groups, lse_groups = [], []
    for gi, (window, dilation) in enumerate(A_PATTERNS):
        o, lse = (_attn_a if dilation == 1 else _attn_a_dilated)(proj3, gi, window, dilation)
        o_groups.append(o)
        lse_groups.append(lse)
    mix_b = _attn_b(proj3)

    merged = _merge(o_groups, lse_groups, mix_b, gates, w_a_up[l].astype(BF16), w_b_up[l].astype(BF16),
                    jnp.swapaxes(perms[1:], 1, 2))
    x1, h2, logits_t = _out_projection(merged, w_out[l].astype(BF16), x2, mod4, g_ffn[l].reshape(1, -1),
                                       w_router[l].T)
    csm, cnt_f, gx = _route(logits_t, e_bias[l].reshape(-1, 1))
    cnt = cnt_f[:, 0, :].astype(jnp.int32)
    goff, tail, tile_e, ntile, next_e = _plan(cnt)
    xs = _dispatch(cnt, goff, tail, h2, gx, csm)
    ys = _routed_experts(tile_e, ntile, next_e, xs, w1[l], w3[l], w2[l])
    shared = _shared_expert(h2, ws1[l].astype(BF16), ws3[l].astype(BF16), ws2[l].astype(BF16))
    out = _combine(cnt, goff, csm, x1, shared, mod4, g_final.reshape(1, -1), ys)
    return out.reshape(BATCH, SEQ, D_MODEL)
```

```python
import functools

import jax
import jax.numpy as jnp
from jax import lax
from jax.experimental import pallas as pl
from jax.experimental.pallas import tpu as pltpu

F32 = jnp.float32
BF16 = jnp.bfloat16

D_MODEL = 2048
BATCH = 2
SEQ = 4096
N_TOK = BATCH * SEQ
HEAD_DIM = 128
EPS = 1e-6
A_PATTERNS = ((128, 1), (512, 4), (2048, 16))
A_N_GROUPS = len(A_PATTERNS)
A_HEADS = 4
A_WIDTH = A_HEADS * HEAD_DIM
ROPE_THETA = 500000.0
ROPE_DIMS = HEAD_DIM // 4
B_Q_HEADS = 16
B_KV_HEADS = 4
B_GROUP = B_Q_HEADS // B_KV_HEADS
B_Q_WIDTH = B_Q_HEADS * HEAD_DIM
B_KV_WIDTH = B_KV_HEADS * HEAD_DIM
AXIAL_THETA = 10000.0
AXIAL_DIMS = HEAD_DIM // 2
GRID_W = 64
A_COLS = A_N_GROUPS * 3 * A_WIDTH
B_COLS = B_Q_WIDTH + 2 * B_KV_WIDTH
GATE_COLS = 2 * D_MODEL
QKV_COLS = A_COLS + B_COLS
IN_COLS = QKV_COLS + GATE_COLS
N_MOD = 6
N_EXPERTS = 64
N_EXPERT_GROUPS = 8
PER_GROUP = N_EXPERTS // N_EXPERT_GROUPS
TOPK_GROUPS = 4
TOP_K = 8
D_EXPERT = D_MODEL // 4
D_SHARED = D_MODEL // 4
ROUTED_SCALE = 2.5
NEG = -1e30
ATTN_SCALE = HEAD_DIM ** -0.5
LOG2E = 1.4426950408889634
LN2 = 0.6931471805599453

LANES = 128
SUBLANES = 8
VMEM_LIMIT = 56 * 1024 * 1024

COL_BLK = 512
PERM_BLOCK = 256
N_COL_BLKS = IN_COLS // COL_BLK
A_BLKS = A_COLS // COL_BLK
BQ_BLK0 = A_BLKS
BK_BLK = BQ_BLK0 + B_Q_WIDTH // COL_BLK
BV_BLK = BK_BLK + 1
GATE_BLK0 = BV_BLK + 1

CHUNK = 512
N_CHUNKS = N_TOK // CHUNK
GRANULE = 16
GRANULE_SHIFT = GRANULE.bit_length() - 1
TILE = 256
TILE_SHIFT = TILE.bit_length() - 1
GRAN_PER_TILE = TILE // GRANULE
GPT_SHIFT = TILE_SHIFT - GRANULE_SHIFT
N_SLOTS = 4
COMBINE_TILE = 1024
COMBINE_GPT = COMBINE_TILE // GRANULE
COMBINE_GPT_SHIFT = COMBINE_GPT.bit_length() - 1
COMBINE_SLOTS = 3
WEIGHT_DMA_PRIORITY = 1
XS_WIDTH = D_MODEL + LANES
CHUNK_ROWS = -(-(CHUNK * TOP_K + N_EXPERTS * (GRANULE - 1)) // TILE) * TILE
MAX_ROWS = -(-(N_TOK * TOP_K + N_EXPERTS * N_CHUNKS * (GRANULE - 1) + N_EXPERTS * (TILE - 1)) // TILE) * TILE
MAX_TILES = MAX_ROWS // TILE


def _cparams(semantics):
    return pltpu.CompilerParams(dimension_semantics=semantics, vmem_limit_bytes=VMEM_LIMIT)


def _ada_kernel(c_ref, w_ref, b_ref, o_ref):
    c = c_ref[...]
    act = c * jax.nn.sigmoid(c)
    o_ref[...] = jnp.dot(act, w_ref[...], preferred_element_type=F32,
                         precision=lax.Precision.HIGHEST) + b_ref[...]


def _ada(c_pad, w_ada, b_ada):
    tn = 1024
    n_out = w_ada.shape[1]
    return pl.pallas_call(
        _ada_kernel,
        out_shape=jax.ShapeDtypeStruct((SUBLANES, n_out), F32),
        grid=(n_out // tn,),
        in_specs=[pl.BlockSpec((SUBLANES, D_MODEL), lambda n: (0, 0)),
                  pl.BlockSpec((D_MODEL, tn), lambda n: (0, n)),
                  pl.BlockSpec((1, tn), lambda n: (0, n))],
        out_specs=pl.BlockSpec((SUBLANES, tn), lambda n: (0, n)),
        compiler_params=_cparams(("parallel",)),
        name="ada_mod",
    )(c_pad, w_ada, b_ada)


def _normmod(x, g, sc, sh):
    y = x * lax.rsqrt(jnp.mean(x * x, axis=-1, keepdims=True) + EPS)
    return (y * g) * (1.0 + sc) + sh


def _normmod_kernel(x_ref, g_ref, sc_ref, sh_ref, o_ref):
    o_ref[...] = _normmod(x_ref[...], g_ref[...], sc_ref[...], sh_ref[...]).astype(o_ref.dtype)


def _norm_modulate(x2, g, mod4, sc_idx, sh_idx):
    tm = 512
    per_b = SEQ // tm
    return pl.pallas_call(
        _normmod_kernel,
        out_shape=jax.ShapeDtypeStruct((N_TOK, D_MODEL), BF16),
        grid=(N_TOK // tm,),
        in_specs=[pl.BlockSpec((tm, D_MODEL), lambda m: (m, 0)),
                  pl.BlockSpec((1, D_MODEL), lambda m: (0, 0)),
                  pl.BlockSpec((None, None, 1, D_MODEL), lambda m: (m // per_b, sc_idx, 0, 0)),
                  pl.BlockSpec((None, None, 1, D_MODEL), lambda m: (m // per_b, sh_idx, 0, 0))],
        out_specs=pl.BlockSpec((tm, D_MODEL), lambda m: (m, 0)),
        compiler_params=_cparams(("parallel",)),
        name="norm_modulate",
    )(x2, g, mod4, mod4)


def _tile4(t):
    return jnp.concatenate([t, t, t, t], axis=1)


def _rotary(y, tabs, rows, shift):
    cos_ref, sin_lo_ref, sin_hi_ref = tabs
    width = y.shape[1]
    return (y * _tile4(cos_ref[rows, :])
            + pltpu.roll(y, width - shift, 1) * _tile4(sin_lo_ref[rows, :])
            + pltpu.roll(y, shift, 1) * _tile4(sin_hi_ref[rows, :]))


def _head_rmsnorm(y, g):
    outs = []
    for h in range(y.shape[1] // HEAD_DIM):
        yh = y[:, h * HEAD_DIM:(h + 1) * HEAD_DIM]
        outs.append(yh * lax.rsqrt(jnp.mean(yh * yh, axis=-1, keepdims=True) + EPS) * g)
    return jnp.concatenate(outs, axis=1)


def _proj_kernel(h_ref, w_ref, ca_ref, sal_ref, sah_ref, cb_ref, sbl_ref, sbh_ref,
                 gain_ref, scale_ref, bg_ref, perm_ref, o_ref, gate_ref, hp_ref):
    n = pl.program_id(1)
    is_a = n < A_BLKS
    part = n % 3
    tabs_a = (ca_ref, sal_ref, sah_ref)
    tabs_b = (cb_ref, sbl_ref, sbh_ref)
    sub_rows = PERM_BLOCK
    n_sub = h_ref.shape[0] // sub_rows

    @pl.when(n == 0)
    def _():
        hp_ref[...] = h_ref[...]

    @pl.when(is_a & (part == 0) & (n > 0))
    def _():
        for r in range(n_sub):
            rows = pl.ds(r * sub_rows, sub_rows)
            hp_ref[rows, :] = jnp.dot(perm_ref[...], h_ref[rows, :], preferred_element_type=F32).astype(BF16)

    def run(epilogue, dst_ref=o_ref, src_ref=h_ref):
        w = w_ref[...].astype(BF16)
        for r in range(n_sub):
            rows = pl.ds(r * sub_rows, sub_rows)
            acc = jnp.dot(src_ref[rows, :], w, preferred_element_type=F32)
            dst_ref[rows, :] = epilogue(acc, rows).astype(dst_ref.dtype)

    @pl.when(is_a & (part < 2))
    def _():
        run(lambda acc, rows: _rotary(acc, tabs_a, rows, ROPE_DIMS // 2) * scale_ref[...], src_ref=hp_ref)

    @pl.when(is_a & (part == 2))
    def _():
        run(lambda acc, rows: acc, src_ref=hp_ref)

    @pl.when((n >= BQ_BLK0) & (n <= BK_BLK))
    def _():
        run(lambda acc, rows: _rotary(_head_rmsnorm(acc, gain_ref[...]), tabs_b, rows, AXIAL_DIMS // 2)
            * scale_ref[...])

    @pl.when(n == BV_BLK)
    def _():
        run(lambda acc, rows: acc)

    @pl.when(n >= GATE_BLK0)
    def _():
        run(lambda acc, rows: jax.nn.sigmoid(acc + bg_ref[...]), gate_ref)


def _residue_source(dilation):
    per_res = PERM_BLOCK // dilation
    row = jnp.arange(PERM_BLOCK)
    return (row % per_res) * dilation + row // per_res


def _residue_perm(dilation):
    return (_residue_source(dilation)[:, None] == jnp.arange(PERM_BLOCK)[None, :]).astype(BF16)


def _projection(h, w_in, tabs_a, tabs_b, qg, kg, b_gate, perms):
    tm = 1024
    per_b = SEQ // tm
    tab_spec = pl.BlockSpec((tm, LANES), lambda m, n: (m % per_b, 0))
    n_gate_blks = GATE_COLS // COL_BLK
    is_query = [(blk < A_BLKS and blk % 3 == 0) or BQ_BLK0 <= blk < BK_BLK for blk in range(N_COL_BLKS)]
    col_scale = jnp.broadcast_to(jnp.where(jnp.array(is_query), ATTN_SCALE * LOG2E, 1.0).astype(F32)[:, None, None],
                                 (N_COL_BLKS, 1, COL_BLK))
    ones = jnp.ones_like(qg)
    col_gain = jnp.stack([qg if BQ_BLK0 <= blk < BK_BLK else kg if blk == BK_BLK else ones
                          for blk in range(N_COL_BLKS)])
    src_rows = [(jnp.arange(SEQ) // PERM_BLOCK) * PERM_BLOCK + _residue_source(dilation)[jnp.arange(SEQ) % PERM_BLOCK]
                for _, dilation in A_PATTERNS]
    tabs_a = [jnp.stack([t[src] for src in src_rows]) for t in tabs_a]
    tab_a_spec = pl.BlockSpec((None, tm, LANES),
                              lambda m, n: (jnp.clip(n // 3, 0, A_N_GROUPS - 1), m % per_b, 0))

    def gate_blk(n):
        return jnp.clip(n - GATE_BLK0, 0, n_gate_blks - 1)

    return pl.pallas_call(
        _proj_kernel,
        out_shape=(jax.ShapeDtypeStruct((N_TOK, QKV_COLS), BF16),
                   jax.ShapeDtypeStruct((N_TOK, GATE_COLS), BF16)),
        grid=(N_TOK // tm, N_COL_BLKS),
        in_specs=[pl.BlockSpec((tm, D_MODEL), lambda m, n: (m, 0)),
                  pl.BlockSpec((D_MODEL, COL_BLK), lambda m, n: (0, n)),
                  tab_a_spec, tab_a_spec, tab_a_spec, tab_spec, tab_spec, tab_spec,
                  pl.BlockSpec((None, 1, HEAD_DIM), lambda m, n: (n, 0, 0)),
                  pl.BlockSpec((None, 1, COL_BLK), lambda m, n: (n, 0, 0)),
                  pl.BlockSpec((1, COL_BLK), lambda m, n: (0, gate_blk(n))),
                  pl.BlockSpec((None, PERM_BLOCK, PERM_BLOCK),
                               lambda m, n: (jnp.clip(n // 3, 0, A_N_GROUPS - 1), 0, 0))],
        out_specs=(pl.BlockSpec((tm, COL_BLK), lambda m, n: (m, jnp.minimum(n, GATE_BLK0 - 1))),
                   pl.BlockSpec((tm, COL_BLK), lambda m, n: (m, gate_blk(n)))),
        scratch_shapes=[pltpu.VMEM((tm, D_MODEL), BF16)],
        compiler_params=_cparams(("parallel", "arbitrary")),
        name="in_projection",
    )(h, w_in, *tabs_a, *tabs_b, col_gain, col_scale, b_gate, perms)


def _attn_a_kernel(q_ref, k_ref, v_ref, o_ref, lse_ref, *, dilation, reach):
    i = pl.program_id(1)
    tq = q_ref.shape[0]
    win = tq + 2 * reach
    start = pl.multiple_of(jnp.clip(i * tq - reach, 0, SEQ - win), GRANULE)
    q = q_ref[...]
    k = k_ref[pl.ds(start, win), :]
    v = v_ref[pl.ds(start, win), :]
    diff = (i * tq - start) + lax.broadcasted_iota(jnp.int32, (tq, win), 0) \
        - lax.broadcasted_iota(jnp.int32, (tq, win), 1)
    valid = (jnp.abs(diff) <= reach) & ((diff & (dilation - 1)) == 0)
    lane = lax.broadcasted_iota(jnp.int32, (tq, LANES), 1)
    lse_tile = jnp.zeros((tq, LANES), F32)
    outs = []
    for h in range(A_HEADS):
        sl = slice(h * HEAD_DIM, (h + 1) * HEAD_DIM)
        s = lax.dot_general(q[:, sl], k[:, sl], (((1,), (1,)), ((), ())), preferred_element_type=F32)
        s = jnp.where(valid, s, NEG)
        m = jnp.max(s, axis=-1, keepdims=True)
        p = jnp.exp2(s - m)
        l = jnp.sum(p, axis=-1, keepdims=True)
        o = jnp.dot(p.astype(BF16), v[:, sl], preferred_element_type=F32)
        outs.append(o / l)
        lse_tile = jnp.where(lane == h, m * LN2 + jnp.log(l), lse_tile)
    o_ref[...] = jnp.concatenate(outs, axis=1).astype(o_ref.dtype)
    lse_ref[...] = lse_tile


def _attn_a(proj3, group, window, dilation):
    assert dilation & (dilation - 1) == 0
    reach = window // 2
    tq = 256
    qb, kb, vb = 3 * group, 3 * group + 1, 3 * group + 2
    o, lse = pl.pallas_call(
        functools.partial(_attn_a_kernel, dilation=dilation, reach=reach),
        out_shape=(jax.ShapeDtypeStruct((BATCH, SEQ, A_WIDTH), BF16),
                   jax.ShapeDtypeStruct((BATCH, SEQ, LANES), F32)),
        grid=(BATCH, SEQ // tq),
        in_specs=[pl.BlockSpec((None, tq, COL_BLK), lambda b, i: (b, i, qb)),
                  pl.BlockSpec((None, SEQ, COL_BLK), lambda b, i: (b, 0, kb)),
                  pl.BlockSpec((None, SEQ, COL_BLK), lambda b, i: (b, 0, vb))],
        out_specs=(pl.BlockSpec((None, tq, A_WIDTH), lambda b, i: (b, i, 0)),
                   pl.BlockSpec((None, tq, LANES), lambda b, i: (b, i, 0))),
        compiler_params=_cparams(("parallel", "arbitrary")),
        name=f"dilated_attention_g{group}",
    )(proj3, proj3, proj3)
    return o.reshape(N_TOK, A_WIDTH), lse.reshape(N_TOK, LANES)


def _attn_res_kernel(q_ref, k_ref, v_ref, o_ref, lse_ref, *, half_w):
    n_blk, per_res, _ = q_ref.shape
    length = n_blk * per_res
    q = q_ref[...].reshape(length, A_WIDTH)
    k = k_ref[...].reshape(length, A_WIDTH)
    v = v_ref[...].reshape(length, A_WIDTH)
    tq = 2 * half_w
    win = 4 * half_w
    lane = lax.broadcasted_iota(jnp.int32, (tq, LANES), 1)
    row_minus_col = lax.broadcasted_iota(jnp.int32, (tq, win), 0) - lax.broadcasted_iota(jnp.int32, (tq, win), 1)
    o_blocks, lse_blocks = [], []
    for qb in range(length // tq):
        start = min(max(qb * tq - half_w, 0), length - win)
        valid = jnp.abs(row_minus_col + (qb * tq - start)) <= half_w
        lse_tile = jnp.zeros((tq, LANES), F32)
        outs = []
        for h in range(A_HEADS):
            sl = slice(h * HEAD_DIM, (h + 1) * HEAD_DIM)
            s = lax.dot_general(q[qb * tq:(qb + 1) * tq, sl], k[start:start + win, sl],
                                (((1,), (1,)), ((), ())), preferred_element_type=F32)
            s = jnp.where(valid, s, NEG)
            m = jnp.max(s, axis=-1, keepdims=True)
            p = jnp.exp2(s - m)
            l = jnp.sum(p, axis=-1, keepdims=True)
            o = jnp.dot(p.astype(BF16), v[start:start + win, sl], preferred_element_type=F32)
            outs.append(o / l)
            lse_tile = jnp.where(lane == h, m * LN2 + jnp.log(l), lse_tile)
        o_blocks.append(jnp.concatenate(outs, axis=1).astype(o_ref.dtype))
        lse_blocks.append(lse_tile)
    o_ref[...] = jnp.concatenate(o_blocks, axis=0).reshape(n_blk, per_res, A_WIDTH)
    lse_ref[...] = jnp.concatenate(lse_blocks, axis=0).reshape(n_blk, per_res, LANES)


def _attn_a_dilated(proj3, group, window, dilation):
    per_res = PERM_BLOCK // dilation
    n_blk = SEQ // PERM_BLOCK
    half_w = (window // 2) // dilation
    view = proj3.reshape(BATCH, n_blk, dilation, per_res, QKV_COLS)

    def spec(width, blk):
        return pl.BlockSpec((None, n_blk, None, per_res, width), lambda b, r: (b, 0, r, 0, blk))

    o, lse = pl.pallas_call(
        functools.partial(_attn_res_kernel, half_w=half_w),
        out_shape=(jax.ShapeDtypeStruct((BATCH, n_blk, dilation, per_res, A_WIDTH), BF16),
                   jax.ShapeDtypeStruct((BATCH, n_blk, dilation, per_res, LANES), F32)),
        grid=(BATCH, dilation),
        in_specs=[spec(COL_BLK, 3 * group), spec(COL_BLK, 3 * group + 1), spec(COL_BLK, 3 * group + 2)],
        out_specs=(spec(A_WIDTH, 0), spec(LANES, 0)),
        compiler_params=_cparams(("parallel", "parallel")),
        name=f"dilated_attention_g{group}",
    )(view, view, view)
    return o.reshape(N_TOK, A_WIDTH), lse.reshape(N_TOK, LANES)


def _attn_b_kernel(q_ref, k_ref, v_ref, o_ref, vx_ref, acc_ref, *, tk):
    tq = q_ref.shape[0]
    n_chunks = SEQ // tk

    @pl.when(pl.program_id(2) == 0)
    def _():
        vx_ref[:, 0:HEAD_DIM] = v_ref[...]
        vx_ref[:, HEAD_DIM:2 * HEAD_DIM] = jnp.ones((SEQ, HEAD_DIM), BF16)

    q = q_ref[...]
    qs = jnp.concatenate([q[:, g * HEAD_DIM:(g + 1) * HEAD_DIM] for g in range(B_GROUP)], axis=0)
    acc_ref[...] = jnp.zeros_like(acc_ref)
    m = jnp.full((B_GROUP * tq, 1), -jnp.inf, F32)
    for c in range(n_chunks):
        keys = slice(c * tk, (c + 1) * tk)
        s = lax.dot_general(qs, k_ref[keys, :], (((1,), (1,)), ((), ())), preferred_element_type=F32)
        m_new = jnp.maximum(m, jnp.max(s, axis=-1, keepdims=True))
        p = jnp.exp2(s - m_new).astype(BF16)
        acc_ref[...] = jnp.exp2(m - m_new) * acc_ref[...] + jnp.dot(p, vx_ref[keys, :],
                                                                    preferred_element_type=F32)
        m = m_new
    o = acc_ref[:, 0:HEAD_DIM] / acc_ref[:, HEAD_DIM:2 * HEAD_DIM]
    o_ref[...] = jnp.concatenate([o[g * tq:(g + 1) * tq] for g in range(B_GROUP)], axis=1).astype(o_ref.dtype)


def _attn_b(proj3):
    tq = 128
    tk = 256
    kcol0 = BK_BLK * COL_BLK // HEAD_DIM
    vcol0 = BV_BLK * COL_BLK // HEAD_DIM
    o = pl.pallas_call(
        functools.partial(_attn_b_kernel, tk=tk),
        out_shape=jax.ShapeDtypeStruct((BATCH, SEQ, B_Q_WIDTH), BF16),
        grid=(BATCH, B_KV_HEADS, SEQ // tq),
        in_specs=[pl.BlockSpec((None, tq, COL_BLK), lambda b, h, i: (b, i, BQ_BLK0 + h)),
                  pl.BlockSpec((None, SEQ, HEAD_DIM), lambda b, h, i: (b, 0, kcol0 + h)),
                  pl.BlockSpec((None, SEQ, HEAD_DIM), lambda b, h, i: (b, 0, vcol0 + h))],
        out_specs=pl.BlockSpec((None, tq, COL_BLK), lambda b, h, i: (b, i, h)),
        scratch_shapes=[pltpu.VMEM((SEQ, 2 * HEAD_DIM), BF16),
                        pltpu.VMEM((B_GROUP * tq, 2 * HEAD_DIM), F32)],
        compiler_params=_cparams(("parallel", "parallel", "arbitrary")),
        name="gqa_attention",
    )(proj3, proj3, proj3)
    return o.reshape(N_TOK, B_Q_WIDTH)


def _merge_kernel(o0_ref, o1_ref, o2_ref, l0_ref, l1_ref, l2_ref, yb_ref, wa_ref, wb_ref, ga_ref, gb_ref,
                  unperm_ref, out_ref):
    sub_rows = PERM_BLOCK
    for r in range(out_ref.shape[0] // sub_rows):
        rows = pl.ds(r * sub_rows, sub_rows)
        outs = [o0_ref[rows, :].astype(F32)]
        lses = [l0_ref[rows, :]]
        for g, (o_ref, l_ref) in enumerate(((o1_ref, l1_ref), (o2_ref, l2_ref))):
            outs.append(jnp.dot(unperm_ref[g], o_ref[rows, :], preferred_element_type=F32))
            lses.append(jnp.dot(unperm_ref[g].astype(F32), l_ref[rows, :], preferred_element_type=F32,
                                precision=lax.Precision.HIGHEST))
        cols = []
        for h in range(A_HEADS):
            lh = [jnp.broadcast_to(l[:, h:h + 1], (sub_rows, HEAD_DIM)) for l in lses]
            mx = jnp.maximum(jnp.maximum(lh[0], lh[1]), lh[2])
            e = [jnp.exp(v - mx) for v in lh]
            den = e[0] + e[1] + e[2]
            sl = slice(h * HEAD_DIM, (h + 1) * HEAD_DIM)
            cols.append(sum((e[g] / den) * outs[g][:, sl] for g in range(A_N_GROUPS)))
        mix_a = jnp.concatenate(cols, axis=1).astype(BF16)
        mix_b = yb_ref[rows, :]
        for cb in range(D_MODEL // COL_BLK):
            cs = pl.ds(cb * COL_BLK, COL_BLK)
            ya = jnp.dot(mix_a, wa_ref[:, cs], preferred_element_type=F32)
            yb = jnp.dot(mix_b, wb_ref[:, cs], preferred_element_type=F32)
            out_ref[rows, cs] = (ga_ref[rows, cs].astype(F32) * ya
                                 + gb_ref[rows, cs].astype(F32) * yb).astype(out_ref.dtype)


def _merge(o_groups, lse_groups, mix_b, gates, w_a_up, w_b_up, unperms):
    tm = 512
    o_spec = pl.BlockSpec((tm, A_WIDTH), lambda m: (m, 0))
    l_spec = pl.BlockSpec((tm, LANES), lambda m: (m, 0))
    return pl.pallas_call(
        _merge_kernel,
        out_shape=jax.ShapeDtypeStruct((N_TOK, D_MODEL), BF16),
        grid=(N_TOK // tm,),
        in_specs=[o_spec, o_spec, o_spec, l_spec, l_spec, l_spec,
                  pl.BlockSpec((tm, B_Q_WIDTH), lambda m: (m, 0)),
                  pl.BlockSpec((A_WIDTH, D_MODEL), lambda m: (0, 0)),
                  pl.BlockSpec((B_Q_WIDTH, D_MODEL), lambda m: (0, 0)),
                  pl.BlockSpec((tm, D_MODEL), lambda m: (m, 0)),
                  pl.BlockSpec((tm, D_MODEL), lambda m: (m, 1)),
                  pl.BlockSpec((A_N_GROUPS - 1, PERM_BLOCK, PERM_BLOCK), lambda m: (0, 0, 0))],
        out_specs=pl.BlockSpec((tm, D_MODEL), lambda m: (m, 0)),
        compiler_params=_cparams(("parallel",)),
        name="branch_merge",
    )(*o_groups, *lse_groups, mix_b, w_a_up, w_b_up, gates, gates, unperms)


def _outproj_kernel(mg_ref, w_ref, x_ref, gt_ref, g_ref, sc_ref, sh_ref, wr_ref, x1_ref, h2_ref, lg_ref):
    sub_rows = 256
    wr = wr_ref[...]
    wr_hi = wr.astype(BF16)
    wr_lo = (wr - wr_hi.astype(F32)).astype(BF16)
    for r in range(mg_ref.shape[0] // sub_rows):
        rows = pl.ds(r * sub_rows, sub_rows)
        y = jnp.dot(mg_ref[rows, :], w_ref[...], preferred_element_type=F32)
        x1 = x_ref[rows, :] + gt_ref[...] * y
        x1_ref[rows, :] = x1
        h2 = _normmod(x1, g_ref[...], sc_ref[...], sh_ref[...])
        h2_hi = h2.astype(BF16)
        h2_ref[rows, :] = h2_hi
        h2_lo = (h2 - h2_hi.astype(F32)).astype(BF16)
        nt = (((1,), (1,)), ((), ()))
        lg_ref[:, rows] = (lax.dot_general(wr_hi, h2_hi, nt, preferred_element_type=F32)
                           + lax.dot_general(wr_hi, h2_lo, nt, preferred_element_type=F32)
                           + lax.dot_general(wr_lo, h2_hi, nt, preferred_element_type=F32))


def _out_projection(merged, w_out, x2, mod4, g_ffn, w_router_t):
    tm = 512
    per_b = SEQ // tm

    def mod_spec(j):
        return pl.BlockSpec((None, None, 1, D_MODEL), lambda m: (m // per_b, j, 0, 0))

    return pl.pallas_call(
        _outproj_kernel,
        out_shape=(jax.ShapeDtypeStruct((N_TOK, D_MODEL), F32),
                   jax.ShapeDtypeStruct((N_TOK, D_MODEL), BF16),
                   jax.ShapeDtypeStruct((N_EXPERTS, N_TOK), F32)),
        grid=(N_TOK // tm,),
        in_specs=[pl.BlockSpec((tm, D_MODEL), lambda m: (m, 0)),
                  pl.BlockSpec((D_MODEL, D_MODEL), lambda m: (0, 0)),
                  pl.BlockSpec((tm, D_MODEL), lambda m: (m, 0)),
                  mod_spec(2),
                  pl.BlockSpec((1, D_MODEL), lambda m: (0, 0)),
                  mod_spec(4), mod_spec(3),
                  pl.BlockSpec((N_EXPERTS, D_MODEL), lambda m: (0, 0))],
        out_specs=(pl.BlockSpec((tm, D_MODEL), lambda m: (m, 0)),
                   pl.BlockSpec((tm, D_MODEL), lambda m: (m, 0)),
                   pl.BlockSpec((N_EXPERTS, tm), lambda m: (0, m))),
        compiler_params=_cparams(("parallel",)),
        name="out_projection",
    )(merged, w_out, x2, mod4, g_ffn, mod4, mod4, w_router_t)


def _route_kernel(lg_ref, eb_ref, csm_ref, cnt_ref, gx_ref):
    tt = lg_ref.shape[1]
    scores = jax.nn.sigmoid(lg_ref[...])
    sel = scores + eb_ref[...]
    neg_inf = -jnp.inf
    sel_g = [sel[PER_GROUP * g:PER_GROUP * (g + 1), :] for g in range(N_EXPERT_GROUPS)]
    grp = []
    for v in sel_g:
        m1 = jnp.max(v, axis=0, keepdims=True)
        is1 = v == m1
        n1 = jnp.sum(jnp.where(is1, 1.0, 0.0), axis=0, keepdims=True)
        rest = jnp.max(jnp.where(is1, neg_inf, v), axis=0, keepdims=True)
        grp.append(m1 + jnp.where(n1 >= 2.0, m1, rest))
    masked = []
    for g in range(N_EXPERT_GROUPS):
        rank = jnp.zeros((1, tt), F32)
        for g2 in range(N_EXPERT_GROUPS):
            if g2 != g:
                beats = (grp[g2] >= grp[g]) if g2 < g else (grp[g2] > grp[g])
                rank = rank + jnp.where(beats, 1.0, 0.0)
        keep = jnp.broadcast_to(rank < TOPK_GROUPS, (PER_GROUP, tt))
        masked.append(jnp.where(keep, sel_g[g], neg_inf))
    sub = lax.broadcasted_iota(jnp.int32, (PER_GROUP, tt), 0)
    ranks = [jnp.zeros((PER_GROUP, tt), F32) for _ in range(N_EXPERT_GROUPS)]
    for g2 in range(N_EXPERT_GROUPS):
        for m2 in range(PER_GROUP):
            vf = jnp.broadcast_to(masked[g2][m2:m2 + 1, :], (PER_GROUP, tt))
            for g in range(N_EXPERT_GROUPS):
                if g2 < g:
                    beats = vf >= masked[g]
                elif g2 > g:
                    beats = vf > masked[g]
                else:
                    beats = (vf > masked[g]) | ((vf == masked[g]) & (sub > m2))
                ranks[g] = ranks[g] + jnp.where(beats, 1.0, 0.0)
    picked = [ranks[g] < TOP_K for g in range(N_EXPERT_GROUPS)]
    chosen = [jnp.where(picked[g], scores[PER_GROUP * g:PER_GROUP * (g + 1), :], 0.0)
              for g in range(N_EXPERT_GROUPS)]
    total = chosen[0]
    for g in range(1, N_EXPERT_GROUPS):
        total = total + chosen[g]
    denom = jnp.sum(total, axis=0, keepdims=True)
    gate = jnp.concatenate([chosen[g] / denom * ROUTED_SCALE for g in range(N_EXPERT_GROUPS)], axis=0)
    mask = jnp.concatenate([jnp.where(picked[g], 1.0, 0.0) for g in range(N_EXPERT_GROUPS)], axis=0)
    mask_bf = mask.astype(BF16)

    upper = jnp.where(lax.broadcasted_iota(jnp.int32, (tt, tt), 0) <= lax.broadcasted_iota(jnp.int32, (tt, tt), 1),
                      1.0, 0.0).astype(BF16)
    cs = jnp.dot(mask_bf, upper, preferred_element_type=F32)
    csm_ref[...] = jnp.where(mask > 0.0, cs, 0.0)
    cnt_ref[...] = lax.dot_general(jnp.ones((SUBLANES, tt), BF16), mask_bf, (((1,), (1,)), ((), ())),
                                   preferred_element_type=F32)

    hi = gate.astype(BF16).astype(F32)
    gx_ref[...] = jnp.concatenate([hi, gate - hi], axis=0).T.astype(BF16)


def _route(logits_t, e_bias_col):
    return pl.pallas_call(
        _route_kernel,
        out_shape=(jax.ShapeDtypeStruct((N_EXPERTS, N_TOK), F32),
                   jax.ShapeDtypeStruct((N_CHUNKS, SUBLANES, N_EXPERTS), F32),
                   jax.ShapeDtypeStruct((N_TOK, LANES), BF16)),
        grid=(N_CHUNKS,),
        in_specs=[pl.BlockSpec((N_EXPERTS, CHUNK), lambda c: (0, c)),
                  pl.BlockSpec((N_EXPERTS, 1), lambda c: (0, 0))],
        out_specs=(pl.BlockSpec((N_EXPERTS, CHUNK), lambda c: (0, c)),
                   pl.BlockSpec((None, SUBLANES, N_EXPERTS), lambda c: (c, 0, 0)),
                   pl.BlockSpec((CHUNK, LANES), lambda c: (c, 0))),
        compiler_params=_cparams(("parallel",)),
        name="routing",
    )(logits_t, e_bias_col)


def _plan_kernel(cnt_ref, goff_ref, tail_ref, tile_e_ref, ntile_ref, next_ref):
    def per_expert(e, carry):
        base, tbase, prev = carry

        def per_chunk(c, off):
            goff_ref[c, e] = base + off
            return off + (((cnt_ref[c, e] + (GRANULE - 1)) >> GRANULE_SHIFT) << GRANULE_SHIFT)

        tot = lax.fori_loop(0, N_CHUNKS, per_chunk, jnp.int32(0))
        region = ((tot + (TILE - 1)) >> TILE_SHIFT) << TILE_SHIFT
        tail_ref[0, e] = base + tot
        tail_ref[1, e] = (region - tot) >> GRANULE_SHIFT
        n_t = region >> TILE_SHIFT

        def per_tile(j, _):
            tile_e_ref[tbase + j] = e
            return 0

        lax.fori_loop(0, n_t, per_tile, 0)
        next_ref[e] = jnp.int32(-1)

        @pl.when((n_t > 0) & (prev >= 0))
        def _():
            next_ref[jnp.maximum(prev, 0)] = e

        return base + region, tbase + n_t, jnp.where(n_t > 0, e, prev)

    _, n_tiles, _ = lax.fori_loop(0, N_EXPERTS, per_expert, (jnp.int32(0), jnp.int32(0), jnp.int32(-1)))
    ntile_ref[0] = n_tiles
    last_e = tile_e_ref[jnp.maximum(n_tiles - 1, 0)]

    def fill(j, _):
        tile_e_ref[j] = last_e
        return 0

    lax.fori_loop(n_tiles, MAX_TILES, fill, 0)


def _plan(cnt):
    smem = pl.BlockSpec(memory_space=pltpu.SMEM)
    return pl.pallas_call(
        _plan_kernel,
        out_shape=(jax.ShapeDtypeStruct((N_CHUNKS, N_EXPERTS), jnp.int32),
                   jax.ShapeDtypeStruct((2, N_EXPERTS), jnp.int32),
                   jax.ShapeDtypeStruct((MAX_TILES,), jnp.int32),
                   jax.ShapeDtypeStruct((1,), jnp.int32),
                   jax.ShapeDtypeStruct((N_EXPERTS,), jnp.int32)),
        in_specs=[smem],
        out_specs=(smem, smem, smem, smem, smem),
        name="row_plan",
    )(cnt)


def _chunk_granules(cnt_ref, goff_ref, c, table_ref, per_granule=None):
    def per_expert(e, n_before):
        n_gran = (cnt_ref[c, e] + (GRANULE - 1)) >> GRANULE_SHIFT
        base = goff_ref[c, e]

        def per_j(j, _):
            table_ref[n_before + j] = base + j * GRANULE
            if per_granule is not None:
                per_granule(e, j, n_before + j)
            return 0

        lax.fori_loop(0, n_gran, per_j, 0)
        return n_before + n_gran

    return lax.fori_loop(0, N_EXPERTS, per_expert, jnp.int32(0))


def _dispatch_kernel(cnt_ref, goff_ref, tail_ref, x_ref, gx_ref, csm_ref, xs_hbm,
                     onehot_ref, table_ref, buf_ref, zero_ref, xcat_ref, sems, zsem):
    c = pl.program_id(0)
    sub = lax.broadcasted_iota(jnp.int32, (GRANULE, CHUNK), 0)
    xcat_ref[:, 0:D_MODEL] = x_ref[...]
    xcat_ref[:, D_MODEL:XS_WIDTH] = gx_ref[...]

    def build_granule(e, j, g):
        want = (sub + (j * GRANULE + 1)).astype(F32)
        hit = csm_ref[pl.ds(e, 1), :] == want
        onehot_ref[pl.ds(pl.multiple_of(g * GRANULE, GRANULE), GRANULE), :] = jnp.where(hit, 1.0, 0.0).astype(BF16)

    n_gran = _chunk_granules(cnt_ref, goff_ref, c, table_ref, build_granule)
    n_tiles = (n_gran + (GRAN_PER_TILE - 1)) >> GPT_SHIFT

    def clear(g, _):
        onehot_ref[pl.ds(pl.multiple_of(g * GRANULE, GRANULE), GRANULE), :] = jnp.zeros((GRANULE, CHUNK), BF16)
        return 0

    lax.fori_loop(n_gran, n_tiles * GRAN_PER_TILE, clear, 0)

    def granule_copy(slot, i, row):
        return pltpu.make_async_copy(buf_ref.at[slot, pl.ds(pl.multiple_of(i * GRANULE, GRANULE), GRANULE), :],
                                     xs_hbm.at[pl.ds(pl.multiple_of(row, GRANULE), GRANULE), :],
                                     sems.at[slot])

    def tile_granules(k):
        return jnp.minimum(GRAN_PER_TILE, n_gran - k * GRAN_PER_TILE)

    def wait_tile(k):
        slot = k % N_SLOTS

        def w(i, _):
            granule_copy(slot, i, 0).wait()
            return 0

        lax.fori_loop(0, tile_granules(k), w, 0)

    def per_tile(k, _):
        slot = k % N_SLOTS

        @pl.when(k >= N_SLOTS)
        def _():
            wait_tile(k - N_SLOTS)

        oh = onehot_ref[pl.ds(pl.multiple_of(k * TILE, TILE), TILE), :]
        buf_ref[slot] = jnp.dot(oh, xcat_ref[...], preferred_element_type=F32).astype(BF16)

        def s(i, _):
            granule_copy(slot, i, table_ref[k * GRAN_PER_TILE + i]).start()
            return 0

        lax.fori_loop(0, tile_granules(k), s, 0)
        return 0

    lax.fori_loop(0, n_tiles, per_tile, 0)

    def drain(k, _):
        wait_tile(k)
        return 0

    lax.fori_loop(jnp.maximum(n_tiles - N_SLOTS, 0), n_tiles, drain, 0)

    @pl.when(c == N_CHUNKS - 1)
    def _():
        zero_ref[...] = jnp.zeros_like(zero_ref)

        def tail_copy(row):
            return pltpu.make_async_copy(zero_ref, xs_hbm.at[pl.ds(pl.multiple_of(row, GRANULE), GRANULE), :], zsem)

        def start_e(e, _):
            def st(j, _):
                tail_copy(tail_ref[0, e] + j * GRANULE).start()
                return 0
            lax.fori_loop(0, tail_ref[1, e], st, 0)
            return 0

        def wait_e(e, _):
            def wt(j, _):
                tail_copy(0).wait()
                return 0
            lax.fori_loop(0, tail_ref[1, e], wt, 0)
            return 0

        lax.fori_loop(0, N_EXPERTS, start_e, 0)
        lax.fori_loop(0, N_EXPERTS, wait_e, 0)


def _dispatch(cnt, goff, tail, h2, gx, csm):
    grid_spec = pltpu.PrefetchScalarGridSpec(
        num_scalar_prefetch=3,
        grid=(N_CHUNKS,),
        in_specs=[pl.BlockSpec((CHUNK, D_MODEL), lambda c, *_: (c, 0)),
                  pl.BlockSpec((CHUNK, LANES), lambda c, *_: (c, 0)),
                  pl.BlockSpec((N_EXPERTS, CHUNK), lambda c, *_: (0, c))],
        out_specs=pl.BlockSpec(memory_space=pl.ANY),
        scratch_shapes=[pltpu.VMEM((CHUNK_ROWS, CHUNK), BF16),
                        pltpu.SMEM((CHUNK_ROWS // GRANULE,), jnp.int32),
                        pltpu.VMEM((N_SLOTS, TILE, XS_WIDTH), BF16),
                        pltpu.VMEM((GRANULE, XS_WIDTH), BF16),
                        pltpu.VMEM((CHUNK, XS_WIDTH), BF16),
                        pltpu.SemaphoreType.DMA((N_SLOTS,)),
                        pltpu.SemaphoreType.DMA(())])
    return pl.pallas_call(
        _dispatch_kernel,
        out_shape=jax.ShapeDtypeStruct((MAX_ROWS, XS_WIDTH), BF16),
        grid_spec=grid_spec,
        compiler_params=_cparams(("arbitrary",)),
        name="moe_dispatch",
    )(cnt, goff, tail, h2, gx, csm)


def _routed_kernel(tile_e_ref, ntile_ref, next_ref, xs_hbm, w1_hbm, w3_hbm, w2_hbm, ys_hbm,
                   xbuf_ref, ybuf_ref, w1f_ref, w3f_ref, w2f_ref, w1s_ref, w3s_ref, w2s_ref, xsem, ysem, wsem):
    n_tiles = ntile_ref[0]

    def x_copy(t, slot):
        return pltpu.make_async_copy(xs_hbm.at[pl.ds(pl.multiple_of(t * TILE, TILE), TILE), :],
                                     xbuf_ref.at[slot], xsem.at[slot])

    def y_copy(t, slot):
        return pltpu.make_async_copy(ybuf_ref.at[slot],
                                     ys_hbm.at[pl.ds(pl.multiple_of(t * TILE, TILE), TILE), :], ysem.at[slot])

    def fetch(expert, slot):
        return (pltpu.make_async_copy(w1_hbm.at[expert], w1f_ref.at[slot], wsem.at[slot]),
                pltpu.make_async_copy(w3_hbm.at[expert], w3f_ref.at[slot], wsem.at[slot]),
                pltpu.make_async_copy(w2_hbm.at[expert], w2f_ref.at[slot], wsem.at[slot]))

    @pl.when(n_tiles > 0)
    def _():
        x_copy(0, 0).start()
        for cp in fetch(tile_e_ref[0], 0):
            cp.start()

    def per_tile(t, n_started):
        slot = t & 1
        e = tile_e_ref[t]
        x_copy(t, slot).wait()

        @pl.when(t + 1 < n_tiles)
        def _():
            x_copy(t + 1, 1 - slot).start()

        first = (t == 0) | (e != tile_e_ref[jnp.maximum(t - 1, 0)])

        @pl.when(first)
        def _():
            wslot = n_started & 1
            nxt = next_ref[e]

            @pl.when(nxt >= 0)
            def _():
                for cp in fetch(jnp.maximum(nxt, 0), 1 - wslot):
                    cp.start(priority=WEIGHT_DMA_PRIORITY)

            for cp in fetch(e, wslot):
                cp.wait()
            w1s_ref[...] = w1f_ref[wslot].astype(BF16)
            w3s_ref[...] = w3f_ref[wslot].astype(BF16)
            w2s_ref[...] = w2f_ref[wslot].astype(BF16)

        @pl.when(t >= 2)
        def _():
            y_copy(t - 2, slot).wait()

        x = xbuf_ref[slot, :, 0:D_MODEL]
        gx = xbuf_ref[slot, :, D_MODEL:XS_WIDTH].astype(F32)
        lane = lax.broadcasted_iota(jnp.int32, gx.shape, 1)
        gate = jnp.sum(jnp.where((lane == e) | (lane == e + N_EXPERTS), gx, 0.0), axis=1, keepdims=True)
        a = jnp.dot(x, w1s_ref[...], preferred_element_type=F32)
        u = jnp.dot(x, w3s_ref[...], preferred_element_type=F32)
        hid = (a * jax.nn.sigmoid(a)) * u * gate
        ybuf_ref[slot] = jnp.dot(hid.astype(BF16), w2s_ref[...], preferred_element_type=F32).astype(ybuf_ref.dtype)
        y_copy(t, slot).start()
        return n_started + jnp.where(first, 1, 0)

    lax.fori_loop(0, n_tiles, per_tile, jnp.int32(0))

    @pl.when(n_tiles >= 2)
    def _():
        y_copy(n_tiles - 2, n_tiles & 1).wait()

    @pl.when(n_tiles >= 1)
    def _():
        y_copy(n_tiles - 1, (n_tiles - 1) & 1).wait()


def _routed_experts(tile_e, ntile, next_e, xs, w1, w3, w2):
    hbm = pl.BlockSpec(memory_space=pl.ANY)
    grid_spec = pltpu.PrefetchScalarGridSpec(
        num_scalar_prefetch=3,
        grid=(1,),
        in_specs=[hbm, hbm, hbm, hbm],
        out_specs=hbm,
        scratch_shapes=[pltpu.VMEM((2, TILE, XS_WIDTH), BF16),
                        pltpu.VMEM((2, TILE, D_MODEL), BF16),
                        pltpu.VMEM((2, D_MODEL, D_EXPERT), F32),
                        pltpu.VMEM((2, D_MODEL, D_EXPERT), F32),
                        pltpu.VMEM((2, D_EXPERT, D_MODEL), F32),
                        pltpu.VMEM((D_MODEL, D_EXPERT), BF16),
                        pltpu.VMEM((D_MODEL, D_EXPERT), BF16),
                        pltpu.VMEM((D_EXPERT, D_MODEL), BF16),
                        pltpu.SemaphoreType.DMA((2,)),
                        pltpu.SemaphoreType.DMA((2,)),
                        pltpu.SemaphoreType.DMA((2,))])
    return pl.pallas_call(
        _routed_kernel,
        out_shape=jax.ShapeDtypeStruct((MAX_ROWS, D_MODEL), BF16),
        grid_spec=grid_spec,
        compiler_params=_cparams(("arbitrary",)),
        name="routed_experts",
    )(tile_e, ntile, next_e, xs, w1, w3, w2)


def _combine_kernel(cnt_ref, goff_ref, csm_ref, x1_ref, sh_ref, gt_ref, gf_ref, ys_hbm, o_ref,
                    table_ref, buf_ref, acc_ref, pick_ref, sems):
    c = pl.program_id(0)
    sub = lax.broadcasted_iota(jnp.int32, (GRANULE, CHUNK), 0)

    def build_granule(e, j, g):
        want = (sub + (j * GRANULE + 1)).astype(F32)
        hit = csm_ref[pl.ds(e, 1), :] == want
        pick_ref[pl.ds(pl.multiple_of(g * GRANULE, GRANULE), GRANULE), :] = jnp.where(hit, 1.0, 0.0).astype(BF16)

    n_gran = _chunk_granules(cnt_ref, goff_ref, c, table_ref, build_granule)
    n_tiles = (n_gran + (COMBINE_GPT - 1)) >> COMBINE_GPT_SHIFT

    def clear_pick(g, _):
        pick_ref[pl.ds(pl.multiple_of(g * GRANULE, GRANULE), GRANULE), :] = jnp.zeros((GRANULE, CHUNK), BF16)
        return 0

    lax.fori_loop(n_gran, n_tiles * COMBINE_GPT, clear_pick, 0)

    def granule_copy(slot, i, row):
        return pltpu.make_async_copy(ys_hbm.at[pl.ds(pl.multiple_of(row, GRANULE), GRANULE), :],
                                     buf_ref.at[slot, pl.ds(pl.multiple_of(i * GRANULE, GRANULE), GRANULE), :],
                                     sems.at[slot])

    def tile_granules(k):
        return jnp.minimum(COMBINE_GPT, n_gran - k * COMBINE_GPT)

    def start_tile(k):
        slot = k % COMBINE_SLOTS

        def s(i, _):
            granule_copy(slot, i, table_ref[k * COMBINE_GPT + i]).start()
            return 0

        lax.fori_loop(0, tile_granules(k), s, 0)

    def wait_tile(k):
        slot = k % COMBINE_SLOTS

        def w(i, _):
            granule_copy(slot, i, 0).wait()
            return 0

        lax.fori_loop(0, tile_granules(k), w, 0)

    for k0 in range(COMBINE_SLOTS - 1):
        @pl.when(k0 < n_tiles)
        def _():
            start_tile(k0)

    acc_ref[...] = jnp.zeros_like(acc_ref)

    def per_tile(k, _):
        @pl.when(k + (COMBINE_SLOTS - 1) < n_tiles)
        def _():
            start_tile(k + (COMBINE_SLOTS - 1))

        wait_tile(k)
        slot = k % COMBINE_SLOTS

        def clear(i, _):
            buf_ref[slot, pl.ds(pl.multiple_of(i * GRANULE, GRANULE), GRANULE), :] = jnp.zeros((GRANULE, D_MODEL), BF16)
            return 0

        lax.fori_loop(tile_granules(k), COMBINE_GPT, clear, 0)

        pick = pick_ref[pl.ds(pl.multiple_of(k * COMBINE_TILE, COMBINE_TILE), COMBINE_TILE), :]
        acc_ref[...] += lax.dot_general(pick, buf_ref[slot], (((0,), (0,)), ((), ())),
                                        preferred_element_type=F32)
        return 0

    lax.fori_loop(0, n_tiles, per_tile, 0)

    x = x1_ref[...] + gt_ref[...] * (acc_ref[...] + sh_ref[...])
    o_ref[...] = x * lax.rsqrt(jnp.mean(x * x, axis=-1, keepdims=True) + EPS) * gf_ref[...]


def _combine(cnt, goff, csm, x1, shared, mod4, g_final, ys):
    per_b = SEQ // CHUNK
    row = pl.BlockSpec((CHUNK, D_MODEL), lambda c, *_: (c, 0))
    grid_spec = pltpu.PrefetchScalarGridSpec(
        num_scalar_prefetch=2,
        grid=(N_CHUNKS,),
        in_specs=[pl.BlockSpec((N_EXPERTS, CHUNK), lambda c, *_: (0, c)),
                  row, row,
                  pl.BlockSpec((None, None, 1, D_MODEL), lambda c, *_: (c // per_b, 5, 0, 0)),
                  pl.BlockSpec((1, D_MODEL), lambda c, *_: (0, 0)),
                  pl.BlockSpec(memory_space=pl.ANY)],
        out_specs=row,
        scratch_shapes=[pltpu.SMEM((CHUNK_ROWS // GRANULE,), jnp.int32),
                        pltpu.VMEM((COMBINE_SLOTS, COMBINE_TILE, D_MODEL), BF16),
                        pltpu.VMEM((CHUNK, D_MODEL), F32),
                        pltpu.VMEM((CHUNK_ROWS, CHUNK), BF16),
                        pltpu.SemaphoreType.DMA((COMBINE_SLOTS,))])
    return pl.pallas_call(
        _combine_kernel,
        out_shape=jax.ShapeDtypeStruct((N_TOK, D_MODEL), F32),
        grid_spec=grid_spec,
        compiler_params=_cparams(("arbitrary",)),
        name="moe_combine",
    )(cnt, goff, csm, x1, shared, mod4, g_final, ys)


def _shared_kernel(x_ref, w1_ref, w3_ref, w2_ref, o_ref):
    x = x_ref[...]
    a = jnp.dot(x, w1_ref[...], preferred_element_type=F32)
    u = jnp.dot(x, w3_ref[...], preferred_element_type=F32)
    hid = (a * jax.nn.sigmoid(a)) * u
    o_ref[...] = jnp.dot(hid.astype(BF16), w2_ref[...], preferred_element_type=F32).astype(o_ref.dtype)


def _shared_expert(h2, w1, w3, w2):
    tm = 1024
    return pl.pallas_call(
        _shared_kernel,
        out_shape=jax.ShapeDtypeStruct((N_TOK, D_MODEL), F32),
        grid=(N_TOK // tm,),
        in_specs=[pl.BlockSpec((tm, D_MODEL), lambda m: (m, 0)),
                  pl.BlockSpec((D_MODEL, D_SHARED), lambda m: (0, 0)),
                  pl.BlockSpec((D_MODEL, D_SHARED), lambda m: (0, 0)),
                  pl.BlockSpec((D_SHARED, D_MODEL), lambda m: (0, 0))],
        out_specs=pl.BlockSpec((tm, D_MODEL), lambda m: (m, 0)),
        compiler_params=_cparams(("parallel",)),
        name="shared_expert",
    )(h2, w1, w3, w2)


def _rope_angles(pos, dims, theta):
    inv = jnp.power(jnp.float32(theta), -jnp.arange(0, dims, 2, dtype=jnp.float32) / dims)
    return pos.astype(jnp.float32)[:, None] * inv[None, :]


def _rotary_tables(angle_blocks):
    cos_parts, lo_parts, hi_parts = [], [], []
    used = 0
    for ang in angle_blocks:
        c, s = jnp.cos(ang), jnp.sin(ang)
        z = jnp.zeros_like(s)
        cos_parts += [c, c]
        lo_parts += [-s, z]
        hi_parts += [z, s]
        used += 2 * ang.shape[1]
    rest = HEAD_DIM - used
    if rest:
        cos_parts.append(jnp.ones((SEQ, rest), F32))
        lo_parts.append(jnp.zeros((SEQ, rest), F32))
        hi_parts.append(jnp.zeros((SEQ, rest), F32))
    return (jnp.concatenate(cos_parts, axis=1), jnp.concatenate(lo_parts, axis=1),
            jnp.concatenate(hi_parts, axis=1))


def kernel(x, c, w_ada, b_ada, g_attn, w_in, b_gate, q_norm_g, k_norm_g, w_a_up, w_b_up, w_out,
           g_ffn, w_router, e_bias, w1, w3, w2, ws1, ws3, ws2, g_final):
    l = 0
    x2 = x.reshape(N_TOK, D_MODEL)
    pos = jnp.arange(SEQ)
    tabs_a = _rotary_tables([_rope_angles(pos, ROPE_DIMS, ROPE_THETA)])
    tabs_b = _rotary_tables([_rope_angles(pos // GRID_W, AXIAL_DIMS, AXIAL_THETA),
                             _rope_angles(pos % GRID_W, AXIAL_DIMS, AXIAL_THETA)])

    c_pad = jnp.zeros((SUBLANES, D_MODEL), F32).at[:BATCH].set(c)
    mod = _ada(c_pad, w_ada[l], b_ada[l].reshape(1, -1))
    mod4 = mod[:BATCH].reshape(BATCH, N_MOD, 1, D_MODEL)

    h = _norm_modulate(x2, g_attn[l].reshape(1, -1), mod4, 1, 0)
    perms = jnp.stack([_residue_perm(dilation) for _, dilation in A_PATTERNS])
    proj, gates = _projection(h, w_in[l], tabs_a, tabs_b, q_norm_g[l].reshape(1, -1),
                              k_norm_g[l].reshape(1, -1), b_gate[l].reshape(1, -1), perms)
    proj3 = proj.reshape(BATCH, SEQ, QKV_COLS)

    o_groups, lse_groups = [], []
    for gi, (window, dilation) in enumerate(A_PATTERNS):
        o, lse = (_attn_a if dilation == 1 else _attn_a_dilated)(proj3, gi, window, dilation)
        o_groups.append(o)
        lse_groups.append(lse)
    mix_b = _attn_b(proj3)

    merged = _merge(o_groups, lse_groups, mix_b, gates, w_a_up[l].astype(BF16), w_b_up[l].astype(BF16),
                    jnp.swapaxes(perms[1:], 1, 2))
    x1, h2, logits_t = _out_projection(merged, w_out[l].astype(BF16), x2, mod4, g_ffn[l].reshape(1, -1),
                                       w_router[l].T)
    csm, cnt_f, gx = _route(logits_t, e_bias[l].reshape(-1, 1))
    cnt = cnt_f[:, 0, :].astype(jnp.int32)
    goff, tail, tile_e, ntile, next_e = _plan(cnt)
    xs = _dispatch(cnt, goff, tail, h2, gx, csm)
    ys = _routed_experts(tile_e, ntile, next_e, xs, w1[l], w3[l], w2[l])
    shared = _shared_expert(h2, ws1[l].astype(BF16), ws3[l].astype(BF16), ws2[l].astype(BF16))
    out = _combine(cnt, goff, csm, x1, shared, mod4, g_final.reshape(1, -1), ys)
    return out.reshape(BATCH, SEQ, D_MODEL)
```

```python
import functools

import jax
import jax.numpy as jnp
from jax import lax
from jax.experimental import pallas as pl
from jax.experimental.pallas import tpu as pltpu

F32 = jnp.float32
BF16 = jnp.bfloat16

D_MODEL = 2048
BATCH = 2
SEQ = 4096
N_TOK = BATCH * SEQ
HEAD_DIM = 128
EPS = 1e-6
A_PATTERNS = ((128, 1), (512, 4), (2048, 16))
A_N_GROUPS = len(A_PATTERNS)
A_HEADS = 4
A_WIDTH = A_HEADS * HEAD_DIM
ROPE_THETA = 500000.0
ROPE_DIMS = HEAD_DIM // 4
B_Q_HEADS = 16
B_KV_HEADS = 4
B_GROUP = B_Q_HEADS // B_KV_HEADS
B_Q_WIDTH = B_Q_HEADS * HEAD_DIM
B_KV_WIDTH = B_KV_HEADS * HEAD_DIM
AXIAL_THETA = 10000.0
AXIAL_DIMS = HEAD_DIM // 2
GRID_W = 64
A_COLS = A_N_GROUPS * 3 * A_WIDTH
B_COLS = B_Q_WIDTH + 2 * B_KV_WIDTH
GATE_COLS = 2 * D_MODEL
QKV_COLS = A_COLS + B_COLS
IN_COLS = QKV_COLS + GATE_COLS
N_MOD = 6
N_EXPERTS = 64
N_EXPERT_GROUPS = 8
PER_GROUP = N_EXPERTS // N_EXPERT_GROUPS
TOPK_GROUPS = 4
TOP_K = 8
D_EXPERT = D_MODEL // 4
D_SHARED = D_MODEL // 4
ROUTED_SCALE = 2.5
NEG = -1e30
ATTN_SCALE = HEAD_DIM ** -0.5
LOG2E = 1.4426950408889634
LN2 = 0.6931471805599453

LANES = 128
SUBLANES = 8
VMEM_LIMIT = 56 * 1024 * 1024

COL_BLK = 512
PERM_BLOCK = 256
N_COL_BLKS = IN_COLS // COL_BLK
A_BLKS = A_COLS // COL_BLK
BQ_BLK0 = A_BLKS
BK_BLK = BQ_BLK0 + B_Q_WIDTH // COL_BLK
BV_BLK = BK_BLK + 1
GATE_BLK0 = BV_BLK + 1

CHUNK = 512
N_CHUNKS = N_TOK // CHUNK
GRANULE = 16
GRANULE_SHIFT = GRANULE.bit_length() - 1
TILE = 256
TILE_SHIFT = TILE.bit_length() - 1
GRAN_PER_TILE = TILE // GRANULE
GPT_SHIFT = TILE_SHIFT - GRANULE_SHIFT
N_SLOTS = 4
COMBINE_TILE = 1024
COMBINE_GPT = COMBINE_TILE // GRANULE
COMBINE_GPT_SHIFT = COMBINE_GPT.bit_length() - 1
COMBINE_SLOTS = 3
WEIGHT_DMA_PRIORITY = 1
XS_WIDTH = D_MODEL + LANES
CHUNK_ROWS = -(-(CHUNK * TOP_K + N_EXPERTS * (GRANULE - 1)) // TILE) * TILE
MAX_ROWS = -(-(N_TOK * TOP_K + N_EXPERTS * N_CHUNKS * (GRANULE - 1) + N_EXPERTS * (TILE - 1)) // TILE) * TILE
MAX_TILES = MAX_ROWS // TILE


def _cparams(semantics):
    return pltpu.CompilerParams(dimension_semantics=semantics, vmem_limit_bytes=VMEM_LIMIT)


def _ada_kernel(ct_ref, w_ref, b_ref, o_ref):
    ct = ct_ref[...]
    act = ct * jax.nn.sigmoid(ct)
    w = w_ref[...]
    for b in range(BATCH):
        o_ref[b:b + 1, :] = jnp.sum(w * act[:, b:b + 1], axis=0, keepdims=True) + b_ref[...]


def _ada(c_t, w_ada, b_ada):
    tn = 1024
    n_out = w_ada.shape[1]
    return pl.pallas_call(
        _ada_kernel,
        out_shape=jax.ShapeDtypeStruct((BATCH, n_out), F32),
        grid=(n_out // tn,),
        in_specs=[pl.BlockSpec((D_MODEL, BATCH), lambda n: (0, 0)),
                  pl.BlockSpec((D_MODEL, tn), lambda n: (0, n)),
                  pl.BlockSpec((1, tn), lambda n: (0, n))],
        out_specs=pl.BlockSpec((BATCH, tn), lambda n: (0, n)),
        compiler_params=_cparams(("parallel",)),
        name="ada_mod",
    )(c_t, w_ada, b_ada)


def _normmod(x, g, sc, sh):
    y = x * lax.rsqrt(jnp.mean(x * x, axis=-1, keepdims=True) + EPS)
    return (y * g) * (1.0 + sc) + sh


def _tile4(t):
    return jnp.concatenate([t, t, t, t], axis=1)


def _rotary(y, tabs, rows, shift):
    cos_ref, sin_lo_ref, sin_hi_ref = tabs
    width = y.shape[1]
    return (y * _tile4(cos_ref[rows, :])
            + pltpu.roll(y, width - shift, 1) * _tile4(sin_lo_ref[rows, :])
            + pltpu.roll(y, shift, 1) * _tile4(sin_hi_ref[rows, :]))


def _head_rmsnorm(y, g):
    outs = []
    for h in range(y.shape[1] // HEAD_DIM):
        yh = y[:, h * HEAD_DIM:(h + 1) * HEAD_DIM]
        outs.append(yh * lax.rsqrt(jnp.mean(yh * yh, axis=-1, keepdims=True) + EPS) * g)
    return jnp.concatenate(outs, axis=1)


def _proj_kernel(x_ref, g_ref, sc_ref, sh_ref, w_ref, ca_ref, sal_ref, sah_ref, cb_ref, sbl_ref, sbh_ref,
                 gain_ref, scale_ref, bg_ref, perm_ref, o_ref, gate_ref, h_ref, hp_ref):
    n = pl.program_id(1)
    is_a = n < A_BLKS
    part = n % 3
    tabs_a = (ca_ref, sal_ref, sah_ref)
    tabs_b = (cb_ref, sbl_ref, sbh_ref)
    sub_rows = PERM_BLOCK
    n_sub = h_ref.shape[0] // sub_rows

    @pl.when(n == 0)
    def _():
        def norm_rows(r, _):
            rows = pl.ds(pl.multiple_of(r * sub_rows, sub_rows), sub_rows)
            h = _normmod(x_ref[rows, :], g_ref[...], sc_ref[...], sh_ref[...]).astype(BF16)
            h_ref[rows, :] = h
            hp_ref[rows, :] = h
            return 0

        lax.fori_loop(0, n_sub, norm_rows, 0)

    @pl.when(is_a & (part == 0) & (n > 0))
    def _():
        for r in range(n_sub):
            rows = pl.ds(r * sub_rows, sub_rows)
            hp_ref[rows, :] = jnp.dot(perm_ref[...], h_ref[rows, :], preferred_element_type=F32).astype(BF16)

    def run(epilogue, dst_ref=o_ref, src_ref=h_ref):
        w = w_ref[...].astype(BF16)
        for r in range(n_sub):
            rows = pl.ds(r * sub_rows, sub_rows)
            acc = jnp.dot(src_ref[rows, :], w, preferred_element_type=F32)
            dst_ref[rows, :] = epilogue(acc, rows).astype(dst_ref.dtype)

    @pl.when(is_a & (part < 2))
    def _():
        run(lambda acc, rows: _rotary(acc, tabs_a, rows, ROPE_DIMS // 2) * scale_ref[...], src_ref=hp_ref)

    @pl.when(is_a & (part == 2))
    def _():
        run(lambda acc, rows: acc, src_ref=hp_ref)

    @pl.when((n >= BQ_BLK0) & (n <= BK_BLK))
    def _():
        run(lambda acc, rows: _rotary(_head_rmsnorm(acc, gain_ref[...]), tabs_b, rows, AXIAL_DIMS // 2)
            * scale_ref[...])

    @pl.when(n == BV_BLK)
    def _():
        run(lambda acc, rows: acc)

    @pl.when(n >= GATE_BLK0)
    def _():
        run(lambda acc, rows: jax.nn.sigmoid(acc + bg_ref[...]), gate_ref)


def _residue_source(dilation):
    per_res = PERM_BLOCK // dilation
    row = jnp.arange(PERM_BLOCK)
    return (row % per_res) * dilation + row // per_res


def _residue_perm(dilation):
    return (_residue_source(dilation)[:, None] == jnp.arange(PERM_BLOCK)[None, :]).astype(BF16)


def _projection(x2, g_attn, mod4, w_in, tabs_a, tabs_b, qg, kg, b_gate, perms):
    tm = 1024
    per_b = SEQ // tm

    def mod_spec(j):
        return pl.BlockSpec((None, None, 1, D_MODEL), lambda m, n: (m // per_b, j, 0, 0))

    tab_spec = pl.BlockSpec((tm, LANES), lambda m, n: (m % per_b, 0))
    n_gate_blks = GATE_COLS // COL_BLK
    is_query = [(blk < A_BLKS and blk % 3 == 0) or BQ_BLK0 <= blk < BK_BLK for blk in range(N_COL_BLKS)]
    col_scale = jnp.broadcast_to(jnp.where(jnp.array(is_query), ATTN_SCALE * LOG2E, 1.0).astype(F32)[:, None, None],
                                 (N_COL_BLKS, 1, COL_BLK))
    ones = jnp.ones_like(qg)
    col_gain = jnp.stack([qg if BQ_BLK0 <= blk < BK_BLK else kg if blk == BK_BLK else ones
                          for blk in range(N_COL_BLKS)])
    def residue_major(t, dilation):
        per_res = PERM_BLOCK // dilation
        t4 = t.reshape(SEQ // PERM_BLOCK, per_res, dilation, LANES)
        return jnp.swapaxes(t4, 1, 2).reshape(SEQ, LANES)

    tabs_a = [jnp.stack([residue_major(t, dilation) for _, dilation in A_PATTERNS]) for t in tabs_a]
    tab_a_spec = pl.BlockSpec((None, tm, LANES),
                              lambda m, n: (jnp.clip(n // 3, 0, A_N_GROUPS - 1), m % per_b, 0))

    def gate_blk(n):
        return jnp.clip(n - GATE_BLK0, 0, n_gate_blks - 1)

    return pl.pallas_call(
        _proj_kernel,
        out_shape=(jax.ShapeDtypeStruct((N_TOK, QKV_COLS), BF16),
                   jax.ShapeDtypeStruct((N_TOK, GATE_COLS), BF16)),
        grid=(N_TOK // tm, N_COL_BLKS),
        in_specs=[pl.BlockSpec((tm, D_MODEL), lambda m, n: (m, 0)),
                  pl.BlockSpec((1, D_MODEL), lambda m, n: (0, 0)),
                  mod_spec(1), mod_spec(0),
                  pl.BlockSpec((D_MODEL, COL_BLK), lambda m, n: (0, n)),
                  tab_a_spec, tab_a_spec, tab_a_spec, tab_spec, tab_spec, tab_spec,
                  pl.BlockSpec((None, 1, HEAD_DIM), lambda m, n: (n, 0, 0)),
                  pl.BlockSpec((None, 1, COL_BLK), lambda m, n: (n, 0, 0)),
                  pl.BlockSpec((1, COL_BLK), lambda m, n: (0, gate_blk(n))),
                  pl.BlockSpec((None, PERM_BLOCK, PERM_BLOCK),
                               lambda m, n: (jnp.clip(n // 3, 0, A_N_GROUPS - 1), 0, 0))],
        out_specs=(pl.BlockSpec((tm, COL_BLK), lambda m, n: (m, jnp.minimum(n, GATE_BLK0 - 1))),
                   pl.BlockSpec((tm, COL_BLK), lambda m, n: (m, gate_blk(n)))),
        scratch_shapes=[pltpu.VMEM((tm, D_MODEL), BF16), pltpu.VMEM((tm, D_MODEL), BF16)],
        compiler_params=_cparams(("parallel", "arbitrary")),
        name="in_projection",
    )(x2, g_attn, mod4, mod4, w_in, *tabs_a, *tabs_b, col_gain, col_scale, b_gate, perms)


def _attn_a_kernel(q_ref, k_ref, v_ref, o_ref, lse_ref, *, dilation, reach):
    i = pl.program_id(1)
    tq = q_ref.shape[0]
    win = tq + 2 * reach
    start = pl.multiple_of(jnp.clip(i * tq - reach, 0, SEQ - win), GRANULE)
    q = q_ref[...]
    k = k_ref[pl.ds(start, win), :]
    v = v_ref[pl.ds(start, win), :]
    diff = (i * tq - start) + lax.broadcasted_iota(jnp.int32, (tq, win), 0) \
        - lax.broadcasted_iota(jnp.int32, (tq, win), 1)
    valid = (jnp.abs(diff) <= reach) & ((diff & (dilation - 1)) == 0)
    lane = lax.broadcasted_iota(jnp.int32, (tq, LANES), 1)
    lse_tile = jnp.zeros((tq, LANES), F32)
    outs = []
    for h in range(A_HEADS):
        sl = slice(h * HEAD_DIM, (h + 1) * HEAD_DIM)
        s = lax.dot_general(q[:, sl], k[:, sl], (((1,), (1,)), ((), ())), preferred_element_type=F32)
        s = jnp.where(valid, s, NEG)
        m = jnp.max(s, axis=-1, keepdims=True)
        p = jnp.exp2(s - m)
        l = jnp.sum(p, axis=-1, keepdims=True)
        o = jnp.dot(p.astype(BF16), v[:, sl], preferred_element_type=F32)
        outs.append(o / l)
        lse_tile = jnp.where(lane == h, m * LN2 + jnp.log(l), lse_tile)
    o_ref[...] = jnp.concatenate(outs, axis=1).astype(o_ref.dtype)
    lse_ref[...] = lse_tile


def _attn_a(proj3, group, window, dilation):
    assert dilation & (dilation - 1) == 0
    reach = window // 2
    tq = 256
    qb, kb, vb = 3 * group, 3 * group + 1, 3 * group + 2
    o, lse = pl.pallas_call(
        functools.partial(_attn_a_kernel, dilation=dilation, reach=reach),
        out_shape=(jax.ShapeDtypeStruct((BATCH, SEQ, A_WIDTH), BF16),
                   jax.ShapeDtypeStruct((BATCH, SEQ, LANES), F32)),
        grid=(BATCH, SEQ // tq),
        in_specs=[pl.BlockSpec((None, tq, COL_BLK), lambda b, i: (b, i, qb)),
                  pl.BlockSpec((None, SEQ, COL_BLK), lambda b, i: (b, 0, kb)),
                  pl.BlockSpec((None, SEQ, COL_BLK), lambda b, i: (b, 0, vb))],
        out_specs=(pl.BlockSpec((None, tq, A_WIDTH), lambda b, i: (b, i, 0)),
                   pl.BlockSpec((None, tq, LANES), lambda b, i: (b, i, 0))),
        compiler_params=_cparams(("parallel", "arbitrary")),
        name=f"dilated_attention_g{group}",
    )(proj3, proj3, proj3)
    return o.reshape(N_TOK, A_WIDTH), lse.reshape(N_TOK, LANES)


def _attn_res_kernel(q_ref, k_ref, v_ref, o_ref, lse_ref, *, half_w):
    n_blk, per_res, _ = q_ref.shape
    length = n_blk * per_res
    q = q_ref[...].reshape(length, A_WIDTH)
    k = k_ref[...].reshape(length, A_WIDTH)
    v = v_ref[...].reshape(length, A_WIDTH)
    tq = 2 * half_w
    win = 4 * half_w
    lane = lax.broadcasted_iota(jnp.int32, (tq, LANES), 1)
    row_minus_col = lax.broadcasted_iota(jnp.int32, (tq, win), 0) - lax.broadcasted_iota(jnp.int32, (tq, win), 1)
    o_blocks, lse_blocks = [], []
    for qb in range(length // tq):
        start = min(max(qb * tq - half_w, 0), length - win)
        valid = jnp.abs(row_minus_col + (qb * tq - start)) <= half_w
        lse_tile = jnp.zeros((tq, LANES), F32)
        outs = []
        for h in range(A_HEADS):
            sl = slice(h * HEAD_DIM, (h + 1) * HEAD_DIM)
            s = lax.dot_general(q[qb * tq:(qb + 1) * tq, sl], k[start:start + win, sl],
                                (((1,), (1,)), ((), ())), preferred_element_type=F32)
            s = jnp.where(valid, s, NEG)
            m = jnp.max(s, axis=-1, keepdims=True)
            p = jnp.exp2(s - m)
            l = jnp.sum(p, axis=-1, keepdims=True)
            o = jnp.dot(p.astype(BF16), v[start:start + win, sl], preferred_element_type=F32)
            outs.append(o / l)
            lse_tile = jnp.where(lane == h, m * LN2 + jnp.log(l), lse_tile)
        o_blocks.append(jnp.concatenate(outs, axis=1).astype(o_ref.dtype))
        lse_blocks.append(lse_tile)
    o_ref[...] = jnp.concatenate(o_blocks, axis=0).reshape(n_blk, per_res, A_WIDTH)
    lse_ref[...] = jnp.concatenate(lse_blocks, axis=0).reshape(n_blk, per_res, LANES)


def _attn_a_dilated(proj3, group, window, dilation):
    per_res = PERM_BLOCK // dilation
    n_blk = SEQ // PERM_BLOCK
    half_w = (window // 2) // dilation
    view = proj3.reshape(BATCH, n_blk, dilation, per_res, QKV_COLS)

    def spec(width, blk):
        return pl.BlockSpec((None, n_blk, None, per_res, width), lambda b, r: (b, 0, r, 0, blk))

    o, lse = pl.pallas_call(
        functools.partial(_attn_res_kernel, half_w=half_w),
        out_shape=(jax.ShapeDtypeStruct((BATCH, n_blk, dilation, per_res, A_WIDTH), BF16),
                   jax.ShapeDtypeStruct((BATCH, n_blk, dilation, per_res, LANES), F32)),
        grid=(BATCH, dilation),
        in_specs=[spec(COL_BLK, 3 * group), spec(COL_BLK, 3 * group + 1), spec(COL_BLK, 3 * group + 2)],
        out_specs=(spec(A_WIDTH, 0), spec(LANES, 0)),
        compiler_params=_cparams(("parallel", "parallel")),
        name=f"dilated_attention_g{group}",
    )(view, view, view)
    return o.reshape(N_TOK, A_WIDTH), lse.reshape(N_TOK, LANES)


def _attn_b_kernel(q_ref, k_ref, v_ref, o_ref, vx_ref, acc_ref, *, tk):
    tq = q_ref.shape[0]
    n_chunks = SEQ // tk

    @pl.when(pl.program_id(2) == 0)
    def _():
        vx_ref[:, 0:HEAD_DIM] = v_ref[...]
        vx_ref[:, HEAD_DIM:2 * HEAD_DIM] = jnp.ones((SEQ, HEAD_DIM), BF16)

    q = q_ref[...]
    qs = jnp.concatenate([q[:, g * HEAD_DIM:(g + 1) * HEAD_DIM] for g in range(B_GROUP)], axis=0)
    acc_ref[...] = jnp.zeros_like(acc_ref)
    m = jnp.full((B_GROUP * tq, 1), -jnp.inf, F32)
    for c in range(n_chunks):
        keys = slice(c * tk, (c + 1) * tk)
        s = lax.dot_general(qs, k_ref[keys, :], (((1,), (1,)), ((), ())), preferred_element_type=F32)
        m_new = jnp.maximum(m, jnp.max(s, axis=-1, keepdims=True))
        p = jnp.exp2(s - m_new).astype(BF16)
        acc_ref[...] = jnp.exp2(m - m_new) * acc_ref[...] + jnp.dot(p, vx_ref[keys, :],
                                                                    preferred_element_type=F32)
        m = m_new
    o = acc_ref[:, 0:HEAD_DIM] / acc_ref[:, HEAD_DIM:2 * HEAD_DIM]
    o_ref[...] = jnp.concatenate([o[g * tq:(g + 1) * tq] for g in range(B_GROUP)], axis=1).astype(o_ref.dtype)


def _attn_b2_kernel(q_ref, k_ref, v_ref, o_ref, vx_ref, s_ref, m_ref, *, tk):
    i = pl.program_id(2)
    tq = q_ref.shape[0]
    rows = B_GROUP * tq
    n_chunks = SEQ // tk

    @pl.when(i == 0)
    def _():
        vx_ref[:, 0:HEAD_DIM] = v_ref[...]
        vx_ref[:, HEAD_DIM:2 * HEAD_DIM] = jnp.ones((SEQ, HEAD_DIM), BF16)

    @pl.when(i >= 0)
    def _():
        q = q_ref[...]
        qs = jnp.concatenate([q[:, g * HEAD_DIM:(g + 1) * HEAD_DIM] for g in range(B_GROUP)], axis=0)
        run_max = jnp.full((rows, LANES), -jnp.inf, F32)
        for c in range(n_chunks):
            s = lax.dot_general(qs, k_ref[c * tk:(c + 1) * tk, :], (((1,), (1,)), ((), ())),
                                preferred_element_type=F32)
            s_ref[c] = s
            for j in range(tk // LANES):
                run_max = jnp.maximum(run_max, s[:, j * LANES:(j + 1) * LANES])
        m_ref[...] = run_max

    @pl.when(pl.program_id(1) >= 0)
    def _():
        m = jnp.max(m_ref[...], axis=-1, keepdims=True)
        acc = jnp.zeros((rows, 2 * HEAD_DIM), F32)
        for c in range(n_chunks):
            p = jnp.exp2(s_ref[c] - m).astype(BF16)
            acc = acc + jnp.dot(p, vx_ref[c * tk:(c + 1) * tk, :], preferred_element_type=F32)
        o = acc[:, 0:HEAD_DIM] / acc[:, HEAD_DIM:2 * HEAD_DIM]
        o_ref[...] = jnp.concatenate([o[g * tq:(g + 1) * tq] for g in range(B_GROUP)],
                                     axis=1).astype(o_ref.dtype)


def _attn_b(proj3):
    tq = 128
    tk = 512
    kcol0 = BK_BLK * COL_BLK // HEAD_DIM
    vcol0 = BV_BLK * COL_BLK // HEAD_DIM
    o = pl.pallas_call(
        functools.partial(_attn_b2_kernel, tk=tk),
        out_shape=jax.ShapeDtypeStruct((BATCH, SEQ, B_Q_WIDTH), BF16),
        grid=(BATCH, B_KV_HEADS, SEQ // tq),
        in_specs=[pl.BlockSpec((None, tq, COL_BLK), lambda b, h, i: (b, i, BQ_BLK0 + h)),
                  pl.BlockSpec((None, SEQ, HEAD_DIM), lambda b, h, i: (b, 0, kcol0 + h)),
                  pl.BlockSpec((None, SEQ, HEAD_DIM), lambda b, h, i: (b, 0, vcol0 + h))],
        out_specs=pl.BlockSpec((None, tq, COL_BLK), lambda b, h, i: (b, i, h)),
        scratch_shapes=[pltpu.VMEM((SEQ, 2 * HEAD_DIM), BF16),
                        pltpu.VMEM((SEQ // tk, B_GROUP * tq, tk), F32),
                        pltpu.VMEM((B_GROUP * tq, LANES), F32)],
        compiler_params=_cparams(("parallel", "parallel", "arbitrary")),
        name="gqa_attention",
    )(proj3, proj3, proj3)
    return o.reshape(N_TOK, B_Q_WIDTH)


def _merge_kernel(o0_ref, o1_ref, o2_ref, l0_ref, l1_ref, l2_ref, yb_ref, wa_ref, wb_ref, ga_ref, gb_ref,
                  unperm_ref, out_ref):
    sub_rows = PERM_BLOCK
    for r in range(out_ref.shape[0] // sub_rows):
        rows = pl.ds(r * sub_rows, sub_rows)
        outs = [o0_ref[rows, :].astype(F32)]
        lses = [l0_ref[rows, :]]
        for g, (o_ref, l_ref) in enumerate(((o1_ref, l1_ref), (o2_ref, l2_ref))):
            outs.append(jnp.dot(unperm_ref[g], o_ref[rows, :], preferred_element_type=F32))
            lses.append(jnp.dot(unperm_ref[g].astype(F32), l_ref[rows, :], preferred_element_type=F32,
                                precision=lax.Precision.HIGHEST))
        cols = []
        for h in range(A_HEADS):
            lh = [jnp.broadcast_to(l[:, h:h + 1], (sub_rows, HEAD_DIM)) for l in lses]
            mx = jnp.maximum(jnp.maximum(lh[0], lh[1]), lh[2])
            e = [jnp.exp(v - mx) for v in lh]
            den = e[0] + e[1] + e[2]
            sl = slice(h * HEAD_DIM, (h + 1) * HEAD_DIM)
            cols.append(sum((e[g] / den) * outs[g][:, sl] for g in range(A_N_GROUPS)))
        mix_a = jnp.concatenate(cols, axis=1).astype(BF16)
        mix_b = yb_ref[rows, :]
        for cb in range(D_MODEL // COL_BLK):
            cs = pl.ds(cb * COL_BLK, COL_BLK)
            ya = jnp.dot(mix_a, wa_ref[:, cs], preferred_element_type=F32)
            yb = jnp.dot(mix_b, wb_ref[:, cs], preferred_element_type=F32)
            out_ref[rows, cs] = (ga_ref[rows, cs].astype(F32) * ya
                                 + gb_ref[rows, cs].astype(F32) * yb).astype(out_ref.dtype)


def _merge(o_groups, lse_groups, mix_b, gates, w_a_up, w_b_up, unperms):
    tm = 512
    o_spec = pl.BlockSpec((tm, A_WIDTH), lambda m: (m, 0))
    l_spec = pl.BlockSpec((tm, LANES), lambda m: (m, 0))
    return pl.pallas_call(
        _merge_kernel,
        out_shape=jax.ShapeDtypeStruct((N_TOK, D_MODEL), BF16),
        grid=(N_TOK // tm,),
        in_specs=[o_spec, o_spec, o_spec, l_spec, l_spec, l_spec,
                  pl.BlockSpec((tm, B_Q_WIDTH), lambda m: (m, 0)),
                  pl.BlockSpec((A_WIDTH, D_MODEL), lambda m: (0, 0)),
                  pl.BlockSpec((B_Q_WIDTH, D_MODEL), lambda m: (0, 0)),
                  pl.BlockSpec((tm, D_MODEL), lambda m: (m, 0)),
                  pl.BlockSpec((tm, D_MODEL), lambda m: (m, 1)),
                  pl.BlockSpec((A_N_GROUPS - 1, PERM_BLOCK, PERM_BLOCK), lambda m: (0, 0, 0))],
        out_specs=pl.BlockSpec((tm, D_MODEL), lambda m: (m, 0)),
        compiler_params=_cparams(("parallel",)),
        name="branch_merge",
    )(*o_groups, *lse_groups, mix_b, w_a_up, w_b_up, gates, gates, unperms)


def _outproj_kernel(mg_ref, w_ref, x_ref, gt_ref, g_ref, sc_ref, sh_ref, wr_ref, x1_ref, h2_ref, lg_ref):
    sub_rows = 256
    wr = wr_ref[...]
    wr_hi = wr.astype(BF16)
    wr_lo = (wr - wr_hi.astype(F32)).astype(BF16)
    for r in range(mg_ref.shape[0] // sub_rows):
        rows = pl.ds(r * sub_rows, sub_rows)
        y = jnp.dot(mg_ref[rows, :], w_ref[...], preferred_element_type=F32)
        x1 = x_ref[rows, :] + gt_ref[...] * y
        x1_ref[rows, :] = x1
        h2 = _normmod(x1, g_ref[...], sc_ref[...], sh_ref[...])
        h2_hi = h2.astype(BF16)
        h2_ref[rows, :] = h2_hi
        h2_lo = (h2 - h2_hi.astype(F32)).astype(BF16)
        nt = (((1,), (1,)), ((), ()))
        lg_ref[:, rows] = (lax.dot_general(wr_hi, h2_hi, nt, preferred_element_type=F32)
                           + lax.dot_general(wr_hi, h2_lo, nt, preferred_element_type=F32)
                           + lax.dot_general(wr_lo, h2_hi, nt, preferred_element_type=F32))


def _out_projection(merged, w_out, x2, mod4, g_ffn, w_router_t):
    tm = 512
    per_b = SEQ // tm

    def mod_spec(j):
        return pl.BlockSpec((None, None, 1, D_MODEL), lambda m: (m // per_b, j, 0, 0))

    return pl.pallas_call(
        _outproj_kernel,
        out_shape=(jax.ShapeDtypeStruct((N_TOK, D_MODEL), F32),
                   jax.ShapeDtypeStruct((N_TOK, D_MODEL), BF16),
                   jax.ShapeDtypeStruct((N_EXPERTS, N_TOK), F32)),
        grid=(N_TOK // tm,),
        in_specs=[pl.BlockSpec((tm, D_MODEL), lambda m: (m, 0)),
                  pl.BlockSpec((D_MODEL, D_MODEL), lambda m: (0, 0)),
                  pl.BlockSpec((tm, D_MODEL), lambda m: (m, 0)),
                  mod_spec(2),
                  pl.BlockSpec((1, D_MODEL), lambda m: (0, 0)),
                  mod_spec(4), mod_spec(3),
                  pl.BlockSpec((N_EXPERTS, D_MODEL), lambda m: (0, 0))],
        out_specs=(pl.BlockSpec((tm, D_MODEL), lambda m: (m, 0)),
                   pl.BlockSpec((tm, D_MODEL), lambda m: (m, 0)),
                   pl.BlockSpec((N_EXPERTS, tm), lambda m: (0, m))),
        compiler_params=_cparams(("parallel",)),
        name="out_projection",
    )(merged, w_out, x2, mod4, g_ffn, mod4, mod4, w_router_t)


def _route_kernel(lg_ref, eb_ref, csm_ref, cnt_ref, gx_ref):
    tt = lg_ref.shape[1]
    scores = jax.nn.sigmoid(lg_ref[...])
    sel = scores + eb_ref[...]
    neg_inf = -jnp.inf
    sel_g = [sel[PER_GROUP * g:PER_GROUP * (g + 1), :] for g in range(N_EXPERT_GROUPS)]
    grp = []
    for v in sel_g:
        m1 = jnp.max(v, axis=0, keepdims=True)
        is1 = v == m1
        n1 = jnp.sum(jnp.where(is1, 1.0, 0.0), axis=0, keepdims=True)
        rest = jnp.max(jnp.where(is1, neg_inf, v), axis=0, keepdims=True)
        grp.append(m1 + jnp.where(n1 >= 2.0, m1, rest))
    masked = []
    for g in range(N_EXPERT_GROUPS):
        rank = jnp.zeros((1, tt), F32)
        for g2 in range(N_EXPERT_GROUPS):
            if g2 != g:
                beats = (grp[g2] >= grp[g]) if g2 < g else (grp[g2] > grp[g])
                rank = rank + jnp.where(beats, 1.0, 0.0)
        keep = jnp.broadcast_to(rank < TOPK_GROUPS, (PER_GROUP, tt))
        masked.append(jnp.where(keep, sel_g[g], neg_inf))
    sub = lax.broadcasted_iota(jnp.int32, (PER_GROUP, tt), 0)
    ranks = [jnp.zeros((PER_GROUP, tt), F32) for _ in range(N_EXPERT_GROUPS)]
    for g2 in range(N_EXPERT_GROUPS):
        for m2 in range(PER_GROUP):
            vf = jnp.broadcast_to(masked[g2][m2:m2 + 1, :], (PER_GROUP, tt))
            for g in range(N_EXPERT_GROUPS):
                if g2 < g:
                    beats = vf >= masked[g]
                elif g2 > g:
                    beats = vf > masked[g]
                else:
                    beats = (vf > masked[g]) | ((vf == masked[g]) & (sub > m2))
                ranks[g] = ranks[g] + jnp.where(beats, 1.0, 0.0)
    picked = [ranks[g] < TOP_K for g in range(N_EXPERT_GROUPS)]
    chosen = [jnp.where(picked[g], scores[PER_GROUP * g:PER_GROUP * (g + 1), :], 0.0)
              for g in range(N_EXPERT_GROUPS)]
    total = chosen[0]
    for g in range(1, N_EXPERT_GROUPS):
        total = total + chosen[g]
    denom = jnp.sum(total, axis=0, keepdims=True)
    gate = jnp.concatenate([chosen[g] / denom * ROUTED_SCALE for g in range(N_EXPERT_GROUPS)], axis=0)
    mask = jnp.concatenate([jnp.where(picked[g], 1.0, 0.0) for g in range(N_EXPERT_GROUPS)], axis=0)
    mask_bf = mask.astype(BF16)

    upper = jnp.where(lax.broadcasted_iota(jnp.int32, (tt, tt), 0) <= lax.broadcasted_iota(jnp.int32, (tt, tt), 1),
                      1.0, 0.0).astype(BF16)
    cs = jnp.dot(mask_bf, upper, preferred_element_type=F32)
    csm_ref[...] = jnp.where(mask > 0.0, cs, 0.0)
    cnt_ref[...] = lax.dot_general(jnp.ones((SUBLANES, tt), BF16), mask_bf, (((1,), (1,)), ((), ())),
                                   preferred_element_type=F32)

    hi = gate.astype(BF16).astype(F32)
    gx_ref[...] = jnp.concatenate([hi, gate - hi], axis=0).T.astype(BF16)


def _route(logits_t, e_bias_col):
    return pl.pallas_call(
        _route_kernel,
        out_shape=(jax.ShapeDtypeStruct((N_EXPERTS, N_TOK), F32),
                   jax.ShapeDtypeStruct((N_CHUNKS, SUBLANES, N_EXPERTS), F32),
                   jax.ShapeDtypeStruct((N_TOK, LANES), BF16)),
        grid=(N_CHUNKS,),
        in_specs=[pl.BlockSpec((N_EXPERTS, CHUNK), lambda c: (0, c)),
                  pl.BlockSpec((N_EXPERTS, 1), lambda c: (0, 0))],
        out_specs=(pl.BlockSpec((N_EXPERTS, CHUNK), lambda c: (0, c)),
                   pl.BlockSpec((None, SUBLANES, N_EXPERTS), lambda c: (c, 0, 0)),
                   pl.BlockSpec((CHUNK, LANES), lambda c: (c, 0))),
        compiler_params=_cparams(("parallel",)),
        name="routing",
    )(logits_t, e_bias_col)


def _plan_kernel(cnt_ref, goff_ref, tail_ref, tile_e_ref, ntile_ref, next_ref):
    def per_expert(e, carry):
        base, tbase, prev = carry

        def per_chunk(c, off):
            goff_ref[c, e] = base + off
            return off + (((cnt_ref[c, e] + (GRANULE - 1)) >> GRANULE_SHIFT) << GRANULE_SHIFT)

        tot = lax.fori_loop(0, N_CHUNKS, per_chunk, jnp.int32(0))
        region = ((tot + (TILE - 1)) >> TILE_SHIFT) << TILE_SHIFT
        tail_ref[0, e] = base + tot
        tail_ref[1, e] = (region - tot) >> GRANULE_SHIFT
        n_t = region >> TILE_SHIFT

        def per_tile(j, _):
            tile_e_ref[tbase + j] = e
            return 0

        lax.fori_loop(0, n_t, per_tile, 0)
        next_ref[e] = jnp.int32(-1)

        @pl.when((n_t > 0) & (prev >= 0))
        def _():
            next_ref[jnp.maximum(prev, 0)] = e

        return base + region, tbase + n_t, jnp.where(n_t > 0, e, prev)

    _, n_tiles, _ = lax.fori_loop(0, N_EXPERTS, per_expert, (jnp.int32(0), jnp.int32(0), jnp.int32(-1)))
    ntile_ref[0] = n_tiles
    last_e = tile_e_ref[jnp.maximum(n_tiles - 1, 0)]

    def fill(j, _):
        tile_e_ref[j] = last_e
        return 0

    lax.fori_loop(n_tiles, MAX_TILES, fill, 0)


def _plan(cnt):
    smem = pl.BlockSpec(memory_space=pltpu.SMEM)
    return pl.pallas_call(
        _plan_kernel,
        out_shape=(jax.ShapeDtypeStruct((N_CHUNKS, N_EXPERTS), jnp.int32),
                   jax.ShapeDtypeStruct((2, N_EXPERTS), jnp.int32),
                   jax.ShapeDtypeStruct((MAX_TILES,), jnp.int32),
                   jax.ShapeDtypeStruct((1,), jnp.int32),
                   jax.ShapeDtypeStruct((N_EXPERTS,), jnp.int32)),
        in_specs=[smem],
        out_specs=(smem, smem, smem, smem, smem),
        name="row_plan",
    )(cnt)


def _chunk_granules(cnt_ref, goff_ref, c, table_ref, per_granule=None):
    def per_expert(e, n_before):
        n_gran = (cnt_ref[c, e] + (GRANULE - 1)) >> GRANULE_SHIFT
        base = goff_ref[c, e]

        def per_j(j, _):
            table_ref[n_before + j] = base + j * GRANULE
            if per_granule is not None:
                per_granule(e, j, n_before + j)
            return 0

        lax.fori_loop(0, n_gran, per_j, 0)
        return n_before + n_gran

    return lax.fori_loop(0, N_EXPERTS, per_expert, jnp.int32(0))


def _dispatch_kernel(cnt_ref, goff_ref, tail_ref, x_ref, gx_ref, csm_ref, xs_hbm,
                     onehot_ref, table_ref, buf_ref, zero_ref, xcat_ref, sems, zsem):
    c = pl.program_id(0)
    sub = lax.broadcasted_iota(jnp.int32, (GRANULE, CHUNK), 0)
    xcat_ref[:, 0:D_MODEL] = x_ref[...]
    xcat_ref[:, D_MODEL:XS_WIDTH] = gx_ref[...]

    def build_granule(e, j, g):
        want = (sub + (j * GRANULE + 1)).astype(F32)
        hit = csm_ref[pl.ds(e, 1), :] == want
        onehot_ref[pl.ds(pl.multiple_of(g * GRANULE, GRANULE), GRANULE), :] = jnp.where(hit, 1.0, 0.0).astype(BF16)

    n_gran = _chunk_granules(cnt_ref, goff_ref, c, table_ref, build_granule)
    n_tiles = (n_gran + (GRAN_PER_TILE - 1)) >> GPT_SHIFT

    def clear(g, _):
        onehot_ref[pl.ds(pl.multiple_of(g * GRANULE, GRANULE), GRANULE), :] = jnp.zeros((GRANULE, CHUNK), BF16)
        return 0

    lax.fori_loop(n_gran, n_tiles * GRAN_PER_TILE, clear, 0)

    def granule_copy(slot, i, row):
        return pltpu.make_async_copy(buf_ref.at[slot, pl.ds(pl.multiple_of(i * GRANULE, GRANULE), GRANULE), :],
                                     xs_hbm.at[pl.ds(pl.multiple_of(row, GRANULE), GRANULE), :],
                                     sems.at[slot])

    def tile_granules(k):
        return jnp.minimum(GRAN_PER_TILE, n_gran - k * GRAN_PER_TILE)

    def wait_tile(k):
        slot = k % N_SLOTS

        def w(i, _):
            granule_copy(slot, i, 0).wait()
            return 0

        lax.fori_loop(0, tile_granules(k), w, 0)

    def per_tile(k, _):
        slot = k % N_SLOTS

        @pl.when(k >= N_SLOTS)
        def _():
            wait_tile(k - N_SLOTS)

        oh = onehot_ref[pl.ds(pl.multiple_of(k * TILE, TILE), TILE), :]
        buf_ref[slot] = jnp.dot(oh, xcat_ref[...], preferred_element_type=F32).astype(BF16)

        def s(i, _):
            granule_copy(slot, i, table_ref[k * GRAN_PER_TILE + i]).start()
            return 0

        lax.fori_loop(0, tile_granules(k), s, 0)
        return 0

    lax.fori_loop(0, n_tiles, per_tile, 0)

    def drain(k, _):
        wait_tile(k)
        return 0

    lax.fori_loop(jnp.maximum(n_tiles - N_SLOTS, 0), n_tiles, drain, 0)

    @pl.when(c == N_CHUNKS - 1)
    def _():
        zero_ref[...] = jnp.zeros_like(zero_ref)

        def tail_copy(row):
            return pltpu.make_async_copy(zero_ref, xs_hbm.at[pl.ds(pl.multiple_of(row, GRANULE), GRANULE), :], zsem)

        def start_e(e, _):
            def st(j, _):
                tail_copy(tail_ref[0, e] + j * GRANULE).start()
                return 0
            lax.fori_loop(0, tail_ref[1, e], st, 0)
            return 0

        def wait_e(e, _):
            def wt(j, _):
                tail_copy(0).wait()
                return 0
            lax.fori_loop(0, tail_ref[1, e], wt, 0)
            return 0

        lax.fori_loop(0, N_EXPERTS, start_e, 0)
        lax.fori_loop(0, N_EXPERTS, wait_e, 0)


def _dispatch(cnt, goff, tail, h2, gx, csm):
    grid_spec = pltpu.PrefetchScalarGridSpec(
        num_scalar_prefetch=3,
        grid=(N_CHUNKS,),
        in_specs=[pl.BlockSpec((CHUNK, D_MODEL), lambda c, *_: (c, 0)),
                  pl.BlockSpec((CHUNK, LANES), lambda c, *_: (c, 0)),
                  pl.BlockSpec((N_EXPERTS, CHUNK), lambda c, *_: (0, c))],
        out_specs=pl.BlockSpec(memory_space=pl.ANY),
        scratch_shapes=[pltpu.VMEM((CHUNK_ROWS, CHUNK), BF16),
                        pltpu.SMEM((CHUNK_ROWS // GRANULE,), jnp.int32),
                        pltpu.VMEM((N_SLOTS, TILE, XS_WIDTH), BF16),
                        pltpu.VMEM((GRANULE, XS_WIDTH), BF16),
                        pltpu.VMEM((CHUNK, XS_WIDTH), BF16),
                        pltpu.SemaphoreType.DMA((N_SLOTS,)),
                        pltpu.SemaphoreType.DMA(())])
    return pl.pallas_call(
        _dispatch_kernel,
        out_shape=jax.ShapeDtypeStruct((MAX_ROWS, XS_WIDTH), BF16),
        grid_spec=grid_spec,
        compiler_params=_cparams(("arbitrary",)),
        name="moe_dispatch",
    )(cnt, goff, tail, h2, gx, csm)


def _routed_kernel(tile_e_ref, ntile_ref, next_ref, xs_hbm, w1_hbm, w3_hbm, w2_hbm, ys_hbm,
                   xbuf_ref, ybuf_ref, w1f_ref, w3f_ref, w2f_ref, w1s_ref, w3s_ref, w2s_ref, xsem, ysem, wsem):
    n_tiles = ntile_ref[0]

    def x_copy(t, slot):
        return pltpu.make_async_copy(xs_hbm.at[pl.ds(pl.multiple_of(t * TILE, TILE), TILE), :],
                                     xbuf_ref.at[slot], xsem.at[slot])

    def y_copy(t, slot):
        return pltpu.make_async_copy(ybuf_ref.at[slot],
                                     ys_hbm.at[pl.ds(pl.multiple_of(t * TILE, TILE), TILE), :], ysem.at[slot])

    def fetch(expert, slot):
        return (pltpu.make_async_copy(w1_hbm.at[expert], w1f_ref.at[slot], wsem.at[slot]),
                pltpu.make_async_copy(w3_hbm.at[expert], w3f_ref.at[slot], wsem.at[slot]),
                pltpu.make_async_copy(w2_hbm.at[expert], w2f_ref.at[slot], wsem.at[slot]))

    @pl.when(n_tiles > 0)
    def _():
        x_copy(0, 0).start()
        for cp in fetch(tile_e_ref[0], 0):
            cp.start()

    def per_tile(t, n_started):
        slot = t & 1
        e = tile_e_ref[t]
        x_copy(t, slot).wait()

        @pl.when(t + 1 < n_tiles)
        def _():
            x_copy(t + 1, 1 - slot).start()

        first = (t == 0) | (e != tile_e_ref[jnp.maximum(t - 1, 0)])

        @pl.when(first)
        def _():
            wslot = n_started & 1
            nxt = next_ref[e]

            @pl.when(nxt >= 0)
            def _():
                for cp in fetch(jnp.maximum(nxt, 0), 1 - wslot):
                    cp.start(priority=WEIGHT_DMA_PRIORITY)

            for cp in fetch(e, wslot):
                cp.wait()
            w1s_ref[...] = w1f_ref[wslot].astype(BF16)
            w3s_ref[...] = w3f_ref[wslot].astype(BF16)
            w2s_ref[...] = w2f_ref[wslot].astype(BF16)

        @pl.when(t >= 2)
        def _():
            y_copy(t - 2, slot).wait()

        x = xbuf_ref[slot, :, 0:D_MODEL]
        gx = xbuf_ref[slot, :, D_MODEL:XS_WIDTH].astype(F32)
        lane = lax.broadcasted_iota(jnp.int32, gx.shape, 1)
        gate = jnp.sum(jnp.where((lane == e) | (lane == e + N_EXPERTS), gx, 0.0), axis=1, keepdims=True)
        a = jnp.dot(x, w1s_ref[...], preferred_element_type=F32)
        u = jnp.dot(x, w3s_ref[...], preferred_element_type=F32)
        hid = (a * jax.nn.sigmoid(a)) * u * gate
        ybuf_ref[slot] = jnp.dot(hid.astype(BF16), w2s_ref[...], preferred_element_type=F32).astype(ybuf_ref.dtype)
        y_copy(t, slot).start()
        return n_started + jnp.where(first, 1, 0)

    lax.fori_loop(0, n_tiles, per_tile, jnp.int32(0))

    @pl.when(n_tiles >= 2)
    def _():
        y_copy(n_tiles - 2, n_tiles & 1).wait()

    @pl.when(n_tiles >= 1)
    def _():
        y_copy(n_tiles - 1, (n_tiles - 1) & 1).wait()


def _routed_experts(tile_e, ntile, next_e, xs, w1, w3, w2):
    hbm = pl.BlockSpec(memory_space=pl.ANY)
    grid_spec = pltpu.PrefetchScalarGridSpec(
        num_scalar_prefetch=3,
        grid=(1,),
        in_specs=[hbm, hbm, hbm, hbm],
        out_specs=hbm,
        scratch_shapes=[pltpu.VMEM((2, TILE, XS_WIDTH), BF16),
                        pltpu.VMEM((2, TILE, D_MODEL), BF16),
                        pltpu.VMEM((2, D_MODEL, D_EXPERT), F32),
                        pltpu.VMEM((2, D_MODEL, D_EXPERT), F32),
                        pltpu.VMEM((2, D_EXPERT, D_MODEL), F32),
                        pltpu.VMEM((D_MODEL, D_EXPERT), BF16),
                        pltpu.VMEM((D_MODEL, D_EXPERT), BF16),
                        pltpu.VMEM((D_EXPERT, D_MODEL), BF16),
                        pltpu.SemaphoreType.DMA((2,)),
                        pltpu.SemaphoreType.DMA((2,)),
                        pltpu.SemaphoreType.DMA((2,))])
    return pl.pallas_call(
        _routed_kernel,
        out_shape=jax.ShapeDtypeStruct((MAX_ROWS, D_MODEL), BF16),
        grid_spec=grid_spec,
        compiler_params=_cparams(("arbitrary",)),
        name="routed_experts",
    )(tile_e, ntile, next_e, xs, w1, w3, w2)


def _combine_kernel(cnt_ref, goff_ref, csm_ref, x1_ref, sh_ref, gt_ref, gf_ref, ys_hbm, o_ref,
                    table_ref, buf_ref, acc_ref, pick_ref, sems):
    c = pl.program_id(0)
    sub = lax.broadcasted_iota(jnp.int32, (GRANULE, CHUNK), 0)

    def build_granule(e, j, g):
        want = (sub + (j * GRANULE + 1)).astype(F32)
        hit = csm_ref[pl.ds(e, 1), :] == want
        pick_ref[pl.ds(pl.multiple_of(g * GRANULE, GRANULE), GRANULE), :] = jnp.where(hit, 1.0, 0.0).astype(BF16)

    n_gran = _chunk_granules(cnt_ref, goff_ref, c, table_ref, build_granule)
    n_tiles = (n_gran + (COMBINE_GPT - 1)) >> COMBINE_GPT_SHIFT

    def clear_pick(g, _):
        pick_ref[pl.ds(pl.multiple_of(g * GRANULE, GRANULE), GRANULE), :] = jnp.zeros((GRANULE, CHUNK), BF16)
        return 0

    lax.fori_loop(n_gran, n_tiles * COMBINE_GPT, clear_pick, 0)

    def granule_copy(slot, i, row):
        return pltpu.make_async_copy(ys_hbm.at[pl.ds(pl.multiple_of(row, GRANULE), GRANULE), :],
                                     buf_ref.at[slot, pl.ds(pl.multiple_of(i * GRANULE, GRANULE), GRANULE), :],
                                     sems.at[slot])

    def tile_granules(k):
        return jnp.minimum(COMBINE_GPT, n_gran - k * COMBINE_GPT)

    def start_tile(k):
        slot = k % COMBINE_SLOTS

        def s(i, _):
            granule_copy(slot, i, table_ref[k * COMBINE_GPT + i]).start()
            return 0

        lax.fori_loop(0, tile_granules(k), s, 0)

    def wait_tile(k):
        slot = k % COMBINE_SLOTS

        def w(i, _):
            granule_copy(slot, i, 0).wait()
            return 0

        lax.fori_loop(0, tile_granules(k), w, 0)

    for k0 in range(COMBINE_SLOTS - 1):
        @pl.when(k0 < n_tiles)
        def _():
            start_tile(k0)

    acc_ref[...] = jnp.zeros_like(acc_ref)

    def per_tile(k, _):
        @pl.when(k + (COMBINE_SLOTS - 1) < n_tiles)
        def _():
            start_tile(k + (COMBINE_SLOTS - 1))

        wait_tile(k)
        slot = k % COMBINE_SLOTS

        def clear(i, _):
            buf_ref[slot, pl.ds(pl.multiple_of(i * GRANULE, GRANULE), GRANULE), :] = jnp.zeros((GRANULE, D_MODEL), BF16)
            return 0

        lax.fori_loop(tile_granules(k), COMBINE_GPT, clear, 0)

        pick = pick_ref[pl.ds(pl.multiple_of(k * COMBINE_TILE, COMBINE_TILE), COMBINE_TILE), :]
        acc_ref[...] += lax.dot_general(pick, buf_ref[slot], (((0,), (0,)), ((), ())),
                                        preferred_element_type=F32)
        return 0

    lax.fori_loop(0, n_tiles, per_tile, 0)

    x = x1_ref[...] + gt_ref[...] * (acc_ref[...] + sh_ref[...])
    o_ref[...] = x * lax.rsqrt(jnp.mean(x * x, axis=-1, keepdims=True) + EPS) * gf_ref[...]


def _combine(cnt, goff, csm, x1, shared, mod4, g_final, ys):
    per_b = SEQ // CHUNK
    row = pl.BlockSpec((CHUNK, D_MODEL), lambda c, *_: (c, 0))
    grid_spec = pltpu.PrefetchScalarGridSpec(
        num_scalar_prefetch=2,
        grid=(N_CHUNKS,),
        in_specs=[pl.BlockSpec((N_EXPERTS, CHUNK), lambda c, *_: (0, c)),
                  row, row,
                  pl.BlockSpec((None, None, 1, D_MODEL), lambda c, *_: (c // per_b, 5, 0, 0)),
                  pl.BlockSpec((1, D_MODEL), lambda c, *_: (0, 0)),
                  pl.BlockSpec(memory_space=pl.ANY)],
        out_specs=row,
        scratch_shapes=[pltpu.SMEM((CHUNK_ROWS // GRANULE,), jnp.int32),
                        pltpu.VMEM((COMBINE_SLOTS, COMBINE_TILE, D_MODEL), BF16),
                        pltpu.VMEM((CHUNK, D_MODEL), F32),
                        pltpu.VMEM((CHUNK_ROWS, CHUNK), BF16),
                        pltpu.SemaphoreType.DMA((COMBINE_SLOTS,))])
    return pl.pallas_call(
        _combine_kernel,
        out_shape=jax.ShapeDtypeStruct((N_TOK, D_MODEL), F32),
        grid_spec=grid_spec,
        compiler_params=_cparams(("arbitrary",)),
        name="moe_combine",
    )(cnt, goff, csm, x1, shared, mod4, g_final, ys)


def _shared_kernel(x_ref, w1_ref, w3_ref, w2_ref, o_ref):
    x = x_ref[...]
    a = jnp.dot(x, w1_ref[...], preferred_element_type=F32)
    u = jnp.dot(x, w3_ref[...], preferred_element_type=F32)
    hid = (a * jax.nn.sigmoid(a)) * u
    o_ref[...] = jnp.dot(hid.astype(BF16), w2_ref[...], preferred_element_type=F32).astype(o_ref.dtype)


def _shared_expert(h2, w1, w3, w2):
    tm = 1024
    return pl.pallas_call(
        _shared_kernel,
        out_shape=jax.ShapeDtypeStruct((N_TOK, D_MODEL), F32),
        grid=(N_TOK // tm,),
        in_specs=[pl.BlockSpec((tm, D_MODEL), lambda m: (m, 0)),
                  pl.BlockSpec((D_MODEL, D_SHARED), lambda m: (0, 0)),
                  pl.BlockSpec((D_MODEL, D_SHARED), lambda m: (0, 0)),
                  pl.BlockSpec((D_SHARED, D_MODEL), lambda m: (0, 0))],
        out_specs=pl.BlockSpec((tm, D_MODEL), lambda m: (m, 0)),
        compiler_params=_cparams(("parallel",)),
        name="shared_expert",
    )(h2, w1, w3, w2)


def _rope_angles(pos, dims, theta):
    inv = jnp.power(jnp.float32(theta), -jnp.arange(0, dims, 2, dtype=jnp.float32) / dims)
    return pos.astype(jnp.float32)[:, None] * inv[None, :]


def _rotary_tables(angle_blocks):
    cos_parts, lo_parts, hi_parts = [], [], []
    used = 0
    for ang in angle_blocks:
        c, s = jnp.cos(ang), jnp.sin(ang)
        z = jnp.zeros_like(s)
        cos_parts += [c, c]
        lo_parts += [-s, z]
        hi_parts += [z, s]
        used += 2 * ang.shape[1]
    rest = HEAD_DIM - used
    if rest:
        cos_parts.append(jnp.ones((SEQ, rest), F32))
        lo_parts.append(jnp.zeros((SEQ, rest), F32))
        hi_parts.append(jnp.zeros((SEQ, rest), F32))
    return (jnp.concatenate(cos_parts, axis=1), jnp.concatenate(lo_parts, axis=1),
            jnp.concatenate(hi_parts, axis=1))


def kernel(x, c, w_ada, b_ada, g_attn, w_in, b_gate, q_norm_g, k_norm_g, w_a_up, w_b_up, w_out,
           g_ffn, w_router, e_bias, w1, w3, w2, ws1, ws3, ws2, g_final):
    l = 0
    x2 = x.reshape(N_TOK, D_MODEL)
    pos = jnp.arange(SEQ)
    tabs_a = _rotary_tables([_rope_angles(pos, ROPE_DIMS, ROPE_THETA)])
    tabs_b = _rotary_tables([_rope_angles(pos // GRID_W, AXIAL_DIMS, AXIAL_THETA),
                             _rope_angles(pos % GRID_W, AXIAL_DIMS, AXIAL_THETA)])

    mod = _ada(c.T, w_ada[l], b_ada[l].reshape(1, -1))
    mod4 = mod.reshape(BATCH, N_MOD, 1, D_MODEL)

    perms = jnp.stack([_residue_perm(dilation) for _, dilation in A_PATTERNS])
    proj, gates = _projection(x2, g_attn[l].reshape(1, -1), mod4, w_in[l], tabs_a, tabs_b,
                              q_norm_g[l].reshape(1, -1), k_norm_g[l].reshape(1, -1), b_gate[l].reshape(1, -1),
                              perms)
    proj3 = proj.reshape(BATCH, SEQ, QKV_COLS)

    o_groups, lse_groups = [], []
    for gi, (window, dilation) in enumerate(A_PATTERNS):
        o, lse = (_attn_a if dilation == 1 else _attn_a_dilated)(proj3, gi, window, dilation)
        o_groups.append(o)
        lse_groups.append(lse)
    mix_b = _attn_b(proj3)

    merged = _merge(o_groups, lse_groups, mix_b, gates, w_a_up[l].astype(BF16), w_b_up[l].astype(BF16),
                    jnp.swapaxes(perms[1:], 1, 2))
    x1, h2, logits_t = _out_projection(merged, w_out[l].astype(BF16), x2, mod4, g_ffn[l].reshape(1, -1),
                                       w_router[l].T)
    csm, cnt_f, gx = _route(logits_t, e_bias[l].reshape(-1, 1))
    cnt = cnt_f[:, 0, :].astype(jnp.int32)
    goff, tail, tile_e, ntile, next_e = _plan(cnt)
    xs = _dispatch(cnt, goff, tail, h2, gx, csm)
    ys = _routed_experts(tile_e, ntile, next_e, xs, w1[l], w3[l], w2[l])
    shared = _shared_expert(h2, ws1[l].astype(BF16), ws3[l].astype(BF16), ws2[l].astype(BF16))
    out = _combine(cnt, goff, csm, x1, shared, mod4, g_final.reshape(1, -1), ys)
    return out.reshape(BATCH, SEQ, D_MODEL)
```

```python
import functools

import jax
import jax.numpy as jnp
from jax import lax
from jax.experimental import pallas as pl
from jax.experimental.pallas import tpu as pltpu

F32 = jnp.float32
BF16 = jnp.bfloat16

D_MODEL = 2048
BATCH = 2
SEQ = 4096
N_TOK = BATCH * SEQ
HEAD_DIM = 128
EPS = 1e-6
A_PATTERNS = ((128, 1), (512, 4), (2048, 16))
A_N_GROUPS = len(A_PATTERNS)
A_HEADS = 4
A_WIDTH = A_HEADS * HEAD_DIM
ROPE_THETA = 500000.0
ROPE_DIMS = HEAD_DIM // 4
B_Q_HEADS = 16
B_KV_HEADS = 4
B_GROUP = B_Q_HEADS // B_KV_HEADS
B_Q_WIDTH = B_Q_HEADS * HEAD_DIM
B_KV_WIDTH = B_KV_HEADS * HEAD_DIM
AXIAL_THETA = 10000.0
AXIAL_DIMS = HEAD_DIM // 2
GRID_W = 64
A_COLS = A_N_GROUPS * 3 * A_WIDTH
B_COLS = B_Q_WIDTH + 2 * B_KV_WIDTH
GATE_COLS = 2 * D_MODEL
QKV_COLS = A_COLS + B_COLS
IN_COLS = QKV_COLS + GATE_COLS
N_MOD = 6
N_EXPERTS = 64
N_EXPERT_GROUPS = 8
PER_GROUP = N_EXPERTS // N_EXPERT_GROUPS
TOPK_GROUPS = 4
TOP_K = 8
D_EXPERT = D_MODEL // 4
D_SHARED = D_MODEL // 4
ROUTED_SCALE = 2.5
NEG = -1e30
ATTN_SCALE = HEAD_DIM ** -0.5
LOG2E = 1.4426950408889634
LN2 = 0.6931471805599453

LANES = 128
SUBLANES = 8
VMEM_LIMIT = 56 * 1024 * 1024

COL_BLK = 512
PERM_BLOCK = 256
N_COL_BLKS = IN_COLS // COL_BLK
A_BLKS = A_COLS // COL_BLK
BQ_BLK0 = A_BLKS
BK_BLK = BQ_BLK0 + B_Q_WIDTH // COL_BLK
BV_BLK = BK_BLK + 1
GATE_BLK0 = BV_BLK + 1

CHUNK = 512
N_CHUNKS = N_TOK // CHUNK
GRANULE = 16
GRANULE_SHIFT = GRANULE.bit_length() - 1
TILE = 256
TILE_SHIFT = TILE.bit_length() - 1
GRAN_PER_TILE = TILE // GRANULE
GPT_SHIFT = TILE_SHIFT - GRANULE_SHIFT
N_SLOTS = 4
COMBINE_TILE = 1024
COMBINE_GPT = COMBINE_TILE // GRANULE
COMBINE_GPT_SHIFT = COMBINE_GPT.bit_length() - 1
COMBINE_SLOTS = 3
WEIGHT_DMA_PRIORITY = 1
XS_WIDTH = D_MODEL + LANES
CHUNK_ROWS = -(-(CHUNK * TOP_K + N_EXPERTS * (GRANULE - 1)) // TILE) * TILE
MAX_ROWS = -(-(N_TOK * TOP_K + N_EXPERTS * N_CHUNKS * (GRANULE - 1) + N_EXPERTS * (TILE - 1)) // TILE) * TILE
MAX_TILES = MAX_ROWS // TILE


def _cparams(semantics):
    return pltpu.CompilerParams(dimension_semantics=semantics, vmem_limit_bytes=VMEM_LIMIT)


def _ada_kernel(ct_ref, w_ref, b_ref, o_ref):
    ct = ct_ref[...]
    act = ct * jax.nn.sigmoid(ct)
    w = w_ref[...]
    for b in range(BATCH):
        o_ref[b:b + 1, :] = jnp.sum(w * act[:, b:b + 1], axis=0, keepdims=True) + b_ref[...]


def _ada(c_t, w_ada, b_ada):
    tn = 1024
    n_out = w_ada.shape[1]
    return pl.pallas_call(
        _ada_kernel,
        out_shape=jax.ShapeDtypeStruct((BATCH, n_out), F32),
        grid=(n_out // tn,),
        in_specs=[pl.BlockSpec((D_MODEL, BATCH), lambda n: (0, 0)),
                  pl.BlockSpec((D_MODEL, tn), lambda n: (0, n)),
                  pl.BlockSpec((1, tn), lambda n: (0, n))],
        out_specs=pl.BlockSpec((BATCH, tn), lambda n: (0, n)),
        compiler_params=_cparams(("parallel",)),
        name="ada_mod",
    )(c_t, w_ada, b_ada)


def _normmod(x, g, sc, sh):
    y = x * lax.rsqrt(jnp.mean(x * x, axis=-1, keepdims=True) + EPS)
    return (y * g) * (1.0 + sc) + sh


def _tile4(t):
    return jnp.concatenate([t, t, t, t], axis=1)


def _rotary(y, tabs, rows, shift):
    cos_ref, sin_lo_ref, sin_hi_ref = tabs
    width = y.shape[1]
    return (y * _tile4(cos_ref[rows, :])
            + pltpu.roll(y, width - shift, 1) * _tile4(sin_lo_ref[rows, :])
            + pltpu.roll(y, shift, 1) * _tile4(sin_hi_ref[rows, :]))


def _head_rmsnorm(y, g):
    outs = []
    for h in range(y.shape[1] // HEAD_DIM):
        yh = y[:, h * HEAD_DIM:(h + 1) * HEAD_DIM]
        outs.append(yh * lax.rsqrt(jnp.mean(yh * yh, axis=-1, keepdims=True) + EPS) * g)
    return jnp.concatenate(outs, axis=1)


def _proj_kernel(x_ref, g_ref, sc_ref, sh_ref, w_ref, ca_ref, sal_ref, sah_ref, cb_ref, sbl_ref, sbh_ref,
                 gain_ref, scale_ref, bg_ref, perm_ref, o_ref, gate_ref, h_ref, hp_ref):
    n = pl.program_id(1)
    is_a = n < A_BLKS
    part = n % 3
    tabs_a = (ca_ref, sal_ref, sah_ref)
    tabs_b = (cb_ref, sbl_ref, sbh_ref)
    sub_rows = PERM_BLOCK
    n_sub = h_ref.shape[0] // sub_rows

    @pl.when(n == 0)
    def _():
        def norm_rows(r, _):
            rows = pl.ds(pl.multiple_of(r * sub_rows, sub_rows), sub_rows)
            h = _normmod(x_ref[rows, :], g_ref[...], sc_ref[...], sh_ref[...]).astype(BF16)
            h_ref[rows, :] = h
            hp_ref[rows, :] = h
            return 0

        lax.fori_loop(0, n_sub, norm_rows, 0)

    @pl.when(is_a & (part == 0) & (n > 0))
    def _():
        for r in range(n_sub):
            rows = pl.ds(r * sub_rows, sub_rows)
            hp_ref[rows, :] = jnp.dot(perm_ref[...], h_ref[rows, :], preferred_element_type=F32).astype(BF16)

    def run(epilogue, dst_ref=o_ref, src_ref=h_ref):
        w = w_ref[...].astype(BF16)
        for r in range(n_sub):
            rows = pl.ds(r * sub_rows, sub_rows)
            acc = jnp.dot(src_ref[rows, :], w, preferred_element_type=F32)
            dst_ref[rows, :] = epilogue(acc, rows).astype(dst_ref.dtype)

    @pl.when(is_a & (part < 2))
    def _():
        run(lambda acc, rows: _rotary(acc, tabs_a, rows, ROPE_DIMS // 2) * scale_ref[...], src_ref=hp_ref)

    @pl.when(is_a & (part == 2))
    def _():
        run(lambda acc, rows: acc, src_ref=hp_ref)

    @pl.when((n >= BQ_BLK0) & (n <= BK_BLK))
    def _():
        run(lambda acc, rows: _rotary(_head_rmsnorm(acc, gain_ref[...]), tabs_b, rows, AXIAL_DIMS // 2)
            * scale_ref[...])

    @pl.when(n == BV_BLK)
    def _():
        run(lambda acc, rows: acc)

    @pl.when(n >= GATE_BLK0)
    def _():
        run(lambda acc, rows: jax.nn.sigmoid(acc + bg_ref[...]), gate_ref)


def _residue_source(dilation):
    per_res = PERM_BLOCK // dilation
    row = jnp.arange(PERM_BLOCK)
    return (row % per_res) * dilation + row // per_res


def _residue_perm(dilation):
    return (_residue_source(dilation)[:, None] == jnp.arange(PERM_BLOCK)[None, :]).astype(BF16)


def _projection(x2, g_attn, mod4, w_in, tabs_a, tabs_b, qg, kg, b_gate, perms):
    tm = 1024
    per_b = SEQ // tm

    def mod_spec(j):
        return pl.BlockSpec((None, None, 1, D_MODEL), lambda m, n: (m // per_b, j, 0, 0))

    tab_spec = pl.BlockSpec((tm, LANES), lambda m, n: (m % per_b, 0))
    n_gate_blks = GATE_COLS // COL_BLK
    is_query = [(blk < A_BLKS and blk % 3 == 0) or BQ_BLK0 <= blk < BK_BLK for blk in range(N_COL_BLKS)]
    col_scale = jnp.broadcast_to(jnp.where(jnp.array(is_query), ATTN_SCALE * LOG2E, 1.0).astype(F32)[:, None, None],
                                 (N_COL_BLKS, 1, COL_BLK))
    ones = jnp.ones_like(qg)
    col_gain = jnp.stack([qg if BQ_BLK0 <= blk < BK_BLK else kg if blk == BK_BLK else ones
                          for blk in range(N_COL_BLKS)])
    def residue_major(t, dilation):
        per_res = PERM_BLOCK // dilation
        t4 = t.reshape(SEQ // PERM_BLOCK, per_res, dilation, LANES)
        return jnp.swapaxes(t4, 1, 2).reshape(SEQ, LANES)

    tabs_a = [jnp.stack([residue_major(t, dilation) for _, dilation in A_PATTERNS]) for t in tabs_a]
    tab_a_spec = pl.BlockSpec((None, tm, LANES),
                              lambda m, n: (jnp.clip(n // 3, 0, A_N_GROUPS - 1), m % per_b, 0))

    def gate_blk(n):
        return jnp.clip(n - GATE_BLK0, 0, n_gate_blks - 1)

    return pl.pallas_call(
        _proj_kernel,
        out_shape=(jax.ShapeDtypeStruct((N_TOK, QKV_COLS), BF16),
                   jax.ShapeDtypeStruct((N_TOK, GATE_COLS), BF16)),
        grid=(N_TOK // tm, N_COL_BLKS),
        in_specs=[pl.BlockSpec((tm, D_MODEL), lambda m, n: (m, 0)),
                  pl.BlockSpec((1, D_MODEL), lambda m, n: (0, 0)),
                  mod_spec(1), mod_spec(0),
                  pl.BlockSpec((D_MODEL, COL_BLK), lambda m, n: (0, n)),
                  tab_a_spec, tab_a_spec, tab_a_spec, tab_spec, tab_spec, tab_spec,
                  pl.BlockSpec((None, 1, HEAD_DIM), lambda m, n: (n, 0, 0)),
                  pl.BlockSpec((None, 1, COL_BLK), lambda m, n: (n, 0, 0)),
                  pl.BlockSpec((1, COL_BLK), lambda m, n: (0, gate_blk(n))),
                  pl.BlockSpec((None, PERM_BLOCK, PERM_BLOCK),
                               lambda m, n: (jnp.clip(n // 3, 0, A_N_GROUPS - 1), 0, 0))],
        out_specs=(pl.BlockSpec((tm, COL_BLK), lambda m, n: (m, jnp.minimum(n, GATE_BLK0 - 1))),
                   pl.BlockSpec((tm, COL_BLK), lambda m, n: (m, gate_blk(n)))),
        scratch_shapes=[pltpu.VMEM((tm, D_MODEL), BF16), pltpu.VMEM((tm, D_MODEL), BF16)],
        compiler_params=_cparams(("parallel", "arbitrary")),
        name="in_projection",
    )(x2, g_attn, mod4, mod4, w_in, *tabs_a, *tabs_b, col_gain, col_scale, b_gate, perms)


def _attn_res_kernel(q_ref, k_ref, v_ref, o_ref, lse_ref, *, half_w):
    n_blk, per_res, _ = q_ref.shape
    length = n_blk * per_res
    q = q_ref[...].reshape(length, A_WIDTH)
    k = k_ref[...].reshape(length, A_WIDTH)
    v = v_ref[...].reshape(length, A_WIDTH)
    tq = 2 * half_w
    win = 4 * half_w
    lane = lax.broadcasted_iota(jnp.int32, (tq, LANES), 1)
    row_minus_col = lax.broadcasted_iota(jnp.int32, (tq, win), 0) - lax.broadcasted_iota(jnp.int32, (tq, win), 1)
    o_blocks, lse_blocks = [], []
    for qb in range(length // tq):
        start = min(max(qb * tq - half_w, 0), length - win)
        valid = jnp.abs(row_minus_col + (qb * tq - start)) <= half_w
        lse_tile = jnp.zeros((tq, LANES), F32)
        outs = []
        for h in range(A_HEADS):
            sl = slice(h * HEAD_DIM, (h + 1) * HEAD_DIM)
            s = lax.dot_general(q[qb * tq:(qb + 1) * tq, sl], k[start:start + win, sl],
                                (((1,), (1,)), ((), ())), preferred_element_type=F32)
            s = jnp.where(valid, s, NEG)
            m = jnp.max(s, axis=-1, keepdims=True)
            p = jnp.exp2(s - m)
            l = jnp.sum(p, axis=-1, keepdims=True)
            o = jnp.dot(p.astype(BF16), v[start:start + win, sl], preferred_element_type=F32)
            outs.append(o / l)
            lse_tile = jnp.where(lane == h, m * LN2 + jnp.log(l), lse_tile)
        o_blocks.append(jnp.concatenate(outs, axis=1).astype(o_ref.dtype))
        lse_blocks.append(lse_tile)
    o_ref[...] = jnp.concatenate(o_blocks, axis=0).reshape(n_blk, per_res, A_WIDTH)
    lse_ref[...] = jnp.concatenate(lse_blocks, axis=0).reshape(n_blk, per_res, LANES)


def _attn_a_dilated(proj3, group, window, dilation):
    per_res = PERM_BLOCK // dilation
    n_blk = SEQ // PERM_BLOCK
    half_w = (window // 2) // dilation
    view = proj3.reshape(BATCH, n_blk, dilation, per_res, QKV_COLS)

    def spec(width, blk):
        return pl.BlockSpec((None, n_blk, None, per_res, width), lambda b, r: (b, 0, r, 0, blk))

    o, lse = pl.pallas_call(
        functools.partial(_attn_res_kernel, half_w=half_w),
        out_shape=(jax.ShapeDtypeStruct((BATCH, n_blk, dilation, per_res, A_WIDTH), BF16),
                   jax.ShapeDtypeStruct((BATCH, n_blk, dilation, per_res, LANES), F32)),
        grid=(BATCH, dilation),
        in_specs=[spec(COL_BLK, 3 * group), spec(COL_BLK, 3 * group + 1), spec(COL_BLK, 3 * group + 2)],
        out_specs=(spec(A_WIDTH, 0), spec(LANES, 0)),
        compiler_params=_cparams(("parallel", "parallel")),
        name=f"dilated_attention_g{group}",
    )(view, view, view)
    return o.reshape(N_TOK, A_WIDTH), lse.reshape(N_TOK, LANES)


def _attn_b_kernel(q_ref, k_ref, v_ref, o_ref, vx_ref, acc_ref, *, tk):
    tq = q_ref.shape[0]
    n_chunks = SEQ // tk

    @pl.when(pl.program_id(2) == 0)
    def _():
        vx_ref[:, 0:HEAD_DIM] = v_ref[...]
        vx_ref[:, HEAD_DIM:2 * HEAD_DIM] = jnp.ones((SEQ, HEAD_DIM), BF16)

    q = q_ref[...]
    qs = jnp.concatenate([q[:, g * HEAD_DIM:(g + 1) * HEAD_DIM] for g in range(B_GROUP)], axis=0)
    acc_ref[...] = jnp.zeros_like(acc_ref)
    m = jnp.full((B_GROUP * tq, 1), -jnp.inf, F32)
    for c in range(n_chunks):
        keys = slice(c * tk, (c + 1) * tk)
        s = lax.dot_general(qs, k_ref[keys, :], (((1,), (1,)), ((), ())), preferred_element_type=F32)
        m_new = jnp.maximum(m, jnp.max(s, axis=-1, keepdims=True))
        p = jnp.exp2(s - m_new).astype(BF16)
        acc_ref[...] = jnp.exp2(m - m_new) * acc_ref[...] + jnp.dot(p, vx_ref[keys, :],
                                                                    preferred_element_type=F32)
        m = m_new
    o = acc_ref[:, 0:HEAD_DIM] / acc_ref[:, HEAD_DIM:2 * HEAD_DIM]
    o_ref[...] = jnp.concatenate([o[g * tq:(g + 1) * tq] for g in range(B_GROUP)], axis=1).astype(o_ref.dtype)


def _attn_b(proj3):
    tq = 128
    tk = 256
    kcol0 = BK_BLK * COL_BLK // HEAD_DIM
    vcol0 = BV_BLK * COL_BLK // HEAD_DIM
    o = pl.pallas_call(
        functools.partial(_attn_b_kernel, tk=tk),
        out_shape=jax.ShapeDtypeStruct((BATCH, SEQ, B_Q_WIDTH), BF16),
        grid=(BATCH, B_KV_HEADS, SEQ // tq),
        in_specs=[pl.BlockSpec((None, tq, COL_BLK), lambda b, h, i: (b, i, BQ_BLK0 + h)),
                  pl.BlockSpec((None, SEQ, HEAD_DIM), lambda b, h, i: (b, 0, kcol0 + h)),
                  pl.BlockSpec((None, SEQ, HEAD_DIM), lambda b, h, i: (b, 0, vcol0 + h))],
        out_specs=pl.BlockSpec((None, tq, COL_BLK), lambda b, h, i: (b, i, h)),
        scratch_shapes=[pltpu.VMEM((SEQ, 2 * HEAD_DIM), BF16),
                        pltpu.VMEM((B_GROUP * tq, 2 * HEAD_DIM), F32)],
        compiler_params=_cparams(("parallel", "parallel", "arbitrary")),
        name="gqa_attention",
    )(proj3, proj3, proj3)
    return o.reshape(N_TOK, B_Q_WIDTH)


def _merge_kernel(o0_ref, o1_ref, o2_ref, l0_ref, l1_ref, l2_ref, yb_ref, wa_ref, wb_ref, ga_ref, gb_ref,
                  unperm_ref, out_ref):
    sub_rows = PERM_BLOCK
    for r in range(out_ref.shape[0] // sub_rows):
        rows = pl.ds(r * sub_rows, sub_rows)
        outs = [o0_ref[rows, :].astype(F32)]
        lses = [l0_ref[rows, :]]
        for g, (o_ref, l_ref) in enumerate(((o1_ref, l1_ref), (o2_ref, l2_ref))):
            outs.append(jnp.dot(unperm_ref[g], o_ref[rows, :], preferred_element_type=F32))
            lses.append(jnp.dot(unperm_ref[g].astype(F32), l_ref[rows, :], preferred_element_type=F32,
                                precision=lax.Precision.HIGHEST))
        cols = []
        for h in range(A_HEADS):
            lh = [jnp.broadcast_to(l[:, h:h + 1], (sub_rows, HEAD_DIM)) for l in lses]
            mx = jnp.maximum(jnp.maximum(lh[0], lh[1]), lh[2])
            e = [jnp.exp(v - mx) for v in lh]
            den = e[0] + e[1] + e[2]
            sl = slice(h * HEAD_DIM, (h + 1) * HEAD_DIM)
            cols.append(sum((e[g] / den) * outs[g][:, sl] for g in range(A_N_GROUPS)))
        mix_a = jnp.concatenate(cols, axis=1).astype(BF16)
        mix_b = yb_ref[rows, :]
        for cb in range(D_MODEL // COL_BLK):
            cs = pl.ds(cb * COL_BLK, COL_BLK)
            ya = jnp.dot(mix_a, wa_ref[:, cs], preferred_element_type=F32)
            yb = jnp.dot(mix_b, wb_ref[:, cs], preferred_element_type=F32)
            out_ref[rows, cs] = (ga_ref[rows, cs].astype(F32) * ya
                                 + gb_ref[rows, cs].astype(F32) * yb).astype(out_ref.dtype)


def _merge(o_groups, lse_groups, mix_b, gates, w_a_up, w_b_up, unperms):
    tm = 512
    o_spec = pl.BlockSpec((tm, A_WIDTH), lambda m: (m, 0))
    l_spec = pl.BlockSpec((tm, LANES), lambda m: (m, 0))
    return pl.pallas_call(
        _merge_kernel,
        out_shape=jax.ShapeDtypeStruct((N_TOK, D_MODEL), BF16),
        grid=(N_TOK // tm,),
        in_specs=[o_spec, o_spec, o_spec, l_spec, l_spec, l_spec,
                  pl.BlockSpec((tm, B_Q_WIDTH), lambda m: (m, 0)),
                  pl.BlockSpec((A_WIDTH, D_MODEL), lambda m: (0, 0)),
                  pl.BlockSpec((B_Q_WIDTH, D_MODEL), lambda m: (0, 0)),
                  pl.BlockSpec((tm, D_MODEL), lambda m: (m, 0)),
                  pl.BlockSpec((tm, D_MODEL), lambda m: (m, 1)),
                  pl.BlockSpec((A_N_GROUPS - 1, PERM_BLOCK, PERM_BLOCK), lambda m: (0, 0, 0))],
        out_specs=pl.BlockSpec((tm, D_MODEL), lambda m: (m, 0)),
        compiler_params=_cparams(("parallel",)),
        name="branch_merge",
    )(*o_groups, *lse_groups, mix_b, w_a_up, w_b_up, gates, gates, unperms)


def _outproj_kernel(mg_ref, w_ref, x_ref, gt_ref, g_ref, sc_ref, sh_ref, wr_ref, x1_ref, h2_ref, lg_ref):
    sub_rows = 256
    wr = wr_ref[...]
    wr_hi = wr.astype(BF16)
    wr_lo = (wr - wr_hi.astype(F32)).astype(BF16)
    for r in range(mg_ref.shape[0] // sub_rows):
        rows = pl.ds(r * sub_rows, sub_rows)
        y = jnp.dot(mg_ref[rows, :], w_ref[...], preferred_element_type=F32)
        x1 = x_ref[rows, :] + gt_ref[...] * y
        x1_ref[rows, :] = x1
        h2 = _normmod(x1, g_ref[...], sc_ref[...], sh_ref[...])
        h2_hi = h2.astype(BF16)
        h2_ref[rows, :] = h2_hi
        h2_lo = (h2 - h2_hi.astype(F32)).astype(BF16)
        nt = (((1,), (1,)), ((), ()))
        lg_ref[:, rows] = (lax.dot_general(wr_hi, h2_hi, nt, preferred_element_type=F32)
                           + lax.dot_general(wr_hi, h2_lo, nt, preferred_element_type=F32)
                           + lax.dot_general(wr_lo, h2_hi, nt, preferred_element_type=F32))


def _out_projection(merged, w_out, x2, mod4, g_ffn, w_router_t):
    tm = 512
    per_b = SEQ // tm

    def mod_spec(j):
        return pl.BlockSpec((None, None, 1, D_MODEL), lambda m: (m // per_b, j, 0, 0))

    return pl.pallas_call(
        _outproj_kernel,
        out_shape=(jax.ShapeDtypeStruct((N_TOK, D_MODEL), F32),
                   jax.ShapeDtypeStruct((N_TOK, D_MODEL), BF16),
                   jax.ShapeDtypeStruct((N_EXPERTS, N_TOK), F32)),
        grid=(N_TOK // tm,),
        in_specs=[pl.BlockSpec((tm, D_MODEL), lambda m: (m, 0)),
                  pl.BlockSpec((D_MODEL, D_MODEL), lambda m: (0, 0)),
                  pl.BlockSpec((tm, D_MODEL), lambda m: (m, 0)),
                  mod_spec(2),
                  pl.BlockSpec((1, D_MODEL), lambda m: (0, 0)),
                  mod_spec(4), mod_spec(3),
                  pl.BlockSpec((N_EXPERTS, D_MODEL), lambda m: (0, 0))],
        out_specs=(pl.BlockSpec((tm, D_MODEL), lambda m: (m, 0)),
                   pl.BlockSpec((tm, D_MODEL), lambda m: (m, 0)),
                   pl.BlockSpec((N_EXPERTS, tm), lambda m: (0, m))),
        compiler_params=_cparams(("parallel",)),
        name="out_projection",
    )(merged, w_out, x2, mod4, g_ffn, mod4, mod4, w_router_t)


def _route_kernel(lg_ref, eb_ref, csm_ref, cnt_ref, gx_ref):
    tt = lg_ref.shape[1]
    scores = jax.nn.sigmoid(lg_ref[...])
    sel = scores + eb_ref[...]
    neg_inf = -jnp.inf
    sel_g = [sel[PER_GROUP * g:PER_GROUP * (g + 1), :] for g in range(N_EXPERT_GROUPS)]
    grp = []
    for v in sel_g:
        m1 = jnp.max(v, axis=0, keepdims=True)
        is1 = v == m1
        n1 = jnp.sum(jnp.where(is1, 1.0, 0.0), axis=0, keepdims=True)
        rest = jnp.max(jnp.where(is1, neg_inf, v), axis=0, keepdims=True)
        grp.append(m1 + jnp.where(n1 >= 2.0, m1, rest))
    masked = []
    for g in range(N_EXPERT_GROUPS):
        rank = jnp.zeros((1, tt), F32)
        for g2 in range(N_EXPERT_GROUPS):
            if g2 != g:
                beats = (grp[g2] >= grp[g]) if g2 < g else (grp[g2] > grp[g])
                rank = rank + jnp.where(beats, 1.0, 0.0)
        keep = jnp.broadcast_to(rank < TOPK_GROUPS, (PER_GROUP, tt))
        masked.append(jnp.where(keep, sel_g[g], neg_inf))
    sub = lax.broadcasted_iota(jnp.int32, (PER_GROUP, tt), 0)
    ranks = [jnp.zeros((PER_GROUP, tt), F32) for _ in range(N_EXPERT_GROUPS)]
    for g2 in range(N_EXPERT_GROUPS):
        for m2 in range(PER_GROUP):
            vf = jnp.broadcast_to(masked[g2][m2:m2 + 1, :], (PER_GROUP, tt))
            for g in range(N_EXPERT_GROUPS):
                if g2 < g:
                    beats = vf >= masked[g]
                elif g2 > g:
                    beats = vf > masked[g]
                else:
                    beats = (vf > masked[g]) | ((vf == masked[g]) & (sub > m2))
                ranks[g] = ranks[g] + jnp.where(beats, 1.0, 0.0)
    picked = [ranks[g] < TOP_K for g in range(N_EXPERT_GROUPS)]
    chosen = [jnp.where(picked[g], scores[PER_GROUP * g:PER_GROUP * (g + 1), :], 0.0)
              for g in range(N_EXPERT_GROUPS)]
    total = chosen[0]
    for g in range(1, N_EXPERT_GROUPS):
        total = total + chosen[g]
    denom = jnp.sum(total, axis=0, keepdims=True)
    gate = jnp.concatenate([chosen[g] / denom * ROUTED_SCALE for g in range(N_EXPERT_GROUPS)], axis=0)
    mask = jnp.concatenate([jnp.where(picked[g], 1.0, 0.0) for g in range(N_EXPERT_GROUPS)], axis=0)
    mask_bf = mask.astype(BF16)

    upper = jnp.where(lax.broadcasted_iota(jnp.int32, (tt, tt), 0) <= lax.broadcasted_iota(jnp.int32, (tt, tt), 1),
                      1.0, 0.0).astype(BF16)
    cs = jnp.dot(mask_bf, upper, preferred_element_type=F32)
    csm_ref[...] = jnp.where(mask > 0.0, cs, 0.0)
    cnt_ref[...] = lax.dot_general(jnp.ones((SUBLANES, tt), BF16), mask_bf, (((1,), (1,)), ((), ())),
                                   preferred_element_type=F32)

    hi = gate.astype(BF16).astype(F32)
    gx_ref[...] = jnp.concatenate([hi, gate - hi], axis=0).T.astype(BF16)


def _route(logits_t, e_bias_col):
    return pl.pallas_call(
        _route_kernel,
        out_shape=(jax.ShapeDtypeStruct((N_EXPERTS, N_TOK), F32),
                   jax.ShapeDtypeStruct((N_CHUNKS, SUBLANES, N_EXPERTS), F32),
                   jax.ShapeDtypeStruct((N_TOK, LANES), BF16)),
        grid=(N_CHUNKS,),
        in_specs=[pl.BlockSpec((N_EXPERTS, CHUNK), lambda c: (0, c)),
                  pl.BlockSpec((N_EXPERTS, 1), lambda c: (0, 0))],
        out_specs=(pl.BlockSpec((N_EXPERTS, CHUNK), lambda c: (0, c)),
                   pl.BlockSpec((None, SUBLANES, N_EXPERTS), lambda c: (c, 0, 0)),
                   pl.BlockSpec((CHUNK, LANES), lambda c: (c, 0))),
        compiler_params=_cparams(("parallel",)),
        name="routing",
    )(logits_t, e_bias_col)


def _plan_kernel(cnt_ref, goff_ref, tail_ref, tile_e_ref, ntile_ref, next_ref):
    def per_expert(e, carry):
        base, tbase, prev = carry

        def per_chunk(c, off):
            goff_ref[c, e] = base + off
            return off + (((cnt_ref[c, e] + (GRANULE - 1)) >> GRANULE_SHIFT) << GRANULE_SHIFT)

        tot = lax.fori_loop(0, N_CHUNKS, per_chunk, jnp.int32(0))
        region = ((tot + (TILE - 1)) >> TILE_SHIFT) << TILE_SHIFT
        tail_ref[0, e] = base + tot
        tail_ref[1, e] = (region - tot) >> GRANULE_SHIFT
        n_t = region >> TILE_SHIFT

        def per_tile(j, _):
            tile_e_ref[tbase + j] = e
            return 0

        lax.fori_loop(0, n_t, per_tile, 0)
        next_ref[e] = jnp.int32(-1)

        @pl.when((n_t > 0) & (prev >= 0))
        def _():
            next_ref[jnp.maximum(prev, 0)] = e

        return base + region, tbase + n_t, jnp.where(n_t > 0, e, prev)

    _, n_tiles, _ = lax.fori_loop(0, N_EXPERTS, per_expert, (jnp.int32(0), jnp.int32(0), jnp.int32(-1)))
    ntile_ref[0] = n_tiles
    last_e = tile_e_ref[jnp.maximum(n_tiles - 1, 0)]

    def fill(j, _):
        tile_e_ref[j] = last_e
        return 0

    lax.fori_loop(n_tiles, MAX_TILES, fill, 0)


def _plan(cnt):
    smem = pl.BlockSpec(memory_space=pltpu.SMEM)
    return pl.pallas_call(
        _plan_kernel,
        out_shape=(jax.ShapeDtypeStruct((N_CHUNKS, N_EXPERTS), jnp.int32),
                   jax.ShapeDtypeStruct((2, N_EXPERTS), jnp.int32),
                   jax.ShapeDtypeStruct((MAX_TILES,), jnp.int32),
                   jax.ShapeDtypeStruct((1,), jnp.int32),
                   jax.ShapeDtypeStruct((N_EXPERTS,), jnp.int32)),
        in_specs=[smem],
        out_specs=(smem, smem, smem, smem, smem),
        name="row_plan",
    )(cnt)


def _chunk_granules(cnt_ref, goff_ref, c, table_ref, per_granule=None):
    def per_expert(e, n_before):
        n_gran = (cnt_ref[c, e] + (GRANULE - 1)) >> GRANULE_SHIFT
        base = goff_ref[c, e]

        def per_j(j, _):
            table_ref[n_before + j] = base + j * GRANULE
            if per_granule is not None:
                per_granule(e, j, n_before + j)
            return 0

        lax.fori_loop(0, n_gran, per_j, 0)
        return n_before + n_gran

    return lax.fori_loop(0, N_EXPERTS, per_expert, jnp.int32(0))


def _dispatch_kernel(cnt_ref, goff_ref, tail_ref, x_ref, gx_ref, csm_ref, xs_hbm,
                     onehot_ref, table_ref, buf_ref, zero_ref, xcat_ref, sems, zsem):
    c = pl.program_id(0)
    sub = lax.broadcasted_iota(jnp.int32, (GRANULE, CHUNK), 0)
    xcat_ref[:, 0:D_MODEL] = x_ref[...]
    xcat_ref[:, D_MODEL:XS_WIDTH] = gx_ref[...]

    def build_granule(e, j, g):
        want = (sub + (j * GRANULE + 1)).astype(F32)
        hit = csm_ref[pl.ds(e, 1), :] == want
        onehot_ref[pl.ds(pl.multiple_of(g * GRANULE, GRANULE), GRANULE), :] = jnp.where(hit, 1.0, 0.0).astype(BF16)

    n_gran = _chunk_granules(cnt_ref, goff_ref, c, table_ref, build_granule)
    n_tiles = (n_gran + (GRAN_PER_TILE - 1)) >> GPT_SHIFT

    def clear(g, _):
        onehot_ref[pl.ds(pl.multiple_of(g * GRANULE, GRANULE), GRANULE), :] = jnp.zeros((GRANULE, CHUNK), BF16)
        return 0

    lax.fori_loop(n_gran, n_tiles * GRAN_PER_TILE, clear, 0)

    def granule_copy(slot, i, row):
        return pltpu.make_async_copy(buf_ref.at[slot, pl.ds(pl.multiple_of(i * GRANULE, GRANULE), GRANULE), :],
                                     xs_hbm.at[pl.ds(pl.multiple_of(row, GRANULE), GRANULE), :],
                                     sems.at[slot])

    def tile_granules(k):
        return jnp.minimum(GRAN_PER_TILE, n_gran - k * GRAN_PER_TILE)

    def wait_tile(k):
        slot = k % N_SLOTS

        def w(i, _):
            granule_copy(slot, i, 0).wait()
            return 0

        lax.fori_loop(0, tile_granules(k), w, 0)

    def per_tile(k, _):
        slot = k % N_SLOTS

        @pl.when(k >= N_SLOTS)
        def _():
            wait_tile(k - N_SLOTS)

        oh = onehot_ref[pl.ds(pl.multiple_of(k * TILE, TILE), TILE), :]
        buf_ref[slot] = jnp.dot(oh, xcat_ref[...], preferred_element_type=F32).astype(BF16)

        def s(i, _):
            granule_copy(slot, i, table_ref[k * GRAN_PER_TILE + i]).start()
            return 0

        lax.fori_loop(0, tile_granules(k), s, 0)
        return 0

    lax.fori_loop(0, n_tiles, per_tile, 0)

    def drain(k, _):
        wait_tile(k)
        return 0

    lax.fori_loop(jnp.maximum(n_tiles - N_SLOTS, 0), n_tiles, drain, 0)

    @pl.when(c == N_CHUNKS - 1)
    def _():
        zero_ref[...] = jnp.zeros_like(zero_ref)

        def tail_copy(row):
            return pltpu.make_async_copy(zero_ref, xs_hbm.at[pl.ds(pl.multiple_of(row, GRANULE), GRANULE), :], zsem)

        def start_e(e, _):
            def st(j, _):
                tail_copy(tail_ref[0, e] + j * GRANULE).start()
                return 0
            lax.fori_loop(0, tail_ref[1, e], st, 0)
            return 0

        def wait_e(e, _):
            def wt(j, _):
                tail_copy(0).wait()
                return 0
            lax.fori_loop(0, tail_ref[1, e], wt, 0)
            return 0

        lax.fori_loop(0, N_EXPERTS, start_e, 0)
        lax.fori_loop(0, N_EXPERTS, wait_e, 0)


def _dispatch(cnt, goff, tail, h2, gx, csm):
    grid_spec = pltpu.PrefetchScalarGridSpec(
        num_scalar_prefetch=3,
        grid=(N_CHUNKS,),
        in_specs=[pl.BlockSpec((CHUNK, D_MODEL), lambda c, *_: (c, 0)),
                  pl.BlockSpec((CHUNK, LANES), lambda c, *_: (c, 0)),
                  pl.BlockSpec((N_EXPERTS, CHUNK), lambda c, *_: (0, c))],
        out_specs=pl.BlockSpec(memory_space=pl.ANY),
        scratch_shapes=[pltpu.VMEM((CHUNK_ROWS, CHUNK), BF16),
                        pltpu.SMEM((CHUNK_ROWS // GRANULE,), jnp.int32),
                        pltpu.VMEM((N_SLOTS, TILE, XS_WIDTH), BF16),
                        pltpu.VMEM((GRANULE, XS_WIDTH), BF16),
                        pltpu.VMEM((CHUNK, XS_WIDTH), BF16),
                        pltpu.SemaphoreType.DMA((N_SLOTS,)),
                        pltpu.SemaphoreType.DMA(())])
    return pl.pallas_call(
        _dispatch_kernel,
        out_shape=jax.ShapeDtypeStruct((MAX_ROWS, XS_WIDTH), BF16),
        grid_spec=grid_spec,
        compiler_params=_cparams(("arbitrary",)),
        name="moe_dispatch",
    )(cnt, goff, tail, h2, gx, csm)


def _routed_kernel(tile_e_ref, ntile_ref, next_ref, xs_hbm, w1_hbm, w3_hbm, w2_hbm, ys_hbm,
                   xbuf_ref, ybuf_ref, w1f_ref, w3f_ref, w2f_ref, w1s_ref, w3s_ref, w2s_ref, xsem, ysem, wsem):
    n_tiles = ntile_ref[0]

    def x_copy(t, slot):
        return pltpu.make_async_copy(xs_hbm.at[pl.ds(pl.multiple_of(t * TILE, TILE), TILE), :],
                                     xbuf_ref.at[slot], xsem.at[slot])

    def y_copy(t, slot):
        return pltpu.make_async_copy(ybuf_ref.at[slot],
                                     ys_hbm.at[pl.ds(pl.multiple_of(t * TILE, TILE), TILE), :], ysem.at[slot])

    def fetch(expert, slot):
        return (pltpu.make_async_copy(w1_hbm.at[expert], w1f_ref.at[slot], wsem.at[slot]),
                pltpu.make_async_copy(w3_hbm.at[expert], w3f_ref.at[slot], wsem.at[slot]),
                pltpu.make_async_copy(w2_hbm.at[expert], w2f_ref.at[slot], wsem.at[slot]))

    @pl.when(n_tiles > 0)
    def _():
        x_copy(0, 0).start()
        for cp in fetch(tile_e_ref[0], 0):
            cp.start()

    def per_tile(t, n_started):
        slot = t & 1
        e = tile_e_ref[t]
        x_copy(t, slot).wait()

        @pl.when(t + 1 < n_tiles)
        def _():
            x_copy(t + 1, 1 - slot).start()

        first = (t == 0) | (e != tile_e_ref[jnp.maximum(t - 1, 0)])

        @pl.when(first)
        def _():
            wslot = n_started & 1
            nxt = next_ref[e]

            @pl.when(nxt >= 0)
            def _():
                for cp in fetch(jnp.maximum(nxt, 0), 1 - wslot):
                    cp.start(priority=WEIGHT_DMA_PRIORITY)

            for cp in fetch(e, wslot):
                cp.wait()
            w1s_ref[...] = w1f_ref[wslot].astype(BF16)
            w3s_ref[...] = w3f_ref[wslot].astype(BF16)
            w2s_ref[...] = w2f_ref[wslot].astype(BF16)

        @pl.when(t >= 2)
        def _():
            y_copy(t - 2, slot).wait()

        x = xbuf_ref[slot, :, 0:D_MODEL]
        gx = xbuf_ref[slot, :, D_MODEL:XS_WIDTH].astype(F32)
        lane = lax.broadcasted_iota(jnp.int32, gx.shape, 1)
        gate = jnp.sum(jnp.where((lane == e) | (lane == e + N_EXPERTS), gx, 0.0), axis=1, keepdims=True)
        a = jnp.dot(x, w1s_ref[...], preferred_element_type=F32)
        u = jnp.dot(x, w3s_ref[...], preferred_element_type=F32)
        hid = (a * jax.nn.sigmoid(a)) * u * gate
        ybuf_ref[slot] = jnp.dot(hid.astype(BF16), w2s_ref[...], preferred_element_type=F32).astype(ybuf_ref.dtype)
        y_copy(t, slot).start()
        return n_started + jnp.where(first, 1, 0)

    lax.fori_loop(0, n_tiles, per_tile, jnp.int32(0))

    @pl.when(n_tiles >= 2)
    def _():
        y_copy(n_tiles - 2, n_tiles & 1).wait()

    @pl.when(n_tiles >= 1)
    def _():
        y_copy(n_tiles - 1, (n_tiles - 1) & 1).wait()


def _routed_experts(tile_e, ntile, next_e, xs, w1, w3, w2):
    hbm = pl.BlockSpec(memory_space=pl.ANY)
    grid_spec = pltpu.PrefetchScalarGridSpec(
        num_scalar_prefetch=3,
        grid=(1,),
        in_specs=[hbm, hbm, hbm, hbm],
        out_specs=hbm,
        scratch_shapes=[pltpu.VMEM((2, TILE, XS_WIDTH), BF16),
                        pltpu.VMEM((2, TILE, D_MODEL), BF16),
                        pltpu.VMEM((2, D_MODEL, D_EXPERT), F32),
                        pltpu.VMEM((2, D_MODEL, D_EXPERT), F32),
                        pltpu.VMEM((2, D_EXPERT, D_MODEL), F32),
                        pltpu.VMEM((D_MODEL, D_EXPERT), BF16),
                        pltpu.VMEM((D_MODEL, D_EXPERT), BF16),
                        pltpu.VMEM((D_EXPERT, D_MODEL), BF16),
                        pltpu.SemaphoreType.DMA((2,)),
                        pltpu.SemaphoreType.DMA((2,)),
                        pltpu.SemaphoreType.DMA((2,))])
    return pl.pallas_call(
        _routed_kernel,
        out_shape=jax.ShapeDtypeStruct((MAX_ROWS, D_MODEL), BF16),
        grid_spec=grid_spec,
        compiler_params=_cparams(("arbitrary",)),
        name="routed_experts",
    )(tile_e, ntile, next_e, xs, w1, w3, w2)


def _combine_kernel(cnt_ref, goff_ref, csm_ref, x1_ref, sh_ref, gt_ref, gf_ref, ys_hbm, o_ref,
                    table_ref, buf_ref, acc_ref, pick_ref, sems):
    c = pl.program_id(0)
    sub = lax.broadcasted_iota(jnp.int32, (GRANULE, CHUNK), 0)

    def build_granule(e, j, g):
        want = (sub + (j * GRANULE + 1)).astype(F32)
        hit = csm_ref[pl.ds(e, 1), :] == want
        pick_ref[pl.ds(pl.multiple_of(g * GRANULE, GRANULE), GRANULE), :] = jnp.where(hit, 1.0, 0.0).astype(BF16)

    n_gran = _chunk_granules(cnt_ref, goff_ref, c, table_ref, build_granule)
    n_tiles = (n_gran + (COMBINE_GPT - 1)) >> COMBINE_GPT_SHIFT

    def clear_pick(g, _):
        pick_ref[pl.ds(pl.multiple_of(g * GRANULE, GRANULE), GRANULE), :] = jnp.zeros((GRANULE, CHUNK), BF16)
        return 0

    lax.fori_loop(n_gran, n_tiles * COMBINE_GPT, clear_pick, 0)

    def granule_copy(slot, i, row):
        return pltpu.make_async_copy(ys_hbm.at[pl.ds(pl.multiple_of(row, GRANULE), GRANULE), :],
                                     buf_ref.at[slot, pl.ds(pl.multiple_of(i * GRANULE, GRANULE), GRANULE), :],
                                     sems.at[slot])

    def tile_granules(k):
        return jnp.minimum(COMBINE_GPT, n_gran - k * COMBINE_GPT)

    def start_tile(k):
        slot = k % COMBINE_SLOTS

        def s(i, _):
            granule_copy(slot, i, table_ref[k * COMBINE_GPT + i]).start()
            return 0

        lax.fori_loop(0, tile_granules(k), s, 0)

    def wait_tile(k):
        slot = k % COMBINE_SLOTS

        def w(i, _):
            granule_copy(slot, i, 0).wait()
            return 0

        lax.fori_loop(0, tile_granules(k), w, 0)

    for k0 in range(COMBINE_SLOTS - 1):
        @pl.when(k0 < n_tiles)
        def _():
            start_tile(k0)

    acc_ref[...] = jnp.zeros_like(acc_ref)

    def per_tile(k, _):
        @pl.when(k + (COMBINE_SLOTS - 1) < n_tiles)
        def _():
            start_tile(k + (COMBINE_SLOTS - 1))

        wait_tile(k)
        slot = k % COMBINE_SLOTS

        def clear(i, _):
            buf_ref[slot, pl.ds(pl.multiple_of(i * GRANULE, GRANULE), GRANULE), :] = jnp.zeros((GRANULE, D_MODEL), BF16)
            return 0

        lax.fori_loop(tile_granules(k), COMBINE_GPT, clear, 0)

        pick = pick_ref[pl.ds(pl.multiple_of(k * COMBINE_TILE, COMBINE_TILE), COMBINE_TILE), :]
        acc_ref[...] += lax.dot_general(pick, buf_ref[slot], (((0,), (0,)), ((), ())),
                                        preferred_element_type=F32)
        return 0

    lax.fori_loop(0, n_tiles, per_tile, 0)

    x = x1_ref[...] + gt_ref[...] * (acc_ref[...] + sh_ref[...])
    o_ref[...] = x * lax.rsqrt(jnp.mean(x * x, axis=-1, keepdims=True) + EPS) * gf_ref[...]


def _combine(cnt, goff, csm, x1, shared, mod4, g_final, ys):
    per_b = SEQ // CHUNK
    row = pl.BlockSpec((CHUNK, D_MODEL), lambda c, *_: (c, 0))
    grid_spec = pltpu.PrefetchScalarGridSpec(
        num_scalar_prefetch=2,
        grid=(N_CHUNKS,),
        in_specs=[pl.BlockSpec((N_EXPERTS, CHUNK), lambda c, *_: (0, c)),
                  row, row,
                  pl.BlockSpec((None, None, 1, D_MODEL), lambda c, *_: (c // per_b, 5, 0, 0)),
                  pl.BlockSpec((1, D_MODEL), lambda c, *_: (0, 0)),
                  pl.BlockSpec(memory_space=pl.ANY)],
        out_specs=row,
        scratch_shapes=[pltpu.SMEM((CHUNK_ROWS // GRANULE,), jnp.int32),
                        pltpu.VMEM((COMBINE_SLOTS, COMBINE_TILE, D_MODEL), BF16),
                        pltpu.VMEM((CHUNK, D_MODEL), F32),
                        pltpu.VMEM((CHUNK_ROWS, CHUNK), BF16),
                        pltpu.SemaphoreType.DMA((COMBINE_SLOTS,))])
    return pl.pallas_call(
        _combine_kernel,
        out_shape=jax.ShapeDtypeStruct((N_TOK, D_MODEL), F32),
        grid_spec=grid_spec,
        compiler_params=_cparams(("arbitrary",)),
        name="moe_combine",
    )(cnt, goff, csm, x1, shared, mod4, g_final, ys)


def _shared_kernel(x_ref, w1_ref, w3_ref, w2_ref, o_ref):
    x = x_ref[...]
    a = jnp.dot(x, w1_ref[...], preferred_element_type=F32)
    u = jnp.dot(x, w3_ref[...], preferred_element_type=F32)
    hid = (a * jax.nn.sigmoid(a)) * u
    o_ref[...] = jnp.dot(hid.astype(BF16), w2_ref[...], preferred_element_type=F32).astype(o_ref.dtype)


def _shared_expert(h2, w1, w3, w2):
    tm = 1024
    return pl.pallas_call(
        _shared_kernel,
        out_shape=jax.ShapeDtypeStruct((N_TOK, D_MODEL), F32),
        grid=(N_TOK // tm,),
        in_specs=[pl.BlockSpec((tm, D_MODEL), lambda m: (m, 0)),
                  pl.BlockSpec((D_MODEL, D_SHARED), lambda m: (0, 0)),
                  pl.BlockSpec((D_MODEL, D_SHARED), lambda m: (0, 0)),
                  pl.BlockSpec((D_SHARED, D_MODEL), lambda m: (0, 0))],
        out_specs=pl.BlockSpec((tm, D_MODEL), lambda m: (m, 0)),
        compiler_params=_cparams(("parallel",)),
        name="shared_expert",
    )(h2, w1, w3, w2)


def _rope_angles(pos, dims, theta):
    inv = jnp.power(jnp.float32(theta), -jnp.arange(0, dims, 2, dtype=jnp.float32) / dims)
    return pos.astype(jnp.float32)[:, None] * inv[None, :]


def _rotary_tables(angle_blocks):
    cos_parts, lo_parts, hi_parts = [], [], []
    used = 0
    for ang in angle_blocks:
        c, s = jnp.cos(ang), jnp.sin(ang)
        z = jnp.zeros_like(s)
        cos_parts += [c, c]
        lo_parts += [-s, z]
        hi_parts += [z, s]
        used += 2 * ang.shape[1]
    rest = HEAD_DIM - used
    if rest:
        cos_parts.append(jnp.ones((SEQ, rest), F32))
        lo_parts.append(jnp.zeros((SEQ, rest), F32))
        hi_parts.append(jnp.zeros((SEQ, rest), F32))
    return (jnp.concatenate(cos_parts, axis=1), jnp.concatenate(lo_parts, axis=1),
            jnp.concatenate(hi_parts, axis=1))


def kernel(x, c, w_ada, b_ada, g_attn, w_in, b_gate, q_norm_g, k_norm_g, w_a_up, w_b_up, w_out,
           g_ffn, w_router, e_bias, w1, w3, w2, ws1, ws3, ws2, g_final):
    l = 0
    x2 = x.reshape(N_TOK, D_MODEL)
    pos = jnp.arange(SEQ)
    tabs_a = _rotary_tables([_rope_angles(pos, ROPE_DIMS, ROPE_THETA)])
    tabs_b = _rotary_tables([_rope_angles(pos // GRID_W, AXIAL_DIMS, AXIAL_THETA),
                             _rope_angles(pos % GRID_W, AXIAL_DIMS, AXIAL_THETA)])
    tabs_a, tabs_b = lax.optimization_barrier((tabs_a, tabs_b))

    mod = _ada(c.T, w_ada[l], b_ada[l].reshape(1, -1))
    mod4 = mod.reshape(BATCH, N_MOD, 1, D_MODEL)

    perms = jnp.stack([_residue_perm(dilation) for _, dilation in A_PATTERNS])
    proj, gates = _projection(x2, g_attn[l].reshape(1, -1), mod4, w_in[l], tabs_a, tabs_b,
                              q_norm_g[l].reshape(1, -1), k_norm_g[l].reshape(1, -1), b_gate[l].reshape(1, -1),
                              perms)
    proj3 = proj.reshape(BATCH, SEQ, QKV_COLS)

    o_groups, lse_groups = [], []
    for gi, (window, dilation) in enumerate(A_PATTERNS):
        o, lse = _attn_a_dilated(proj3, gi, window, dilation)
        o_groups.append(o)
        lse_groups.append(lse)
    mix_b = _attn_b(proj3)

    merged = _merge(o_groups, lse_groups, mix_b, gates, w_a_up[l].astype(BF16), w_b_up[l].astype(BF16),
                    jnp.swapaxes(perms[1:], 1, 2))
    x1, h2, logits_t = _out_projection(merged, w_out[l].astype(BF16), x2, mod4, g_ffn[l].reshape(1, -1),
                                       w_router[l].T)
    csm, cnt_f, gx = _route(logits_t, e_bias[l].reshape(-1, 1))
    cnt = cnt_f[:, 0, :].astype(jnp.int32)
    goff, tail, tile_e, ntile, next_e = _plan(cnt)
    xs = _dispatch(cnt, goff, tail, h2, gx, csm)
    ys = _routed_experts(tile_e, ntile, next_e, xs, w1[l], w3[l], w2[l])
    shared = _shared_expert(h2, ws1[l].astype(BF16), ws3[l].astype(BF16), ws2[l].astype(BF16))
    out = _combine(cnt, goff, csm, x1, shared, mod4, g_final.reshape(1, -1), ys)
    return out.reshape(BATCH, SEQ, D_MODEL)
```

```python
import functools

import jax
import jax.numpy as jnp
from jax import lax
from jax.experimental import pallas as pl
from jax.experimental.pallas import tpu as pltpu

F32 = jnp.float32
BF16 = jnp.bfloat16

D_MODEL = 2048
BATCH = 2
SEQ = 4096
N_TOK = BATCH * SEQ
HEAD_DIM = 128
EPS = 1e-6
A_PATTERNS = ((128, 1), (512, 4), (2048, 16))
A_N_GROUPS = len(A_PATTERNS)
A_HEADS = 4
A_WIDTH = A_HEADS * HEAD_DIM
ROPE_THETA = 500000.0
ROPE_DIMS = HEAD_DIM // 4
B_Q_HEADS = 16
B_KV_HEADS = 4
B_GROUP = B_Q_HEADS // B_KV_HEADS
B_Q_WIDTH = B_Q_HEADS * HEAD_DIM
B_KV_WIDTH = B_KV_HEADS * HEAD_DIM
AXIAL_THETA = 10000.0
AXIAL_DIMS = HEAD_DIM // 2
GRID_W = 64
A_COLS = A_N_GROUPS * 3 * A_WIDTH
B_COLS = B_Q_WIDTH + 2 * B_KV_WIDTH
GATE_COLS = 2 * D_MODEL
QKV_COLS = A_COLS + B_COLS
IN_COLS = QKV_COLS + GATE_COLS
N_MOD = 6
N_EXPERTS = 64
N_EXPERT_GROUPS = 8
PER_GROUP = N_EXPERTS // N_EXPERT_GROUPS
TOPK_GROUPS = 4
TOP_K = 8
D_EXPERT = D_MODEL // 4
D_SHARED = D_MODEL // 4
ROUTED_SCALE = 2.5
NEG = -1e30
ATTN_SCALE = HEAD_DIM ** -0.5
LOG2E = 1.4426950408889634
LN2 = 0.6931471805599453

LANES = 128
SUBLANES = 8
VMEM_LIMIT = 56 * 1024 * 1024

COL_BLK = 512
PERM_BLOCK = 256
N_COL_BLKS = IN_COLS // COL_BLK
A_BLKS = A_COLS // COL_BLK
BQ_BLK0 = A_BLKS
BK_BLK = BQ_BLK0 + B_Q_WIDTH // COL_BLK
BV_BLK = BK_BLK + 1
GATE_BLK0 = BV_BLK + 1

CHUNK = 512
N_CHUNKS = N_TOK // CHUNK
GRANULE = 16
GRANULE_SHIFT = GRANULE.bit_length() - 1
TILE = 256
TILE_SHIFT = TILE.bit_length() - 1
GRAN_PER_TILE = TILE // GRANULE
GPT_SHIFT = TILE_SHIFT - GRANULE_SHIFT
N_SLOTS = 4
COMBINE_TILE = 1024
COMBINE_GPT = COMBINE_TILE // GRANULE
COMBINE_GPT_SHIFT = COMBINE_GPT.bit_length() - 1
COMBINE_SLOTS = 3
WEIGHT_DMA_PRIORITY = 1
XS_WIDTH = D_MODEL + LANES
CHUNK_ROWS = -(-(CHUNK * TOP_K + N_EXPERTS * (GRANULE - 1)) // TILE) * TILE
MAX_ROWS = -(-(N_TOK * TOP_K + N_EXPERTS * N_CHUNKS * (GRANULE - 1) + N_EXPERTS * (TILE - 1)) // TILE) * TILE
MAX_TILES = MAX_ROWS // TILE


def _cparams(semantics):
    return pltpu.CompilerParams(dimension_semantics=semantics, vmem_limit_bytes=VMEM_LIMIT)


def _ada_kernel(ct_ref, w_ref, b_ref, o_ref):
    ct = ct_ref[...]
    act = ct * jax.nn.sigmoid(ct)
    w = w_ref[...]
    for b in range(BATCH):
        o_ref[b:b + 1, :] = jnp.sum(w * act[:, b:b + 1], axis=0, keepdims=True) + b_ref[...]


def _ada(c_t, w_ada, b_ada):
    tn = 1024
    n_out = w_ada.shape[1]
    return pl.pallas_call(
        _ada_kernel,
        out_shape=jax.ShapeDtypeStruct((BATCH, n_out), F32),
        grid=(n_out // tn,),
        in_specs=[pl.BlockSpec((D_MODEL, BATCH), lambda n: (0, 0)),
                  pl.BlockSpec((D_MODEL, tn), lambda n: (0, n)),
                  pl.BlockSpec((1, tn), lambda n: (0, n))],
        out_specs=pl.BlockSpec((BATCH, tn), lambda n: (0, n)),
        compiler_params=_cparams(("parallel",)),
        name="ada_mod",
    )(c_t, w_ada, b_ada)


def _normmod(x, g, sc, sh):
    y = x * lax.rsqrt(jnp.mean(x * x, axis=-1, keepdims=True) + EPS)
    return (y * g) * (1.0 + sc) + sh


def _tile4(t):
    return jnp.concatenate([t, t, t, t], axis=1)


def _rotary(y, tabs, rows, shift):
    cos_ref, sin_lo_ref, sin_hi_ref = tabs
    width = y.shape[1]
    return (y * _tile4(cos_ref[rows, :])
            + pltpu.roll(y, width - shift, 1) * _tile4(sin_lo_ref[rows, :])
            + pltpu.roll(y, shift, 1) * _tile4(sin_hi_ref[rows, :]))


def _head_rmsnorm(y, g):
    outs = []
    for h in range(y.shape[1] // HEAD_DIM):
        yh = y[:, h * HEAD_DIM:(h + 1) * HEAD_DIM]
        outs.append(yh * lax.rsqrt(jnp.mean(yh * yh, axis=-1, keepdims=True) + EPS) * g)
    return jnp.concatenate(outs, axis=1)


def _proj_kernel(x_ref, g_ref, sc_ref, sh_ref, w_ref, ca_ref, sal_ref, sah_ref, cb_ref, sbl_ref, sbh_ref,
                 gain_ref, scale_ref, bg_ref, perm_ref, o_ref, gate_ref, h_ref, hp_ref):
    n = pl.program_id(1)
    is_a = n < A_BLKS
    part = n % 3
    tabs_a = (ca_ref, sal_ref, sah_ref)
    tabs_b = (cb_ref, sbl_ref, sbh_ref)
    sub_rows = PERM_BLOCK
    n_sub = h_ref.shape[0] // sub_rows

    @pl.when(n == 0)
    def _():
        def norm_rows(r, _):
            rows = pl.ds(pl.multiple_of(r * sub_rows, sub_rows), sub_rows)
            h = _normmod(x_ref[rows, :], g_ref[...], sc_ref[...], sh_ref[...]).astype(BF16)
            h_ref[rows, :] = h
            hp_ref[rows, :] = h
            return 0

        lax.fori_loop(0, n_sub, norm_rows, 0)

    @pl.when(is_a & (part == 0) & (n > 0))
    def _():
        for r in range(n_sub):
            rows = pl.ds(r * sub_rows, sub_rows)
            hp_ref[rows, :] = jnp.dot(perm_ref[...], h_ref[rows, :], preferred_element_type=F32).astype(BF16)

    def run(epilogue, dst_ref=o_ref, src_ref=h_ref):
        w = w_ref[...].astype(BF16)
        for r in range(n_sub):
            rows = pl.ds(r * sub_rows, sub_rows)
            acc = jnp.dot(src_ref[rows, :], w, preferred_element_type=F32)
            dst_ref[rows, :] = epilogue(acc, rows).astype(dst_ref.dtype)

    @pl.when(is_a & (part < 2))
    def _():
        run(lambda acc, rows: _rotary(acc, tabs_a, rows, ROPE_DIMS // 2) * scale_ref[...], src_ref=hp_ref)

    @pl.when(is_a & (part == 2))
    def _():
        run(lambda acc, rows: acc, src_ref=hp_ref)

    @pl.when((n >= BQ_BLK0) & (n <= BK_BLK))
    def _():
        run(lambda acc, rows: _rotary(_head_rmsnorm(acc, gain_ref[...]), tabs_b, rows, AXIAL_DIMS // 2)
            * scale_ref[...])

    @pl.when(n == BV_BLK)
    def _():
        run(lambda acc, rows: acc)

    @pl.when(n >= GATE_BLK0)
    def _():
        run(lambda acc, rows: jax.nn.sigmoid(acc + bg_ref[...]), gate_ref)


def _residue_source(dilation):
    per_res = PERM_BLOCK // dilation
    row = jnp.arange(PERM_BLOCK)
    return (row % per_res) * dilation + row // per_res


def _residue_perm(dilation):
    return (_residue_source(dilation)[:, None] == jnp.arange(PERM_BLOCK)[None, :]).astype(BF16)


def _projection(x2, g_attn, mod4, w_in, tabs_a, tabs_b, qg, kg, b_gate, perms):
    tm = 1024
    per_b = SEQ // tm

    def mod_spec(j):
        return pl.BlockSpec((None, None, 1, D_MODEL), lambda m, n: (m // per_b, j, 0, 0))

    tab_spec = pl.BlockSpec((tm, LANES), lambda m, n: (m % per_b, 0))
    n_gate_blks = GATE_COLS // COL_BLK
    is_query = [(blk < A_BLKS and blk % 3 == 0) or BQ_BLK0 <= blk < BK_BLK for blk in range(N_COL_BLKS)]
    col_scale = jnp.broadcast_to(jnp.where(jnp.array(is_query), ATTN_SCALE * LOG2E, 1.0).astype(F32)[:, None, None],
                                 (N_COL_BLKS, 1, COL_BLK))
    ones = jnp.ones_like(qg)
    col_gain = jnp.stack([qg if BQ_BLK0 <= blk < BK_BLK else kg if blk == BK_BLK else ones
                          for blk in range(N_COL_BLKS)])
    def residue_major(t, dilation):
        per_res = PERM_BLOCK // dilation
        t4 = t.reshape(SEQ // PERM_BLOCK, per_res, dilation, LANES)
        return jnp.swapaxes(t4, 1, 2).reshape(SEQ, LANES)

    tabs_a = [jnp.stack([residue_major(t, dilation) for _, dilation in A_PATTERNS]) for t in tabs_a]
    tab_a_spec = pl.BlockSpec((None, tm, LANES),
                              lambda m, n: (jnp.clip(n // 3, 0, A_N_GROUPS - 1), m % per_b, 0))

    def gate_blk(n):
        return jnp.clip(n - GATE_BLK0, 0, n_gate_blks - 1)

    return pl.pallas_call(
        _proj_kernel,
        out_shape=(jax.ShapeDtypeStruct((N_TOK, QKV_COLS), BF16),
                   jax.ShapeDtypeStruct((N_TOK, GATE_COLS), BF16)),
        grid=(N_TOK // tm, N_COL_BLKS),
        in_specs=[pl.BlockSpec((tm, D_MODEL), lambda m, n: (m, 0)),
                  pl.BlockSpec((1, D_MODEL), lambda m, n: (0, 0)),
                  mod_spec(1), mod_spec(0),
                  pl.BlockSpec((D_MODEL, COL_BLK), lambda m, n: (0, n)),
                  tab_a_spec, tab_a_spec, tab_a_spec, tab_spec, tab_spec, tab_spec,
                  pl.BlockSpec((None, 1, HEAD_DIM), lambda m, n: (n, 0, 0)),
                  pl.BlockSpec((None, 1, COL_BLK), lambda m, n: (n, 0, 0)),
                  pl.BlockSpec((1, COL_BLK), lambda m, n: (0, gate_blk(n))),
                  pl.BlockSpec((None, PERM_BLOCK, PERM_BLOCK),
                               lambda m, n: (jnp.clip(n // 3, 0, A_N_GROUPS - 1), 0, 0))],
        out_specs=(pl.BlockSpec((tm, COL_BLK), lambda m, n: (m, jnp.minimum(n, GATE_BLK0 - 1))),
                   pl.BlockSpec((tm, COL_BLK), lambda m, n: (m, gate_blk(n)))),
        scratch_shapes=[pltpu.VMEM((tm, D_MODEL), BF16), pltpu.VMEM((tm, D_MODEL), BF16)],
        compiler_params=_cparams(("parallel", "arbitrary")),
        name="in_projection",
    )(x2, g_attn, mod4, mod4, w_in, *tabs_a, *tabs_b, col_gain, col_scale, b_gate, perms)


def _attn_res_kernel(q_ref, k_ref, v_ref, o_ref, lse_ref, *, half_w):
    n_blk, per_res, _ = q_ref.shape
    length = n_blk * per_res
    q = q_ref[...].reshape(length, A_WIDTH)
    k = k_ref[...].reshape(length, A_WIDTH)
    v = v_ref[...].reshape(length, A_WIDTH)
    tq = 2 * half_w
    win = 4 * half_w
    lane = lax.broadcasted_iota(jnp.int32, (tq, LANES), 1)
    row_minus_col = lax.broadcasted_iota(jnp.int32, (tq, win), 0) - lax.broadcasted_iota(jnp.int32, (tq, win), 1)
    o_blocks, lse_blocks = [], []
    for qb in range(length // tq):
        start = min(max(qb * tq - half_w, 0), length - win)
        valid = jnp.abs(row_minus_col + (qb * tq - start)) <= half_w
        lse_tile = jnp.zeros((tq, LANES), F32)
        outs = []
        for h in range(A_HEADS):
            sl = slice(h * HEAD_DIM, (h + 1) * HEAD_DIM)
            s = lax.dot_general(q[qb * tq:(qb + 1) * tq, sl], k[start:start + win, sl],
                                (((1,), (1,)), ((), ())), preferred_element_type=F32)
            s = jnp.where(valid, s, NEG)
            m = jnp.max(s, axis=-1, keepdims=True)
            p = jnp.exp2(s - m)
            l = jnp.sum(p, axis=-1, keepdims=True)
            o = jnp.dot(p.astype(BF16), v[start:start + win, sl], preferred_element_type=F32)
            outs.append(o / l)
            lse_tile = jnp.where(lane == h, m * LN2 + jnp.log(l), lse_tile)
        o_blocks.append(jnp.concatenate(outs, axis=1).astype(o_ref.dtype))
        lse_blocks.append(lse_tile)
    o_ref[...] = jnp.concatenate(o_blocks, axis=0).reshape(n_blk, per_res, A_WIDTH)
    lse_ref[...] = jnp.concatenate(lse_blocks, axis=0).reshape(n_blk, per_res, LANES)


def _attn_a_dilated(proj3, group, window, dilation):
    per_res = PERM_BLOCK // dilation
    n_blk = SEQ // PERM_BLOCK
    half_w = (window // 2) // dilation
    view = proj3.reshape(BATCH, n_blk, dilation, per_res, QKV_COLS)

    def spec(width, blk):
        return pl.BlockSpec((None, n_blk, None, per_res, width), lambda b, r: (b, 0, r, 0, blk))

    o, lse = pl.pallas_call(
        functools.partial(_attn_res_kernel, half_w=half_w),
        out_shape=(jax.ShapeDtypeStruct((BATCH, n_blk, dilation, per_res, A_WIDTH), BF16),
                   jax.ShapeDtypeStruct((BATCH, n_blk, dilation, per_res, LANES), F32)),
        grid=(BATCH, dilation),
        in_specs=[spec(COL_BLK, 3 * group), spec(COL_BLK, 3 * group + 1), spec(COL_BLK, 3 * group + 2)],
        out_specs=(spec(A_WIDTH, 0), spec(LANES, 0)),
        compiler_params=_cparams(("parallel", "parallel")),
        name=f"dilated_attention_g{group}",
    )(view, view, view)
    return o.reshape(N_TOK, A_WIDTH), lse.reshape(N_TOK, LANES)


def _attn_b_kernel(q_ref, k_ref, v_ref, o_ref, vx_ref, acc_ref, *, tk):
    tq = q_ref.shape[0]
    n_chunks = SEQ // tk

    @pl.when(pl.program_id(2) == 0)
    def _():
        vx_ref[:, 0:HEAD_DIM] = v_ref[...]
        vx_ref[:, HEAD_DIM:2 * HEAD_DIM] = jnp.ones((SEQ, HEAD_DIM), BF16)

    q = q_ref[...]
    qs = jnp.concatenate([q[:, g * HEAD_DIM:(g + 1) * HEAD_DIM] for g in range(B_GROUP)], axis=0)
    acc_ref[...] = jnp.zeros_like(acc_ref)
    m = jnp.full((B_GROUP * tq, 1), -jnp.inf, F32)
    for c in range(n_chunks):
        keys = slice(c * tk, (c + 1) * tk)
        s = lax.dot_general(qs, k_ref[keys, :], (((1,), (1,)), ((), ())), preferred_element_type=F32)
        m_new = jnp.maximum(m, jnp.max(s, axis=-1, keepdims=True))
        p = jnp.exp2(s - m_new).astype(BF16)
        acc_ref[...] = jnp.exp2(m - m_new) * acc_ref[...] + jnp.dot(p, vx_ref[keys, :],
                                                                    preferred_element_type=F32)
        m = m_new
    o = acc_ref[:, 0:HEAD_DIM] / acc_ref[:, HEAD_DIM:2 * HEAD_DIM]
    o_ref[...] = jnp.concatenate([o[g * tq:(g + 1) * tq] for g in range(B_GROUP)], axis=1).astype(o_ref.dtype)


def _attn_b(proj3):
    tq = 256
    tk = 256
    kcol0 = BK_BLK * COL_BLK // HEAD_DIM
    vcol0 = BV_BLK * COL_BLK // HEAD_DIM
    o = pl.pallas_call(
        functools.partial(_attn_b_kernel, tk=tk),
        out_shape=jax.ShapeDtypeStruct((BATCH, SEQ, B_Q_WIDTH), BF16),
        grid=(BATCH, B_KV_HEADS, SEQ // tq),
        in_specs=[pl.BlockSpec((None, tq, COL_BLK), lambda b, h, i: (b, i, BQ_BLK0 + h)),
                  pl.BlockSpec((None, SEQ, HEAD_DIM), lambda b, h, i: (b, 0, kcol0 + h)),
                  pl.BlockSpec((None, SEQ, HEAD_DIM), lambda b, h, i: (b, 0, vcol0 + h))],
        out_specs=pl.BlockSpec((None, tq, COL_BLK), lambda b, h, i: (b, i, h)),
        scratch_shapes=[pltpu.VMEM((SEQ, 2 * HEAD_DIM), BF16),
                        pltpu.VMEM((B_GROUP * tq, 2 * HEAD_DIM), F32)],
        compiler_params=_cparams(("parallel", "parallel", "arbitrary")),
        name="gqa_attention",
    )(proj3, proj3, proj3)
    return o.reshape(N_TOK, B_Q_WIDTH)


def _merge_kernel(o0_ref, o1_ref, o2_ref, l0_ref, l1_ref, l2_ref, yb_ref, wa_ref, wb_ref, ga_ref, gb_ref,
                  unperm_ref, out_ref):
    sub_rows = PERM_BLOCK
    for r in range(out_ref.shape[0] // sub_rows):
        rows = pl.ds(r * sub_rows, sub_rows)
        outs = [o0_ref[rows, :].astype(F32)]
        lses = [l0_ref[rows, :]]
        for g, (o_ref, l_ref) in enumerate(((o1_ref, l1_ref), (o2_ref, l2_ref))):
            outs.append(jnp.dot(unperm_ref[g], o_ref[rows, :], preferred_element_type=F32))
            lses.append(jnp.dot(unperm_ref[g].astype(F32), l_ref[rows, :], preferred_element_type=F32,
                                precision=lax.Precision.HIGHEST))
        cols = []
        for h in range(A_HEADS):
            lh = [jnp.broadcast_to(l[:, h:h + 1], (sub_rows, HEAD_DIM)) for l in lses]
            mx = jnp.maximum(jnp.maximum(lh[0], lh[1]), lh[2])
            e = [jnp.exp(v - mx) for v in lh]
            den = e[0] + e[1] + e[2]
            sl = slice(h * HEAD_DIM, (h + 1) * HEAD_DIM)
            cols.append(sum((e[g] / den) * outs[g][:, sl] for g in range(A_N_GROUPS)))
        mix_a = jnp.concatenate(cols, axis=1).astype(BF16)
        mix_b = yb_ref[rows, :]
        for cb in range(D_MODEL // COL_BLK):
            cs = pl.ds(cb * COL_BLK, COL_BLK)
            ya = jnp.dot(mix_a, wa_ref[:, cs], preferred_element_type=F32)
            yb = jnp.dot(mix_b, wb_ref[:, cs], preferred_element_type=F32)
            out_ref[rows, cs] = (ga_ref[rows, cs].astype(F32) * ya
                                 + gb_ref[rows, cs].astype(F32) * yb).astype(out_ref.dtype)


def _merge(o_groups, lse_groups, mix_b, gates, w_a_up, w_b_up, unperms):
    tm = 512
    o_spec = pl.BlockSpec((tm, A_WIDTH), lambda m: (m, 0))
    l_spec = pl.BlockSpec((tm, LANES), lambda m: (m, 0))
    return pl.pallas_call(
        _merge_kernel,
        out_shape=jax.ShapeDtypeStruct((N_TOK, D_MODEL), BF16),
        grid=(N_TOK // tm,),
        in_specs=[o_spec, o_spec, o_spec, l_spec, l_spec, l_spec,
                  pl.BlockSpec((tm, B_Q_WIDTH), lambda m: (m, 0)),
                  pl.BlockSpec((A_WIDTH, D_MODEL), lambda m: (0, 0)),
                  pl.BlockSpec((B_Q_WIDTH, D_MODEL), lambda m: (0, 0)),
                  pl.BlockSpec((tm, D_MODEL), lambda m: (m, 0)),
                  pl.BlockSpec((tm, D_MODEL), lambda m: (m, 1)),
                  pl.BlockSpec((A_N_GROUPS - 1, PERM_BLOCK, PERM_BLOCK), lambda m: (0, 0, 0))],
        out_specs=pl.BlockSpec((tm, D_MODEL), lambda m: (m, 0)),
        compiler_params=_cparams(("parallel",)),
        name="branch_merge",
    )(*o_groups, *lse_groups, mix_b, w_a_up, w_b_up, gates, gates, unperms)


def _outproj_kernel(mg_ref, w_ref, x_ref, gt_ref, g_ref, sc_ref, sh_ref, wr_ref, x1_ref, h2_ref, lg_ref):
    sub_rows = 256
    wr = wr_ref[...]
    wr_hi = wr.astype(BF16)
    wr_lo = (wr - wr_hi.astype(F32)).astype(BF16)
    for r in range(mg_ref.shape[0] // sub_rows):
        rows = pl.ds(r * sub_rows, sub_rows)
        y = jnp.dot(mg_ref[rows, :], w_ref[...], preferred_element_type=F32)
        x1 = x_ref[rows, :] + gt_ref[...] * y
        x1_ref[rows, :] = x1
        h2 = _normmod(x1, g_ref[...], sc_ref[...], sh_ref[...])
        h2_hi = h2.astype(BF16)
        h2_ref[rows, :] = h2_hi
        h2_lo = (h2 - h2_hi.astype(F32)).astype(BF16)
        nt = (((1,), (1,)), ((), ()))
        lg_ref[:, rows] = (lax.dot_general(wr_hi, h2_hi, nt, preferred_element_type=F32)
                           + lax.dot_general(wr_hi, h2_lo, nt, preferred_element_type=F32)
                           + lax.dot_general(wr_lo, h2_hi, nt, preferred_element_type=F32))


def _out_projection(merged, w_out, x2, mod4, g_ffn, w_router_t):
    tm = 512
    per_b = SEQ // tm

    def mod_spec(j):
        return pl.BlockSpec((None, None, 1, D_MODEL), lambda m: (m // per_b, j, 0, 0))

    return pl.pallas_call(
        _outproj_kernel,
        out_shape=(jax.ShapeDtypeStruct((N_TOK, D_MODEL), F32),
                   jax.ShapeDtypeStruct((N_TOK, D_MODEL), BF16),
                   jax.ShapeDtypeStruct((N_EXPERTS, N_TOK), F32)),
        grid=(N_TOK // tm,),
        in_specs=[pl.BlockSpec((tm, D_MODEL), lambda m: (m, 0)),
                  pl.BlockSpec((D_MODEL, D_MODEL), lambda m: (0, 0)),
                  pl.BlockSpec((tm, D_MODEL), lambda m: (m, 0)),
                  mod_spec(2),
                  pl.BlockSpec((1, D_MODEL), lambda m: (0, 0)),
                  mod_spec(4), mod_spec(3),
                  pl.BlockSpec((N_EXPERTS, D_MODEL), lambda m: (0, 0))],
        out_specs=(pl.BlockSpec((tm, D_MODEL), lambda m: (m, 0)),
                   pl.BlockSpec((tm, D_MODEL), lambda m: (m, 0)),
                   pl.BlockSpec((N_EXPERTS, tm), lambda m: (0, m))),
        compiler_params=_cparams(("parallel",)),
        name="out_projection",
    )(merged, w_out, x2, mod4, g_ffn, mod4, mod4, w_router_t)


def _route_kernel(lg_ref, eb_ref, csm_ref, cnt_ref, gx_ref):
    tt = lg_ref.shape[1]
    scores = jax.nn.sigmoid(lg_ref[...])
    sel = scores + eb_ref[...]
    neg_inf = -jnp.inf
    sel_g = [sel[PER_GROUP * g:PER_GROUP * (g + 1), :] for g in range(N_EXPERT_GROUPS)]
    grp = []
    for v in sel_g:
        m1 = jnp.max(v, axis=0, keepdims=True)
        is1 = v == m1
        n1 = jnp.sum(jnp.where(is1, 1.0, 0.0), axis=0, keepdims=True)
        rest = jnp.max(jnp.where(is1, neg_inf, v), axis=0, keepdims=True)
        grp.append(m1 + jnp.where(n1 >= 2.0, m1, rest))
    masked = []
    for g in range(N_EXPERT_GROUPS):
        rank = jnp.zeros((1, tt), F32)
        for g2 in range(N_EXPERT_GROUPS):
            if g2 != g:
                beats = (grp[g2] >= grp[g]) if g2 < g else (grp[g2] > grp[g])
                rank = rank + jnp.where(beats, 1.0, 0.0)
        keep = jnp.broadcast_to(rank < TOPK_GROUPS, (PER_GROUP, tt))
        masked.append(jnp.where(keep, sel_g[g], neg_inf))
    sub = lax.broadcasted_iota(jnp.int32, (PER_GROUP, tt), 0)
    ranks = [jnp.zeros((PER_GROUP, tt), F32) for _ in range(N_EXPERT_GROUPS)]
    for g2 in range(N_EXPERT_GROUPS):
        for m2 in range(PER_GROUP):
            vf = jnp.broadcast_to(masked[g2][m2:m2 + 1, :], (PER_GROUP, tt))
            for g in range(N_EXPERT_GROUPS):
                if g2 < g:
                    beats = vf >= masked[g]
                elif g2 > g:
                    beats = vf > masked[g]
                else:
                    beats = (vf > masked[g]) | ((vf == masked[g]) & (sub > m2))
                ranks[g] = ranks[g] + jnp.where(beats, 1.0, 0.0)
    picked = [ranks[g] < TOP_K for g in range(N_EXPERT_GROUPS)]
    chosen = [jnp.where(picked[g], scores[PER_GROUP * g:PER_GROUP * (g + 1), :], 0.0)
              for g in range(N_EXPERT_GROUPS)]
    total = chosen[0]
    for g in range(1, N_EXPERT_GROUPS):
        total = total + chosen[g]
    denom = jnp.sum(total, axis=0, keepdims=True)
    gate = jnp.concatenate([chosen[g] / denom * ROUTED_SCALE for g in range(N_EXPERT_GROUPS)], axis=0)
    mask = jnp.concatenate([jnp.where(picked[g], 1.0, 0.0) for g in range(N_EXPERT_GROUPS)], axis=0)
    mask_bf = mask.astype(BF16)

    upper = jnp.where(lax.broadcasted_iota(jnp.int32, (tt, tt), 0) <= lax.broadcasted_iota(jnp.int32, (tt, tt), 1),
                      1.0, 0.0).astype(BF16)
    cs = jnp.dot(mask_bf, upper, preferred_element_type=F32)
    csm_ref[...] = jnp.where(mask > 0.0, cs, 0.0)
    cnt_ref[...] = lax.dot_general(jnp.ones((SUBLANES, tt), BF16), mask_bf, (((1,), (1,)), ((), ())),
                                   preferred_element_type=F32)

    hi = gate.astype(BF16).astype(F32)
    gx_ref[...] = jnp.concatenate([hi, gate - hi], axis=0).T.astype(BF16)


def _route(logits_t, e_bias_col):
    return pl.pallas_call(
        _route_kernel,
        out_shape=(jax.ShapeDtypeStruct((N_EXPERTS, N_TOK), F32),
                   jax.ShapeDtypeStruct((N_CHUNKS, SUBLANES, N_EXPERTS), F32),
                   jax.ShapeDtypeStruct((N_TOK, LANES), BF16)),
        grid=(N_CHUNKS,),
        in_specs=[pl.BlockSpec((N_EXPERTS, CHUNK), lambda c: (0, c)),
                  pl.BlockSpec((N_EXPERTS, 1), lambda c: (0, 0))],
        out_specs=(pl.BlockSpec((N_EXPERTS, CHUNK), lambda c: (0, c)),
                   pl.BlockSpec((None, SUBLANES, N_EXPERTS), lambda c: (c, 0, 0)),
                   pl.BlockSpec((CHUNK, LANES), lambda c: (c, 0))),
        compiler_params=_cparams(("parallel",)),
        name="routing",
    )(logits_t, e_bias_col)


def _plan_kernel(cnt_ref, goff_ref, tail_ref, tile_e_ref, ntile_ref, next_ref):
    def per_expert(e, carry):
        base, tbase, prev = carry

        def per_chunk(c, off):
            goff_ref[c, e] = base + off
            return off + (((cnt_ref[c, e] + (GRANULE - 1)) >> GRANULE_SHIFT) << GRANULE_SHIFT)

        tot = lax.fori_loop(0, N_CHUNKS, per_chunk, jnp.int32(0))
        region = ((tot + (TILE - 1)) >> TILE_SHIFT) << TILE_SHIFT
        tail_ref[0, e] = base + tot
        tail_ref[1, e] = (region - tot) >> GRANULE_SHIFT
        n_t = region >> TILE_SHIFT

        def per_tile(j, _):
            tile_e_ref[tbase + j] = e
            return 0

        lax.fori_loop(0, n_t, per_tile, 0)
        next_ref[e] = jnp.int32(-1)

        @pl.when((n_t > 0) & (prev >= 0))
        def _():
            next_ref[jnp.maximum(prev, 0)] = e

        return base + region, tbase + n_t, jnp.where(n_t > 0, e, prev)

    _, n_tiles, _ = lax.fori_loop(0, N_EXPERTS, per_expert, (jnp.int32(0), jnp.int32(0), jnp.int32(-1)))
    ntile_ref[0] = n_tiles
    last_e = tile_e_ref[jnp.maximum(n_tiles - 1, 0)]

    def fill(j, _):
        tile_e_ref[j] = last_e
        return 0

    lax.fori_loop(n_tiles, MAX_TILES, fill, 0)


def _plan(cnt):
    smem = pl.BlockSpec(memory_space=pltpu.SMEM)
    return pl.pallas_call(
        _plan_kernel,
        out_shape=(jax.ShapeDtypeStruct((N_CHUNKS, N_EXPERTS), jnp.int32),
                   jax.ShapeDtypeStruct((2, N_EXPERTS), jnp.int32),
                   jax.ShapeDtypeStruct((MAX_TILES,), jnp.int32),
                   jax.ShapeDtypeStruct((1,), jnp.int32),
                   jax.ShapeDtypeStruct((N_EXPERTS,), jnp.int32)),
        in_specs=[smem],
        out_specs=(smem, smem, smem, smem, smem),
        name="row_plan",
    )(cnt)


def _chunk_granules(cnt_ref, goff_ref, c, table_ref, per_granule=None):
    def per_expert(e, n_before):
        n_gran = (cnt_ref[c, e] + (GRANULE - 1)) >> GRANULE_SHIFT
        base = goff_ref[c, e]

        def per_j(j, _):
            table_ref[n_before + j] = base + j * GRANULE
            if per_granule is not None:
                per_granule(e, j, n_before + j)
            return 0

        lax.fori_loop(0, n_gran, per_j, 0)
        return n_before + n_gran

    return lax.fori_loop(0, N_EXPERTS, per_expert, jnp.int32(0))


def _dispatch_kernel(cnt_ref, goff_ref, tail_ref, ntile_ref, x_ref, gx_ref, csm_ref, xs_hbm,
                     onehot_ref, table_ref, buf_ref, zero_ref, xcat_ref, sems, zsem):
    c = pl.program_id(0)
    sub = lax.broadcasted_iota(jnp.int32, (GRANULE, CHUNK), 0)
    xcat_ref[:, 0:D_MODEL] = x_ref[...]
    xcat_ref[:, D_MODEL:XS_WIDTH] = gx_ref[...]

    def build_granule(e, j, g):
        want = (sub + (j * GRANULE + 1)).astype(F32)
        hit = csm_ref[pl.ds(e, 1), :] == want
        onehot_ref[pl.ds(pl.multiple_of(g * GRANULE, GRANULE), GRANULE), :] = jnp.where(hit, 1.0, 0.0).astype(BF16)

    n_gran = _chunk_granules(cnt_ref, goff_ref, c, table_ref, build_granule)
    n_tiles = (n_gran + (GRAN_PER_TILE - 1)) >> GPT_SHIFT

    def clear(g, _):
        onehot_ref[pl.ds(pl.multiple_of(g * GRANULE, GRANULE), GRANULE), :] = jnp.zeros((GRANULE, CHUNK), BF16)
        return 0

    lax.fori_loop(n_gran, n_tiles * GRAN_PER_TILE, clear, 0)

    def granule_copy(slot, i, row):
        return pltpu.make_async_copy(buf_ref.at[slot, pl.ds(pl.multiple_of(i * GRANULE, GRANULE), GRANULE), :],
                                     xs_hbm.at[pl.ds(pl.multiple_of(row, GRANULE), GRANULE), :],
                                     sems.at[slot])

    def tile_granules(k):
        return jnp.minimum(GRAN_PER_TILE, n_gran - k * GRAN_PER_TILE)

    def wait_tile(k):
        slot = k % N_SLOTS

        def w(i, _):
            granule_copy(slot, i, 0).wait()
            return 0

        lax.fori_loop(0, tile_granules(k), w, 0)

    def per_tile(k, _):
        slot = k % N_SLOTS

        @pl.when(k >= N_SLOTS)
        def _():
            wait_tile(k - N_SLOTS)

        oh = onehot_ref[pl.ds(pl.multiple_of(k * TILE, TILE), TILE), :]
        buf_ref[slot] = jnp.dot(oh, xcat_ref[...], preferred_element_type=F32).astype(BF16)

        def s(i, _):
            granule_copy(slot, i, table_ref[k * GRAN_PER_TILE + i]).start()
            return 0

        lax.fori_loop(0, tile_granules(k), s, 0)
        return 0

    lax.fori_loop(0, n_tiles, per_tile, 0)

    def drain(k, _):
        wait_tile(k)
        return 0

    lax.fori_loop(jnp.maximum(n_tiles - N_SLOTS, 0), n_tiles, drain, 0)

    @pl.when(c == N_CHUNKS - 1)
    def _():
        zero_ref[...] = jnp.zeros_like(zero_ref)

        def tail_copy(row):
            return pltpu.make_async_copy(zero_ref, xs_hbm.at[pl.ds(pl.multiple_of(row, GRANULE), GRANULE), :], zsem)

        def start_e(e, _):
            def st(j, _):
                tail_copy(tail_ref[0, e] + j * GRANULE).start()
                return 0
            lax.fori_loop(0, tail_ref[1, e], st, 0)
            return 0

        def wait_e(e, _):
            def wt(j, _):
                tail_copy(0).wait()
                return 0
            lax.fori_loop(0, tail_ref[1, e], wt, 0)
            return 0

        lax.fori_loop(0, N_EXPERTS, start_e, 0)
        lax.fori_loop(0, N_EXPERTS, wait_e, 0)

        buf_ref[0] = jnp.zeros((TILE, XS_WIDTH), BF16)

        def unused_copy(t):
            return pltpu.make_async_copy(buf_ref.at[0],
                                         xs_hbm.at[pl.ds(pl.multiple_of(t * TILE, TILE), TILE), :], sems.at[0])

        def start_unused(t, _):
            unused_copy(t).start()
            return 0

        def wait_unused(t, _):
            unused_copy(t).wait()
            return 0

        lax.fori_loop(ntile_ref[0], MAX_TILES, start_unused, 0)
        lax.fori_loop(ntile_ref[0], MAX_TILES, wait_unused, 0)


def _dispatch(cnt, goff, tail, ntile, h2, gx, csm):
    grid_spec = pltpu.PrefetchScalarGridSpec(
        num_scalar_prefetch=4,
        grid=(N_CHUNKS,),
        in_specs=[pl.BlockSpec((CHUNK, D_MODEL), lambda c, *_: (c, 0)),
                  pl.BlockSpec((CHUNK, LANES), lambda c, *_: (c, 0)),
                  pl.BlockSpec((N_EXPERTS, CHUNK), lambda c, *_: (0, c))],
        out_specs=pl.BlockSpec(memory_space=pl.ANY),
        scratch_shapes=[pltpu.VMEM((CHUNK_ROWS, CHUNK), BF16),
                        pltpu.SMEM((CHUNK_ROWS // GRANULE,), jnp.int32),
                        pltpu.VMEM((N_SLOTS, TILE, XS_WIDTH), BF16),
                        pltpu.VMEM((GRANULE, XS_WIDTH), BF16),
                        pltpu.VMEM((CHUNK, XS_WIDTH), BF16),
                        pltpu.SemaphoreType.DMA((N_SLOTS,)),
                        pltpu.SemaphoreType.DMA(())])
    return pl.pallas_call(
        _dispatch_kernel,
        out_shape=jax.ShapeDtypeStruct((MAX_ROWS, XS_WIDTH), BF16),
        grid_spec=grid_spec,
        compiler_params=_cparams(("arbitrary",)),
        name="moe_dispatch",
    )(cnt, goff, tail, ntile, h2, gx, csm)


def _routed_kernel(tile_e_ref, ntile_ref, next_ref, xs_hbm, w1_hbm, w3_hbm, w2_hbm, ys_hbm,
                   xbuf_ref, ybuf_ref, w1f_ref, w3f_ref, w2f_ref, w1s_ref, w3s_ref, w2s_ref, xsem, ysem, wsem):
    n_tiles = ntile_ref[0]

    def x_copy(t, slot):
        return pltpu.make_async_copy(xs_hbm.at[pl.ds(pl.multiple_of(t * TILE, TILE), TILE), :],
                                     xbuf_ref.at[slot], xsem.at[slot])

    def y_copy(t, slot):
        return pltpu.make_async_copy(ybuf_ref.at[slot],
                                     ys_hbm.at[pl.ds(pl.multiple_of(t * TILE, TILE), TILE), :], ysem.at[slot])

    def fetch(expert, slot):
        return (pltpu.make_async_copy(w1_hbm.at[expert], w1f_ref.at[slot], wsem.at[slot]),
                pltpu.make_async_copy(w3_hbm.at[expert], w3f_ref.at[slot], wsem.at[slot]),
                pltpu.make_async_copy(w2_hbm.at[expert], w2f_ref.at[slot], wsem.at[slot]))

    @pl.when(n_tiles > 0)
    def _():
        x_copy(0, 0).start()
        for cp in fetch(tile_e_ref[0], 0):
            cp.start()

    def per_tile(t, n_started):
        slot = t & 1
        e = tile_e_ref[t]
        x_copy(t, slot).wait()

        @pl.when(t + 1 < n_tiles)
        def _():
            x_copy(t + 1, 1 - slot).start()

        first = (t == 0) | (e != tile_e_ref[jnp.maximum(t - 1, 0)])

        @pl.when(first)
        def _():
            wslot = n_started & 1
            nxt = next_ref[e]

            @pl.when(nxt >= 0)
            def _():
                for cp in fetch(jnp.maximum(nxt, 0), 1 - wslot):
                    cp.start(priority=WEIGHT_DMA_PRIORITY)

            for cp in fetch(e, wslot):
                cp.wait()
            w1s_ref[...] = w1f_ref[wslot].astype(BF16)
            w3s_ref[...] = w3f_ref[wslot].astype(BF16)
            w2s_ref[...] = w2f_ref[wslot].astype(BF16)

        @pl.when(t >= 2)
        def _():
            y_copy(t - 2, slot).wait()

        x = xbuf_ref[slot, :, 0:D_MODEL]
        gx = xbuf_ref[slot, :, D_MODEL:XS_WIDTH].astype(F32)
        lane = lax.broadcasted_iota(jnp.int32, gx.shape, 1)
        gate = jnp.sum(jnp.where((lane == e) | (lane == e + N_EXPERTS), gx, 0.0), axis=1, keepdims=True)
        a = jnp.dot(x, w1s_ref[...], preferred_element_type=F32)
        u = jnp.dot(x, w3s_ref[...], preferred_element_type=F32)
        hid = (a * jax.nn.sigmoid(a)) * u * gate
        ybuf_ref[slot] = jnp.dot(hid.astype(BF16), w2s_ref[...], preferred_element_type=F32).astype(ybuf_ref.dtype)
        y_copy(t, slot).start()
        return n_started + jnp.where(first, 1, 0)

    lax.fori_loop(0, n_tiles, per_tile, jnp.int32(0))

    @pl.when(n_tiles >= 2)
    def _():
        y_copy(n_tiles - 2, n_tiles & 1).wait()

    @pl.when(n_tiles >= 1)
    def _():
        y_copy(n_tiles - 1, (n_tiles - 1) & 1).wait()

    ybuf_ref[0] = jnp.zeros((TILE, D_MODEL), BF16)

    def start_unused(t, _):
        y_copy(t, 0).start()
        return 0

    def wait_unused(t, _):
        y_copy(t, 0).wait()
        return 0

    lax.fori_loop(n_tiles, MAX_TILES, start_unused, 0)
    lax.fori_loop(n_tiles, MAX_TILES, wait_unused, 0)


def _routed_experts(tile_e, ntile, next_e, xs, w1, w3, w2):
    hbm = pl.BlockSpec(memory_space=pl.ANY)
    grid_spec = pltpu.PrefetchScalarGridSpec(
        num_scalar_prefetch=3,
        grid=(1,),
        in_specs=[hbm, hbm, hbm, hbm],
        out_specs=hbm,
        scratch_shapes=[pltpu.VMEM((2, TILE, XS_WIDTH), BF16),
                        pltpu.VMEM((2, TILE, D_MODEL), BF16),
                        pltpu.VMEM((2, D_MODEL, D_EXPERT), F32),
                        pltpu.VMEM((2, D_MODEL, D_EXPERT), F32),
                        pltpu.VMEM((2, D_EXPERT, D_MODEL), F32),
                        pltpu.VMEM((D_MODEL, D_EXPERT), BF16),
                        pltpu.VMEM((D_MODEL, D_EXPERT), BF16),
                        pltpu.VMEM((D_EXPERT, D_MODEL), BF16),
                        pltpu.SemaphoreType.DMA((2,)),
                        pltpu.SemaphoreType.DMA((2,)),
                        pltpu.SemaphoreType.DMA((2,))])
    return pl.pallas_call(
        _routed_kernel,
        out_shape=jax.ShapeDtypeStruct((MAX_ROWS, D_MODEL), BF16),
        grid_spec=grid_spec,
        compiler_params=_cparams(("arbitrary",)),
        name="routed_experts",
    )(tile_e, ntile, next_e, xs, w1, w3, w2)


def _combine_kernel(cnt_ref, goff_ref, csm_ref, x1_ref, sh_ref, gt_ref, gf_ref, ys_hbm, o_ref,
                    table_ref, buf_ref, acc_ref, pick_ref, sems):
    c = pl.program_id(0)
    sub = lax.broadcasted_iota(jnp.int32, (GRANULE, CHUNK), 0)

    def build_granule(e, j, g):
        want = (sub + (j * GRANULE + 1)).astype(F32)
        hit = csm_ref[pl.ds(e, 1), :] == want
        pick_ref[pl.ds(pl.multiple_of(g * GRANULE, GRANULE), GRANULE), :] = jnp.where(hit, 1.0, 0.0).astype(BF16)

    n_gran = _chunk_granules(cnt_ref, goff_ref, c, table_ref, build_granule)
    n_tiles = (n_gran + (COMBINE_GPT - 1)) >> COMBINE_GPT_SHIFT

    def clear_pick(g, _):
        pick_ref[pl.ds(pl.multiple_of(g * GRANULE, GRANULE), GRANULE), :] = jnp.zeros((GRANULE, CHUNK), BF16)
        return 0

    lax.fori_loop(n_gran, n_tiles * COMBINE_GPT, clear_pick, 0)

    def granule_copy(slot, i, row):
        return pltpu.make_async_copy(ys_hbm.at[pl.ds(pl.multiple_of(row, GRANULE), GRANULE), :],
                                     buf_ref.at[slot, pl.ds(pl.multiple_of(i * GRANULE, GRANULE), GRANULE), :],
                                     sems.at[slot])

    def tile_granules(k):
        return jnp.minimum(COMBINE_GPT, n_gran - k * COMBINE_GPT)

    def start_tile(k):
        slot = k % COMBINE_SLOTS

        def s(i, _):
            granule_copy(slot, i, table_ref[k * COMBINE_GPT + i]).start()
            return 0

        lax.fori_loop(0, tile_granules(k), s, 0)

    def wait_tile(k):
        slot = k % COMBINE_SLOTS

        def w(i, _):
            granule_copy(slot, i, 0).wait()
            return 0

        lax.fori_loop(0, tile_granules(k), w, 0)

    for k0 in range(COMBINE_SLOTS - 1):
        @pl.when(k0 < n_tiles)
        def _():
            start_tile(k0)

    acc_ref[...] = jnp.zeros_like(acc_ref)

    def per_tile(k, _):
        @pl.when(k + (COMBINE_SLOTS - 1) < n_tiles)
        def _():
            start_tile(k + (COMBINE_SLOTS - 1))

        wait_tile(k)
        slot = k % COMBINE_SLOTS

        def clear(i, _):
            buf_ref[slot, pl.ds(pl.multiple_of(i * GRANULE, GRANULE), GRANULE), :] = jnp.zeros((GRANULE, D_MODEL), BF16)
            return 0

        lax.fori_loop(tile_granules(k), COMBINE_GPT, clear, 0)

        pick = pick_ref[pl.ds(pl.multiple_of(k * COMBINE_TILE, COMBINE_TILE), COMBINE_TILE), :]
        acc_ref[...] += lax.dot_general(pick, buf_ref[slot], (((0,), (0,)), ((), ())),
                                        preferred_element_type=F32)
        return 0

    lax.fori_loop(0, n_tiles, per_tile, 0)

    x = x1_ref[...] + gt_ref[...] * (acc_ref[...] + sh_ref[...])
    o_ref[...] = x * lax.rsqrt(jnp.mean(x * x, axis=-1, keepdims=True) + EPS) * gf_ref[...]


def _combine(cnt, goff, csm, x1, shared, mod4, g_final, ys):
    per_b = SEQ // CHUNK
    row = pl.BlockSpec((CHUNK, D_MODEL), lambda c, *_: (c, 0))
    grid_spec = pltpu.PrefetchScalarGridSpec(
        num_scalar_prefetch=2,
        grid=(N_CHUNKS,),
        in_specs=[pl.BlockSpec((N_EXPERTS, CHUNK), lambda c, *_: (0, c)),
                  row, row,
                  pl.BlockSpec((None, None, 1, D_MODEL), lambda c, *_: (c // per_b, 5, 0, 0)),
                  pl.BlockSpec((1, D_MODEL), lambda c, *_: (0, 0)),
                  pl.BlockSpec(memory_space=pl.ANY)],
        out_specs=row,
        scratch_shapes=[pltpu.SMEM((CHUNK_ROWS // GRANULE,), jnp.int32),
                        pltpu.VMEM((COMBINE_SLOTS, COMBINE_TILE, D_MODEL), BF16),
                        pltpu.VMEM((CHUNK, D_MODEL), F32),
                        pltpu.VMEM((CHUNK_ROWS, CHUNK), BF16),
                        pltpu.SemaphoreType.DMA((COMBINE_SLOTS,))])
    return pl.pallas_call(
        _combine_kernel,
        out_shape=jax.ShapeDtypeStruct((N_TOK, D_MODEL), F32),
        grid_spec=grid_spec,
        compiler_params=_cparams(("arbitrary",)),
        name="moe_combine",
    )(cnt, goff, csm, x1, shared, mod4, g_final, ys)


def _shared_kernel(x_ref, w1_ref, w3_ref, w2_ref, o_ref):
    x = x_ref[...]
    a = jnp.dot(x, w1_ref[...], preferred_element_type=F32)
    u = jnp.dot(x, w3_ref[...], preferred_element_type=F32)
    hid = (a * jax.nn.sigmoid(a)) * u
    o_ref[...] = jnp.dot(hid.astype(BF16), w2_ref[...], preferred_element_type=F32).astype(o_ref.dtype)


def _shared_expert(h2, w1, w3, w2):
    tm = 1024
    return pl.pallas_call(
        _shared_kernel,
        out_shape=jax.ShapeDtypeStruct((N_TOK, D_MODEL), F32),
        grid=(N_TOK // tm,),
        in_specs=[pl.BlockSpec((tm, D_MODEL), lambda m: (m, 0)),
                  pl.BlockSpec((D_MODEL, D_SHARED), lambda m: (0, 0)),
                  pl.BlockSpec((D_MODEL, D_SHARED), lambda m: (0, 0)),
                  pl.BlockSpec((D_SHARED, D_MODEL), lambda m: (0, 0))],
        out_specs=pl.BlockSpec((tm, D_MODEL), lambda m: (m, 0)),
        compiler_params=_cparams(("parallel",)),
        name="shared_expert",
    )(h2, w1, w3, w2)


def _rope_angles(pos, dims, theta):
    inv = jnp.power(jnp.float32(theta), -jnp.arange(0, dims, 2, dtype=jnp.float32) / dims)
    return pos.astype(jnp.float32)[:, None] * inv[None, :]


def _rotary_tables(angle_blocks):
    cos_parts, lo_parts, hi_parts = [], [], []
    used = 0
    for ang in angle_blocks:
        c, s = jnp.cos(ang), jnp.sin(ang)
        z = jnp.zeros_like(s)
        cos_parts += [c, c]
        lo_parts += [-s, z]
        hi_parts += [z, s]
        used += 2 * ang.shape[1]
    rest = HEAD_DIM - used
    if rest:
        cos_parts.append(jnp.ones((SEQ, rest), F32))
        lo_parts.append(jnp.zeros((SEQ, rest), F32))
        hi_parts.append(jnp.zeros((SEQ, rest), F32))
    return (jnp.concatenate(cos_parts, axis=1), jnp.concatenate(lo_parts, axis=1),
            jnp.concatenate(hi_parts, axis=1))


def kernel(x, c, w_ada, b_ada, g_attn, w_in, b_gate, q_norm_g, k_norm_g, w_a_up, w_b_up, w_out,
           g_ffn, w_router, e_bias, w1, w3, w2, ws1, ws3, ws2, g_final):
    l = 0
    x2 = x.reshape(N_TOK, D_MODEL)
    pos = jnp.arange(SEQ)
    tabs_a = _rotary_tables([_rope_angles(pos, ROPE_DIMS, ROPE_THETA)])
    tabs_b = _rotary_tables([_rope_angles(pos // GRID_W, AXIAL_DIMS, AXIAL_THETA),
                             _rope_angles(pos % GRID_W, AXIAL_DIMS, AXIAL_THETA)])
    tabs_a, tabs_b = lax.optimization_barrier((tabs_a, tabs_b))

    mod = _ada(c.T, w_ada[l], b_ada[l].reshape(1, -1))
    mod4 = mod.reshape(BATCH, N_MOD, 1, D_MODEL)

    perms = jnp.stack([_residue_perm(dilation) for _, dilation in A_PATTERNS])
    proj, gates = _projection(x2, g_attn[l].reshape(1, -1), mod4, w_in[l], tabs_a, tabs_b,
                              q_norm_g[l].reshape(1, -1), k_norm_g[l].reshape(1, -1), b_gate[l].reshape(1, -1),
                              perms)
    proj3 = proj.reshape(BATCH, SEQ, QKV_COLS)

    o_groups, lse_groups = [], []
    for gi, (window, dilation) in enumerate(A_PATTERNS):
        o, lse = _attn_a_dilated(proj3, gi, window, dilation)
        o_groups.append(o)
        lse_groups.append(lse)
    mix_b = _attn_b(proj3)

    merged = _merge(o_groups, lse_groups, mix_b, gates, w_a_up[l].astype(BF16), w_b_up[l].astype(BF16),
                    jnp.swapaxes(perms[1:], 1, 2))
    x1, h2, logits_t = _out_projection(merged, w_out[l].astype(BF16), x2, mod4, g_ffn[l].reshape(1, -1),
                                       w_router[l].T)
    csm, cnt_f, gx = _route(logits_t, e_bias[l].reshape(-1, 1))
    cnt = cnt_f[:, 0, :].astype(jnp.int32)
    goff, tail, tile_e, ntile, next_e = _plan(cnt)
    xs = _dispatch(cnt, goff, tail, ntile, h2, gx, csm)
    ys = _routed_experts(tile_e, ntile, next_e, xs, w1[l], w3[l], w2[l])
    shared = _shared_expert(h2, ws1[l].astype(BF16), ws3[l].astype(BF16), ws2[l].astype(BF16))
    out = _combine(cnt, goff, csm, x1, shared, mod4, g_final.reshape(1, -1), ys)
    return out.reshape(BATCH, SEQ, D_MODEL)
```

```python
import functools

import jax
import jax.numpy as jnp
from jax import lax
from jax.experimental import pallas as pl
from jax.experimental.pallas import tpu as pltpu

F32 = jnp.float32
BF16 = jnp.bfloat16

D_MODEL = 2048
BATCH = 2
SEQ = 4096
N_TOK = BATCH * SEQ
HEAD_DIM = 128
EPS = 1e-6
A_PATTERNS = ((128, 1), (512, 4), (2048, 16))
A_N_GROUPS = len(A_PATTERNS)
A_HEADS = 4
A_WIDTH = A_HEADS * HEAD_DIM
ROPE_THETA = 500000.0
ROPE_DIMS = HEAD_DIM // 4
B_Q_HEADS = 16
B_KV_HEADS = 4
B_GROUP = B_Q_HEADS // B_KV_HEADS
B_Q_WIDTH = B_Q_HEADS * HEAD_DIM
B_KV_WIDTH = B_KV_HEADS * HEAD_DIM
AXIAL_THETA = 10000.0
AXIAL_DIMS = HEAD_DIM // 2
GRID_W = 64
A_COLS = A_N_GROUPS * 3 * A_WIDTH
B_COLS = B_Q_WIDTH + 2 * B_KV_WIDTH
GATE_COLS = 2 * D_MODEL
QKV_COLS = A_COLS + B_COLS
IN_COLS = QKV_COLS + GATE_COLS
N_MOD = 6
N_EXPERTS = 64
N_EXPERT_GROUPS = 8
PER_GROUP = N_EXPERTS // N_EXPERT_GROUPS
TOPK_GROUPS = 4
TOP_K = 8
D_EXPERT = D_MODEL // 4
D_SHARED = D_MODEL // 4
ROUTED_SCALE = 2.5
NEG = -1e30
ATTN_SCALE = HEAD_DIM ** -0.5
LOG2E = 1.4426950408889634
LN2 = 0.6931471805599453

LANES = 128
SUBLANES = 8
VMEM_LIMIT = 56 * 1024 * 1024

COL_BLK = 512
PERM_BLOCK = 256
N_COL_BLKS = IN_COLS // COL_BLK
A_BLKS = A_COLS // COL_BLK
BQ_BLK0 = A_BLKS
BK_BLK = BQ_BLK0 + B_Q_WIDTH // COL_BLK
BV_BLK = BK_BLK + 1
GATE_BLK0 = BV_BLK + 1

CHUNK = 512
N_CHUNKS = N_TOK // CHUNK
GRANULE = 16
GRANULE_SHIFT = GRANULE.bit_length() - 1
TILE = 256
TILE_SHIFT = TILE.bit_length() - 1
GRAN_PER_TILE = TILE // GRANULE
GPT_SHIFT = TILE_SHIFT - GRANULE_SHIFT
N_SLOTS = 4
COMBINE_TILE = 1024
COMBINE_GPT = COMBINE_TILE // GRANULE
COMBINE_GPT_SHIFT = COMBINE_GPT.bit_length() - 1
COMBINE_SLOTS = 3
WEIGHT_DMA_PRIORITY = 1
XS_WIDTH = D_MODEL + LANES
CHUNK_ROWS = -(-(CHUNK * TOP_K + N_EXPERTS * (GRANULE - 1)) // TILE) * TILE
MAX_ROWS = -(-(N_TOK * TOP_K + N_EXPERTS * N_CHUNKS * (GRANULE - 1) + N_EXPERTS * (TILE - 1)) // TILE) * TILE
MAX_TILES = MAX_ROWS // TILE


def _cparams(semantics):
    return pltpu.CompilerParams(dimension_semantics=semantics, vmem_limit_bytes=VMEM_LIMIT)


def _ada_kernel(ct_ref, w_ref, b_ref, o_ref):
    ct = ct_ref[...]
    act = ct * jax.nn.sigmoid(ct)
    w = w_ref[...]
    for b in range(BATCH):
        o_ref[b:b + 1, :] = jnp.sum(w * act[:, b:b + 1], axis=0, keepdims=True) + b_ref[...]


def _ada(c_t, w_ada, b_ada):
    tn = 1024
    n_out = w_ada.shape[1]
    return pl.pallas_call(
        _ada_kernel,
        out_shape=jax.ShapeDtypeStruct((BATCH, n_out), F32),
        grid=(n_out // tn,),
        in_specs=[pl.BlockSpec((D_MODEL, BATCH), lambda n: (0, 0)),
                  pl.BlockSpec((D_MODEL, tn), lambda n: (0, n)),
                  pl.BlockSpec((1, tn), lambda n: (0, n))],
        out_specs=pl.BlockSpec((BATCH, tn), lambda n: (0, n)),
        compiler_params=_cparams(("parallel",)),
        name="ada_mod",
    )(c_t, w_ada, b_ada)


def _normmod(x, g, sc, sh):
    y = x * lax.rsqrt(jnp.mean(x * x, axis=-1, keepdims=True) + EPS)
    return (y * g) * (1.0 + sc) + sh


def _tile4(t):
    return jnp.concatenate([t, t, t, t], axis=1)


def _rotary(y, tabs, rows, shift):
    cos_ref, sin_lo_ref, sin_hi_ref = tabs
    width = y.shape[1]
    return (y * _tile4(cos_ref[rows, :])
            + pltpu.roll(y, width - shift, 1) * _tile4(sin_lo_ref[rows, :])
            + pltpu.roll(y, shift, 1) * _tile4(sin_hi_ref[rows, :]))


def _head_rmsnorm(y, g):
    outs = []
    for h in range(y.shape[1] // HEAD_DIM):
        yh = y[:, h * HEAD_DIM:(h + 1) * HEAD_DIM]
        outs.append(yh * lax.rsqrt(jnp.mean(yh * yh, axis=-1, keepdims=True) + EPS) * g)
    return jnp.concatenate(outs, axis=1)


def _proj_kernel(x_ref, g_ref, sc_ref, sh_ref, w_ref, ca_ref, sal_ref, sah_ref, cb_ref, sbl_ref, sbh_ref,
                 gain_ref, scale_ref, bg_ref, perm_ref, o_ref, gate_ref, h_ref, hp_ref):
    n = pl.program_id(1)
    is_a = n < A_BLKS
    part = n % 3
    tabs_a = (ca_ref, sal_ref, sah_ref)
    tabs_b = (cb_ref, sbl_ref, sbh_ref)
    sub_rows = PERM_BLOCK
    n_sub = h_ref.shape[0] // sub_rows

    @pl.when(n == 0)
    def _():
        def norm_rows(r, _):
            rows = pl.ds(pl.multiple_of(r * sub_rows, sub_rows), sub_rows)
            h = _normmod(x_ref[rows, :], g_ref[...], sc_ref[...], sh_ref[...]).astype(BF16)
            h_ref[rows, :] = h
            hp_ref[rows, :] = h
            return 0

        lax.fori_loop(0, n_sub, norm_rows, 0)

    @pl.when(is_a & (part == 0) & (n > 0))
    def _():
        for r in range(n_sub):
            rows = pl.ds(r * sub_rows, sub_rows)
            hp_ref[rows, :] = jnp.dot(perm_ref[...], h_ref[rows, :], preferred_element_type=F32).astype(BF16)

    def run(epilogue, dst_ref=o_ref, src_ref=h_ref):
        w = w_ref[...].astype(BF16)
        for r in range(n_sub):
            rows = pl.ds(r * sub_rows, sub_rows)
            acc = jnp.dot(src_ref[rows, :], w, preferred_element_type=F32)
            dst_ref[rows, :] = epilogue(acc, rows).astype(dst_ref.dtype)

    @pl.when(is_a & (part < 2))
    def _():
        run(lambda acc, rows: _rotary(acc, tabs_a, rows, ROPE_DIMS // 2) * scale_ref[...], src_ref=hp_ref)

    @pl.when(is_a & (part == 2))
    def _():
        run(lambda acc, rows: acc, src_ref=hp_ref)

    @pl.when((n >= BQ_BLK0) & (n <= BK_BLK))
    def _():
        run(lambda acc, rows: _rotary(_head_rmsnorm(acc, gain_ref[...]), tabs_b, rows, AXIAL_DIMS // 2)
            * scale_ref[...])

    @pl.when(n == BV_BLK)
    def _():
        run(lambda acc, rows: acc)

    @pl.when(n >= GATE_BLK0)
    def _():
        run(lambda acc, rows: jax.nn.sigmoid(acc + bg_ref[...]), gate_ref)


def _residue_source(dilation):
    per_res = PERM_BLOCK // dilation
    row = jnp.arange(PERM_BLOCK)
    return (row % per_res) * dilation + row // per_res


def _residue_perm(dilation):
    return (_residue_source(dilation)[:, None] == jnp.arange(PERM_BLOCK)[None, :]).astype(BF16)


def _projection(x2, g_attn, mod4, w_in, tabs_a, tabs_b, qg, kg, b_gate, perms):
    tm = 1024
    per_b = SEQ // tm

    def mod_spec(j):
        return pl.BlockSpec((None, None, 1, D_MODEL), lambda m, n: (m // per_b, j, 0, 0))

    tab_spec = pl.BlockSpec((tm, LANES), lambda m, n: (m % per_b, 0))
    n_gate_blks = GATE_COLS // COL_BLK
    is_query = [(blk < A_BLKS and blk % 3 == 0) or BQ_BLK0 <= blk < BK_BLK for blk in range(N_COL_BLKS)]
    col_scale = jnp.broadcast_to(jnp.where(jnp.array(is_query), ATTN_SCALE * LOG2E, 1.0).astype(F32)[:, None, None],
                                 (N_COL_BLKS, 1, COL_BLK))
    ones = jnp.ones_like(qg)
    col_gain = jnp.stack([qg if BQ_BLK0 <= blk < BK_BLK else kg if blk == BK_BLK else ones
                          for blk in range(N_COL_BLKS)])
    def residue_major(t, dilation):
        per_res = PERM_BLOCK // dilation
        t4 = t.reshape(SEQ // PERM_BLOCK, per_res, dilation, LANES)
        return jnp.swapaxes(t4, 1, 2).reshape(SEQ, LANES)

    tabs_a = [jnp.stack([residue_major(t, dilation) for _, dilation in A_PATTERNS]) for t in tabs_a]
    tab_a_spec = pl.BlockSpec((None, tm, LANES),
                              lambda m, n: (jnp.clip(n // 3, 0, A_N_GROUPS - 1), m % per_b, 0))

    def gate_blk(n):
        return jnp.clip(n - GATE_BLK0, 0, n_gate_blks - 1)

    return pl.pallas_call(
        _proj_kernel,
        out_shape=(jax.ShapeDtypeStruct((N_TOK, QKV_COLS), BF16),
                   jax.ShapeDtypeStruct((N_TOK, GATE_COLS), BF16)),
        grid=(N_TOK // tm, N_COL_BLKS),
        in_specs=[pl.BlockSpec((tm, D_MODEL), lambda m, n: (m, 0)),
                  pl.BlockSpec((1, D_MODEL), lambda m, n: (0, 0)),
                  mod_spec(1), mod_spec(0),
                  pl.BlockSpec((D_MODEL, COL_BLK), lambda m, n: (0, n)),
                  tab_a_spec, tab_a_spec, tab_a_spec, tab_spec, tab_spec, tab_spec,
                  pl.BlockSpec((None, 1, HEAD_DIM), lambda m, n: (n, 0, 0)),
                  pl.BlockSpec((None, 1, COL_BLK), lambda m, n: (n, 0, 0)),
                  pl.BlockSpec((1, COL_BLK), lambda m, n: (0, gate_blk(n))),
                  pl.BlockSpec((None, PERM_BLOCK, PERM_BLOCK),
                               lambda m, n: (jnp.clip(n // 3, 0, A_N_GROUPS - 1), 0, 0))],
        out_specs=(pl.BlockSpec((tm, COL_BLK), lambda m, n: (m, jnp.minimum(n, GATE_BLK0 - 1))),
                   pl.BlockSpec((tm, COL_BLK), lambda m, n: (m, gate_blk(n)))),
        scratch_shapes=[pltpu.VMEM((tm, D_MODEL), BF16), pltpu.VMEM((tm, D_MODEL), BF16)],
        compiler_params=_cparams(("parallel", "arbitrary")),
        name="in_projection",
    )(x2, g_attn, mod4, mod4, w_in, *tabs_a, *tabs_b, col_gain, col_scale, b_gate, perms)


def _attn_res_kernel(q_ref, k_ref, v_ref, o_ref, lse_ref, *, half_w):
    n_blk, per_res, _ = q_ref.shape
    length = n_blk * per_res
    q = q_ref[...].reshape(length, A_WIDTH)
    k = k_ref[...].reshape(length, A_WIDTH)
    v = v_ref[...].reshape(length, A_WIDTH)
    tq = 2 * half_w
    win = 4 * half_w
    lane = lax.broadcasted_iota(jnp.int32, (tq, LANES), 1)
    row_minus_col = lax.broadcasted_iota(jnp.int32, (tq, win), 0) - lax.broadcasted_iota(jnp.int32, (tq, win), 1)
    o_blocks, lse_blocks = [], []
    for qb in range(length // tq):
        start = min(max(qb * tq - half_w, 0), length - win)
        valid = jnp.abs(row_minus_col + (qb * tq - start)) <= half_w
        lse_tile = jnp.zeros((tq, LANES), F32)
        outs = []
        for h in range(A_HEADS):
            sl = slice(h * HEAD_DIM, (h + 1) * HEAD_DIM)
            s = lax.dot_general(q[qb * tq:(qb + 1) * tq, sl], k[start:start + win, sl],
                                (((1,), (1,)), ((), ())), preferred_element_type=F32)
            s = jnp.where(valid, s, NEG)
            m = jnp.max(s, axis=-1, keepdims=True)
            p = jnp.exp2(s - m)
            l = jnp.sum(p, axis=-1, keepdims=True)
            o = jnp.dot(p.astype(BF16), v[start:start + win, sl], preferred_element_type=F32)
            outs.append(o / l)
            lse_tile = jnp.where(lane == h, m * LN2 + jnp.log(l), lse_tile)
        o_blocks.append(jnp.concatenate(outs, axis=1).astype(o_ref.dtype))
        lse_blocks.append(lse_tile)
    o_ref[...] = jnp.concatenate(o_blocks, axis=0).reshape(n_blk, per_res, A_WIDTH)
    lse_ref[...] = jnp.concatenate(lse_blocks, axis=0).reshape(n_blk, per_res, LANES)


def _attn_a_dilated(proj3, group, window, dilation):
    per_res = PERM_BLOCK // dilation
    n_blk = SEQ // PERM_BLOCK
    half_w = (window // 2) // dilation
    view = proj3.reshape(BATCH, n_blk, dilation, per_res, QKV_COLS)

    def spec(width, blk):
        return pl.BlockSpec((None, n_blk, None, per_res, width), lambda b, r: (b, 0, r, 0, blk))

    o, lse = pl.pallas_call(
        functools.partial(_attn_res_kernel, half_w=half_w),
        out_shape=(jax.ShapeDtypeStruct((BATCH, n_blk, dilation, per_res, A_WIDTH), BF16),
                   jax.ShapeDtypeStruct((BATCH, n_blk, dilation, per_res, LANES), F32)),
        grid=(BATCH, dilation),
        in_specs=[spec(COL_BLK, 3 * group), spec(COL_BLK, 3 * group + 1), spec(COL_BLK, 3 * group + 2)],
        out_specs=(spec(A_WIDTH, 0), spec(LANES, 0)),
        compiler_params=_cparams(("parallel", "parallel")),
        name=f"dilated_attention_g{group}",
    )(view, view, view)
    return o.reshape(N_TOK, A_WIDTH), lse.reshape(N_TOK, LANES)


def _attn_b_kernel(q_ref, k_ref, v_ref, o_ref, vx_ref, acc_ref, *, tk):
    tq = q_ref.shape[0]
    n_chunks = SEQ // tk

    @pl.when(pl.program_id(2) == 0)
    def _():
        vx_ref[:, 0:HEAD_DIM] = v_ref[...]
        vx_ref[:, HEAD_DIM:2 * HEAD_DIM] = jnp.ones((SEQ, HEAD_DIM), BF16)

    q = q_ref[...]
    qs = jnp.concatenate([q[:, g * HEAD_DIM:(g + 1) * HEAD_DIM] for g in range(B_GROUP)], axis=0)
    acc_ref[...] = jnp.zeros_like(acc_ref)
    m = jnp.full((B_GROUP * tq, 1), -jnp.inf, F32)
    for c in range(n_chunks):
        keys = slice(c * tk, (c + 1) * tk)
        s = lax.dot_general(qs, k_ref[keys, :], (((1,), (1,)), ((), ())), preferred_element_type=F32)
        m_new = jnp.maximum(m, jnp.max(s, axis=-1, keepdims=True))
        p = jnp.exp2(s - m_new).astype(BF16)
        acc_ref[...] = jnp.exp2(m - m_new) * acc_ref[...] + jnp.dot(p, vx_ref[keys, :],
                                                                    preferred_element_type=F32)
        m = m_new
    o = acc_ref[:, 0:HEAD_DIM] / acc_ref[:, HEAD_DIM:2 * HEAD_DIM]
    o_ref[...] = jnp.concatenate([o[g * tq:(g + 1) * tq] for g in range(B_GROUP)], axis=1).astype(o_ref.dtype)


def _attn_b(proj3):
    tq = 256
    tk = 256
    kcol0 = BK_BLK * COL_BLK // HEAD_DIM
    vcol0 = BV_BLK * COL_BLK // HEAD_DIM
    o = pl.pallas_call(
        functools.partial(_attn_b_kernel, tk=tk),
        out_shape=jax.ShapeDtypeStruct((BATCH, SEQ, B_Q_WIDTH), BF16),
        grid=(BATCH, B_KV_HEADS, SEQ // tq),
        in_specs=[pl.BlockSpec((None, tq, COL_BLK), lambda b, h, i: (b, i, BQ_BLK0 + h)),
                  pl.BlockSpec((None, SEQ, HEAD_DIM), lambda b, h, i: (b, 0, kcol0 + h)),
                  pl.BlockSpec((None, SEQ, HEAD_DIM), lambda b, h, i: (b, 0, vcol0 + h))],
        out_specs=pl.BlockSpec((None, tq, COL_BLK), lambda b, h, i: (b, i, h)),
        scratch_shapes=[pltpu.VMEM((SEQ, 2 * HEAD_DIM), BF16),
                        pltpu.VMEM((B_GROUP * tq, 2 * HEAD_DIM), F32)],
        compiler_params=_cparams(("parallel", "parallel", "arbitrary")),
        name="gqa_attention",
    )(proj3, proj3, proj3)
    return o.reshape(N_TOK, B_Q_WIDTH)


def _merge_kernel(o0_ref, o1_ref, o2_ref, l0_ref, l1_ref, l2_ref, yb_ref, wa_ref, wb_ref, ga_ref, gb_ref,
                  unperm_ref, out_ref):
    sub_rows = PERM_BLOCK
    for r in range(out_ref.shape[0] // sub_rows):
        rows = pl.ds(r * sub_rows, sub_rows)
        outs = [o0_ref[rows, :].astype(F32)]
        lses = [l0_ref[rows, :]]
        for g, (o_ref, l_ref) in enumerate(((o1_ref, l1_ref), (o2_ref, l2_ref))):
            outs.append(jnp.dot(unperm_ref[g], o_ref[rows, :], preferred_element_type=F32))
            lses.append(jnp.dot(unperm_ref[g].astype(F32), l_ref[rows, :], preferred_element_type=F32,
                                precision=lax.Precision.HIGHEST))
        cols = []
        for h in range(A_HEADS):
            lh = [jnp.broadcast_to(l[:, h:h + 1], (sub_rows, HEAD_DIM)) for l in lses]
            mx = jnp.maximum(jnp.maximum(lh[0], lh[1]), lh[2])
            e = [jnp.exp(v - mx) for v in lh]
            den = e[0] + e[1] + e[2]
            sl = slice(h * HEAD_DIM, (h + 1) * HEAD_DIM)
            cols.append(sum((e[g] / den) * outs[g][:, sl] for g in range(A_N_GROUPS)))
        mix_a = jnp.concatenate(cols, axis=1).astype(BF16)
        mix_b = yb_ref[rows, :]
        for cb in range(D_MODEL // COL_BLK):
            cs = pl.ds(cb * COL_BLK, COL_BLK)
            ya = jnp.dot(mix_a, wa_ref[:, cs], preferred_element_type=F32)
            yb = jnp.dot(mix_b, wb_ref[:, cs], preferred_element_type=F32)
            out_ref[rows, cs] = (ga_ref[rows, cs].astype(F32) * ya
                                 + gb_ref[rows, cs].astype(F32) * yb).astype(out_ref.dtype)


def _merge(o_groups, lse_groups, mix_b, gates, w_a_up, w_b_up, unperms):
    tm = 512
    o_spec = pl.BlockSpec((tm, A_WIDTH), lambda m: (m, 0))
    l_spec = pl.BlockSpec((tm, LANES), lambda m: (m, 0))
    return pl.pallas_call(
        _merge_kernel,
        out_shape=jax.ShapeDtypeStruct((N_TOK, D_MODEL), BF16),
        grid=(N_TOK // tm,),
        in_specs=[o_spec, o_spec, o_spec, l_spec, l_spec, l_spec,
                  pl.BlockSpec((tm, B_Q_WIDTH), lambda m: (m, 0)),
                  pl.BlockSpec((A_WIDTH, D_MODEL), lambda m: (0, 0)),
                  pl.BlockSpec((B_Q_WIDTH, D_MODEL), lambda m: (0, 0)),
                  pl.BlockSpec((tm, D_MODEL), lambda m: (m, 0)),
                  pl.BlockSpec((tm, D_MODEL), lambda m: (m, 1)),
                  pl.BlockSpec((A_N_GROUPS - 1, PERM_BLOCK, PERM_BLOCK), lambda m: (0, 0, 0))],
        out_specs=pl.BlockSpec((tm, D_MODEL), lambda m: (m, 0)),
        compiler_params=_cparams(("parallel",)),
        name="branch_merge",
    )(*o_groups, *lse_groups, mix_b, w_a_up, w_b_up, gates, gates, unperms)


def _outproj_kernel(mg_ref, w_ref, x_ref, gt_ref, g_ref, sc_ref, sh_ref, wr_ref, x1_ref, h2_ref, lg_ref):
    sub_rows = 256
    wr = wr_ref[...]
    wr_hi = wr.astype(BF16)
    wr_lo = (wr - wr_hi.astype(F32)).astype(BF16)
    for r in range(mg_ref.shape[0] // sub_rows):
        rows = pl.ds(r * sub_rows, sub_rows)
        y = jnp.dot(mg_ref[rows, :], w_ref[...], preferred_element_type=F32)
        x1 = x_ref[rows, :] + gt_ref[...] * y
        x1_ref[rows, :] = x1
        h2 = _normmod(x1, g_ref[...], sc_ref[...], sh_ref[...])
        h2_hi = h2.astype(BF16)
        h2_ref[rows, :] = h2_hi
        h2_lo = (h2 - h2_hi.astype(F32)).astype(BF16)
        nt = (((1,), (1,)), ((), ()))
        lg_ref[:, rows] = (lax.dot_general(wr_hi, h2_hi, nt, preferred_element_type=F32)
                           + lax.dot_general(wr_hi, h2_lo, nt, preferred_element_type=F32)
                           + lax.dot_general(wr_lo, h2_hi, nt, preferred_element_type=F32))


def _out_projection(merged, w_out, x2, mod4, g_ffn, w_router_t):
    tm = 512
    per_b = SEQ // tm

    def mod_spec(j):
        return pl.BlockSpec((None, None, 1, D_MODEL), lambda m: (m // per_b, j, 0, 0))

    return pl.pallas_call(
        _outproj_kernel,
        out_shape=(jax.ShapeDtypeStruct((N_TOK, D_MODEL), F32),
                   jax.ShapeDtypeStruct((N_TOK, D_MODEL), BF16),
                   jax.ShapeDtypeStruct((N_EXPERTS, N_TOK), F32)),
        grid=(N_TOK // tm,),
        in_specs=[pl.BlockSpec((tm, D_MODEL), lambda m: (m, 0)),
                  pl.BlockSpec((D_MODEL, D_MODEL), lambda m: (0, 0)),
                  pl.BlockSpec((tm, D_MODEL), lambda m: (m, 0)),
                  mod_spec(2),
                  pl.BlockSpec((1, D_MODEL), lambda m: (0, 0)),
                  mod_spec(4), mod_spec(3),
                  pl.BlockSpec((N_EXPERTS, D_MODEL), lambda m: (0, 0))],
        out_specs=(pl.BlockSpec((tm, D_MODEL), lambda m: (m, 0)),
                   pl.BlockSpec((tm, D_MODEL), lambda m: (m, 0)),
                   pl.BlockSpec((N_EXPERTS, tm), lambda m: (0, m))),
        compiler_params=_cparams(("parallel",)),
        name="out_projection",
    )(merged, w_out, x2, mod4, g_ffn, mod4, mod4, w_router_t)


def _route_kernel(lg_ref, eb_ref, csm_ref, cnt_ref, gx_ref):
    tt = lg_ref.shape[1]
    scores = jax.nn.sigmoid(lg_ref[...])
    sel = scores + eb_ref[...]
    neg_inf = -jnp.inf
    sel_g = [sel[PER_GROUP * g:PER_GROUP * (g + 1), :] for g in range(N_EXPERT_GROUPS)]
    grp = []
    for v in sel_g:
        m1 = jnp.max(v, axis=0, keepdims=True)
        is1 = v == m1
        n1 = jnp.sum(jnp.where(is1, 1.0, 0.0), axis=0, keepdims=True)
        rest = jnp.max(jnp.where(is1, neg_inf, v), axis=0, keepdims=True)
        grp.append(m1 + jnp.where(n1 >= 2.0, m1, rest))
    masked = []
    for g in range(N_EXPERT_GROUPS):
        rank = jnp.zeros((1, tt), F32)
        for g2 in range(N_EXPERT_GROUPS):
            if g2 != g:
                beats = (grp[g2] >= grp[g]) if g2 < g else (grp[g2] > grp[g])
                rank = rank + jnp.where(beats, 1.0, 0.0)
        keep = jnp.broadcast_to(rank < TOPK_GROUPS, (PER_GROUP, tt))
        masked.append(jnp.where(keep, sel_g[g], neg_inf))
    sub = lax.broadcasted_iota(jnp.int32, (PER_GROUP, tt), 0)
    ranks = [jnp.zeros((PER_GROUP, tt), F32) for _ in range(N_EXPERT_GROUPS)]
    for g2 in range(N_EXPERT_GROUPS):
        for m2 in range(PER_GROUP):
            vf = jnp.broadcast_to(masked[g2][m2:m2 + 1, :], (PER_GROUP, tt))
            for g in range(N_EXPERT_GROUPS):
                if g2 < g:
                    beats = vf >= masked[g]
                elif g2 > g:
                    beats = vf > masked[g]
                else:
                    beats = (vf > masked[g]) | ((vf == masked[g]) & (sub > m2))
                ranks[g] = ranks[g] + jnp.where(beats, 1.0, 0.0)
    picked = [ranks[g] < TOP_K for g in range(N_EXPERT_GROUPS)]
    chosen = [jnp.where(picked[g], scores[PER_GROUP * g:PER_GROUP * (g + 1), :], 0.0)
              for g in range(N_EXPERT_GROUPS)]
    total = chosen[0]
    for g in range(1, N_EXPERT_GROUPS):
        total = total + chosen[g]
    denom = jnp.sum(total, axis=0, keepdims=True)
    gate = jnp.concatenate([chosen[g] / denom * ROUTED_SCALE for g in range(N_EXPERT_GROUPS)], axis=0)
    mask = jnp.concatenate([jnp.where(picked[g], 1.0, 0.0) for g in range(N_EXPERT_GROUPS)], axis=0)
    mask_bf = mask.astype(BF16)

    upper = jnp.where(lax.broadcasted_iota(jnp.int32, (tt, tt), 0) <= lax.broadcasted_iota(jnp.int32, (tt, tt), 1),
                      1.0, 0.0).astype(BF16)
    cs = jnp.dot(mask_bf, upper, preferred_element_type=F32)
    csm_ref[...] = jnp.where(mask > 0.0, cs, 0.0)
    cnt_ref[...] = lax.dot_general(jnp.ones((SUBLANES, tt), BF16), mask_bf, (((1,), (1,)), ((), ())),
                                   preferred_element_type=F32)

    hi = gate.astype(BF16).astype(F32)
    gx_ref[...] = jnp.concatenate([hi, gate - hi], axis=0).T.astype(BF16)


def _route(logits_t, e_bias_col):
    return pl.pallas_call(
        _route_kernel,
        out_shape=(jax.ShapeDtypeStruct((N_EXPERTS, N_TOK), F32),
                   jax.ShapeDtypeStruct((N_CHUNKS, SUBLANES, N_EXPERTS), F32),
                   jax.ShapeDtypeStruct((N_TOK, LANES), BF16)),
        grid=(N_CHUNKS,),
        in_specs=[pl.BlockSpec((N_EXPERTS, CHUNK), lambda c: (0, c)),
                  pl.BlockSpec((N_EXPERTS, 1), lambda c: (0, 0))],
        out_specs=(pl.BlockSpec((N_EXPERTS, CHUNK), lambda c: (0, c)),
                   pl.BlockSpec((None, SUBLANES, N_EXPERTS), lambda c: (c, 0, 0)),
                   pl.BlockSpec((CHUNK, LANES), lambda c: (c, 0))),
        compiler_params=_cparams(("parallel",)),
        name="routing",
    )(logits_t, e_bias_col)


def _plan_kernel(cnt_ref, goff_ref, tail_ref, tile_e_ref, ntile_ref, next_ref):
    def per_expert(e, carry):
        base, tbase, prev = carry

        def per_chunk(c, off):
            goff_ref[c, e] = base + off
            return off + (((cnt_ref[c, e] + (GRANULE - 1)) >> GRANULE_SHIFT) << GRANULE_SHIFT)

        tot = lax.fori_loop(0, N_CHUNKS, per_chunk, jnp.int32(0))
        region = ((tot + (TILE - 1)) >> TILE_SHIFT) << TILE_SHIFT
        tail_ref[0, e] = base + tot
        tail_ref[1, e] = (region - tot) >> GRANULE_SHIFT
        n_t = region >> TILE_SHIFT

        def per_tile(j, _):
            tile_e_ref[tbase + j] = e
            return 0

        lax.fori_loop(0, n_t, per_tile, 0)
        next_ref[e] = jnp.int32(-1)

        @pl.when((n_t > 0) & (prev >= 0))
        def _():
            next_ref[jnp.maximum(prev, 0)] = e

        return base + region, tbase + n_t, jnp.where(n_t > 0, e, prev)

    _, n_tiles, _ = lax.fori_loop(0, N_EXPERTS, per_expert, (jnp.int32(0), jnp.int32(0), jnp.int32(-1)))
    ntile_ref[0] = n_tiles
    last_e = tile_e_ref[jnp.maximum(n_tiles - 1, 0)]

    def fill(j, _):
        tile_e_ref[j] = last_e
        return 0

    lax.fori_loop(n_tiles, MAX_TILES, fill, 0)


def _plan(cnt):
    smem = pl.BlockSpec(memory_space=pltpu.SMEM)
    return pl.pallas_call(
        _plan_kernel,
        out_shape=(jax.ShapeDtypeStruct((N_CHUNKS, N_EXPERTS), jnp.int32),
                   jax.ShapeDtypeStruct((2, N_EXPERTS), jnp.int32),
                   jax.ShapeDtypeStruct((MAX_TILES,), jnp.int32),
                   jax.ShapeDtypeStruct((1,), jnp.int32),
                   jax.ShapeDtypeStruct((N_EXPERTS,), jnp.int32)),
        in_specs=[smem],
        out_specs=(smem, smem, smem, smem, smem),
        name="row_plan",
    )(cnt)


def _chunk_granules(cnt_ref, goff_ref, c, table_ref, per_granule=None):
    def per_expert(e, n_before):
        n_gran = (cnt_ref[c, e] + (GRANULE - 1)) >> GRANULE_SHIFT
        base = goff_ref[c, e]

        def per_j(j, _):
            table_ref[n_before + j] = base + j * GRANULE
            if per_granule is not None:
                per_granule(e, j, n_before + j)
            return 0

        lax.fori_loop(0, n_gran, per_j, 0)
        return n_before + n_gran

    return lax.fori_loop(0, N_EXPERTS, per_expert, jnp.int32(0))


def _dispatch_kernel(cnt_ref, goff_ref, tail_ref, ntile_ref, x_ref, gx_ref, csm_ref, xs_hbm,
                     onehot_ref, table_ref, buf_ref, zero_ref, xcat_ref, ztile_ref, sems, zsem, usem):
    c = pl.program_id(0)
    sub = lax.broadcasted_iota(jnp.int32, (GRANULE, CHUNK), 0)

    def unused_copy(t):
        return pltpu.make_async_copy(ztile_ref, xs_hbm.at[pl.ds(pl.multiple_of(t * TILE, TILE), TILE), :], usem)

    @pl.when(c == 0)
    def _():
        ztile_ref[...] = jnp.zeros_like(ztile_ref)

        def start_unused(t, _):
            unused_copy(t).start(priority=WEIGHT_DMA_PRIORITY)
            return 0

        lax.fori_loop(ntile_ref[0], MAX_TILES, start_unused, 0)

    xcat_ref[:, 0:D_MODEL] = x_ref[...]
    xcat_ref[:, D_MODEL:XS_WIDTH] = gx_ref[...]

    def build_granule(e, j, g):
        want = (sub + (j * GRANULE + 1)).astype(F32)
        hit = csm_ref[pl.ds(e, 1), :] == want
        onehot_ref[pl.ds(pl.multiple_of(g * GRANULE, GRANULE), GRANULE), :] = jnp.where(hit, 1.0, 0.0).astype(BF16)

    n_gran = _chunk_granules(cnt_ref, goff_ref, c, table_ref, build_granule)
    n_tiles = (n_gran + (GRAN_PER_TILE - 1)) >> GPT_SHIFT

    def clear(g, _):
        onehot_ref[pl.ds(pl.multiple_of(g * GRANULE, GRANULE), GRANULE), :] = jnp.zeros((GRANULE, CHUNK), BF16)
        return 0

    lax.fori_loop(n_gran, n_tiles * GRAN_PER_TILE, clear, 0)

    def granule_copy(slot, i, row):
        return pltpu.make_async_copy(buf_ref.at[slot, pl.ds(pl.multiple_of(i * GRANULE, GRANULE), GRANULE), :],
                                     xs_hbm.at[pl.ds(pl.multiple_of(row, GRANULE), GRANULE), :],
                                     sems.at[slot])

    def tile_granules(k):
        return jnp.minimum(GRAN_PER_TILE, n_gran - k * GRAN_PER_TILE)

    def wait_tile(k):
        slot = k % N_SLOTS

        def w(i, _):
            granule_copy(slot, i, 0).wait()
            return 0

        lax.fori_loop(0, tile_granules(k), w, 0)

    def per_tile(k, _):
        slot = k % N_SLOTS

        @pl.when(k >= N_SLOTS)
        def _():
            wait_tile(k - N_SLOTS)

        oh = onehot_ref[pl.ds(pl.multiple_of(k * TILE, TILE), TILE), :]
        buf_ref[slot] = jnp.dot(oh, xcat_ref[...], preferred_element_type=F32).astype(BF16)

        def s(i, _):
            granule_copy(slot, i, table_ref[k * GRAN_PER_TILE + i]).start()
            return 0

        lax.fori_loop(0, tile_granules(k), s, 0)
        return 0

    lax.fori_loop(0, n_tiles, per_tile, 0)

    def drain(k, _):
        wait_tile(k)
        return 0

    lax.fori_loop(jnp.maximum(n_tiles - N_SLOTS, 0), n_tiles, drain, 0)

    @pl.when(c == N_CHUNKS - 1)
    def _():
        zero_ref[...] = jnp.zeros_like(zero_ref)

        def tail_copy(row):
            return pltpu.make_async_copy(zero_ref, xs_hbm.at[pl.ds(pl.multiple_of(row, GRANULE), GRANULE), :], zsem)

        def start_e(e, _):
            def st(j, _):
                tail_copy(tail_ref[0, e] + j * GRANULE).start()
                return 0
            lax.fori_loop(0, tail_ref[1, e], st, 0)
            return 0

        def wait_e(e, _):
            def wt(j, _):
                tail_copy(0).wait()
                return 0
            lax.fori_loop(0, tail_ref[1, e], wt, 0)
            return 0

        lax.fori_loop(0, N_EXPERTS, start_e, 0)
        lax.fori_loop(0, N_EXPERTS, wait_e, 0)

        def wait_unused(t, _):
            unused_copy(t).wait()
            return 0

        lax.fori_loop(ntile_ref[0], MAX_TILES, wait_unused, 0)


def _dispatch(cnt, goff, tail, ntile, h2, gx, csm):
    grid_spec = pltpu.PrefetchScalarGridSpec(
        num_scalar_prefetch=4,
        grid=(N_CHUNKS,),
        in_specs=[pl.BlockSpec((CHUNK, D_MODEL), lambda c, *_: (c, 0)),
                  pl.BlockSpec((CHUNK, LANES), lambda c, *_: (c, 0)),
                  pl.BlockSpec((N_EXPERTS, CHUNK), lambda c, *_: (0, c))],
        out_specs=pl.BlockSpec(memory_space=pl.ANY),
        scratch_shapes=[pltpu.VMEM((CHUNK_ROWS, CHUNK), BF16),
                        pltpu.SMEM((CHUNK_ROWS // GRANULE,), jnp.int32),
                        pltpu.VMEM((N_SLOTS, TILE, XS_WIDTH), BF16),
                        pltpu.VMEM((GRANULE, XS_WIDTH), BF16),
                        pltpu.VMEM((CHUNK, XS_WIDTH), BF16),
                        pltpu.VMEM((TILE, XS_WIDTH), BF16),
                        pltpu.SemaphoreType.DMA((N_SLOTS,)),
                        pltpu.SemaphoreType.DMA(()),
                        pltpu.SemaphoreType.DMA(())])
    return pl.pallas_call(
        _dispatch_kernel,
        out_shape=jax.ShapeDtypeStruct((MAX_ROWS, XS_WIDTH), BF16),
        grid_spec=grid_spec,
        compiler_params=_cparams(("arbitrary",)),
        name="moe_dispatch",
    )(cnt, goff, tail, ntile, h2, gx, csm)


def _routed_kernel(tile_e_ref, ntile_ref, next_ref, xs_hbm, w1_hbm, w3_hbm, w2_hbm, ys_hbm,
                   xbuf_ref, ybuf_ref, w1f_ref, w3f_ref, w2f_ref, w1s_ref, w3s_ref, w2s_ref, ztile_ref,
                   xsem, ysem, wsem, usem):
    n_tiles = ntile_ref[0]

    def x_copy(t, slot):
        return pltpu.make_async_copy(xs_hbm.at[pl.ds(pl.multiple_of(t * TILE, TILE), TILE), :],
                                     xbuf_ref.at[slot], xsem.at[slot])

    def y_copy(t, slot):
        return pltpu.make_async_copy(ybuf_ref.at[slot],
                                     ys_hbm.at[pl.ds(pl.multiple_of(t * TILE, TILE), TILE), :], ysem.at[slot])

    def fetch(expert, slot):
        return (pltpu.make_async_copy(w1_hbm.at[expert], w1f_ref.at[slot], wsem.at[slot]),
                pltpu.make_async_copy(w3_hbm.at[expert], w3f_ref.at[slot], wsem.at[slot]),
                pltpu.make_async_copy(w2_hbm.at[expert], w2f_ref.at[slot], wsem.at[slot]))

    @pl.when(n_tiles > 0)
    def _():
        x_copy(0, 0).start()
        for cp in fetch(tile_e_ref[0], 0):
            cp.start()

    def unused_copy(t):
        return pltpu.make_async_copy(ztile_ref, ys_hbm.at[pl.ds(pl.multiple_of(t * TILE, TILE), TILE), :], usem)

    ztile_ref[...] = jnp.zeros_like(ztile_ref)

    def start_unused(t, _):
        unused_copy(t).start(priority=WEIGHT_DMA_PRIORITY)
        return 0

    def per_tile(t, n_started):
        slot = t & 1
        e = tile_e_ref[t]
        x_copy(t, slot).wait()

        @pl.when(n_tiles + t < MAX_TILES)
        def _():
            start_unused(n_tiles + t, 0)

        @pl.when(t + 1 < n_tiles)
        def _():
            x_copy(t + 1, 1 - slot).start()

        first = (t == 0) | (e != tile_e_ref[jnp.maximum(t - 1, 0)])

        @pl.when(first)
        def _():
            wslot = n_started & 1
            nxt = next_ref[e]

            @pl.when(nxt >= 0)
            def _():
                for cp in fetch(jnp.maximum(nxt, 0), 1 - wslot):
                    cp.start(priority=WEIGHT_DMA_PRIORITY)

            for cp in fetch(e, wslot):
                cp.wait()
            w1s_ref[...] = w1f_ref[wslot].astype(BF16)
            w3s_ref[...] = w3f_ref[wslot].astype(BF16)
            w2s_ref[...] = w2f_ref[wslot].astype(BF16)

        @pl.when(t >= 2)
        def _():
            y_copy(t - 2, slot).wait()

        x = xbuf_ref[slot, :, 0:D_MODEL]
        gx = xbuf_ref[slot, :, D_MODEL:XS_WIDTH].astype(F32)
        lane = lax.broadcasted_iota(jnp.int32, gx.shape, 1)
        gate = jnp.sum(jnp.where((lane == e) | (lane == e + N_EXPERTS), gx, 0.0), axis=1, keepdims=True)
        a = jnp.dot(x, w1s_ref[...], preferred_element_type=F32)
        u = jnp.dot(x, w3s_ref[...], preferred_element_type=F32)
        hid = (a * jax.nn.sigmoid(a)) * u * gate
        ybuf_ref[slot] = jnp.dot(hid.astype(BF16), w2s_ref[...], preferred_element_type=F32).astype(ybuf_ref.dtype)
        y_copy(t, slot).start()
        return n_started + jnp.where(first, 1, 0)

    lax.fori_loop(0, n_tiles, per_tile, jnp.int32(0))

    @pl.when(n_tiles >= 2)
    def _():
        y_copy(n_tiles - 2, n_tiles & 1).wait()

    @pl.when(n_tiles >= 1)
    def _():
        y_copy(n_tiles - 1, (n_tiles - 1) & 1).wait()

    def wait_unused(t, _):
        unused_copy(t).wait()
        return 0

    lax.fori_loop(jnp.minimum(2 * n_tiles, MAX_TILES), MAX_TILES, start_unused, 0)
    lax.fori_loop(n_tiles, MAX_TILES, wait_unused, 0)


def _routed_experts(tile_e, ntile, next_e, xs, w1, w3, w2):
    hbm = pl.BlockSpec(memory_space=pl.ANY)
    grid_spec = pltpu.PrefetchScalarGridSpec(
        num_scalar_prefetch=3,
        grid=(1,),
        in_specs=[hbm, hbm, hbm, hbm],
        out_specs=hbm,
        scratch_shapes=[pltpu.VMEM((2, TILE, XS_WIDTH), BF16),
                        pltpu.VMEM((2, TILE, D_MODEL), BF16),
                        pltpu.VMEM((2, D_MODEL, D_EXPERT), F32),
                        pltpu.VMEM((2, D_MODEL, D_EXPERT), F32),
                        pltpu.VMEM((2, D_EXPERT, D_MODEL), F32),
                        pltpu.VMEM((D_MODEL, D_EXPERT), BF16),
                        pltpu.VMEM((D_MODEL, D_EXPERT), BF16),
                        pltpu.VMEM((D_EXPERT, D_MODEL), BF16),
                        pltpu.VMEM((TILE, D_MODEL), BF16),
                        pltpu.SemaphoreType.DMA((2,)),
                        pltpu.SemaphoreType.DMA((2,)),
                        pltpu.SemaphoreType.DMA((2,)),
                        pltpu.SemaphoreType.DMA(())])
    return pl.pallas_call(
        _routed_kernel,
        out_shape=jax.ShapeDtypeStruct((MAX_ROWS, D_MODEL), BF16),
        grid_spec=grid_spec,
        compiler_params=_cparams(("arbitrary",)),
        name="routed_experts",
    )(tile_e, ntile, next_e, xs, w1, w3, w2)


def _combine_kernel(cnt_ref, goff_ref, csm_ref, x1_ref, sh_ref, gt_ref, gf_ref, ys_hbm, o_ref,
                    table_ref, buf_ref, acc_ref, pick_ref, sems):
    c = pl.program_id(0)
    sub = lax.broadcasted_iota(jnp.int32, (GRANULE, CHUNK), 0)

    def build_granule(e, j, g):
        want = (sub + (j * GRANULE + 1)).astype(F32)
        hit = csm_ref[pl.ds(e, 1), :] == want
        pick_ref[pl.ds(pl.multiple_of(g * GRANULE, GRANULE), GRANULE), :] = jnp.where(hit, 1.0, 0.0).astype(BF16)

    n_gran = _chunk_granules(cnt_ref, goff_ref, c, table_ref, build_granule)
    n_tiles = (n_gran + (COMBINE_GPT - 1)) >> COMBINE_GPT_SHIFT

    def clear_pick(g, _):
        pick_ref[pl.ds(pl.multiple_of(g * GRANULE, GRANULE), GRANULE), :] = jnp.zeros((GRANULE, CHUNK), BF16)
        return 0

    lax.fori_loop(n_gran, n_tiles * COMBINE_GPT, clear_pick, 0)

    def granule_copy(slot, i, row):
        return pltpu.make_async_copy(ys_hbm.at[pl.ds(pl.multiple_of(row, GRANULE), GRANULE), :],
                                     buf_ref.at[slot, pl.ds(pl.multiple_of(i * GRANULE, GRANULE), GRANULE), :],
                                     sems.at[slot])

    def tile_granules(k):
        return jnp.minimum(COMBINE_GPT, n_gran - k * COMBINE_GPT)

    def start_tile(k):
        slot = k % COMBINE_SLOTS

        def s(i, _):
            granule_copy(slot, i, table_ref[k * COMBINE_GPT + i]).start()
            return 0

        lax.fori_loop(0, tile_granules(k), s, 0)

    def wait_tile(k):
        slot = k % COMBINE_SLOTS

        def w(i, _):
            granule_copy(slot, i, 0).wait()
            return 0

        lax.fori_loop(0, tile_granules(k), w, 0)

    for k0 in range(COMBINE_SLOTS - 1):
        @pl.when(k0 < n_tiles)
        def _():
            start_tile(k0)

    acc_ref[...] = jnp.zeros_like(acc_ref)

    def per_tile(k, _):
        @pl.when(k + (COMBINE_SLOTS - 1) < n_tiles)
        def _():
            start_tile(k + (COMBINE_SLOTS - 1))

        wait_tile(k)
        slot = k % COMBINE_SLOTS

        def clear(i, _):
            buf_ref[slot, pl.ds(pl.multiple_of(i * GRANULE, GRANULE), GRANULE), :] = jnp.zeros((GRANULE, D_MODEL), BF16)
            return 0

        lax.fori_loop(tile_granules(k), COMBINE_GPT, clear, 0)

        pick = pick_ref[pl.ds(pl.multiple_of(k * COMBINE_TILE, COMBINE_TILE), COMBINE_TILE), :]
        acc_ref[...] += lax.dot_general(pick, buf_ref[slot], (((0,), (0,)), ((), ())),
                                        preferred_element_type=F32)
        return 0

    lax.fori_loop(0, n_tiles, per_tile, 0)

    x = x1_ref[...] + gt_ref[...] * (acc_ref[...] + sh_ref[...])
    o_ref[...] = x * lax.rsqrt(jnp.mean(x * x, axis=-1, keepdims=True) + EPS) * gf_ref[...]


def _combine(cnt, goff, csm, x1, shared, mod4, g_final, ys):
    per_b = SEQ // CHUNK
    row = pl.BlockSpec((CHUNK, D_MODEL), lambda c, *_: (c, 0))
    grid_spec = pltpu.PrefetchScalarGridSpec(
        num_scalar_prefetch=2,
        grid=(N_CHUNKS,),
        in_specs=[pl.BlockSpec((N_EXPERTS, CHUNK), lambda c, *_: (0, c)),
                  row, row,
                  pl.BlockSpec((None, None, 1, D_MODEL), lambda c, *_: (c // per_b, 5, 0, 0)),
                  pl.BlockSpec((1, D_MODEL), lambda c, *_: (0, 0)),
                  pl.BlockSpec(memory_space=pl.ANY)],
        out_specs=row,
        scratch_shapes=[pltpu.SMEM((CHUNK_ROWS // GRANULE,), jnp.int32),
                        pltpu.VMEM((COMBINE_SLOTS, COMBINE_TILE, D_MODEL), BF16),
                        pltpu.VMEM((CHUNK, D_MODEL), F32),
                        pltpu.VMEM((CHUNK_ROWS, CHUNK), BF16),
                        pltpu.SemaphoreType.DMA((COMBINE_SLOTS,))])
    return pl.pallas_call(
        _combine_kernel,
        out_shape=jax.ShapeDtypeStruct((N_TOK, D_MODEL), F32),
        grid_spec=grid_spec,
        compiler_params=_cparams(("arbitrary",)),
        name="moe_combine",
    )(cnt, goff, csm, x1, shared, mod4, g_final, ys)


def _shared_kernel(x_ref, w1_ref, w3_ref, w2_ref, o_ref):
    x = x_ref[...]
    a = jnp.dot(x, w1_ref[...], preferred_element_type=F32)
    u = jnp.dot(x, w3_ref[...], preferred_element_type=F32)
    hid = (a * jax.nn.sigmoid(a)) * u
    o_ref[...] = jnp.dot(hid.astype(BF16), w2_ref[...], preferred_element_type=F32).astype(o_ref.dtype)


def _shared_expert(h2, w1, w3, w2):
    tm = 1024
    return pl.pallas_call(
        _shared_kernel,
        out_shape=jax.ShapeDtypeStruct((N_TOK, D_MODEL), F32),
        grid=(N_TOK // tm,),
        in_specs=[pl.BlockSpec((tm, D_MODEL), lambda m: (m, 0)),
                  pl.BlockSpec((D_MODEL, D_SHARED), lambda m: (0, 0)),
                  pl.BlockSpec((D_MODEL, D_SHARED), lambda m: (0, 0)),
                  pl.BlockSpec((D_SHARED, D_MODEL), lambda m: (0, 0))],
        out_specs=pl.BlockSpec((tm, D_MODEL), lambda m: (m, 0)),
        compiler_params=_cparams(("parallel",)),
        name="shared_expert",
    )(h2, w1, w3, w2)


def _rope_angles(pos, dims, theta):
    inv = jnp.power(jnp.float32(theta), -jnp.arange(0, dims, 2, dtype=jnp.float32) / dims)
    return pos.astype(jnp.float32)[:, None] * inv[None, :]


def _rotary_tables(angle_blocks):
    cos_parts, lo_parts, hi_parts = [], [], []
    used = 0
    for ang in angle_blocks:
        c, s = jnp.cos(ang), jnp.sin(ang)
        z = jnp.zeros_like(s)
        cos_parts += [c, c]
        lo_parts += [-s, z]
        hi_parts += [z, s]
        used += 2 * ang.shape[1]
    rest = HEAD_DIM - used
    if rest:
        cos_parts.append(jnp.ones((SEQ, rest), F32))
        lo_parts.append(jnp.zeros((SEQ, rest), F32))
        hi_parts.append(jnp.zeros((SEQ, rest), F32))
    return (jnp.concatenate(cos_parts, axis=1), jnp.concatenate(lo_parts, axis=1),
            jnp.concatenate(hi_parts, axis=1))


def kernel(x, c, w_ada, b_ada, g_attn, w_in, b_gate, q_norm_g, k_norm_g, w_a_up, w_b_up, w_out,
           g_ffn, w_router, e_bias, w1, w3, w2, ws1, ws3, ws2, g_final):
    l = 0
    x2 = x.reshape(N_TOK, D_MODEL)
    pos = jnp.arange(SEQ)
    tabs_a = _rotary_tables([_rope_angles(pos, ROPE_DIMS, ROPE_THETA)])
    tabs_b = _rotary_tables([_rope_angles(pos // GRID_W, AXIAL_DIMS, AXIAL_THETA),
                             _rope_angles(pos % GRID_W, AXIAL_DIMS, AXIAL_THETA)])
    tabs_a, tabs_b = lax.optimization_barrier((tabs_a, tabs_b))

    mod = _ada(c.T, w_ada[l], b_ada[l].reshape(1, -1))
    mod4 = mod.reshape(BATCH, N_MOD, 1, D_MODEL)

    perms = jnp.stack([_residue_perm(dilation) for _, dilation in A_PATTERNS])
    proj, gates = _projection(x2, g_attn[l].reshape(1, -1), mod4, w_in[l], tabs_a, tabs_b,
                              q_norm_g[l].reshape(1, -1), k_norm_g[l].reshape(1, -1), b_gate[l].reshape(1, -1),
                              perms)
    proj3 = proj.reshape(BATCH, SEQ, QKV_COLS)

    o_groups, lse_groups = [], []
    for gi, (window, dilation) in enumerate(A_PATTERNS):
        o, lse = _attn_a_dilated(proj3, gi, window, dilation)
        o_groups.append(o)
        lse_groups.append(lse)
    mix_b = _attn_b(proj3)

    merged = _merge(o_groups, lse_groups, mix_b, gates, w_a_up[l].astype(BF16), w_b_up[l].astype(BF16),
                    jnp.swapaxes(perms[1:], 1, 2))
    x1, h2, logits_t = _out_projection(merged, w_out[l].astype(BF16), x2, mod4, g_ffn[l].reshape(1, -1),
                                       w_router[l].T)
    csm, cnt_f, gx = _route(logits_t, e_bias[l].reshape(-1, 1))
    cnt = cnt_f[:, 0, :].astype(jnp.int32)
    goff, tail, tile_e, ntile, next_e = _plan(cnt)
    xs = _dispatch(cnt, goff, tail, ntile, h2, gx, csm)
    ys = _routed_experts(tile_e, ntile, next_e, xs, w1[l], w3[l], w2[l])
    shared = _shared_expert(h2, ws1[l].astype(BF16), ws3[l].astype(BF16), ws2[l].astype(BF16))
    out = _combine(cnt, goff, csm, x1, shared, mod4, g_final.reshape(1, -1), ys)
    return out.reshape(BATCH, SEQ, D_MODEL)
```

```python
import functools

import jax
import jax.numpy as jnp
from jax import lax
from jax.experimental import pallas as pl
from jax.experimental.pallas import tpu as pltpu

F32 = jnp.float32
BF16 = jnp.bfloat16

D_MODEL = 2048
BATCH = 2
SEQ = 4096
N_TOK = BATCH * SEQ
HEAD_DIM = 128
EPS = 1e-6
A_PATTERNS = ((128, 1), (512, 4), (2048, 16))
A_N_GROUPS = len(A_PATTERNS)
A_HEADS = 4
A_WIDTH = A_HEADS * HEAD_DIM
ROPE_THETA = 500000.0
ROPE_DIMS = HEAD_DIM // 4
B_Q_HEADS = 16
B_KV_HEADS = 4
B_GROUP = B_Q_HEADS // B_KV_HEADS
B_Q_WIDTH = B_Q_HEADS * HEAD_DIM
B_KV_WIDTH = B_KV_HEADS * HEAD_DIM
AXIAL_THETA = 10000.0
AXIAL_DIMS = HEAD_DIM // 2
GRID_W = 64
A_COLS = A_N_GROUPS * 3 * A_WIDTH
B_COLS = B_Q_WIDTH + 2 * B_KV_WIDTH
GATE_COLS = 2 * D_MODEL
QKV_COLS = A_COLS + B_COLS
IN_COLS = QKV_COLS + GATE_COLS
N_MOD = 6
N_EXPERTS = 64
N_EXPERT_GROUPS = 8
PER_GROUP = N_EXPERTS // N_EXPERT_GROUPS
TOPK_GROUPS = 4
TOP_K = 8
D_EXPERT = D_MODEL // 4
D_SHARED = D_MODEL // 4
ROUTED_SCALE = 2.5
NEG = -1e30
ATTN_SCALE = HEAD_DIM ** -0.5
LOG2E = 1.4426950408889634
LN2 = 0.6931471805599453

LANES = 128
SUBLANES = 8
VMEM_LIMIT = 56 * 1024 * 1024

COL_BLK = 512
PERM_BLOCK = 256
N_COL_BLKS = IN_COLS // COL_BLK
A_BLKS = A_COLS // COL_BLK
BQ_BLK0 = A_BLKS
BK_BLK = BQ_BLK0 + B_Q_WIDTH // COL_BLK
BV_BLK = BK_BLK + 1
GATE_BLK0 = BV_BLK + 1

CHUNK = 512
N_CHUNKS = N_TOK // CHUNK
GRANULE = 16
GRANULE_SHIFT = GRANULE.bit_length() - 1
TILE = 256
TILE_SHIFT = TILE.bit_length() - 1
GRAN_PER_TILE = TILE // GRANULE
GPT_SHIFT = TILE_SHIFT - GRANULE_SHIFT
N_SLOTS = 4
COMBINE_TILE = 1024
COMBINE_GPT = COMBINE_TILE // GRANULE
COMBINE_GPT_SHIFT = COMBINE_GPT.bit_length() - 1
COMBINE_SLOTS = 3
WEIGHT_DMA_PRIORITY = 1
XS_WIDTH = D_MODEL + LANES
CHUNK_ROWS = -(-(CHUNK * TOP_K + N_EXPERTS * (GRANULE - 1)) // TILE) * TILE
MAX_ROWS = -(-(N_TOK * TOP_K + N_EXPERTS * N_CHUNKS * (GRANULE - 1) + N_EXPERTS * (TILE - 1)) // TILE) * TILE
MAX_TILES = MAX_ROWS // TILE


def _cparams(semantics):
    return pltpu.CompilerParams(dimension_semantics=semantics, vmem_limit_bytes=VMEM_LIMIT)


def _ada_kernel(ct_ref, w_ref, b_ref, o_ref):
    ct = ct_ref[...]
    act = ct * jax.nn.sigmoid(ct)
    w = w_ref[...]
    for b in range(BATCH):
        o_ref[b:b + 1, :] = jnp.sum(w * act[:, b:b + 1], axis=0, keepdims=True) + b_ref[...]


def _ada(c_t, w_ada, b_ada):
    tn = 1024
    n_out = w_ada.shape[1]
    return pl.pallas_call(
        _ada_kernel,
        out_shape=jax.ShapeDtypeStruct((BATCH, n_out), F32),
        grid=(n_out // tn,),
        in_specs=[pl.BlockSpec((D_MODEL, BATCH), lambda n: (0, 0)),
                  pl.BlockSpec((D_MODEL, tn), lambda n: (0, n)),
                  pl.BlockSpec((1, tn), lambda n: (0, n))],
        out_specs=pl.BlockSpec((BATCH, tn), lambda n: (0, n)),
        compiler_params=_cparams(("parallel",)),
        name="ada_mod",
    )(c_t, w_ada, b_ada)


def _normmod(x, g, sc, sh):
    y = x * lax.rsqrt(jnp.mean(x * x, axis=-1, keepdims=True) + EPS)
    return (y * g) * (1.0 + sc) + sh


def _tile4(t):
    return jnp.concatenate([t, t, t, t], axis=1)


def _rotary(y, tabs, rows, shift):
    cos_ref, sin_lo_ref, sin_hi_ref = tabs
    width = y.shape[1]
    return (y * _tile4(cos_ref[rows, :])
            + pltpu.roll(y, width - shift, 1) * _tile4(sin_lo_ref[rows, :])
            + pltpu.roll(y, shift, 1) * _tile4(sin_hi_ref[rows, :]))


def _head_rmsnorm(y, g):
    outs = []
    for h in range(y.shape[1] // HEAD_DIM):
        yh = y[:, h * HEAD_DIM:(h + 1) * HEAD_DIM]
        outs.append(yh * lax.rsqrt(jnp.mean(yh * yh, axis=-1, keepdims=True) + EPS) * g)
    return jnp.concatenate(outs, axis=1)


def _proj_kernel(x_ref, g_ref, sc_ref, sh_ref, w_ref, ca_ref, sal_ref, sah_ref, cb_ref, sbl_ref, sbh_ref,
                 gain_ref, scale_ref, bg_ref, perm_ref, o_ref, gate_ref, h_ref, hp_ref):
    n = pl.program_id(1)
    is_a = n < A_BLKS
    part = n % 3
    tabs_a = (ca_ref, sal_ref, sah_ref)
    tabs_b = (cb_ref, sbl_ref, sbh_ref)
    sub_rows = PERM_BLOCK
    n_sub = h_ref.shape[0] // sub_rows

    @pl.when(n == 0)
    def _():
        def norm_rows(r, _):
            rows = pl.ds(pl.multiple_of(r * sub_rows, sub_rows), sub_rows)
            h = _normmod(x_ref[rows, :], g_ref[...], sc_ref[...], sh_ref[...]).astype(BF16)
            h_ref[rows, :] = h
            hp_ref[rows, :] = h
            return 0

        lax.fori_loop(0, n_sub, norm_rows, 0)

    @pl.when(is_a & (part == 0) & (n > 0))
    def _():
        for r in range(n_sub):
            rows = pl.ds(r * sub_rows, sub_rows)
            hp_ref[rows, :] = jnp.dot(perm_ref[...], h_ref[rows, :], preferred_element_type=F32).astype(BF16)

    def run(epilogue, dst_ref=o_ref, src_ref=h_ref):
        w = w_ref[...].astype(BF16)
        for r in range(n_sub):
            rows = pl.ds(r * sub_rows, sub_rows)
            acc = jnp.dot(src_ref[rows, :], w, preferred_element_type=F32)
            dst_ref[rows, :] = epilogue(acc, rows).astype(dst_ref.dtype)

    @pl.when(is_a & (part < 2))
    def _():
        run(lambda acc, rows: _rotary(acc, tabs_a, rows, ROPE_DIMS // 2) * scale_ref[...], src_ref=hp_ref)

    @pl.when(is_a & (part == 2))
    def _():
        run(lambda acc, rows: acc, src_ref=hp_ref)

    @pl.when((n >= BQ_BLK0) & (n <= BK_BLK))
    def _():
        run(lambda acc, rows: _rotary(_head_rmsnorm(acc, gain_ref[...]), tabs_b, rows, AXIAL_DIMS // 2)
            * scale_ref[...])

    @pl.when(n == BV_BLK)
    def _():
        run(lambda acc, rows: acc)

    @pl.when(n >= GATE_BLK0)
    def _():
        run(lambda acc, rows: jax.nn.sigmoid(acc + bg_ref[...]), gate_ref)


def _residue_source(dilation):
    per_res = PERM_BLOCK // dilation
    row = jnp.arange(PERM_BLOCK)
    return (row % per_res) * dilation + row // per_res


def _residue_perm(dilation):
    return (_residue_source(dilation)[:, None] == jnp.arange(PERM_BLOCK)[None, :]).astype(BF16)


def _projection(x2, g_attn, mod4, w_in, tabs_a, tabs_b, qg, kg, b_gate, perms):
    tm = 1024
    per_b = SEQ // tm

    def mod_spec(j):
        return pl.BlockSpec((None, None, 1, D_MODEL), lambda m, n: (m // per_b, j, 0, 0))

    tab_spec = pl.BlockSpec((tm, LANES), lambda m, n: (m % per_b, 0))
    n_gate_blks = GATE_COLS // COL_BLK
    is_query = [(blk < A_BLKS and blk % 3 == 0) or BQ_BLK0 <= blk < BK_BLK for blk in range(N_COL_BLKS)]
    col_scale = jnp.broadcast_to(jnp.where(jnp.array(is_query), ATTN_SCALE * LOG2E, 1.0).astype(F32)[:, None, None],
                                 (N_COL_BLKS, 1, COL_BLK))
    ones = jnp.ones_like(qg)
    col_gain = jnp.stack([qg if BQ_BLK0 <= blk < BK_BLK else kg if blk == BK_BLK else ones
                          for blk in range(N_COL_BLKS)])
    def residue_major(t, dilation):
        per_res = PERM_BLOCK // dilation
        t4 = t.reshape(SEQ // PERM_BLOCK, per_res, dilation, LANES)
        return jnp.swapaxes(t4, 1, 2).reshape(SEQ, LANES)

    tabs_a = [jnp.stack([residue_major(t, dilation) for _, dilation in A_PATTERNS]) for t in tabs_a]
    tab_a_spec = pl.BlockSpec((None, tm, LANES),
                              lambda m, n: (jnp.clip(n // 3, 0, A_N_GROUPS - 1), m % per_b, 0))

    def gate_blk(n):
        return jnp.clip(n - GATE_BLK0, 0, n_gate_blks - 1)

    return pl.pallas_call(
        _proj_kernel,
        out_shape=(jax.ShapeDtypeStruct((N_TOK, QKV_COLS), BF16),
                   jax.ShapeDtypeStruct((N_TOK, GATE_COLS), BF16)),
        grid=(N_TOK // tm, N_COL_BLKS),
        in_specs=[pl.BlockSpec((tm, D_MODEL), lambda m, n: (m, 0)),
                  pl.BlockSpec((1, D_MODEL), lambda m, n: (0, 0)),
                  mod_spec(1), mod_spec(0),
                  pl.BlockSpec((D_MODEL, COL_BLK), lambda m, n: (0, n)),
                  tab_a_spec, tab_a_spec, tab_a_spec, tab_spec, tab_spec, tab_spec,
                  pl.BlockSpec((None, 1, HEAD_DIM), lambda m, n: (n, 0, 0)),
                  pl.BlockSpec((None, 1, COL_BLK), lambda m, n: (n, 0, 0)),
                  pl.BlockSpec((1, COL_BLK), lambda m, n: (0, gate_blk(n))),
                  pl.BlockSpec((None, PERM_BLOCK, PERM_BLOCK),
                               lambda m, n: (jnp.clip(n // 3, 0, A_N_GROUPS - 1), 0, 0))],
        out_specs=(pl.BlockSpec((tm, COL_BLK), lambda m, n: (m, jnp.minimum(n, GATE_BLK0 - 1))),
                   pl.BlockSpec((tm, COL_BLK), lambda m, n: (m, gate_blk(n)))),
        scratch_shapes=[pltpu.VMEM((tm, D_MODEL), BF16), pltpu.VMEM((tm, D_MODEL), BF16)],
        compiler_params=_cparams(("parallel", "arbitrary")),
        name="in_projection",
    )(x2, g_attn, mod4, mod4, w_in, *tabs_a, *tabs_b, col_gain, col_scale, b_gate, perms)


def _attn_res_kernel(q_ref, k_ref, v_ref, o_ref, lse_ref, *, half_w):
    n_blk, per_res, _ = q_ref.shape
    length = n_blk * per_res
    q = q_ref[...].reshape(length, A_WIDTH)
    k = k_ref[...].reshape(length, A_WIDTH)
    v = v_ref[...].reshape(length, A_WIDTH)
    tq = 2 * half_w
    win = 4 * half_w
    lane = lax.broadcasted_iota(jnp.int32, (tq, LANES), 1)
    row_minus_col = lax.broadcasted_iota(jnp.int32, (tq, win), 0) - lax.broadcasted_iota(jnp.int32, (tq, win), 1)
    o_blocks, lse_blocks = [], []
    for qb in range(length // tq):
        start = min(max(qb * tq - half_w, 0), length - win)
        valid = jnp.abs(row_minus_col + (qb * tq - start)) <= half_w
        lse_tile = jnp.zeros((tq, LANES), F32)
        outs = []
        for h in range(A_HEADS):
            sl = slice(h * HEAD_DIM, (h + 1) * HEAD_DIM)
            s = lax.dot_general(q[qb * tq:(qb + 1) * tq, sl], k[start:start + win, sl],
                                (((1,), (1,)), ((), ())), preferred_element_type=F32)
            s = jnp.where(valid, s, NEG)
            m = jnp.max(s, axis=-1, keepdims=True)
            p = jnp.exp2(s - m)
            l = jnp.sum(p, axis=-1, keepdims=True)
            o = jnp.dot(p.astype(BF16), v[start:start + win, sl], preferred_element_type=F32)
            outs.append(o / l)
            lse_tile = jnp.where(lane == h, m * LN2 + jnp.log(l), lse_tile)
        o_blocks.append(jnp.concatenate(outs, axis=1).astype(o_ref.dtype))
        lse_blocks.append(lse_tile)
    o_ref[...] = jnp.concatenate(o_blocks, axis=0).reshape(n_blk, per_res, A_WIDTH)
    lse_ref[...] = jnp.concatenate(lse_blocks, axis=0).reshape(n_blk, per_res, LANES)


def _attn_a_dilated(proj3, group, window, dilation):
    per_res = PERM_BLOCK // dilation
    n_blk = SEQ // PERM_BLOCK
    half_w = (window // 2) // dilation
    view = proj3.reshape(BATCH, n_blk, dilation, per_res, QKV_COLS)

    def spec(width, blk):
        return pl.BlockSpec((None, n_blk, None, per_res, width), lambda b, r: (b, 0, r, 0, blk))

    o, lse = pl.pallas_call(
        functools.partial(_attn_res_kernel, half_w=half_w),
        out_shape=(jax.ShapeDtypeStruct((BATCH, n_blk, dilation, per_res, A_WIDTH), BF16),
                   jax.ShapeDtypeStruct((BATCH, n_blk, dilation, per_res, LANES), F32)),
        grid=(BATCH, dilation),
        in_specs=[spec(COL_BLK, 3 * group), spec(COL_BLK, 3 * group + 1), spec(COL_BLK, 3 * group + 2)],
        out_specs=(spec(A_WIDTH, 0), spec(LANES, 0)),
        compiler_params=_cparams(("parallel", "parallel")),
        name=f"dilated_attention_g{group}",
    )(view, view, view)
    return o.reshape(N_TOK, A_WIDTH), lse.reshape(N_TOK, LANES)


def _attn_b_kernel(q_ref, k_ref, v_ref, o_ref, vx_ref, acc_ref, *, tk):
    tq = q_ref.shape[0]
    n_chunks = SEQ // tk

    @pl.when(pl.program_id(2) == 0)
    def _():
        vx_ref[:, 0:HEAD_DIM] = v_ref[...]
        vx_ref[:, HEAD_DIM:2 * HEAD_DIM] = jnp.ones((SEQ, HEAD_DIM), BF16)

    q = q_ref[...]
    qs = jnp.concatenate([q[:, g * HEAD_DIM:(g + 1) * HEAD_DIM] for g in range(B_GROUP)], axis=0)
    acc_ref[...] = jnp.zeros_like(acc_ref)
    m = jnp.full((B_GROUP * tq, 1), -jnp.inf, F32)
    for c in range(n_chunks):
        keys = slice(c * tk, (c + 1) * tk)
        s = lax.dot_general(qs, k_ref[keys, :], (((1,), (1,)), ((), ())), preferred_element_type=F32)
        m_new = jnp.maximum(m, jnp.max(s, axis=-1, keepdims=True))
        p = jnp.exp2(s - m_new).astype(BF16)
        acc_ref[...] = jnp.exp2(m - m_new) * acc_ref[...] + jnp.dot(p, vx_ref[keys, :],
                                                                    preferred_element_type=F32)
        m = m_new
    o = acc_ref[:, 0:HEAD_DIM] / acc_ref[:, HEAD_DIM:2 * HEAD_DIM]
    o_ref[...] = jnp.concatenate([o[g * tq:(g + 1) * tq] for g in range(B_GROUP)], axis=1).astype(o_ref.dtype)


def _attn_b(proj3):
    tq = 512
    tk = 256
    kcol0 = BK_BLK * COL_BLK // HEAD_DIM
    vcol0 = BV_BLK * COL_BLK // HEAD_DIM
    o = pl.pallas_call(
        functools.partial(_attn_b_kernel, tk=tk),
        out_shape=jax.ShapeDtypeStruct((BATCH, SEQ, B_Q_WIDTH), BF16),
        grid=(BATCH, B_KV_HEADS, SEQ // tq),
        in_specs=[pl.BlockSpec((None, tq, COL_BLK), lambda b, h, i: (b, i, BQ_BLK0 + h)),
                  pl.BlockSpec((None, SEQ, HEAD_DIM), lambda b, h, i: (b, 0, kcol0 + h)),
                  pl.BlockSpec((None, SEQ, HEAD_DIM), lambda b, h, i: (b, 0, vcol0 + h))],
        out_specs=pl.BlockSpec((None, tq, COL_BLK), lambda b, h, i: (b, i, h)),
        scratch_shapes=[pltpu.VMEM((SEQ, 2 * HEAD_DIM), BF16),
                        pltpu.VMEM((B_GROUP * tq, 2 * HEAD_DIM), F32)],
        compiler_params=_cparams(("parallel", "parallel", "arbitrary")),
        name="gqa_attention",
    )(proj3, proj3, proj3)
    return o.reshape(N_TOK, B_Q_WIDTH)


def _merge_kernel(o0_ref, o1_ref, o2_ref, l0_ref, l1_ref, l2_ref, yb_ref, wa_ref, wb_ref, ga_ref, gb_ref,
                  unperm_ref, out_ref):
    sub_rows = PERM_BLOCK
    for r in range(out_ref.shape[0] // sub_rows):
        rows = pl.ds(r * sub_rows, sub_rows)
        outs = [o0_ref[rows, :].astype(F32)]
        lses = [l0_ref[rows, :]]
        for g, (o_ref, l_ref) in enumerate(((o1_ref, l1_ref), (o2_ref, l2_ref))):
            outs.append(jnp.dot(unperm_ref[g], o_ref[rows, :], preferred_element_type=F32))
            lses.append(jnp.dot(unperm_ref[g].astype(F32), l_ref[rows, :], preferred_element_type=F32,
                                precision=lax.Precision.HIGHEST))
        cols = []
        for h in range(A_HEADS):
            lh = [jnp.broadcast_to(l[:, h:h + 1], (sub_rows, HEAD_DIM)) for l in lses]
            mx = jnp.maximum(jnp.maximum(lh[0], lh[1]), lh[2])
            e = [jnp.exp(v - mx) for v in lh]
            den = e[0] + e[1] + e[2]
            sl = slice(h * HEAD_DIM, (h + 1) * HEAD_DIM)
            cols.append(sum((e[g] / den) * outs[g][:, sl] for g in range(A_N_GROUPS)))
        mix_a = jnp.concatenate(cols, axis=1).astype(BF16)
        mix_b = yb_ref[rows, :]
        for cb in range(D_MODEL // COL_BLK):
            cs = pl.ds(cb * COL_BLK, COL_BLK)
            ya = jnp.dot(mix_a, wa_ref[:, cs], preferred_element_type=F32)
            yb = jnp.dot(mix_b, wb_ref[:, cs], preferred_element_type=F32)
            out_ref[rows, cs] = (ga_ref[rows, cs].astype(F32) * ya
                                 + gb_ref[rows, cs].astype(F32) * yb).astype(out_ref.dtype)


def _merge(o_groups, lse_groups, mix_b, gates, w_a_up, w_b_up, unperms):
    tm = 512
    o_spec = pl.BlockSpec((tm, A_WIDTH), lambda m: (m, 0))
    l_spec = pl.BlockSpec((tm, LANES), lambda m: (m, 0))
    return pl.pallas_call(
        _merge_kernel,
        out_shape=jax.ShapeDtypeStruct((N_TOK, D_MODEL), BF16),
        grid=(N_TOK // tm,),
        in_specs=[o_spec, o_spec, o_spec, l_spec, l_spec, l_spec,
                  pl.BlockSpec((tm, B_Q_WIDTH), lambda m: (m, 0)),
                  pl.BlockSpec((A_WIDTH, D_MODEL), lambda m: (0, 0)),
                  pl.BlockSpec((B_Q_WIDTH, D_MODEL), lambda m: (0, 0)),
                  pl.BlockSpec((tm, D_MODEL), lambda m: (m, 0)),
                  pl.BlockSpec((tm, D_MODEL), lambda m: (m, 1)),
                  pl.BlockSpec((A_N_GROUPS - 1, PERM_BLOCK, PERM_BLOCK), lambda m: (0, 0, 0))],
        out_specs=pl.BlockSpec((tm, D_MODEL), lambda m: (m, 0)),
        compiler_params=_cparams(("parallel",)),
        name="branch_merge",
    )(*o_groups, *lse_groups, mix_b, w_a_up, w_b_up, gates, gates, unperms)


def _outproj_kernel(mg_ref, w_ref, x_ref, gt_ref, g_ref, sc_ref, sh_ref, wr_ref, x1_ref, h2_ref, lg_ref):
    sub_rows = 256
    wr = wr_ref[...]
    wr_hi = wr.astype(BF16)
    wr_lo = (wr - wr_hi.astype(F32)).astype(BF16)
    for r in range(mg_ref.shape[0] // sub_rows):
        rows = pl.ds(r * sub_rows, sub_rows)
        y = jnp.dot(mg_ref[rows, :], w_ref[...], preferred_element_type=F32)
        x1 = x_ref[rows, :] + gt_ref[...] * y
        x1_ref[rows, :] = x1
        h2 = _normmod(x1, g_ref[...], sc_ref[...], sh_ref[...])
        h2_hi = h2.astype(BF16)
        h2_ref[rows, :] = h2_hi
        h2_lo = (h2 - h2_hi.astype(F32)).astype(BF16)
        nt = (((1,), (1,)), ((), ()))
        lg_ref[:, rows] = (lax.dot_general(wr_hi, h2_hi, nt, preferred_element_type=F32)
                           + lax.dot_general(wr_hi, h2_lo, nt, preferred_element_type=F32)
                           + lax.dot_general(wr_lo, h2_hi, nt, preferred_element_type=F32))


def _out_projection(merged, w_out, x2, mod4, g_ffn, w_router_t):
    tm = 512
    per_b = SEQ // tm

    def mod_spec(j):
        return pl.BlockSpec((None, None, 1, D_MODEL), lambda m: (m // per_b, j, 0, 0))

    return pl.pallas_call(
        _outproj_kernel,
        out_shape=(jax.ShapeDtypeStruct((N_TOK, D_MODEL), F32),
                   jax.ShapeDtypeStruct((N_TOK, D_MODEL), BF16),
                   jax.ShapeDtypeStruct((N_EXPERTS, N_TOK), F32)),
        grid=(N_TOK // tm,),
        in_specs=[pl.BlockSpec((tm, D_MODEL), lambda m: (m, 0)),
                  pl.BlockSpec((D_MODEL, D_MODEL), lambda m: (0, 0)),
                  pl.BlockSpec((tm, D_MODEL), lambda m: (m, 0)),
                  mod_spec(2),
                  pl.BlockSpec((1, D_MODEL), lambda m: (0, 0)),
                  mod_spec(4), mod_spec(3),
                  pl.BlockSpec((N_EXPERTS, D_MODEL), lambda m: (0, 0))],
        out_specs=(pl.BlockSpec((tm, D_MODEL), lambda m: (m, 0)),
                   pl.BlockSpec((tm, D_MODEL), lambda m: (m, 0)),
                   pl.BlockSpec((N_EXPERTS, tm), lambda m: (0, m))),
        compiler_params=_cparams(("parallel",)),
        name="out_projection",
    )(merged, w_out, x2, mod4, g_ffn, mod4, mod4, w_router_t)


def _route_kernel(lg_ref, eb_ref, csm_ref, cnt_ref, gx_ref):
    tt = lg_ref.shape[1]
    scores = jax.nn.sigmoid(lg_ref[...])
    sel = scores + eb_ref[...]
    neg_inf = -jnp.inf
    sel_g = [sel[PER_GROUP * g:PER_GROUP * (g + 1), :] for g in range(N_EXPERT_GROUPS)]
    grp = []
    for v in sel_g:
        m1 = jnp.max(v, axis=0, keepdims=True)
        is1 = v == m1
        n1 = jnp.sum(jnp.where(is1, 1.0, 0.0), axis=0, keepdims=True)
        rest = jnp.max(jnp.where(is1, neg_inf, v), axis=0, keepdims=True)
        grp.append(m1 + jnp.where(n1 >= 2.0, m1, rest))
    masked = []
    for g in range(N_EXPERT_GROUPS):
        rank = jnp.zeros((1, tt), F32)
        for g2 in range(N_EXPERT_GROUPS):
            if g2 != g:
                beats = (grp[g2] >= grp[g]) if g2 < g else (grp[g2] > grp[g])
                rank = rank + jnp.where(beats, 1.0, 0.0)
        keep = jnp.broadcast_to(rank < TOPK_GROUPS, (PER_GROUP, tt))
        masked.append(jnp.where(keep, sel_g[g], neg_inf))
    sub = lax.broadcasted_iota(jnp.int32, (PER_GROUP, tt), 0)
    ranks = [jnp.zeros((PER_GROUP, tt), F32) for _ in range(N_EXPERT_GROUPS)]
    for g2 in range(N_EXPERT_GROUPS):
        for m2 in range(PER_GROUP):
            vf = jnp.broadcast_to(masked[g2][m2:m2 + 1, :], (PER_GROUP, tt))
            for g in range(N_EXPERT_GROUPS):
                if g2 < g:
                    beats = vf >= masked[g]
                elif g2 > g:
                    beats = vf > masked[g]
                else:
                    beats = (vf > masked[g]) | ((vf == masked[g]) & (sub > m2))
                ranks[g] = ranks[g] + jnp.where(beats, 1.0, 0.0)
    picked = [ranks[g] < TOP_K for g in range(N_EXPERT_GROUPS)]
    chosen = [jnp.where(picked[g], scores[PER_GROUP * g:PER_GROUP * (g + 1), :], 0.0)
              for g in range(N_EXPERT_GROUPS)]
    total = chosen[0]
    for g in range(1, N_EXPERT_GROUPS):
        total = total + chosen[g]
    denom = jnp.sum(total, axis=0, keepdims=True)
    gate = jnp.concatenate([chosen[g] / denom * ROUTED_SCALE for g in range(N_EXPERT_GROUPS)], axis=0)
    mask = jnp.concatenate([jnp.where(picked[g], 1.0, 0.0) for g in range(N_EXPERT_GROUPS)], axis=0)
    mask_bf = mask.astype(BF16)

    upper = jnp.where(lax.broadcasted_iota(jnp.int32, (tt, tt), 0) <= lax.broadcasted_iota(jnp.int32, (tt, tt), 1),
                      1.0, 0.0).astype(BF16)
    cs = jnp.dot(mask_bf, upper, preferred_element_type=F32)
    csm_ref[...] = jnp.where(mask > 0.0, cs, 0.0)
    cnt_ref[...] = lax.dot_general(jnp.ones((SUBLANES, tt), BF16), mask_bf, (((1,), (1,)), ((), ())),
                                   preferred_element_type=F32)

    hi = gate.astype(BF16).astype(F32)
    gx_ref[...] = jnp.concatenate([hi, gate - hi], axis=0).T.astype(BF16)


def _route(logits_t, e_bias_col):
    return pl.pallas_call(
        _route_kernel,
        out_shape=(jax.ShapeDtypeStruct((N_EXPERTS, N_TOK), F32),
                   jax.ShapeDtypeStruct((N_CHUNKS, SUBLANES, N_EXPERTS), F32),
                   jax.ShapeDtypeStruct((N_TOK, LANES), BF16)),
        grid=(N_CHUNKS,),
        in_specs=[pl.BlockSpec((N_EXPERTS, CHUNK), lambda c: (0, c)),
                  pl.BlockSpec((N_EXPERTS, 1), lambda c: (0, 0))],
        out_specs=(pl.BlockSpec((N_EXPERTS, CHUNK), lambda c: (0, c)),
                   pl.BlockSpec((None, SUBLANES, N_EXPERTS), lambda c: (c, 0, 0)),
                   pl.BlockSpec((CHUNK, LANES), lambda c: (c, 0))),
        compiler_params=_cparams(("parallel",)),
        name="routing",
    )(logits_t, e_bias_col)


def _plan_kernel(cnt_ref, goff_ref, tail_ref, tile_e_ref, ntile_ref, next_ref):
    def per_expert(e, carry):
        base, tbase, prev = carry

        def per_chunk(c, off):
            goff_ref[c, e] = base + off
            return off + (((cnt_ref[c, e] + (GRANULE - 1)) >> GRANULE_SHIFT) << GRANULE_SHIFT)

        tot = lax.fori_loop(0, N_CHUNKS, per_chunk, jnp.int32(0))
        region = ((tot + (TILE - 1)) >> TILE_SHIFT) << TILE_SHIFT
        tail_ref[0, e] = base + tot
        tail_ref[1, e] = (region - tot) >> GRANULE_SHIFT
        n_t = region >> TILE_SHIFT

        def per_tile(j, _):
            tile_e_ref[tbase + j] = e
            return 0

        lax.fori_loop(0, n_t, per_tile, 0)
        next_ref[e] = jnp.int32(-1)

        @pl.when((n_t > 0) & (prev >= 0))
        def _():
            next_ref[jnp.maximum(prev, 0)] = e

        return base + region, tbase + n_t, jnp.where(n_t > 0, e, prev)

    _, n_tiles, _ = lax.fori_loop(0, N_EXPERTS, per_expert, (jnp.int32(0), jnp.int32(0), jnp.int32(-1)))
    ntile_ref[0] = n_tiles
    last_e = tile_e_ref[jnp.maximum(n_tiles - 1, 0)]

    def fill(j, _):
        tile_e_ref[j] = last_e
        return 0

    lax.fori_loop(n_tiles, MAX_TILES, fill, 0)


def _plan(cnt):
    smem = pl.BlockSpec(memory_space=pltpu.SMEM)
    return pl.pallas_call(
        _plan_kernel,
        out_shape=(jax.ShapeDtypeStruct((N_CHUNKS, N_EXPERTS), jnp.int32),
                   jax.ShapeDtypeStruct((2, N_EXPERTS), jnp.int32),
                   jax.ShapeDtypeStruct((MAX_TILES,), jnp.int32),
                   jax.ShapeDtypeStruct((1,), jnp.int32),
                   jax.ShapeDtypeStruct((N_EXPERTS,), jnp.int32)),
        in_specs=[smem],
        out_specs=(smem, smem, smem, smem, smem),
        name="row_plan",
    )(cnt)


def _chunk_granules(cnt_ref, goff_ref, c, table_ref, per_granule=None):
    def per_expert(e, n_before):
        n_gran = (cnt_ref[c, e] + (GRANULE - 1)) >> GRANULE_SHIFT
        base = goff_ref[c, e]

        def per_j(j, _):
            table_ref[n_before + j] = base + j * GRANULE
            if per_granule is not None:
                per_granule(e, j, n_before + j)
            return 0

        lax.fori_loop(0, n_gran, per_j, 0)
        return n_before + n_gran

    return lax.fori_loop(0, N_EXPERTS, per_expert, jnp.int32(0))


def _dispatch_kernel(cnt_ref, goff_ref, tail_ref, ntile_ref, x_ref, gx_ref, csm_ref, xs_hbm,
                     onehot_ref, table_ref, buf_ref, zero_ref, xcat_ref, ztile_ref, sems, zsem, usem):
    c = pl.program_id(0)
    sub = lax.broadcasted_iota(jnp.int32, (GRANULE, CHUNK), 0)

    def unused_copy(t):
        return pltpu.make_async_copy(ztile_ref, xs_hbm.at[pl.ds(pl.multiple_of(t * TILE, TILE), TILE), :], usem)

    @pl.when(c == 0)
    def _():
        ztile_ref[...] = jnp.zeros_like(ztile_ref)

        def start_unused(t, _):
            unused_copy(t).start(priority=WEIGHT_DMA_PRIORITY)
            return 0

        lax.fori_loop(ntile_ref[0], MAX_TILES, start_unused, 0)

    xcat_ref[:, 0:D_MODEL] = x_ref[...]
    xcat_ref[:, D_MODEL:XS_WIDTH] = gx_ref[...]

    def build_granule(e, j, g):
        want = (sub + (j * GRANULE + 1)).astype(F32)
        hit = csm_ref[pl.ds(e, 1), :] == want
        onehot_ref[pl.ds(pl.multiple_of(g * GRANULE, GRANULE), GRANULE), :] = jnp.where(hit, 1.0, 0.0).astype(BF16)

    n_gran = _chunk_granules(cnt_ref, goff_ref, c, table_ref, build_granule)
    n_tiles = (n_gran + (GRAN_PER_TILE - 1)) >> GPT_SHIFT

    def clear(g, _):
        onehot_ref[pl.ds(pl.multiple_of(g * GRANULE, GRANULE), GRANULE), :] = jnp.zeros((GRANULE, CHUNK), BF16)
        return 0

    lax.fori_loop(n_gran, n_tiles * GRAN_PER_TILE, clear, 0)

    def granule_copy(slot, i, row):
        return pltpu.make_async_copy(buf_ref.at[slot, pl.ds(pl.multiple_of(i * GRANULE, GRANULE), GRANULE), :],
                                     xs_hbm.at[pl.ds(pl.multiple_of(row, GRANULE), GRANULE), :],
                                     sems.at[slot])

    def tile_granules(k):
        return jnp.minimum(GRAN_PER_TILE, n_gran - k * GRAN_PER_TILE)

    def wait_tile(k):
        slot = k % N_SLOTS

        def w(i, _):
            granule_copy(slot, i, 0).wait()
            return 0

        lax.fori_loop(0, tile_granules(k), w, 0)

    def per_tile(k, _):
        slot = k % N_SLOTS

        @pl.when(k >= N_SLOTS)
        def _():
            wait_tile(k - N_SLOTS)

        oh = onehot_ref[pl.ds(pl.multiple_of(k * TILE, TILE), TILE), :]
        buf_ref[slot] = jnp.dot(oh, xcat_ref[...], preferred_element_type=F32).astype(BF16)

        def s(i, _):
            granule_copy(slot, i, table_ref[k * GRAN_PER_TILE + i]).start()
            return 0

        lax.fori_loop(0, tile_granules(k), s, 0)
        return 0

    lax.fori_loop(0, n_tiles, per_tile, 0)

    def drain(k, _):
        wait_tile(k)
        return 0

    lax.fori_loop(jnp.maximum(n_tiles - N_SLOTS, 0), n_tiles, drain, 0)

    @pl.when(c == N_CHUNKS - 1)
    def _():
        zero_ref[...] = jnp.zeros_like(zero_ref)

        def tail_copy(row):
            return pltpu.make_async_copy(zero_ref, xs_hbm.at[pl.ds(pl.multiple_of(row, GRANULE), GRANULE), :], zsem)

        def start_e(e, _):
            def st(j, _):
                tail_copy(tail_ref[0, e] + j * GRANULE).start()
                return 0
            lax.fori_loop(0, tail_ref[1, e], st, 0)
            return 0

        def wait_e(e, _):
            def wt(j, _):
                tail_copy(0).wait()
                return 0
            lax.fori_loop(0, tail_ref[1, e], wt, 0)
            return 0

        lax.fori_loop(0, N_EXPERTS, start_e, 0)
        lax.fori_loop(0, N_EXPERTS, wait_e, 0)

        def wait_unused(t, _):
            unused_copy(t).wait()
            return 0

        lax.fori_loop(ntile_ref[0], MAX_TILES, wait_unused, 0)


def _dispatch(cnt, goff, tail, ntile, h2, gx, csm):
    grid_spec = pltpu.PrefetchScalarGridSpec(
        num_scalar_prefetch=4,
        grid=(N_CHUNKS,),
        in_specs=[pl.BlockSpec((CHUNK, D_MODEL), lambda c, *_: (c, 0)),
                  pl.BlockSpec((CHUNK, LANES), lambda c, *_: (c, 0)),
                  pl.BlockSpec((N_EXPERTS, CHUNK), lambda c, *_: (0, c))],
        out_specs=pl.BlockSpec(memory_space=pl.ANY),
        scratch_shapes=[pltpu.VMEM((CHUNK_ROWS, CHUNK), BF16),
                        pltpu.SMEM((CHUNK_ROWS // GRANULE,), jnp.int32),
                        pltpu.VMEM((N_SLOTS, TILE, XS_WIDTH), BF16),
                        pltpu.VMEM((GRANULE, XS_WIDTH), BF16),
                        pltpu.VMEM((CHUNK, XS_WIDTH), BF16),
                        pltpu.VMEM((TILE, XS_WIDTH), BF16),
                        pltpu.SemaphoreType.DMA((N_SLOTS,)),
                        pltpu.SemaphoreType.DMA(()),
                        pltpu.SemaphoreType.DMA(())])
    return pl.pallas_call(
        _dispatch_kernel,
        out_shape=jax.ShapeDtypeStruct((MAX_ROWS, XS_WIDTH), BF16),
        grid_spec=grid_spec,
        compiler_params=_cparams(("arbitrary",)),
        name="moe_dispatch",
    )(cnt, goff, tail, ntile, h2, gx, csm)


def _routed_kernel(tile_e_ref, ntile_ref, next_ref, xs_hbm, w1_hbm, w3_hbm, w2_hbm, ys_hbm,
                   xbuf_ref, ybuf_ref, w1f_ref, w3f_ref, w2f_ref, w1s_ref, w3s_ref, w2s_ref, ztile_ref,
                   xsem, ysem, wsem, usem):
    n_tiles = ntile_ref[0]

    def x_copy(t, slot):
        return pltpu.make_async_copy(xs_hbm.at[pl.ds(pl.multiple_of(t * TILE, TILE), TILE), :],
                                     xbuf_ref.at[slot], xsem.at[slot])

    def y_copy(t, slot):
        return pltpu.make_async_copy(ybuf_ref.at[slot],
                                     ys_hbm.at[pl.ds(pl.multiple_of(t * TILE, TILE), TILE), :], ysem.at[slot])

    def fetch(expert, slot):
        return (pltpu.make_async_copy(w1_hbm.at[expert], w1f_ref.at[slot], wsem.at[slot]),
                pltpu.make_async_copy(w3_hbm.at[expert], w3f_ref.at[slot], wsem.at[slot]),
                pltpu.make_async_copy(w2_hbm.at[expert], w2f_ref.at[slot], wsem.at[slot]))

    @pl.when(n_tiles > 0)
    def _():
        x_copy(0, 0).start()
        for cp in fetch(tile_e_ref[0], 0):
            cp.start()

    def unused_copy(t):
        return pltpu.make_async_copy(ztile_ref, ys_hbm.at[pl.ds(pl.multiple_of(t * TILE, TILE), TILE), :], usem)

    ztile_ref[...] = jnp.zeros_like(ztile_ref)

    def start_unused(t, _):
        unused_copy(t).start(priority=WEIGHT_DMA_PRIORITY)
        return 0

    def per_tile(t, n_started):
        slot = t & 1
        e = tile_e_ref[t]
        x_copy(t, slot).wait()

        @pl.when(n_tiles + t < MAX_TILES)
        def _():
            start_unused(n_tiles + t, 0)

        @pl.when(t + 1 < n_tiles)
        def _():
            x_copy(t + 1, 1 - slot).start()

        first = (t == 0) | (e != tile_e_ref[jnp.maximum(t - 1, 0)])

        @pl.when(first)
        def _():
            wslot = n_started & 1
            nxt = next_ref[e]

            @pl.when(nxt >= 0)
            def _():
                for cp in fetch(jnp.maximum(nxt, 0), 1 - wslot):
                    cp.start(priority=WEIGHT_DMA_PRIORITY)

            for cp in fetch(e, wslot):
                cp.wait()
            w1s_ref[...] = w1f_ref[wslot].astype(BF16)
            w3s_ref[...] = w3f_ref[wslot].astype(BF16)
            w2s_ref[...] = w2f_ref[wslot].astype(BF16)

        @pl.when(t >= 2)
        def _():
            y_copy(t - 2, slot).wait()

        x = xbuf_ref[slot, :, 0:D_MODEL]
        gx = xbuf_ref[slot, :, D_MODEL:XS_WIDTH].astype(F32)
        lane = lax.broadcasted_iota(jnp.int32, gx.shape, 1)
        gate = jnp.sum(jnp.where((lane == e) | (lane == e + N_EXPERTS), gx, 0.0), axis=1, keepdims=True)
        a = jnp.dot(x, w1s_ref[...], preferred_element_type=F32)
        u = jnp.dot(x, w3s_ref[...], preferred_element_type=F32)
        hid = (a * jax.nn.sigmoid(a)) * u * gate
        ybuf_ref[slot] = jnp.dot(hid.astype(BF16), w2s_ref[...], preferred_element_type=F32).astype(ybuf_ref.dtype)
        y_copy(t, slot).start()
        return n_started + jnp.where(first, 1, 0)

    lax.fori_loop(0, n_tiles, per_tile, jnp.int32(0))

    @pl.when(n_tiles >= 2)
    def _():
        y_copy(n_tiles - 2, n_tiles & 1).wait()

    @pl.when(n_tiles >= 1)
    def _():
        y_copy(n_tiles - 1, (n_tiles - 1) & 1).wait()

    def wait_unused(t, _):
        unused_copy(t).wait()
        return 0

    lax.fori_loop(jnp.minimum(2 * n_tiles, MAX_TILES), MAX_TILES, start_unused, 0)
    lax.fori_loop(n_tiles, MAX_TILES, wait_unused, 0)


def _routed_experts(tile_e, ntile, next_e, xs, w1, w3, w2):
    hbm = pl.BlockSpec(memory_space=pl.ANY)
    grid_spec = pltpu.PrefetchScalarGridSpec(
        num_scalar_prefetch=3,
        grid=(1,),
        in_specs=[hbm, hbm, hbm, hbm],
        out_specs=hbm,
        scratch_shapes=[pltpu.VMEM((2, TILE, XS_WIDTH), BF16),
                        pltpu.VMEM((2, TILE, D_MODEL), BF16),
                        pltpu.VMEM((2, D_MODEL, D_EXPERT), F32),
                        pltpu.VMEM((2, D_MODEL, D_EXPERT), F32),
                        pltpu.VMEM((2, D_EXPERT, D_MODEL), F32),
                        pltpu.VMEM((D_MODEL, D_EXPERT), BF16),
                        pltpu.VMEM((D_MODEL, D_EXPERT), BF16),
                        pltpu.VMEM((D_EXPERT, D_MODEL), BF16),
                        pltpu.VMEM((TILE, D_MODEL), BF16),
                        pltpu.SemaphoreType.DMA((2,)),
                        pltpu.SemaphoreType.DMA((2,)),
                        pltpu.SemaphoreType.DMA((2,)),
                        pltpu.SemaphoreType.DMA(())])
    return pl.pallas_call(
        _routed_kernel,
        out_shape=jax.ShapeDtypeStruct((MAX_ROWS, D_MODEL), BF16),
        grid_spec=grid_spec,
        compiler_params=_cparams(("arbitrary",)),
        name="routed_experts",
    )(tile_e, ntile, next_e, xs, w1, w3, w2)


def _combine_kernel(cnt_ref, goff_ref, csm_ref, x1_ref, sh_ref, gt_ref, gf_ref, ys_hbm, o_ref,
                    table_ref, buf_ref, acc_ref, pick_ref, sems):
    c = pl.program_id(0)
    sub = lax.broadcasted_iota(jnp.int32, (GRANULE, CHUNK), 0)

    def build_granule(e, j, g):
        want = (sub + (j * GRANULE + 1)).astype(F32)
        hit = csm_ref[pl.ds(e, 1), :] == want
        pick_ref[pl.ds(pl.multiple_of(g * GRANULE, GRANULE), GRANULE), :] = jnp.where(hit, 1.0, 0.0).astype(BF16)

    n_gran = _chunk_granules(cnt_ref, goff_ref, c, table_ref, build_granule)
    n_tiles = (n_gran + (COMBINE_GPT - 1)) >> COMBINE_GPT_SHIFT

    def clear_pick(g, _):
        pick_ref[pl.ds(pl.multiple_of(g * GRANULE, GRANULE), GRANULE), :] = jnp.zeros((GRANULE, CHUNK), BF16)
        return 0

    lax.fori_loop(n_gran, n_tiles * COMBINE_GPT, clear_pick, 0)

    def granule_copy(slot, i, row):
        return pltpu.make_async_copy(ys_hbm.at[pl.ds(pl.multiple_of(row, GRANULE), GRANULE), :],
                                     buf_ref.at[slot, pl.ds(pl.multiple_of(i * GRANULE, GRANULE), GRANULE), :],
                                     sems.at[slot])

    def tile_granules(k):
        return jnp.minimum(COMBINE_GPT, n_gran - k * COMBINE_GPT)

    def start_tile(k):
        slot = k % COMBINE_SLOTS

        def s(i, _):
            granule_copy(slot, i, table_ref[k * COMBINE_GPT + i]).start()
            return 0

        lax.fori_loop(0, tile_granules(k), s, 0)

    def wait_tile(k):
        slot = k % COMBINE_SLOTS

        def w(i, _):
            granule_copy(slot, i, 0).wait()
            return 0

        lax.fori_loop(0, tile_granules(k), w, 0)

    for k0 in range(COMBINE_SLOTS - 1):
        @pl.when(k0 < n_tiles)
        def _():
            start_tile(k0)

    acc_ref[...] = jnp.zeros_like(acc_ref)

    def per_tile(k, _):
        @pl.when(k + (COMBINE_SLOTS - 1) < n_tiles)
        def _():
            start_tile(k + (COMBINE_SLOTS - 1))

        wait_tile(k)
        slot = k % COMBINE_SLOTS

        def clear(i, _):
            buf_ref[slot, pl.ds(pl.multiple_of(i * GRANULE, GRANULE), GRANULE), :] = jnp.zeros((GRANULE, D_MODEL), BF16)
            return 0

        lax.fori_loop(tile_granules(k), COMBINE_GPT, clear, 0)

        pick = pick_ref[pl.ds(pl.multiple_of(k * COMBINE_TILE, COMBINE_TILE), COMBINE_TILE), :]
        acc_ref[...] += lax.dot_general(pick, buf_ref[slot], (((0,), (0,)), ((), ())),
                                        preferred_element_type=F32)
        return 0

    lax.fori_loop(0, n_tiles, per_tile, 0)

    x = x1_ref[...] + gt_ref[...] * (acc_ref[...] + sh_ref[...])
    o_ref[...] = x * lax.rsqrt(jnp.mean(x * x, axis=-1, keepdims=True) + EPS) * gf_ref[...]


def _combine(cnt, goff, csm, x1, shared, mod4, g_final, ys):
    per_b = SEQ // CHUNK
    row = pl.BlockSpec((CHUNK, D_MODEL), lambda c, *_: (c, 0))
    grid_spec = pltpu.PrefetchScalarGridSpec(
        num_scalar_prefetch=2,
        grid=(N_CHUNKS,),
        in_specs=[pl.BlockSpec((N_EXPERTS, CHUNK), lambda c, *_: (0, c)),
                  row, row,
                  pl.BlockSpec((None, None, 1, D_MODEL), lambda c, *_: (c // per_b, 5, 0, 0)),
                  pl.BlockSpec((1, D_MODEL), lambda c, *_: (0, 0)),
                  pl.BlockSpec(memory_space=pl.ANY)],
        out_specs=row,
        scratch_shapes=[pltpu.SMEM((CHUNK_ROWS // GRANULE,), jnp.int32),
                        pltpu.VMEM((COMBINE_SLOTS, COMBINE_TILE, D_MODEL), BF16),
                        pltpu.VMEM((CHUNK, D_MODEL), F32),
                        pltpu.VMEM((CHUNK_ROWS, CHUNK), BF16),
                        pltpu.SemaphoreType.DMA((COMBINE_SLOTS,))])
    return pl.pallas_call(
        _combine_kernel,
        out_shape=jax.ShapeDtypeStruct((N_TOK, D_MODEL), F32),
        grid_spec=grid_spec,
        compiler_params=_cparams(("arbitrary",)),
        name="moe_combine",
    )(cnt, goff, csm, x1, shared, mod4, g_final, ys)


def _shared_kernel(x_ref, w1_ref, w3_ref, w2_ref, o_ref):
    x = x_ref[...]
    a = jnp.dot(x, w1_ref[...], preferred_element_type=F32)
    u = jnp.dot(x, w3_ref[...], preferred_element_type=F32)
    hid = (a * jax.nn.sigmoid(a)) * u
    o_ref[...] = jnp.dot(hid.astype(BF16), w2_ref[...], preferred_element_type=F32).astype(o_ref.dtype)


def _shared_expert(h2, w1, w3, w2):
    tm = 1024
    return pl.pallas_call(
        _shared_kernel,
        out_shape=jax.ShapeDtypeStruct((N_TOK, D_MODEL), F32),
        grid=(N_TOK // tm,),
        in_specs=[pl.BlockSpec((tm, D_MODEL), lambda m: (m, 0)),
                  pl.BlockSpec((D_MODEL, D_SHARED), lambda m: (0, 0)),
                  pl.BlockSpec((D_MODEL, D_SHARED), lambda m: (0, 0)),
                  pl.BlockSpec((D_SHARED, D_MODEL), lambda m: (0, 0))],
        out_specs=pl.BlockSpec((tm, D_MODEL), lambda m: (m, 0)),
        compiler_params=_cparams(("parallel",)),
        name="shared_expert",
    )(h2, w1, w3, w2)


def _rope_angles(pos, dims, theta):
    inv = jnp.power(jnp.float32(theta), -jnp.arange(0, dims, 2, dtype=jnp.float32) / dims)
    return pos.astype(jnp.float32)[:, None] * inv[None, :]


def _rotary_tables(angle_blocks):
    cos_parts, lo_parts, hi_parts = [], [], []
    used = 0
    for ang in angle_blocks:
        c, s = jnp.cos(ang), jnp.sin(ang)
        z = jnp.zeros_like(s)
        cos_parts += [c, c]
        lo_parts += [-s, z]
        hi_parts += [z, s]
        used += 2 * ang.shape[1]
    rest = HEAD_DIM - used
    if rest:
        cos_parts.append(jnp.ones((SEQ, rest), F32))
        lo_parts.append(jnp.zeros((SEQ, rest), F32))
        hi_parts.append(jnp.zeros((SEQ, rest), F32))
    return (jnp.concatenate(cos_parts, axis=1), jnp.concatenate(lo_parts, axis=1),
            jnp.concatenate(hi_parts, axis=1))


def kernel(x, c, w_ada, b_ada, g_attn, w_in, b_gate, q_norm_g, k_norm_g, w_a_up, w_b_up, w_out,
           g_ffn, w_router, e_bias, w1, w3, w2, ws1, ws3, ws2, g_final):
    l = 0
    x2 = x.reshape(N_TOK, D_MODEL)
    pos = jnp.arange(SEQ)
    tabs_a = _rotary_tables([_rope_angles(pos, ROPE_DIMS, ROPE_THETA)])
    tabs_b = _rotary_tables([_rope_angles(pos // GRID_W, AXIAL_DIMS, AXIAL_THETA),
                             _rope_angles(pos % GRID_W, AXIAL_DIMS, AXIAL_THETA)])
    tabs_a, tabs_b = lax.optimization_barrier((tabs_a, tabs_b))

    mod = _ada(c.T, w_ada[l], b_ada[l].reshape(1, -1))
    mod4 = mod.reshape(BATCH, N_MOD, 1, D_MODEL)

    perms = jnp.stack([_residue_perm(dilation) for _, dilation in A_PATTERNS])
    proj, gates = _projection(x2, g_attn[l].reshape(1, -1), mod4, w_in[l], tabs_a, tabs_b,
                              q_norm_g[l].reshape(1, -1), k_norm_g[l].reshape(1, -1), b_gate[l].reshape(1, -1),
                              perms)
    proj3 = proj.reshape(BATCH, SEQ, QKV_COLS)

    o_groups, lse_groups = [], []
    for gi, (window, dilation) in enumerate(A_PATTERNS):
        o, lse = _attn_a_dilated(proj3, gi, window, dilation)
        o_groups.append(o)
        lse_groups.append(lse)
    mix_b = _attn_b(proj3)

    merged = _merge(o_groups, lse_groups, mix_b, gates, w_a_up[l].astype(BF16), w_b_up[l].astype(BF16),
                    jnp.swapaxes(perms[1:], 1, 2))
    x1, h2, logits_t = _out_projection(merged, w_out[l].astype(BF16), x2, mod4, g_ffn[l].reshape(1, -1),
                                       w_router[l].T)
    csm, cnt_f, gx = _route(logits_t, e_bias[l].reshape(-1, 1))
    cnt = cnt_f[:, 0, :].astype(jnp.int32)
    goff, tail, tile_e, ntile, next_e = _plan(cnt)
    xs = _dispatch(cnt, goff, tail, ntile, h2, gx, csm)
    ys = _routed_experts(tile_e, ntile, next_e, xs, w1[l], w3[l], w2[l])
    shared = _shared_expert(h2, ws1[l].astype(BF16), ws3[l].astype(BF16), ws2[l].astype(BF16))
    out = _combine(cnt, goff, csm, x1, shared, mod4, g_final.reshape(1, -1), ys)
    return out.reshape(BATCH, SEQ, D_MODEL)
```
